```python
import math
import jax, jax.numpy as jnp
from jax import lax
import numpy as np

D_MODEL = 2048
BATCH = 4
SEQ = 4096
DEPTH = 1

HEAD_DIM = 64
BLOCK_Q = 128
SWA_HEADS = 16
SWA_KV_HEADS = 2
SWA_WINDOW = 128
NSA_HEADS = 16
NSA_KV_HEADS = 2
CMP_BLOCK = 32
CMP_STRIDE = 16
CMP_HIDDEN = 256
SEL_BLOCK = 64
SEL_TOPK = 16
SEL_LOCAL = 2
NSA_WINDOW = 512
REL_BUCKETS = 32
REL_MAX_DIST = 4096
REL_HEADS = SWA_HEADS + NSA_HEADS
MEM_LEN = 256
XA_HEADS = 4
XA_HEAD_DIM = 128
D_FF = 4 * D_MODEL
DN_ALPHA = (2.0 * DEPTH) ** 0.25
DN_BETA = (8.0 * DEPTH) ** -0.25
LN_EPS = 1e-5
NEG_INF = -1e30
FORCE_SCORE = 1e4

SPLIT_SIZES = (SWA_HEADS * HEAD_DIM, SWA_KV_HEADS * HEAD_DIM, SWA_KV_HEADS * HEAD_DIM, NSA_HEADS * HEAD_DIM,
               NSA_KV_HEADS * HEAD_DIM, NSA_KV_HEADS * HEAD_DIM, NSA_KV_HEADS * HEAD_DIM,
               NSA_KV_HEADS * HEAD_DIM, NSA_KV_HEADS * HEAD_DIM, NSA_KV_HEADS * HEAD_DIM,
               3 * NSA_HEADS, 2 * D_MODEL)
VALUE_COLS = (False, False, True, False, False, True, False, True, False, True, False, False)

kernel_name = "hybrid_swa_sink_nsa_gated_deepnorm"


def layer_norm(x, g, b):
    xf = x.astype(jnp.float32)
    mu = xf.mean(-1, keepdims=True)
    var = jnp.square(xf - mu).mean(-1, keepdims=True)
    y = (xf - mu) * lax.rsqrt(var + LN_EPS) * g.astype(jnp.float32) + b.astype(jnp.float32)
    return y.astype(x.dtype)


def rel_bucket(dist):
    exact = REL_BUCKETS // 2
    d = jnp.maximum(dist, 0)
    log_ratio = jnp.log(jnp.maximum(d, 1).astype(jnp.float32) / exact) / math.log(REL_MAX_DIST / exact)
    large = jnp.minimum(exact + (log_ratio * (REL_BUCKETS - exact)).astype(jnp.int32), REL_BUCKETS - 1)
    return jnp.where(d < exact, d, large)


def band_blocks(t, n_prev):
    B, S = t.shape[:2]
    nb = S // BLOCK_Q
    blk = t.reshape((B, nb, BLOCK_Q) + t.shape[2:])
    pad = jnp.zeros((B, n_prev) + blk.shape[2:], t.dtype)
    padded = jnp.concatenate([pad, blk], axis=1)
    return jnp.concatenate([padded[:, j:j + nb] for j in range(n_prev + 1)], axis=2)


def swa_sink_attention(q, k, v, sinks, rel_table):
    B, S = q.shape[:2]
    nb = S // BLOCK_Q
    G = SWA_KV_HEADS
    R = SWA_HEADS // G
    n_prev = -(-(SWA_WINDOW - 1) // BLOCK_Q)
    C = (n_prev + 1) * BLOCK_Q
    qb = q.reshape(B, nb, BLOCK_Q, G, R, HEAD_DIM) * (HEAD_DIM ** -0.5)
    kc = band_blocks(k, n_prev)
    vc = band_blocks(v, n_prev)
    logits = jnp.einsum('bnqgrd,bnkgd->bngrqk', qb, kc, preferred_element_type=jnp.float32)
    qi = jnp.arange(BLOCK_Q)
    kj = jnp.arange(C)
    dist = qi[:, None] + n_prev * BLOCK_Q - kj[None, :]
    bias = rel_table[rel_bucket(dist)].astype(jnp.float32).transpose(2, 0, 1).reshape(G, R, BLOCK_Q, C)
    kpos = jnp.arange(nb)[:, None, None] * BLOCK_Q - n_prev * BLOCK_Q + kj
    mask = (dist >= 0) & (dist < SWA_WINDOW) & (kpos >= 0)
    logits = jnp.where(mask[None, :, None, None], logits + bias, NEG_INF)
    sink = jnp.broadcast_to(sinks.astype(jnp.float32).reshape(1, 1, G, R, 1, 1), logits.shape[:-1] + (1,))
    probs = jax.nn.softmax(jnp.concatenate([logits, sink], axis=-1), axis=-1)[..., :-1]
    out = jnp.einsum('bngrqk,bnkgd->bnqgrd', probs.astype(v.dtype), vc)
    return out.reshape(B, S, SWA_HEADS * HEAD_DIM)


def compress_tokens(t, pos_emb, w1, w2):
    B, S, G = t.shape[:3]
    nc = (S - CMP_BLOCK) // CMP_STRIDE + 1
    idx = jnp.arange(nc)[:, None] * CMP_STRIDE + jnp.arange(CMP_BLOCK)[None, :]
    blocks = t[:, idx] + pos_emb[:, None, :]
    flat = blocks.transpose(0, 1, 3, 2, 4).reshape(B, nc, G, CMP_BLOCK * HEAD_DIM)
    return jax.nn.gelu(flat @ w1) @ w2


def nsa_attention(q, k_cmp, v_cmp, k_sel, v_sel, k_win, v_win, gate_logits,
                  cmp_pos_k, cmp_w1_k, cmp_w2_k, cmp_pos_v, cmp_w1_v, cmp_w2_v, rel_table):
    B, S = q.shape[:2]
    G = NSA_KV_HEADS
    R = NSA_HEADS // G
    qg = q.reshape(B, S, G, R, HEAD_DIM) * (HEAD_DIM ** -0.5)
    pos = jnp.arange(S)
    tbl = rel_table.reshape(REL_BUCKETS, G, R)

    kc = compress_tokens(k_cmp, cmp_pos_k, cmp_w1_k, cmp_w2_k)
    vc = compress_tokens(v_cmp, cmp_pos_v, cmp_w1_v, cmp_w2_v)
    nc = kc.shape[1]
    c_start = jnp.arange(nc) * CMP_STRIDE
    dist_c = pos[:, None] - (c_start + CMP_BLOCK - 1)[None, :]
    mask_c = dist_c >= 0
    bias_c = tbl[rel_bucket(dist_c)].astype(jnp.float32).transpose(2, 3, 0, 1)
    logits_c = jnp.einsum('bsgrd,bcgd->bgrsc', qg, kc, preferred_element_type=jnp.float32) + bias_c
    p_c = jax.nn.softmax(jnp.where(mask_c, logits_c, NEG_INF), axis=-1) * mask_c
    o_c = jnp.einsum('bgrsc,bcgd->bsgrd', p_c.astype(vc.dtype), vc)

    nsel = S // SEL_BLOCK
    s_start = jnp.arange(nsel) * SEL_BLOCK
    overlap = ((c_start[:, None] < s_start[None, :] + SEL_BLOCK) &
               (c_start[:, None] + CMP_BLOCK > s_start[None, :])).astype(jnp.float32)
    score = jnp.einsum('bgrsc,cj->bgsj', p_c, overlap)
    qblk = pos // SEL_BLOCK
    jsel = jnp.arange(nsel)
    causal = s_start[None, :] <= pos[:, None]
    back = qblk[:, None] - jsel[None, :]
    forced = (jsel[None, :] == 0) | ((back >= 0) & (back < SEL_LOCAL))
    score = jnp.where(causal, jnp.where(forced, FORCE_SCORE, score), -1.0)
    top = min(SEL_TOPK, nsel)
    top_val, top_idx = lax.top_k(score, top)
    top_ok = top_val >= 0

    nb = S // BLOCK_Q
    ks_blocks = k_sel.reshape(B, nsel, SEL_BLOCK, G, HEAD_DIM).transpose(0, 3, 1, 2, 4)
    vs_blocks = v_sel.reshape(B, nsel, SEL_BLOCK, G, HEAD_DIM).transpose(0, 3, 1, 2, 4)
    n_prev = -(-(NSA_WINDOW - 1) // BLOCK_Q)
    P = n_prev * BLOCK_Q
    kw_pad = jnp.pad(k_win, ((0, 0), (P, 0), (0, 0), (0, 0)))
    vw_pad = jnp.pad(v_win, ((0, 0), (P, 0), (0, 0), (0, 0)))
    b_ix = jnp.arange(B)[:, None, None, None]
    g_ix = jnp.arange(G)[None, :, None, None]

    def block_fn(args):
        i, q_i, idx_i, ok_i = args
        qpos = i * BLOCK_Q + jnp.arange(BLOCK_Q)
        kg = ks_blocks[b_ix, g_ix, idx_i]
        vg = vs_blocks[b_ix, g_ix, idx_i]
        kpos = idx_i[..., None] * SEL_BLOCK + jnp.arange(SEL_BLOCK)
        dist = qpos[:, None, None] - kpos
        mask = ok_i[..., None] & (dist >= 0)
        bias = tbl[rel_bucket(dist), g_ix[..., None]].astype(jnp.float32)
        logits = jnp.einsum('bqgrd,bgqnkd->bgrqnk', q_i, kg, preferred_element_type=jnp.float32)
        logits = jnp.where(mask[:, :, None], logits + bias.transpose(0, 1, 5, 2, 3, 4), NEG_INF)
        shp = logits.shape
        p = jax.nn.softmax(logits.reshape(shp[:4] + (-1,)), axis=-1).reshape(shp)
        o_s = jnp.einsum('bgrqnk,bgqnkd->bqgrd', p.astype(vg.dtype), vg)

        kw = lax.dynamic_slice_in_dim(kw_pad, i * BLOCK_Q, P + BLOCK_Q, axis=1)
        vw = lax.dynamic_slice_in_dim(vw_pad, i * BLOCK_Q, P + BLOCK_Q, axis=1)
        kpos_w = i * BLOCK_Q - P + jnp.arange(P + BLOCK_Q)
        dist_w = qpos[:, None] - kpos_w[None, :]
        mask_w = (dist_w >= 0) & (dist_w < NSA_WINDOW) & (kpos_w[None, :] >= 0)
        bias_w = tbl[rel_bucket(dist_w)].astype(jnp.float32).transpose(2, 3, 0, 1)
        logits_w = jnp.einsum('bqgrd,bkgd->bgrqk', q_i, kw, preferred_element_type=jnp.float32) + bias_w
        p_w = jax.nn.softmax(jnp.where(mask_w, logits_w, NEG_INF), axis=-1)
        o_w = jnp.einsum('bgrqk,bkgd->bqgrd', p_w.astype(vw.dtype), vw)
        return o_s, o_w

    q_blocks = qg.reshape(B, nb, BLOCK_Q, G, R, HEAD_DIM).transpose(1, 0, 2, 3, 4, 5)
    idx_blocks = top_idx.reshape(B, G, nb, BLOCK_Q, top).transpose(2, 0, 1, 3, 4)
    ok_blocks = top_ok.reshape(B, G, nb, BLOCK_Q, top).transpose(2, 0, 1, 3, 4)
    o_s, o_w = lax.map(block_fn, (jnp.arange(nb), q_blocks, idx_blocks, ok_blocks))
    o_s = o_s.transpose(1, 0, 2, 3, 4, 5).reshape(B, S, G, R, HEAD_DIM)
    o_w = o_w.transpose(1, 0, 2, 3, 4, 5).reshape(B, S, G, R, HEAD_DIM)

    g = jax.nn.sigmoid(gate_logits).reshape(B, S, G, R, 3)
    out = g[..., 0:1] * o_c + g[..., 1:2] * o_s + g[..., 2:3] * o_w
    return out.reshape(B, S, NSA_HEADS * HEAD_DIM)


def hybrid_mixer(x, w_in, attn_sinks, rel_table, cmp_pos_k, cmp_w1_k, cmp_w2_k, cmp_pos_v, cmp_w1_v, cmp_w2_v,
                 w_branch_swa, w_branch_nsa, w_mix_out):
    B, S, _ = x.shape
    h = x @ w_in
    offsets = np.cumsum(SPLIT_SIZES)[:-1].tolist()
    (q_a, k_a, v_a, q_b, k_c, v_c, k_s, v_s, k_w, v_w, g_nsa, g_merge) = jnp.split(h, offsets, axis=-1)
    kv = lambda t, n: t.reshape(B, S, n, HEAD_DIM)
    y_a = swa_sink_attention(kv(q_a, SWA_HEADS), kv(k_a, SWA_KV_HEADS), kv(v_a, SWA_KV_HEADS),
                             attn_sinks, rel_table[:, :SWA_HEADS])
    y_b = nsa_attention(kv(q_b, NSA_HEADS), kv(k_c, NSA_KV_HEADS), kv(v_c, NSA_KV_HEADS),
                        kv(k_s, NSA_KV_HEADS), kv(v_s, NSA_KV_HEADS), kv(k_w, NSA_KV_HEADS), kv(v_w, NSA_KV_HEADS),
                        g_nsa, cmp_pos_k, cmp_w1_k, cmp_w2_k, cmp_pos_v, cmp_w1_v, cmp_w2_v,
                        rel_table[:, SWA_HEADS:])
    g_a, g_b = jnp.split(g_merge, 2, axis=-1)
    merged = jax.nn.sigmoid(g_a) * (y_a @ w_branch_swa) + jax.nn.sigmoid(g_b) * (y_b @ w_branch_nsa)
    return merged @ w_mix_out


def memory_cross_attention(x, mem, w_q, w_kv, w_o):
    B, S, _ = x.shape
    M = mem.shape[1]
    q = (x @ w_q).reshape(B, S, XA_HEADS, XA_HEAD_DIM) * (XA_HEAD_DIM ** -0.5)
    k, v = jnp.split(mem @ w_kv, 2, axis=-1)
    k = k.reshape(B, M, XA_HEADS, XA_HEAD_DIM)
    v = v.reshape(B, M, XA_HEADS, XA_HEAD_DIM)
    p = jax.nn.softmax(jnp.einsum('bshd,bmhd->bhsm', q, k, preferred_element_type=jnp.float32), axis=-1)
    o = jnp.einsum('bhsm,bmhd->bshd', p.astype(v.dtype), v).reshape(B, S, XA_HEADS * XA_HEAD_DIM)
    return o @ w_o


def squared_relu_mlp(x, w1, w2):
    return jnp.square(jax.nn.relu(x @ w1)) @ w2


def setup_inputs(seed: int = 0) -> dict:
    key = jax.random.key(seed)
    keys = iter(jax.random.split(key, 64))
    f32 = jnp.float32
    L = DEPTH

    def normal(shape, std):
        return jax.random.normal(next(keys), shape, f32) * std

    x = normal((BATCH, SEQ, D_MODEL), 1.0)
    mem = normal((BATCH, MEM_LEN, D_MODEL), 1.0)
    w_in = jnp.concatenate([normal((L, D_MODEL, n), D_MODEL ** -0.5 * (DN_BETA if is_v else 1.0))
                            for n, is_v in zip(SPLIT_SIZES, VALUE_COLS)], axis=-1)
    attn_sinks = normal((L, SWA_HEADS), 0.5)
    rel_bias_table = normal((REL_BUCKETS, REL_HEADS), 0.5)
    cmp_pos_k = normal((L, CMP_BLOCK, HEAD_DIM), 0.1)
    cmp_w1_k = normal((L, CMP_BLOCK * HEAD_DIM, CMP_HIDDEN), (CMP_BLOCK * HEAD_DIM) ** -0.5)
    cmp_w2_k = normal((L, CMP_HIDDEN, HEAD_DIM), CMP_HIDDEN ** -0.5)
    cmp_pos_v = normal((L, CMP_BLOCK, HEAD_DIM), 0.1)
    cmp_w1_v = normal((L, CMP_BLOCK * HEAD_DIM, CMP_HIDDEN), (CMP_BLOCK * HEAD_DIM) ** -0.5)
    cmp_w2_v = normal((L, CMP_HIDDEN, HEAD_DIM), CMP_HIDDEN ** -0.5)
    w_branch_swa = normal((L, SWA_HEADS * HEAD_DIM, D_MODEL), DN_BETA * (SWA_HEADS * HEAD_DIM) ** -0.5)
    w_branch_nsa = normal((L, NSA_HEADS * HEAD_DIM, D_MODEL), DN_BETA * (NSA_HEADS * HEAD_DIM) ** -0.5)
    w_mix_out = normal((L, D_MODEL, D_MODEL), DN_BETA * D_MODEL ** -0.5)
    ln1_g = 1.0 + normal((L, D_MODEL), 0.02)
    ln1_b = normal((L, D_MODEL), 0.02)
    xa_w_q = normal((L, D_MODEL, XA_HEADS * XA_HEAD_DIM), D_MODEL ** -0.5)
    xa_w_kv = jnp.concatenate([normal((L, D_MODEL, XA_HEADS * XA_HEAD_DIM), D_MODEL ** -0.5),
                               normal((L, D_MODEL, XA_HEADS * XA_HEAD_DIM), DN_BETA * D_MODEL ** -0.5)], axis=-1)
    xa_w_o = normal((L, XA_HEADS * XA_HEAD_DIM, D_MODEL), DN_BETA * (XA_HEADS * XA_HEAD_DIM) ** -0.5)
    ln2_g = 1.0 + normal((L, D_MODEL), 0.02)
    ln2_b = normal((L, D_MODEL), 0.02)
    mlp_w1 = normal((L, D_MODEL, D_FF), DN_BETA * D_MODEL ** -0.5)
    mlp_w2 = normal((L, D_FF, D_MODEL), DN_BETA * D_FF ** -0.5)
    ln3_g = 1.0 + normal((L, D_MODEL), 0.02)
    ln3_b = normal((L, D_MODEL), 0.02)
    return {"x": x, "mem": mem, "w_in": w_in, "attn_sinks": attn_sinks, "rel_bias_table": rel_bias_table,
            "cmp_pos_k": cmp_pos_k, "cmp_w1_k": cmp_w1_k, "cmp_w2_k": cmp_w2_k,
            "cmp_pos_v": cmp_pos_v, "cmp_w1_v": cmp_w1_v, "cmp_w2_v": cmp_w2_v,
            "w_branch_swa": w_branch_swa, "w_branch_nsa": w_branch_nsa, "w_mix_out": w_mix_out,
            "ln1_g": ln1_g, "ln1_b": ln1_b, "xa_w_q": xa_w_q, "xa_w_kv": xa_w_kv, "xa_w_o": xa_w_o,
            "ln2_g": ln2_g, "ln2_b": ln2_b, "mlp_w1": mlp_w1, "mlp_w2": mlp_w2,
            "ln3_g": ln3_g, "ln3_b": ln3_b}


def reference(x, mem, w_in, attn_sinks, rel_bias_table, cmp_pos_k, cmp_w1_k, cmp_w2_k, cmp_pos_v, cmp_w1_v,
              cmp_w2_v, w_branch_swa, w_branch_nsa, w_mix_out, ln1_g, ln1_b, xa_w_q, xa_w_kv, xa_w_o,
              ln2_g, ln2_b, mlp_w1, mlp_w2, ln3_g, ln3_b):
    h = x
    for l in range(DEPTH):
        mix = hybrid_mixer(h, w_in[l], attn_sinks[l], rel_bias_table, cmp_pos_k[l], cmp_w1_k[l], cmp_w2_k[l],
                           cmp_pos_v[l], cmp_w1_v[l], cmp_w2_v[l], w_branch_swa[l], w_branch_nsa[l], w_mix_out[l])
        h = layer_norm(DN_ALPHA * h + mix, ln1_g[l], ln1_b[l])
        xa = memory_cross_attention(h, mem, xa_w_q[l], xa_w_kv[l], xa_w_o[l])
        h = layer_norm(DN_ALPHA * h + xa, ln2_g[l], ln2_b[l])
        ff = squared_relu_mlp(h, mlp_w1[l], mlp_w2[l])
        h = layer_norm(DN_ALPHA * h + ff, ln3_g[l], ln3_b[l])
    return h
```

```python
import functools
import math

import numpy as np
import jax
import jax.numpy as jnp
from jax import lax
from jax.experimental import pallas as pl
from jax.experimental.pallas import tpu as pltpu

F32 = jnp.float32
BF16 = jnp.bfloat16

HEAD_DIM = 64
BLOCK_Q = 128
SWA_HEADS = 16
SWA_KV_HEADS = 2
SWA_WINDOW = 128
NSA_HEADS = 16
NSA_KV_HEADS = 2
CMP_BLOCK = 32
CMP_STRIDE = 16
SEL_BLOCK = 64
SEL_TOPK = 16
SEL_LOCAL = 2
NSA_WINDOW = 512
REL_BUCKETS = 32
REL_MAX_DIST = 4096
XA_HEADS = 4
XA_HEAD_DIM = 128
DEPTH = 1
DN_ALPHA = (2.0 * DEPTH) ** 0.25
LN_EPS = 1e-5
NEG_INF = -1e30
FORCE_SCORE = 1e4

GROUPS = 2
HEADS_PER_GROUP = 8
GROUP_LANES = HEADS_PER_GROUP * BLOCK_Q
GROUP_COLS = HEADS_PER_GROUP * HEAD_DIM

V7X_VMEM_LIMIT_BYTES = 56 * 1024 * 1024


def _params(n_axes):
    return pltpu.CompilerParams(dimension_semantics=("arbitrary",) * n_axes,
                                vmem_limit_bytes=V7X_VMEM_LIMIT_BYTES)


def _mm_kernel(a_ref, b_ref, o_ref, *, nt):
    if nt:
        out = lax.dot_general(a_ref[...], b_ref[...], (((1,), (1,)), ((), ())), preferred_element_type=F32)
    else:
        out = jnp.dot(a_ref[...], b_ref[...], preferred_element_type=F32)
    o_ref[...] = out.astype(o_ref.dtype)


def _matmul(a, b, *, nt, tm, tn, out_dtype=BF16):
    m, k = a.shape
    n = b.shape[0] if nt else b.shape[1]
    b_spec = pl.BlockSpec((tn, k), lambda i, j: (j, 0)) if nt else pl.BlockSpec((k, tn), lambda i, j: (0, j))
    return pl.pallas_call(
        functools.partial(_mm_kernel, nt=nt),
        grid=(m // tm, n // tn),
        in_specs=[pl.BlockSpec((tm, k), lambda i, j: (i, 0)), b_spec],
        out_specs=pl.BlockSpec((tm, tn), lambda i, j: (i, j)),
        out_shape=jax.ShapeDtypeStruct((m, n), out_dtype),
        compiler_params=_params(2),
        name="proj_nt" if nt else "proj_nn",
    )(a, b)


def _rel_bucket(dist):
    exact = REL_BUCKETS // 2
    d = jnp.maximum(dist, 0)
    log_ratio = jnp.log(jnp.maximum(d, 1).astype(F32) / exact) / math.log(REL_MAX_DIST / exact)
    large = jnp.minimum(exact + (log_ratio * (REL_BUCKETS - exact)).astype(jnp.int32), REL_BUCKETS - 1)
    return jnp.where(d < exact, d, large)


def _bias_by_distance(rel_table, seq):
    by_dist = rel_table[_rel_bucket(jnp.arange(seq))].astype(F32).T
    padded = jnp.pad(by_dist, ((0, 0), (BLOCK_Q, 0)))
    return padded.reshape(rel_table.shape[1], seq // BLOCK_Q + 1, BLOCK_Q)


def _build_skew_table(vb_ref, tbl_ref, n_delta):
    def body(d, carry):
        for h in range(HEADS_PER_GROUP):
            lo = jnp.broadcast_to(vb_ref[h, pl.ds(d, 1), :], (BLOCK_Q, BLOCK_Q))
            hi = jnp.broadcast_to(vb_ref[h, pl.ds(d + 1, 1), :], (BLOCK_Q, BLOCK_Q))
            y = pltpu.roll(jnp.concatenate([lo, hi], axis=1), 0, 1, stride=1, stride_axis=0)
            tbl_ref[d, :, h * BLOCK_Q:(h + 1) * BLOCK_Q] = y[:, BLOCK_Q:]
        return carry

    lax.fori_loop(0, n_delta, body, 0)


def _gather_heads_to_lanes(q_ref):
    q = jnp.concatenate([q_ref[h * HEAD_DIM:(h + 1) * HEAD_DIM, :] for h in range(HEADS_PER_GROUP)], axis=1)
    return q * jnp.asarray(HEAD_DIM ** -0.5, q.dtype)


def _heads_to_columns(o_t):
    stacked = jnp.concatenate([o_t[:, h * BLOCK_Q:(h + 1) * BLOCK_Q] for h in range(HEADS_PER_GROUP)], axis=0)
    return stacked.T


def _tile_iotas():
    k_io = lax.broadcasted_iota(jnp.int32, (BLOCK_Q, GROUP_LANES), 0)
    q_io = lax.broadcasted_iota(jnp.int32, (BLOCK_Q, GROUP_LANES), 1) & (BLOCK_Q - 1)
    return k_io, q_io


def _online_softmax_step(carry, s, valid, v_t):
    m, l, acc = carry
    s = jnp.where(valid, s, NEG_INF)
    m_new = jnp.maximum(m, jnp.max(s, axis=0, keepdims=True))
    alpha = jnp.exp(m - m_new)
    p = jnp.exp(s - m_new)
    l = alpha * l + jnp.sum(p, axis=0, keepdims=True)
    acc = alpha * acc + jnp.dot(v_t, p.astype(BF16), preferred_element_type=F32)
    return m_new, l, acc


def _swa_kernel(q_ref, kprev_ref, kcur_ref, vprev_ref, vcur_ref, sink_ref, vb_ref, o_ref, tbl_ref):
    b, i = pl.program_id(1), pl.program_id(2)

    @pl.when((b == 0) & (i == 0))
    def _():
        _build_skew_table(vb_ref, tbl_ref, 2)

    q_t = _gather_heads_to_lanes(q_ref)
    k_io, q_io = _tile_iotas()
    carry = (sink_ref[...], jnp.ones((1, GROUP_LANES), F32), jnp.zeros((HEAD_DIM, GROUP_LANES), F32))
    s = jnp.dot(kcur_ref[...], q_t, preferred_element_type=F32) + tbl_ref[0]
    carry = _online_softmax_step(carry, s, q_io >= k_io, vcur_ref[...])
    s = jnp.dot(kprev_ref[...], q_t, preferred_element_type=F32) + tbl_ref[1]
    carry = _online_softmax_step(carry, s, (q_io < k_io) & (i > 0), vprev_ref[...])
    _, l, acc = carry
    o_ref[...] = _heads_to_columns(acc * (1.0 / l)).astype(o_ref.dtype)


def _swa_attention(h_t, kk, sink_rows, vb, batch, seq):
    nq = seq // BLOCK_Q
    tokens = batch * seq
    v_row0 = (SWA_HEADS + NSA_HEADS) * HEAD_DIM // HEAD_DIM
    return pl.pallas_call(
        _swa_kernel,
        grid=(GROUPS, batch, nq),
        in_specs=[
            pl.BlockSpec((GROUP_COLS, BLOCK_Q), lambda g, b, i: (g, b * nq + i)),
            pl.BlockSpec((None, None, BLOCK_Q, HEAD_DIM), lambda g, b, i: (0, g, b * nq + jnp.maximum(i - 1, 0), 0)),
            pl.BlockSpec((None, None, BLOCK_Q, HEAD_DIM), lambda g, b, i: (0, g, b * nq + i, 0)),
            pl.BlockSpec((HEAD_DIM, BLOCK_Q), lambda g, b, i: (v_row0 + g, b * nq + jnp.maximum(i - 1, 0))),
            pl.BlockSpec((HEAD_DIM, BLOCK_Q), lambda g, b, i: (v_row0 + g, b * nq + i)),
            pl.BlockSpec((None, 1, GROUP_LANES), lambda g, b, i: (g, 0, 0)),
            pl.BlockSpec((None, HEADS_PER_GROUP, 3, BLOCK_Q), lambda g, b, i: (g, 0, 0, 0)),
        ],
        out_specs=pl.BlockSpec((BLOCK_Q, GROUP_COLS), lambda g, b, i: (b * nq + i, g)),
        out_shape=jax.ShapeDtypeStruct((tokens, SWA_HEADS * HEAD_DIM), BF16),
        scratch_shapes=[pltpu.VMEM((2, BLOCK_Q, GROUP_LANES), F32)],
        compiler_params=_params(3),
        name="swa_attention",
    )(h_t, kk, kk, h_t, h_t, sink_rows, vb)


def _compress_kernel(c_ref, pos_ref, w1_ref, w2_ref, w2t_ref, cn_ref, ct_ref):
    c = c_ref[...].astype(F32)
    top = (c + pos_ref[0]).astype(BF16)
    bot = (c + pos_ref[1]).astype(BF16)
    a = jnp.dot(top, w1_ref[0], preferred_element_type=F32)
    bm = jnp.dot(bot, w1_ref[1], preferred_element_type=F32)
    n = a.shape[0]
    pre = a + pltpu.roll(bm, n - 1, 0)
    hid = jax.nn.gelu(pre).astype(BF16)
    cn_ref[...] = jnp.dot(hid, w2_ref[...], preferred_element_type=F32).astype(cn_ref.dtype)
    ct_ref[...] = lax.dot_general(w2t_ref[...], hid, (((1,), (1,)), ((), ())),
                                  preferred_element_type=F32).astype(ct_ref.dtype)


def _compress(chunks, pos, w1, w2, w2t):
    _, g, b, ncp, width = chunks.shape
    hidden = w1.shape[-1]
    return pl.pallas_call(
        _compress_kernel,
        grid=(2, g, b),
        in_specs=[
            pl.BlockSpec((None, None, None, ncp, width), lambda t, g, b: (t, g, b, 0, 0)),
            pl.BlockSpec((None, 2, 1, width), lambda t, g, b: (t, 0, 0, 0)),
            pl.BlockSpec((None, 2, width, hidden), lambda t, g, b: (t, 0, 0, 0)),
            pl.BlockSpec((None, hidden, HEAD_DIM), lambda t, g, b: (t, 0, 0)),
            pl.BlockSpec((None, HEAD_DIM, hidden), lambda t, g, b: (t, 0, 0)),
        ],
        out_specs=[
            pl.BlockSpec((None, None, None, ncp, HEAD_DIM), lambda t, g, b: (t, g, b, 0, 0)),
            pl.BlockSpec((None, None, None, HEAD_DIM, ncp), lambda t, g, b: (t, g, b, 0, 0)),
        ],
        out_shape=[jax.ShapeDtypeStruct((2, g, b, ncp, HEAD_DIM), BF16),
                   jax.ShapeDtypeStruct((2, g, b, HEAD_DIM, ncp), BF16)],
        compiler_params=_params(3),
        name="nsa_compress",
    )(chunks, pos, w1, w2, w2t)


def _cmp_kernel(q_ref, kc_ref, vct_ref, bias_ref, ov_ref, oc_ref, sel_ref, *, nq):
    i = pl.program_id(2)
    ncp = kc_ref.shape[0]
    nsel = sel_ref.shape[0]
    q_t = _gather_heads_to_lanes(q_ref)
    row0 = pl.multiple_of((nq - 1 - i) * 8, 8)
    s = jnp.dot(kc_ref[...], q_t, preferred_element_type=F32) + bias_ref[pl.ds(row0, ncp), :]
    c_io = lax.broadcasted_iota(jnp.int32, (ncp, GROUP_LANES), 0)
    pos = i * BLOCK_Q + (lax.broadcasted_iota(jnp.int32, (ncp, GROUP_LANES), 1) & (BLOCK_Q - 1))
    valid = pos - c_io * CMP_STRIDE - (CMP_BLOCK - 1) >= 0
    s = jnp.where(valid, s, NEG_INF)
    m = jnp.max(s, axis=0, keepdims=True)
    p = jnp.where(valid, jnp.exp(s - m), 0.0)
    l = jnp.sum(p, axis=0, keepdims=True)
    p = p * (1.0 / jnp.where(l > 0.0, l, 1.0))
    oc_ref[...] = jnp.dot(vct_ref[...], p.astype(BF16), preferred_element_type=F32).astype(oc_ref.dtype)

    psum = p[:, 0:BLOCK_Q]
    for h in range(1, HEADS_PER_GROUP):
        psum = psum + p[:, h * BLOCK_Q:(h + 1) * BLOCK_Q]
    hi = psum.astype(BF16)
    lo = (psum - hi.astype(F32)).astype(BF16)
    ov = ov_ref[...]
    score = jnp.dot(ov, hi, preferred_element_type=F32) + jnp.dot(ov, lo, preferred_element_type=F32)

    j_io = lax.broadcasted_iota(jnp.int32, (nsel, BLOCK_Q), 0)
    qpos = i * BLOCK_Q + lax.broadcasted_iota(jnp.int32, (nsel, BLOCK_Q), 1)
    causal = j_io * SEL_BLOCK <= qpos
    back = qpos // SEL_BLOCK - j_io
    forced = (j_io == 0) | ((back >= 0) & (back < SEL_LOCAL))
    score = jnp.where(causal, jnp.where(forced, FORCE_SCORE, score), -1.0)
    rank = jnp.zeros((nsel, BLOCK_Q), jnp.int32)
    for r in range(nsel):
        row = score[r:r + 1, :]
        ge = jnp.where(row >= score, 1, 0)
        gt = jnp.where(row > score, 1, 0)
        rank = rank + jnp.where(j_io > r, ge, gt)
    sel_ref[...] = ((rank < min(SEL_TOPK, nsel)) & causal).astype(sel_ref.dtype)


def _cmp_attention(h_t, cn, ct, cmp_bias, overlap_t, batch, seq):
    nq = seq // BLOCK_Q
    ncp = seq // CMP_STRIDE
    nsel = seq // SEL_BLOCK
    q_blk0 = SWA_HEADS * HEAD_DIM // GROUP_COLS
    return pl.pallas_call(
        functools.partial(_cmp_kernel, nq=nq),
        grid=(GROUPS, batch, nq),
        in_specs=[
            pl.BlockSpec((GROUP_COLS, BLOCK_Q), lambda g, b, i: (q_blk0 + g, b * nq + i)),
            pl.BlockSpec((None, None, None, ncp, HEAD_DIM), lambda g, b, i: (0, g, b, 0, 0)),
            pl.BlockSpec((None, None, None, HEAD_DIM, ncp), lambda g, b, i: (1, g, b, 0, 0)),
            pl.BlockSpec((None, (2 * nq - 1) * 8, GROUP_LANES), lambda g, b, i: (g, 0, 0)),
            pl.BlockSpec((nsel, ncp), lambda g, b, i: (0, 0)),
        ],
        out_specs=[
            pl.BlockSpec((None, None, None, HEAD_DIM, GROUP_LANES), lambda g, b, i: (b, g, i, 0, 0)),
            pl.BlockSpec((None, None, nsel, BLOCK_Q), lambda g, b, i: (b, g, 0, i)),
        ],
        out_shape=[jax.ShapeDtypeStruct((batch, GROUPS, nq, HEAD_DIM, GROUP_LANES), BF16),
                   jax.ShapeDtypeStruct((batch, GROUPS, nsel, seq), F32)],
        compiler_params=_params(3),
        name="nsa_cmp_select",
    )(h_t, cn, ct, cmp_bias, overlap_t)


def _selwin_kernel(q_ref, ksel_ref, kwin_ref, vsel_ref, vwin_ref, sel_ref, oc_ref, gate_ref, vb_ref,
                   o_ref, tbl_ref, *, nq):
    b, i = pl.program_id(1), pl.program_id(2)

    @pl.when((b == 0) & (i == 0))
    def _():
        _build_skew_table(vb_ref, tbl_ref, nq)

    q_t = _gather_heads_to_lanes(q_ref)
    k_io, q_io = _tile_iotas()
    init = (jnp.full((1, GROUP_LANES), NEG_INF, F32), jnp.zeros((1, GROUP_LANES), F32),
            jnp.zeros((HEAD_DIM, GROUP_LANES), F32))

    def sel_body(n, carry):
        t = i - n
        k_t = ksel_ref[pl.ds(pl.multiple_of(t * BLOCK_Q, BLOCK_Q), BLOCK_Q), :]
        rows = sel_ref[t]
        chosen = jnp.concatenate([jnp.broadcast_to(rows[0:1, :], (SEL_BLOCK, BLOCK_Q)),
                                  jnp.broadcast_to(rows[1:2, :], (SEL_BLOCK, BLOCK_Q))], axis=0)
        chosen = jnp.concatenate([chosen] * HEADS_PER_GROUP, axis=1)
        valid = (chosen > 0.5) & (n * BLOCK_Q + q_io - k_io >= 0)
        s = jnp.dot(k_t, q_t, preferred_element_type=F32) + tbl_ref[n]
        return _online_softmax_step(carry, s, valid, vsel_ref[t])

    _, l_s, acc_s = lax.fori_loop(0, i + 1, sel_body, init)

    def win_body(n, carry):
        t = i - n
        k_t = kwin_ref[pl.ds(pl.multiple_of(t * BLOCK_Q, BLOCK_Q), BLOCK_Q), :]
        dist = n * BLOCK_Q + q_io - k_io
        valid = (dist >= 0) & (dist < NSA_WINDOW)
        s = jnp.dot(k_t, q_t, preferred_element_type=F32) + tbl_ref[n]
        return _online_softmax_step(carry, s, valid, vwin_ref[t])

    n_win = -(-(NSA_WINDOW - 1) // BLOCK_Q) + 1
    _, l_w, acc_w = lax.fori_loop(0, jnp.minimum(i + 1, n_win), win_body, init)

    def gate_row(branch):
        g = jnp.concatenate([gate_ref[branch, h:h + 1, :] for h in range(HEADS_PER_GROUP)], axis=1)
        return jax.nn.sigmoid(g.astype(F32))

    out = (gate_row(0) * oc_ref[...].astype(F32) + gate_row(1) * (acc_s * (1.0 / l_s))
           + gate_row(2) * (acc_w * (1.0 / l_w)))
    o_ref[...] = _heads_to_columns(out).astype(o_ref.dtype)


def _selwin_attention(h_t, kk, vsel_t, vwin_t, sel, oc_t, gates_t, vb, batch, seq):
    nq = seq // BLOCK_Q
    tokens = batch * seq
    q_blk0 = SWA_HEADS * HEAD_DIM // GROUP_COLS
    return pl.pallas_call(
        functools.partial(_selwin_kernel, nq=nq),
        grid=(GROUPS, batch, nq),
        in_specs=[
            pl.BlockSpec((GROUP_COLS, BLOCK_Q), lambda g, b, i: (q_blk0 + g, b * nq + i)),
            pl.BlockSpec((None, None, seq, HEAD_DIM), lambda g, b, i: (2, g, b, 0)),
            pl.BlockSpec((None, None, seq, HEAD_DIM), lambda g, b, i: (3, g, b, 0)),
            pl.BlockSpec((None, None, nq, HEAD_DIM, BLOCK_Q), lambda g, b, i: (g, b, 0, 0, 0)),
            pl.BlockSpec((None, None, nq, HEAD_DIM, BLOCK_Q), lambda g, b, i: (g, b, 0, 0, 0)),
            pl.BlockSpec((None, None, nq, 2, BLOCK_Q), lambda g, b, i: (b, g, 0, 0, i)),
            pl.BlockSpec((None, None, None, HEAD_DIM, GROUP_LANES), lambda g, b, i: (b, g, i, 0, 0)),
            pl.BlockSpec((None, 3, HEADS_PER_GROUP, BLOCK_Q), lambda g, b, i: (g, 0, 0, b * nq + i)),
            pl.BlockSpec((None, HEADS_PER_GROUP, nq + 1, BLOCK_Q), lambda g, b, i: (g, 0, 0, 0)),
        ],
        out_specs=pl.BlockSpec((BLOCK_Q, GROUP_COLS), lambda g, b, i: (b * nq + i, g)),
        out_shape=jax.ShapeDtypeStruct((tokens, NSA_HEADS * HEAD_DIM), BF16),
        scratch_shapes=[pltpu.VMEM((nq, BLOCK_Q, GROUP_LANES), F32)],
        compiler_params=_params(3),
        name="nsa_sel_win",
    )(h_t, kk, kk, vsel_t, vwin_t, sel, oc_t, gates_t, vb)


def _layer_norm(y, g_ref, b_ref):
    mu = jnp.mean(y, axis=-1, keepdims=True)
    yc = y - mu
    var = jnp.mean(yc * yc, axis=-1, keepdims=True)
    return yc * lax.rsqrt(var + LN_EPS) * g_ref[...] + b_ref[...]


def _merge_kernel(x_ref, ya_ref, yb_ref, wga_ref, wgb_ref, wa_ref, wb_ref, o_ref):
    x = x_ref[...]
    ga = jax.nn.sigmoid(jnp.dot(x, wga_ref[...], preferred_element_type=F32))
    gb = jax.nn.sigmoid(jnp.dot(x, wgb_ref[...], preferred_element_type=F32))
    a = jnp.dot(ya_ref[...], wa_ref[...], preferred_element_type=F32)
    bb = jnp.dot(yb_ref[...], wb_ref[...], preferred_element_type=F32)
    o_ref[...] = (ga * a + gb * bb).astype(o_ref.dtype)


def _merge(x_bf, y_a, y_b, w_ga, w_gb, w_a, w_b, *, tm, tn):
    tokens, d = x_bf.shape
    ya_cols, yb_cols = y_a.shape[1], y_b.shape[1]
    return pl.pallas_call(
        _merge_kernel,
        grid=(tokens // tm, d // tn),
        in_specs=[
            pl.BlockSpec((tm, d), lambda i, j: (i, 0)),
            pl.BlockSpec((tm, ya_cols), lambda i, j: (i, 0)),
            pl.BlockSpec((tm, yb_cols), lambda i, j: (i, 0)),
            pl.BlockSpec((d, tn), lambda i, j: (0, j)),
            pl.BlockSpec((d, tn), lambda i, j: (0, j)),
            pl.BlockSpec((ya_cols, tn), lambda i, j: (0, j)),
            pl.BlockSpec((yb_cols, tn), lambda i, j: (0, j)),
        ],
        out_specs=pl.BlockSpec((tm, tn), lambda i, j: (i, j)),
        out_shape=jax.ShapeDtypeStruct((tokens, d), BF16),
        compiler_params=_params(2),
        name="branch_merge",
    )(x_bf, y_a, y_b, w_ga, w_gb, w_a, w_b)


def _mix_ln_kernel(m_ref, x_ref, w_ref, g_ref, b_ref, o_ref):
    mix = jnp.dot(m_ref[...], w_ref[...], preferred_element_type=F32)
    o_ref[...] = _layer_norm(DN_ALPHA * x_ref[...] + mix, g_ref, b_ref)


def _mix_ln(merged, x, w_mix, ln_g, ln_b, *, tm):
    tokens, d = x.shape
    return pl.pallas_call(
        _mix_ln_kernel,
        grid=(tokens // tm,),
        in_specs=[
            pl.BlockSpec((tm, d), lambda i: (i, 0)),
            pl.BlockSpec((tm, d), lambda i: (i, 0)),
            pl.BlockSpec((d, d), lambda i: (0, 0)),
            pl.BlockSpec((1, d), lambda i: (0, 0)),
            pl.BlockSpec((1, d), lambda i: (0, 0)),
        ],
        out_specs=pl.BlockSpec((tm, d), lambda i: (i, 0)),
        out_shape=jax.ShapeDtypeStruct((tokens, d), F32),
        compiler_params=_params(1),
        name="mix_out_ln1",
    )(merged, x, w_mix, ln_g, ln_b)


def _xa_kernel(h_ref, wq_ref, k_ref, v_ref, wo_ref, g_ref, b_ref, o_ref):
    h = h_ref[...]
    q = jnp.dot(h.astype(BF16), wq_ref[...], preferred_element_type=F32) * (XA_HEAD_DIM ** -0.5)
    q = q.astype(BF16)
    outs = []
    for hd in range(XA_HEADS):
        cols = slice(hd * XA_HEAD_DIM, (hd + 1) * XA_HEAD_DIM)
        s = lax.dot_general(q[:, cols], k_ref[:, cols], (((1,), (1,)), ((), ())), preferred_element_type=F32)
        p = jnp.exp(s - jnp.max(s, axis=-1, keepdims=True))
        l = jnp.sum(p, axis=-1, keepdims=True)
        o = jnp.dot(p.astype(BF16), v_ref[:, cols], preferred_element_type=F32)
        outs.append(o * (1.0 / l))
    o = jnp.concatenate(outs, axis=1).astype(BF16)
    xa = jnp.dot(o, wo_ref[...], preferred_element_type=F32)
    o_ref[...] = _layer_norm(DN_ALPHA * h + xa, g_ref, b_ref)


def _cross_attention_ln(h, kv_mem, w_q, w_o, ln_g, ln_b, batch, seq, *, tm):
    tokens, d = h.shape
    mem_len = kv_mem.shape[0] // batch
    xa_dim = XA_HEADS * XA_HEAD_DIM
    nt = seq // tm
    return pl.pallas_call(
        _xa_kernel,
        grid=(batch, nt),
        in_specs=[
            pl.BlockSpec((tm, d), lambda b, i: (b * nt + i, 0)),
            pl.BlockSpec((d, xa_dim), lambda b, i: (0, 0)),
            pl.BlockSpec((mem_len, xa_dim), lambda b, i: (b, 0)),
            pl.BlockSpec((mem_len, xa_dim), lambda b, i: (b, 1)),
            pl.BlockSpec((xa_dim, d), lambda b, i: (0, 0)),
            pl.BlockSpec((1, d), lambda b, i: (0, 0)),
            pl.BlockSpec((1, d), lambda b, i: (0, 0)),
        ],
        out_specs=pl.BlockSpec((tm, d), lambda b, i: (b * nt + i, 0)),
        out_shape=jax.ShapeDtypeStruct((tokens, d), F32),
        compiler_params=_params(2),
        name="cross_attention_ln2",
    )(h, w_q, kv_mem, kv_mem, w_o, ln_g, ln_b)


def _mlp_kernel(h_ref, w1_ref, w2_ref, g_ref, b_ref, o_ref, hb_ref):
    j = pl.program_id(1)

    @pl.when(j == 0)
    def _():
        hb_ref[...] = h_ref[...].astype(BF16)
        o_ref[...] = jnp.zeros_like(o_ref)

    u = jnp.dot(hb_ref[...], w1_ref[...], preferred_element_type=F32)
    u = jnp.square(jnp.maximum(u, 0.0)).astype(BF16)
    o_ref[...] += jnp.dot(u, w2_ref[...], preferred_element_type=F32)

    @pl.when(j == pl.num_programs(1) - 1)
    def _():
        o_ref[...] = _layer_norm(DN_ALPHA * h_ref[...] + o_ref[...], g_ref, b_ref)


def _mlp_ln(h, w1, w2, ln_g, ln_b, *, tm, tf):
    tokens, d = h.shape
    d_ff = w1.shape[1]
    return pl.pallas_call(
        _mlp_kernel,
        grid=(tokens // tm, d_ff // tf),
        in_specs=[
            pl.BlockSpec((tm, d), lambda i, j: (i, 0)),
            pl.BlockSpec((d, tf), lambda i, j: (0, j)),
            pl.BlockSpec((tf, d), lambda i, j: (j, 0)),
            pl.BlockSpec((1, d), lambda i, j: (0, 0)),
            pl.BlockSpec((1, d), lambda i, j: (0, 0)),
        ],
        out_specs=pl.BlockSpec((tm, d), lambda i, j: (i, 0)),
        out_shape=jax.ShapeDtypeStruct((tokens, d), F32),
        scratch_shapes=[pltpu.VMEM((tm, d), BF16)],
        compiler_params=_params(2),
        name="mlp_ln3",
    )(h, w1, w2, ln_g, ln_b)


def _cmp_bias_table(rel_nsa, seq):
    nq = seq // BLOCK_Q
    rows = np.arange((2 * nq - 1) * 8)
    delta = (nq - 1) - rows // 8
    dist = (BLOCK_Q * delta - CMP_STRIDE * (rows % 8) - (CMP_BLOCK - 1))[:, None] + np.arange(BLOCK_Q)[None, :]
    tbl = rel_nsa[_rel_bucket(jnp.asarray(dist, jnp.int32))].astype(F32)
    tbl = tbl.reshape(rows.size, BLOCK_Q, GROUPS, HEADS_PER_GROUP).transpose(2, 0, 3, 1)
    return tbl.reshape(GROUPS, rows.size, GROUP_LANES)


def _overlap_t(seq):
    ncp, nsel = seq // CMP_STRIDE, seq // SEL_BLOCK
    c_start = np.arange(ncp)[None, :] * CMP_STRIDE
    s_start = np.arange(nsel)[:, None] * SEL_BLOCK
    ov = (c_start < s_start + SEL_BLOCK) & (c_start + CMP_BLOCK > s_start) & (np.arange(ncp)[None, :] < ncp - 1)
    return jnp.asarray(ov, BF16)


def kernel(x, mem, w_in, attn_sinks, rel_bias_table, cmp_pos_k, cmp_w1_k, cmp_w2_k, cmp_pos_v, cmp_w1_v, cmp_w2_v,
           w_branch_swa, w_branch_nsa, w_mix_out, ln1_g, ln1_b, xa_w_q, xa_w_kv, xa_w_o, ln2_g, ln2_b,
           mlp_w1, mlp_w2, ln3_g, ln3_b):
    batch, seq, d = x.shape
    tokens = batch * seq
    nq = seq // BLOCK_Q
    assert w_in.shape[0] == 1, "one layer"
    assert seq % (8 * BLOCK_Q) == 0 and d % 512 == 0

    w = w_in[0]
    sizes = (SWA_HEADS * HEAD_DIM, 128, 128, NSA_HEADS * HEAD_DIM, 128, 128, 128, 128, 128, 128, 3 * NSA_HEADS, d, d)
    offs = np.concatenate([[0], np.cumsum(sizes)])
    (w_qa, w_ka, w_va, w_qb, w_kc, w_vc, w_ks, w_vs, w_kw, w_vw, w_gn, w_ga, w_gb) = [
        w[:, offs[n]:offs[n + 1]] for n in range(len(sizes))]
    w_rows = jnp.concatenate([w_ka, w_kc, w_ks, w_kw, w_vc], axis=1).astype(BF16)
    w_gn_pad = jnp.pad(w_gn, ((0, 0), (0, 128 - w_gn.shape[1])))
    w_cols_t = jnp.concatenate([w_qa, w_qb, w_va, w_vs, w_vw, w_gn_pad], axis=1).T.astype(BF16)

    x2 = x.reshape(tokens, d)
    x_bf = x2.astype(BF16)

    h_rows = _matmul(x_bf, w_rows, nt=False, tm=min(2048, tokens), tn=w_rows.shape[1])
    h_t = _matmul(w_cols_t, x_bf, nt=True, tm=1280, tn=min(1024, tokens))

    kk = h_rows[:, :512].reshape(tokens, 4, GROUPS, HEAD_DIM).transpose(1, 2, 0, 3)
    v_cmp = h_rows[:, 512:].reshape(tokens, GROUPS, HEAD_DIM).transpose(1, 0, 2)

    def value_tiles(row0):
        return h_t[row0:row0 + 128].reshape(GROUPS, HEAD_DIM, batch, nq, BLOCK_Q).transpose(0, 2, 3, 1, 4)

    vsel_t, vwin_t = value_tiles(2176), value_tiles(2304)
    gates_t = h_t[2432:2432 + 3 * NSA_HEADS].reshape(GROUPS, HEADS_PER_GROUP, 3, tokens).transpose(0, 2, 1, 3)

    vb = _bias_by_distance(rel_bias_table, seq)
    vb_swa = vb[:SWA_HEADS, :3].reshape(GROUPS, HEADS_PER_GROUP, 3, BLOCK_Q)
    vb_nsa = vb[SWA_HEADS:].reshape(GROUPS, HEADS_PER_GROUP, nq + 1, BLOCK_Q)
    sink_rows = jnp.repeat(attn_sinks[0].astype(F32), BLOCK_Q).reshape(GROUPS, 1, GROUP_LANES)

    y_a = _swa_attention(h_t, kk, sink_rows, vb_swa, batch, seq)

    ncp = seq // CMP_STRIDE
    chunk_w = CMP_STRIDE * HEAD_DIM
    chunks = jnp.stack([kk[1], v_cmp]).reshape(2, GROUPS, batch, ncp, chunk_w)
    pos = jnp.stack([cmp_pos_k[0], cmp_pos_v[0]]).astype(F32).reshape(2, 2, 1, chunk_w)
    w1 = jnp.stack([cmp_w1_k[0], cmp_w1_v[0]]).astype(BF16).reshape(2, 2, chunk_w, -1)
    w2 = jnp.stack([cmp_w2_k[0], cmp_w2_v[0]]).astype(BF16)
    cn, ct = _compress(chunks, pos, w1, w2, w2.transpose(0, 2, 1))
    oc_t, sel = _cmp_attention(h_t, cn, ct, _cmp_bias_table(rel_bias_table[:, SWA_HEADS:], seq),
                               _overlap_t(seq), batch, seq)
    sel = sel.reshape(batch, GROUPS, nq, 2, seq)
    y_b = _selwin_attention(h_t, kk, vsel_t, vwin_t, sel, oc_t, gates_t, vb_nsa, batch, seq)

    merged = _merge(x_bf, y_a, y_b, w_ga.astype(BF16), w_gb.astype(BF16),
                    w_branch_swa[0].astype(BF16), w_branch_nsa[0].astype(BF16), tm=min(1024, tokens), tn=512)
    h1 = _mix_ln(merged, x2, w_mix_out[0].astype(BF16), ln1_g, ln1_b, tm=512)

    mem_bf = mem.reshape(-1, d).astype(BF16)
    kv_mem = _matmul(mem_bf, xa_w_kv[0].astype(BF16), nt=False, tm=mem_bf.shape[0], tn=512)
    h2 = _cross_attention_ln(h1, kv_mem, xa_w_q[0].astype(BF16), xa_w_o[0].astype(BF16), ln2_g, ln2_b,
                             batch, seq, tm=512)

    h3 = _mlp_ln(h2, mlp_w1[0].astype(BF16), mlp_w2[0].astype(BF16), ln3_g, ln3_b, tm=512, tf=1024)
    return h3.reshape(batch, seq, d)
```

```python
import functools
import math

import numpy as np
import jax
import jax.numpy as jnp
from jax import lax
from jax.experimental import pallas as pl
from jax.experimental.pallas import tpu as pltpu

F32 = jnp.float32
BF16 = jnp.bfloat16

HEAD_DIM = 64
BLOCK_Q = 128
SWA_HEADS = 16
SWA_KV_HEADS = 2
SWA_WINDOW = 128
NSA_HEADS = 16
NSA_KV_HEADS = 2
CMP_BLOCK = 32
CMP_STRIDE = 16
SEL_BLOCK = 64
SEL_TOPK = 16
SEL_LOCAL = 2
NSA_WINDOW = 512
REL_BUCKETS = 32
REL_MAX_DIST = 4096
XA_HEADS = 4
XA_HEAD_DIM = 128
DEPTH = 1
DN_ALPHA = (2.0 * DEPTH) ** 0.25
LN_EPS = 1e-5
NEG_INF = -1e30
FORCE_SCORE = 1e4

GROUPS = 2
HEADS_PER_GROUP = 8
GROUP_LANES = HEADS_PER_GROUP * BLOCK_Q
GROUP_COLS = HEADS_PER_GROUP * HEAD_DIM

V7X_VMEM_LIMIT_BYTES = 56 * 1024 * 1024


def _params(n_axes):
    return pltpu.CompilerParams(dimension_semantics=("arbitrary",) * n_axes,
                                vmem_limit_bytes=V7X_VMEM_LIMIT_BYTES)


def _mm_kernel(a_ref, b_ref, o_ref, *, nt):
    if nt:
        out = lax.dot_general(a_ref[...], b_ref[...], (((1,), (1,)), ((), ())), preferred_element_type=F32)
    else:
        out = jnp.dot(a_ref[...], b_ref[...], preferred_element_type=F32)
    o_ref[...] = out.astype(o_ref.dtype)


def _matmul(a, b, *, nt, tm, tn, out_dtype=BF16):
    m, k = a.shape
    n = b.shape[0] if nt else b.shape[1]
    b_spec = pl.BlockSpec((tn, k), lambda i, j: (j, 0)) if nt else pl.BlockSpec((k, tn), lambda i, j: (0, j))
    return pl.pallas_call(
        functools.partial(_mm_kernel, nt=nt),
        grid=(m // tm, n // tn),
        in_specs=[pl.BlockSpec((tm, k), lambda i, j: (i, 0)), b_spec],
        out_specs=pl.BlockSpec((tm, tn), lambda i, j: (i, j)),
        out_shape=jax.ShapeDtypeStruct((m, n), out_dtype),
        compiler_params=_params(2),
        name="proj_nt" if nt else "proj_nn",
    )(a, b)


def _rel_bucket(dist):
    exact = REL_BUCKETS // 2
    d = jnp.maximum(dist, 0)
    log_ratio = jnp.log(jnp.maximum(d, 1).astype(F32) / exact) / math.log(REL_MAX_DIST / exact)
    large = jnp.minimum(exact + (log_ratio * (REL_BUCKETS - exact)).astype(jnp.int32), REL_BUCKETS - 1)
    return jnp.where(d < exact, d, large)


def _bias_by_distance(rel_table, seq):
    by_dist = rel_table[_rel_bucket(jnp.arange(seq))].astype(F32).T
    padded = jnp.pad(by_dist, ((0, 0), (2 * BLOCK_Q, 0)))
    return padded.reshape(rel_table.shape[1], seq // BLOCK_Q + 2, BLOCK_Q)


def _build_skew_table(vb_ref, tbl_ref, n_delta):
    def body(d, carry):
        for h in range(HEADS_PER_GROUP):
            lo = jnp.broadcast_to(vb_ref[h, pl.ds(d + 1, 1), :], (BLOCK_Q, BLOCK_Q))
            hi = jnp.broadcast_to(vb_ref[h, pl.ds(d + 2, 1), :], (BLOCK_Q, BLOCK_Q))
            y = pltpu.roll(jnp.concatenate([lo, hi], axis=1), 0, 1, stride=1, stride_axis=0)
            tbl_ref[d, :, h * BLOCK_Q:(h + 1) * BLOCK_Q] = y[:, BLOCK_Q:]
        return carry

    lax.fori_loop(0, n_delta, body, 0)


def _build_cmp_bias_table(vb_ref, tbl_ref, skew_ref, nq):
    per_tile = BLOCK_Q // CMP_STRIDE
    assert per_tile == 8 and CMP_BLOCK - 1 + CMP_STRIDE * (per_tile - 1) - BLOCK_Q == CMP_STRIDE - 1
    tbl_ref[pl.ds(nq * 8, (nq - 1) * 8), :] = jnp.zeros(((nq - 1) * 8, GROUP_LANES), F32)

    def body(d, last_row):
        for h in range(HEADS_PER_GROUP):
            lo = jnp.broadcast_to(vb_ref[h, pl.ds(d + 1, 1), :], (BLOCK_Q, BLOCK_Q))
            hi = jnp.broadcast_to(vb_ref[h, pl.ds(d + 2, 1), :], (BLOCK_Q, BLOCK_Q))
            y = pltpu.roll(jnp.concatenate([lo, hi], axis=1), 0, 1, stride=1, stride_axis=0)
            skew_ref[:, h * BLOCK_Q:(h + 1) * BLOCK_Q] = y[:, BLOCK_Q:]
        offsets = [CMP_BLOCK - 1 + CMP_STRIDE * c for c in range(per_tile - 1)]
        rows = [skew_ref[k:k + 1, :] for k in offsets] + [last_row]
        tbl_ref[pl.ds(pl.multiple_of((nq - 1 - d) * 8, 8), 8), :] = jnp.concatenate(rows, axis=0)
        return skew_ref[CMP_STRIDE - 1:CMP_STRIDE, :]

    lax.fori_loop(0, nq, body, jnp.zeros((1, GROUP_LANES), F32))


def _gather_heads_to_lanes(q_ref):
    q = jnp.concatenate([q_ref[h * HEAD_DIM:(h + 1) * HEAD_DIM, :] for h in range(HEADS_PER_GROUP)], axis=1)
    return q * jnp.asarray(HEAD_DIM ** -0.5, q.dtype)


def _heads_to_columns(o_t):
    stacked = jnp.concatenate([o_t[:, h * BLOCK_Q:(h + 1) * BLOCK_Q] for h in range(HEADS_PER_GROUP)], axis=0)
    return stacked.T


def _softmax_update(carry, s, bias_fn, valid, v_t):
    m, l, acc = carry
    ms, ls, alphas, ps = [], [], [], []
    for h in range(HEADS_PER_GROUP):
        lanes = slice(h * BLOCK_Q, (h + 1) * BLOCK_Q)
        s_h = jnp.where(valid, s[:, lanes] + bias_fn(h), NEG_INF)
        m_new = jnp.maximum(m[:, lanes], jnp.max(s_h, axis=0, keepdims=True))
        alpha = jnp.exp(m[:, lanes] - m_new)
        p = jnp.exp(s_h - m_new)
        ms.append(m_new)
        ls.append(alpha * l[:, lanes] + jnp.sum(p, axis=0, keepdims=True))
        alphas.append(alpha)
        ps.append(p.astype(BF16))
    p = jnp.concatenate(ps, axis=1)
    acc = jnp.concatenate(alphas, axis=1) * acc + jnp.dot(v_t, p, preferred_element_type=F32)
    return jnp.concatenate(ms, axis=1), jnp.concatenate(ls, axis=1), acc


def _swa_kernel(q_ref, kprev_ref, kcur_ref, vprev_ref, vcur_ref, sink_ref, vb_ref, o_ref, tbl_ref):
    b, i = pl.program_id(1), pl.program_id(2)

    @pl.when((b == 0) & (i == 0))
    def _():
        _build_skew_table(vb_ref, tbl_ref, 2)

    q_t = _gather_heads_to_lanes(q_ref)
    k2 = jnp.concatenate([kprev_ref[...], kcur_ref[...]], axis=0)
    v2 = jnp.concatenate([vprev_ref[...], vcur_ref[...]], axis=1)
    row = lax.broadcasted_iota(jnp.int32, (2 * BLOCK_Q, BLOCK_Q), 0)
    dist = BLOCK_Q + lax.broadcasted_iota(jnp.int32, (2 * BLOCK_Q, BLOCK_Q), 1) - row
    valid = (dist >= 0) & (dist < SWA_WINDOW) & ((row >= BLOCK_Q) | (i > 0))
    carry = (sink_ref[...], jnp.ones((1, GROUP_LANES), F32), jnp.zeros((HEAD_DIM, GROUP_LANES), F32))
    s = jnp.dot(k2, q_t, preferred_element_type=F32)

    def bias(h):
        lanes = slice(h * BLOCK_Q, (h + 1) * BLOCK_Q)
        return jnp.concatenate([tbl_ref[1, :, lanes], tbl_ref[0, :, lanes]], axis=0)

    _, l, acc = _softmax_update(carry, s, bias, valid, v2)
    o_ref[...] = _heads_to_columns(acc * (1.0 / l)).astype(o_ref.dtype)


def _swa_attention(h_t, kk, sink_rows, vb, batch, seq):
    nq = seq // BLOCK_Q
    tokens = batch * seq
    v_row0 = (SWA_HEADS + NSA_HEADS) * HEAD_DIM // HEAD_DIM
    return pl.pallas_call(
        _swa_kernel,
        grid=(GROUPS, batch, nq),
        in_specs=[
            pl.BlockSpec((GROUP_COLS, BLOCK_Q), lambda g, b, i: (g, b * nq + i)),
            pl.BlockSpec((None, None, BLOCK_Q, HEAD_DIM), lambda g, b, i: (0, g, b * nq + jnp.maximum(i - 1, 0), 0)),
            pl.BlockSpec((None, None, BLOCK_Q, HEAD_DIM), lambda g, b, i: (0, g, b * nq + i, 0)),
            pl.BlockSpec((HEAD_DIM, BLOCK_Q), lambda g, b, i: (v_row0 + g, b * nq + jnp.maximum(i - 1, 0))),
            pl.BlockSpec((HEAD_DIM, BLOCK_Q), lambda g, b, i: (v_row0 + g, b * nq + i)),
            pl.BlockSpec((None, 1, GROUP_LANES), lambda g, b, i: (g, 0, 0)),
            pl.BlockSpec((None, HEADS_PER_GROUP, 4, BLOCK_Q), lambda g, b, i: (g, 0, 0, 0)),
        ],
        out_specs=pl.BlockSpec((BLOCK_Q, GROUP_COLS), lambda g, b, i: (b * nq + i, g)),
        out_shape=jax.ShapeDtypeStruct((tokens, SWA_HEADS * HEAD_DIM), BF16),
        scratch_shapes=[pltpu.VMEM((2, BLOCK_Q, GROUP_LANES), F32)],
        compiler_params=_params(3),
        name="swa_attention",
    )(h_t, kk, kk, h_t, h_t, sink_rows, vb)


def _compress_kernel(c_ref, pos_ref, w1_ref, w2_ref, w2t_ref, cn_ref, ct_ref):
    c = c_ref[...].astype(F32)
    top = (c + pos_ref[0]).astype(BF16)
    bot = (c + pos_ref[1]).astype(BF16)
    a = jnp.dot(top, w1_ref[0], preferred_element_type=F32)
    bm = jnp.dot(bot, w1_ref[1], preferred_element_type=F32)
    n = a.shape[0]
    pre = a + pltpu.roll(bm, n - 1, 0)
    hid = jax.nn.gelu(pre).astype(BF16)
    cn_ref[...] = jnp.dot(hid, w2_ref[...], preferred_element_type=F32).astype(cn_ref.dtype)
    ct_ref[...] = lax.dot_general(w2t_ref[...], hid, (((1,), (1,)), ((), ())),
                                  preferred_element_type=F32).astype(ct_ref.dtype)


def _compress(chunks, pos, w1, w2, w2t):
    _, g, b, ncp, width = chunks.shape
    hidden = w1.shape[-1]
    return pl.pallas_call(
        _compress_kernel,
        grid=(2, g, b),
        in_specs=[
            pl.BlockSpec((None, None, None, ncp, width), lambda t, g, b: (t, g, b, 0, 0)),
            pl.BlockSpec((None, 2, 1, width), lambda t, g, b: (t, 0, 0, 0)),
            pl.BlockSpec((None, 2, width, hidden), lambda t, g, b: (t, 0, 0, 0)),
            pl.BlockSpec((None, hidden, HEAD_DIM), lambda t, g, b: (t, 0, 0)),
            pl.BlockSpec((None, HEAD_DIM, hidden), lambda t, g, b: (t, 0, 0)),
        ],
        out_specs=[
            pl.BlockSpec((None, None, None, ncp, HEAD_DIM), lambda t, g, b: (t, g, b, 0, 0)),
            pl.BlockSpec((None, None, None, HEAD_DIM, ncp), lambda t, g, b: (t, g, b, 0, 0)),
        ],
        out_shape=[jax.ShapeDtypeStruct((2, g, b, ncp, HEAD_DIM), BF16),
                   jax.ShapeDtypeStruct((2, g, b, HEAD_DIM, ncp), BF16)],
        compiler_params=_params(3),
        name="nsa_compress",
    )(chunks, pos, w1, w2, w2t)


def _cmp_kernel(q_ref, kc_ref, vct_ref, vb_ref, ov_ref, oc_ref, sel_ref, bias_ref, skew_ref, *, nq):
    b, i = pl.program_id(1), pl.program_id(2)

    @pl.when((b == 0) & (i == 0))
    def _():
        _build_cmp_bias_table(vb_ref, bias_ref, skew_ref, nq)

    ncp = kc_ref.shape[0]
    nsel = sel_ref.shape[0]
    q_t = _gather_heads_to_lanes(q_ref)
    row0 = pl.multiple_of((nq - 1 - i) * 8, 8)
    s = jnp.dot(kc_ref[...], q_t, preferred_element_type=F32) + bias_ref[pl.ds(row0, ncp), :]
    c_io = lax.broadcasted_iota(jnp.int32, (ncp, GROUP_LANES), 0)
    pos = i * BLOCK_Q + (lax.broadcasted_iota(jnp.int32, (ncp, GROUP_LANES), 1) & (BLOCK_Q - 1))
    valid = pos - c_io * CMP_STRIDE - (CMP_BLOCK - 1) >= 0
    s = jnp.where(valid, s, NEG_INF)
    m = jnp.max(s, axis=0, keepdims=True)
    p = jnp.where(valid, jnp.exp(s - m), 0.0)
    l = jnp.sum(p, axis=0, keepdims=True)
    p = p * (1.0 / jnp.where(l > 0.0, l, 1.0))
    oc_ref[...] = jnp.dot(vct_ref[...], p.astype(BF16), preferred_element_type=F32).astype(oc_ref.dtype)

    psum = p[:, 0:BLOCK_Q]
    for h in range(1, HEADS_PER_GROUP):
        psum = psum + p[:, h * BLOCK_Q:(h + 1) * BLOCK_Q]
    hi = psum.astype(BF16)
    lo = (psum - hi.astype(F32)).astype(BF16)
    ov = ov_ref[...]
    score = jnp.dot(ov, hi, preferred_element_type=F32) + jnp.dot(ov, lo, preferred_element_type=F32)

    j_io = lax.broadcasted_iota(jnp.int32, (nsel, BLOCK_Q), 0)
    qpos = i * BLOCK_Q + lax.broadcasted_iota(jnp.int32, (nsel, BLOCK_Q), 1)
    causal = j_io * SEL_BLOCK <= qpos
    back = qpos // SEL_BLOCK - j_io
    forced = (j_io == 0) | ((back >= 0) & (back < SEL_LOCAL))
    score = jnp.where(causal, jnp.where(forced, FORCE_SCORE, score), -1.0)
    rank = jnp.zeros((nsel, BLOCK_Q), jnp.int32)
    for r in range(nsel):
        row = score[r:r + 1, :]
        ge = jnp.where(row >= score, 1, 0)
        gt = jnp.where(row > score, 1, 0)
        rank = rank + jnp.where(j_io > r, ge, gt)
    sel_ref[...] = ((rank < min(SEL_TOPK, nsel)) & causal).astype(sel_ref.dtype)


def _cmp_attention(h_t, cn, ct, vb, overlap_t, batch, seq):
    nq = seq // BLOCK_Q
    ncp = seq // CMP_STRIDE
    nsel = seq // SEL_BLOCK
    q_blk0 = SWA_HEADS * HEAD_DIM // GROUP_COLS
    return pl.pallas_call(
        functools.partial(_cmp_kernel, nq=nq),
        grid=(GROUPS, batch, nq),
        in_specs=[
            pl.BlockSpec((GROUP_COLS, BLOCK_Q), lambda g, b, i: (q_blk0 + g, b * nq + i)),
            pl.BlockSpec((None, None, None, ncp, HEAD_DIM), lambda g, b, i: (0, g, b, 0, 0)),
            pl.BlockSpec((None, None, None, HEAD_DIM, ncp), lambda g, b, i: (1, g, b, 0, 0)),
            pl.BlockSpec((None, HEADS_PER_GROUP, nq + 2, BLOCK_Q), lambda g, b, i: (g, 0, 0, 0)),
            pl.BlockSpec((nsel, ncp), lambda g, b, i: (0, 0)),
        ],
        out_specs=[
            pl.BlockSpec((None, None, None, HEAD_DIM, GROUP_LANES), lambda g, b, i: (b, g, i, 0, 0)),
            pl.BlockSpec((None, None, nsel, BLOCK_Q), lambda g, b, i: (b, g, 0, i)),
        ],
        out_shape=[jax.ShapeDtypeStruct((batch, GROUPS, nq, HEAD_DIM, GROUP_LANES), BF16),
                   jax.ShapeDtypeStruct((batch, GROUPS, nsel, seq), F32)],
        scratch_shapes=[pltpu.VMEM(((2 * nq - 1) * 8, GROUP_LANES), F32), pltpu.VMEM((BLOCK_Q, GROUP_LANES), F32)],
        compiler_params=_params(3),
        name="nsa_cmp_select",
    )(h_t, cn, ct, vb, overlap_t)


def _selwin_kernel(q_ref, ksel_ref, kwin_ref, vsel_ref, vwin_ref, sel_ref, oc_ref, gate_ref, vb_ref,
                   o_ref, tbl_ref, *, nq):
    b, i = pl.program_id(1), pl.program_id(2)

    @pl.when((b == 0) & (i == 0))
    def _():
        _build_skew_table(vb_ref, tbl_ref, nq)

    q_t = _gather_heads_to_lanes(q_ref)
    pair = 2 * BLOCK_Q
    row = lax.broadcasted_iota(jnp.int32, (pair, BLOCK_Q), 0)
    q_minus_row = i * BLOCK_Q + lax.broadcasted_iota(jnp.int32, (pair, BLOCK_Q), 1) - row
    init = (jnp.full((1, GROUP_LANES), NEG_INF, F32), jnp.zeros((1, GROUP_LANES), F32),
            jnp.zeros((HEAD_DIM, GROUP_LANES), F32))
    p_diag = i // 2

    def attend(carry, p, k_ref, v_ref, valid):
        k2 = k_ref[pl.ds(pl.multiple_of(p * pair, pair), pair), :]
        s = jnp.dot(k2, q_t, preferred_element_type=F32)
        d_lo = i - 2 * p
        d_hi = jnp.maximum(d_lo - 1, 0)

        def bias(h):
            lanes = slice(h * BLOCK_Q, (h + 1) * BLOCK_Q)
            return jnp.concatenate([tbl_ref[d_lo, :, lanes], tbl_ref[d_hi, :, lanes]], axis=0)

        return _softmax_update(carry, s, bias, valid, v_ref[p])

    def sel_body(n, carry):
        p = p_diag - n
        rows = sel_ref[p]
        chosen = jnp.concatenate([jnp.broadcast_to(rows[r:r + 1, :], (SEL_BLOCK, BLOCK_Q))
                                  for r in range(pair // SEL_BLOCK)], axis=0)
        valid = (chosen > 0.5) & (q_minus_row - p * pair >= 0)
        return attend(carry, p, ksel_ref, vsel_ref, valid)

    _, l_s, acc_s = lax.fori_loop(0, p_diag + 1, sel_body, init)

    def win_body(n, carry):
        p = p_diag - n
        dist = q_minus_row - p * pair
        return attend(carry, p, kwin_ref, vwin_ref, (dist >= 0) & (dist < NSA_WINDOW))

    n_win = -(-(NSA_WINDOW - 1) // pair) + 1
    _, l_w, acc_w = lax.fori_loop(0, jnp.minimum(p_diag + 1, n_win), win_body, init)

    def gate_row(branch):
        g = jnp.concatenate([gate_ref[branch, h:h + 1, :] for h in range(HEADS_PER_GROUP)], axis=1)
        return jax.nn.sigmoid(g.astype(F32))

    out = (gate_row(0) * oc_ref[...].astype(F32) + gate_row(1) * (acc_s * (1.0 / l_s))
           + gate_row(2) * (acc_w * (1.0 / l_w)))
    o_ref[...] = _heads_to_columns(out).astype(o_ref.dtype)


def _selwin_attention(h_t, kk, vsel_t, vwin_t, sel, oc_t, gates_t, vb, batch, seq):
    nq = seq // BLOCK_Q
    tokens = batch * seq
    q_blk0 = SWA_HEADS * HEAD_DIM // GROUP_COLS
    return pl.pallas_call(
        functools.partial(_selwin_kernel, nq=nq),
        grid=(GROUPS, batch, nq),
        in_specs=[
            pl.BlockSpec((GROUP_COLS, BLOCK_Q), lambda g, b, i: (q_blk0 + g, b * nq + i)),
            pl.BlockSpec((None, None, seq, HEAD_DIM), lambda g, b, i: (2, g, b, 0)),
            pl.BlockSpec((None, None, seq, HEAD_DIM), lambda g, b, i: (3, g, b, 0)),
            pl.BlockSpec((None, None, nq // 2, HEAD_DIM, 2 * BLOCK_Q), lambda g, b, i: (g, b, 0, 0, 0)),
            pl.BlockSpec((None, None, nq // 2, HEAD_DIM, 2 * BLOCK_Q), lambda g, b, i: (g, b, 0, 0, 0)),
            pl.BlockSpec((None, None, nq // 2, 4, BLOCK_Q), lambda g, b, i: (b, g, 0, 0, i)),
            pl.BlockSpec((None, None, None, HEAD_DIM, GROUP_LANES), lambda g, b, i: (b, g, i, 0, 0)),
            pl.BlockSpec((None, 3, HEADS_PER_GROUP, BLOCK_Q), lambda g, b, i: (g, 0, 0, b * nq + i)),
            pl.BlockSpec((None, HEADS_PER_GROUP, nq + 2, BLOCK_Q), lambda g, b, i: (g, 0, 0, 0)),
        ],
        out_specs=pl.BlockSpec((BLOCK_Q, GROUP_COLS), lambda g, b, i: (b * nq + i, g)),
        out_shape=jax.ShapeDtypeStruct((tokens, NSA_HEADS * HEAD_DIM), BF16),
        scratch_shapes=[pltpu.VMEM((nq, BLOCK_Q, GROUP_LANES), F32)],
        compiler_params=_params(3),
        name="nsa_sel_win",
    )(h_t, kk, kk, vsel_t, vwin_t, sel, oc_t, gates_t, vb)


def _layer_norm(y, g_ref, b_ref):
    mu = jnp.mean(y, axis=-1, keepdims=True)
    yc = y - mu
    var = jnp.mean(yc * yc, axis=-1, keepdims=True)
    return yc * lax.rsqrt(var + LN_EPS) * g_ref[...] + b_ref[...]


def _merge_kernel(x_ref, ya_ref, yb_ref, wga_ref, wgb_ref, wa_ref, wb_ref, o_ref):
    x = x_ref[...]
    ga = jax.nn.sigmoid(jnp.dot(x, wga_ref[...], preferred_element_type=F32))
    gb = jax.nn.sigmoid(jnp.dot(x, wgb_ref[...], preferred_element_type=F32))
    a = jnp.dot(ya_ref[...], wa_ref[...], preferred_element_type=F32)
    bb = jnp.dot(yb_ref[...], wb_ref[...], preferred_element_type=F32)
    o_ref[...] = (ga * a + gb * bb).astype(o_ref.dtype)


def _merge(x_bf, y_a, y_b, w_ga, w_gb, w_a, w_b, *, tm, tn):
    tokens, d = x_bf.shape
    ya_cols, yb_cols = y_a.shape[1], y_b.shape[1]
    return pl.pallas_call(
        _merge_kernel,
        grid=(tokens // tm, d // tn),
        in_specs=[
            pl.BlockSpec((tm, d), lambda i, j: (i, 0)),
            pl.BlockSpec((tm, ya_cols), lambda i, j: (i, 0)),
            pl.BlockSpec((tm, yb_cols), lambda i, j: (i, 0)),
            pl.BlockSpec((d, tn), lambda i, j: (0, j)),
            pl.BlockSpec((d, tn), lambda i, j: (0, j)),
            pl.BlockSpec((ya_cols, tn), lambda i, j: (0, j)),
            pl.BlockSpec((yb_cols, tn), lambda i, j: (0, j)),
        ],
        out_specs=pl.BlockSpec((tm, tn), lambda i, j: (i, j)),
        out_shape=jax.ShapeDtypeStruct((tokens, d), BF16),
        compiler_params=_params(2),
        name="branch_merge",
    )(x_bf, y_a, y_b, w_ga, w_gb, w_a, w_b)


def _mix_ln_kernel(m_ref, x_ref, w_ref, g_ref, b_ref, o_ref):
    mix = jnp.dot(m_ref[...], w_ref[...], preferred_element_type=F32)
    o_ref[...] = _layer_norm(DN_ALPHA * x_ref[...] + mix, g_ref, b_ref)


def _mix_ln(merged, x, w_mix, ln_g, ln_b, *, tm):
    tokens, d = x.shape
    return pl.pallas_call(
        _mix_ln_kernel,
        grid=(tokens // tm,),
        in_specs=[
            pl.BlockSpec((tm, d), lambda i: (i, 0)),
            pl.BlockSpec((tm, d), lambda i: (i, 0)),
            pl.BlockSpec((d, d), lambda i: (0, 0)),
            pl.BlockSpec((1, d), lambda i: (0, 0)),
            pl.BlockSpec((1, d), lambda i: (0, 0)),
        ],
        out_specs=pl.BlockSpec((tm, d), lambda i: (i, 0)),
        out_shape=jax.ShapeDtypeStruct((tokens, d), F32),
        compiler_params=_params(1),
        name="mix_out_ln1",
    )(merged, x, w_mix, ln_g, ln_b)


def _xa_kernel(h_ref, wq_ref, k_ref, v_ref, wo_ref, g_ref, b_ref, o_ref):
    h = h_ref[...]
    q = jnp.dot(h.astype(BF16), wq_ref[...], preferred_element_type=F32) * (XA_HEAD_DIM ** -0.5)
    q = q.astype(BF16)
    outs = []
    for hd in range(XA_HEADS):
        cols = slice(hd * XA_HEAD_DIM, (hd + 1) * XA_HEAD_DIM)
        s = lax.dot_general(q[:, cols], k_ref[:, cols], (((1,), (1,)), ((), ())), preferred_element_type=F32)
        p = jnp.exp(s - jnp.max(s, axis=-1, keepdims=True))
        l = jnp.sum(p, axis=-1, keepdims=True)
        o = jnp.dot(p.astype(BF16), v_ref[:, cols], preferred_element_type=F32)
        outs.append(o * (1.0 / l))
    o = jnp.concatenate(outs, axis=1).astype(BF16)
    xa = jnp.dot(o, wo_ref[...], preferred_element_type=F32)
    o_ref[...] = _layer_norm(DN_ALPHA * h + xa, g_ref, b_ref)


def _cross_attention_ln(h, kv_mem, w_q, w_o, ln_g, ln_b, batch, seq, *, tm):
    tokens, d = h.shape
    mem_len = kv_mem.shape[0] // batch
    xa_dim = XA_HEADS * XA_HEAD_DIM
    nt = seq // tm
    return pl.pallas_call(
        _xa_kernel,
        grid=(batch, nt),
        in_specs=[
            pl.BlockSpec((tm, d), lambda b, i: (b * nt + i, 0)),
            pl.BlockSpec((d, xa_dim), lambda b, i: (0, 0)),
            pl.BlockSpec((mem_len, xa_dim), lambda b, i: (b, 0)),
            pl.BlockSpec((mem_len, xa_dim), lambda b, i: (b, 1)),
            pl.BlockSpec((xa_dim, d), lambda b, i: (0, 0)),
            pl.BlockSpec((1, d), lambda b, i: (0, 0)),
            pl.BlockSpec((1, d), lambda b, i: (0, 0)),
        ],
        out_specs=pl.BlockSpec((tm, d), lambda b, i: (b * nt + i, 0)),
        out_shape=jax.ShapeDtypeStruct((tokens, d), F32),
        compiler_params=_params(2),
        name="cross_attention_ln2",
    )(h, w_q, kv_mem, kv_mem, w_o, ln_g, ln_b)


def _mlp_kernel(h_ref, w1_ref, w2_ref, g_ref, b_ref, o_ref, hb_ref):
    j = pl.program_id(1)

    @pl.when(j == 0)
    def _():
        hb_ref[...] = h_ref[...].astype(BF16)
        o_ref[...] = jnp.zeros_like(o_ref)

    u = jnp.dot(hb_ref[...], w1_ref[...], preferred_element_type=F32)
    u = jnp.square(jnp.maximum(u, 0.0)).astype(BF16)
    o_ref[...] += jnp.dot(u, w2_ref[...], preferred_element_type=F32)

    @pl.when(j == pl.num_programs(1) - 1)
    def _():
        o_ref[...] = _layer_norm(DN_ALPHA * h_ref[...] + o_ref[...], g_ref, b_ref)


def _mlp_ln(h, w1, w2, ln_g, ln_b, *, tm, tf):
    tokens, d = h.shape
    d_ff = w1.shape[1]
    return pl.pallas_call(
        _mlp_kernel,
        grid=(tokens // tm, d_ff // tf),
        in_specs=[
            pl.BlockSpec((tm, d), lambda i, j: (i, 0)),
            pl.BlockSpec((d, tf), lambda i, j: (0, j)),
            pl.BlockSpec((tf, d), lambda i, j: (j, 0)),
            pl.BlockSpec((1, d), lambda i, j: (0, 0)),
            pl.BlockSpec((1, d), lambda i, j: (0, 0)),
        ],
        out_specs=pl.BlockSpec((tm, d), lambda i, j: (i, 0)),
        out_shape=jax.ShapeDtypeStruct((tokens, d), F32),
        scratch_shapes=[pltpu.VMEM((tm, d), BF16)],
        compiler_params=_params(2),
        name="mlp_ln3",
    )(h, w1, w2, ln_g, ln_b)


def _overlap_t(seq):
    ncp, nsel = seq // CMP_STRIDE, seq // SEL_BLOCK
    c_start = np.arange(ncp)[None, :] * CMP_STRIDE
    s_start = np.arange(nsel)[:, None] * SEL_BLOCK
    ov = (c_start < s_start + SEL_BLOCK) & (c_start + CMP_BLOCK > s_start) & (np.arange(ncp)[None, :] < ncp - 1)
    return jnp.asarray(ov, BF16)


def kernel(x, mem, w_in, attn_sinks, rel_bias_table, cmp_pos_k, cmp_w1_k, cmp_w2_k, cmp_pos_v, cmp_w1_v, cmp_w2_v,
           w_branch_swa, w_branch_nsa, w_mix_out, ln1_g, ln1_b, xa_w_q, xa_w_kv, xa_w_o, ln2_g, ln2_b,
           mlp_w1, mlp_w2, ln3_g, ln3_b):
    batch, seq, d = x.shape
    tokens = batch * seq
    nq = seq // BLOCK_Q
    assert w_in.shape[0] == 1, "one layer"
    assert seq % (8 * BLOCK_Q) == 0 and d % 512 == 0

    w = w_in[0]
    sizes = (SWA_HEADS * HEAD_DIM, 128, 128, NSA_HEADS * HEAD_DIM, 128, 128, 128, 128, 128, 128, 3 * NSA_HEADS, d, d)
    offs = np.concatenate([[0], np.cumsum(sizes)])
    (w_qa, w_ka, w_va, w_qb, w_kc, w_vc, w_ks, w_vs, w_kw, w_vw, w_gn, w_ga, w_gb) = [
        w[:, offs[n]:offs[n + 1]] for n in range(len(sizes))]
    w_rows = jnp.concatenate([w_ka, w_kc, w_ks, w_kw, w_vc], axis=1).astype(BF16)
    w_gn_pad = jnp.pad(w_gn, ((0, 0), (0, 128 - w_gn.shape[1])))
    w_cols_t = jnp.concatenate([w_qa, w_qb, w_va, w_vs, w_vw, w_gn_pad], axis=1).T.astype(BF16)

    x2 = x.reshape(tokens, d)
    x_bf = x2.astype(BF16)

    h_rows = _matmul(x_bf, w_rows, nt=False, tm=min(2048, tokens), tn=w_rows.shape[1])
    h_t = _matmul(w_cols_t, x_bf, nt=True, tm=1280, tn=min(1024, tokens))

    kk = h_rows[:, :512].reshape(tokens, 4, GROUPS, HEAD_DIM).transpose(1, 2, 0, 3)
    v_cmp = h_rows[:, 512:].reshape(tokens, GROUPS, HEAD_DIM).transpose(1, 0, 2)

    def value_tiles(row0):
        return h_t[row0:row0 + 128].reshape(GROUPS, HEAD_DIM, batch, nq // 2, 2 * BLOCK_Q).transpose(0, 2, 3, 1, 4)

    vsel_t, vwin_t = value_tiles(2176), value_tiles(2304)
    gates_t = h_t[2432:2432 + 3 * NSA_HEADS].reshape(GROUPS, HEADS_PER_GROUP, 3, tokens).transpose(0, 2, 1, 3)

    vb = _bias_by_distance(rel_bias_table, seq)
    vb_swa = vb[:SWA_HEADS, :4].reshape(GROUPS, HEADS_PER_GROUP, 4, BLOCK_Q)
    vb_nsa = vb[SWA_HEADS:].reshape(GROUPS, HEADS_PER_GROUP, nq + 2, BLOCK_Q)
    sink_rows = jnp.repeat(attn_sinks[0].astype(F32), BLOCK_Q).reshape(GROUPS, 1, GROUP_LANES)

    y_a = _swa_attention(h_t, kk, sink_rows, vb_swa, batch, seq)

    ncp = seq // CMP_STRIDE
    chunk_w = CMP_STRIDE * HEAD_DIM
    chunks = jnp.stack([kk[1], v_cmp]).reshape(2, GROUPS, batch, ncp, chunk_w)
    pos = jnp.stack([cmp_pos_k[0], cmp_pos_v[0]]).astype(F32).reshape(2, 2, 1, chunk_w)
    w1 = jnp.stack([cmp_w1_k[0], cmp_w1_v[0]]).astype(BF16).reshape(2, 2, chunk_w, -1)
    w2 = jnp.stack([cmp_w2_k[0], cmp_w2_v[0]]).astype(BF16)
    cn, ct = _compress(chunks, pos, w1, w2, w2.transpose(0, 2, 1))
    oc_t, sel = _cmp_attention(h_t, cn, ct, vb_nsa, _overlap_t(seq), batch, seq)
    sel = sel.reshape(batch, GROUPS, nq // 2, 4, seq)
    y_b = _selwin_attention(h_t, kk, vsel_t, vwin_t, sel, oc_t, gates_t, vb_nsa, batch, seq)

    merged = _merge(x_bf, y_a, y_b, w_ga.astype(BF16), w_gb.astype(BF16),
                    w_branch_swa[0].astype(BF16), w_branch_nsa[0].astype(BF16), tm=min(1024, tokens), tn=512)
    h1 = _mix_ln(merged, x2, w_mix_out[0].astype(BF16), ln1_g, ln1_b, tm=512)

    mem_bf = mem.reshape(-1, d).astype(BF16)
    kv_mem = _matmul(mem_bf, xa_w_kv[0].astype(BF16), nt=False, tm=mem_bf.shape[0], tn=512)
    h2 = _cross_attention_ln(h1, kv_mem, xa_w_q[0].astype(BF16), xa_w_o[0].astype(BF16), ln2_g, ln2_b,
                             batch, seq, tm=512)

    h3 = _mlp_ln(h2, mlp_w1[0].astype(BF16), mlp_w2[0].astype(BF16), ln3_g, ln3_b, tm=512, tf=1024)
    return h3.reshape(batch, seq, d)
```

```python
import functools
import math

import numpy as np
import jax
import jax.numpy as jnp
from jax import lax
from jax.experimental import pallas as pl
from jax.experimental.pallas import tpu as pltpu

F32 = jnp.float32
BF16 = jnp.bfloat16

HEAD_DIM = 64
BLOCK_Q = 128
SWA_HEADS = 16
SWA_KV_HEADS = 2
SWA_WINDOW = 128
NSA_HEADS = 16
NSA_KV_HEADS = 2
CMP_BLOCK = 32
CMP_STRIDE = 16
SEL_BLOCK = 64
SEL_TOPK = 16
SEL_LOCAL = 2
NSA_WINDOW = 512
REL_BUCKETS = 32
REL_MAX_DIST = 4096
XA_HEADS = 4
XA_HEAD_DIM = 128
DEPTH = 1
DN_ALPHA = (2.0 * DEPTH) ** 0.25
LN_EPS = 1e-5
NEG_INF = -1e30
FORCE_SCORE = 1e4

GROUPS = 2
HEADS_PER_GROUP = 8
GROUP_LANES = HEADS_PER_GROUP * BLOCK_Q
GROUP_COLS = HEADS_PER_GROUP * HEAD_DIM

V7X_VMEM_LIMIT_BYTES = 56 * 1024 * 1024


def _params(n_axes):
    return pltpu.CompilerParams(dimension_semantics=("arbitrary",) * n_axes,
                                vmem_limit_bytes=V7X_VMEM_LIMIT_BYTES)


def _mm_kernel(a_ref, b_ref, o_ref, *, nt):
    if nt:
        out = lax.dot_general(a_ref[...], b_ref[...], (((1,), (1,)), ((), ())), preferred_element_type=F32)
    else:
        out = jnp.dot(a_ref[...], b_ref[...], preferred_element_type=F32)
    o_ref[...] = out.astype(o_ref.dtype)


def _matmul(a, b, *, nt, tm, tn, out_dtype=BF16):
    m, k = a.shape
    n = b.shape[0] if nt else b.shape[1]
    b_spec = pl.BlockSpec((tn, k), lambda i, j: (j, 0)) if nt else pl.BlockSpec((k, tn), lambda i, j: (0, j))
    return pl.pallas_call(
        functools.partial(_mm_kernel, nt=nt),
        grid=(m // tm, n // tn),
        in_specs=[pl.BlockSpec((tm, k), lambda i, j: (i, 0)), b_spec],
        out_specs=pl.BlockSpec((tm, tn), lambda i, j: (i, j)),
        out_shape=jax.ShapeDtypeStruct((m, n), out_dtype),
        compiler_params=_params(2),
        name="proj_nt" if nt else "proj_nn",
    )(a, b)


def _rel_bucket(dist):
    exact = REL_BUCKETS // 2
    d = jnp.maximum(dist, 0)
    log_ratio = jnp.log(jnp.maximum(d, 1).astype(F32) / exact) / math.log(REL_MAX_DIST / exact)
    large = jnp.minimum(exact + (log_ratio * (REL_BUCKETS - exact)).astype(jnp.int32), REL_BUCKETS - 1)
    return jnp.where(d < exact, d, large)


def _bias_by_distance(rel_table, seq, *, window=None, negative=0.0):
    dist = jnp.arange(seq)
    by_dist = rel_table[_rel_bucket(dist)].astype(F32).T
    if window is not None:
        by_dist = jnp.where(dist < window, by_dist, NEG_INF)
    padded = jnp.pad(by_dist, ((0, 0), (2 * BLOCK_Q, 0)), constant_values=negative)
    return padded.reshape(rel_table.shape[1], seq // BLOCK_Q + 2, BLOCK_Q)


def _build_skew_table(vb_ref, tbl_ref, n_tiles):
    def body(d, carry):
        for h in range(HEADS_PER_GROUP):
            lo = jnp.broadcast_to(vb_ref[h, pl.ds(d, 1), :], (BLOCK_Q, BLOCK_Q))
            hi = jnp.broadcast_to(vb_ref[h, pl.ds(d + 1, 1), :], (BLOCK_Q, BLOCK_Q))
            y = pltpu.roll(jnp.concatenate([lo, hi], axis=1), 0, 1, stride=1, stride_axis=0)
            tbl_ref[d, :, h * BLOCK_Q:(h + 1) * BLOCK_Q] = y[:, BLOCK_Q:].astype(tbl_ref.dtype)
        return carry

    lax.fori_loop(0, n_tiles, body, 0)


def _build_cmp_bias_table(vb_ref, tbl_ref, skew_ref, nq):
    per_tile = BLOCK_Q // CMP_STRIDE
    assert per_tile == 8 and CMP_BLOCK - 1 + CMP_STRIDE * (per_tile - 1) - BLOCK_Q == CMP_STRIDE - 1
    tbl_ref[pl.ds(nq * 8, (nq - 1) * 8), :] = jnp.zeros(((nq - 1) * 8, GROUP_LANES), F32)

    def body(d, last_row):
        for h in range(HEADS_PER_GROUP):
            lo = jnp.broadcast_to(vb_ref[h, pl.ds(d + 1, 1), :], (BLOCK_Q, BLOCK_Q))
            hi = jnp.broadcast_to(vb_ref[h, pl.ds(d + 2, 1), :], (BLOCK_Q, BLOCK_Q))
            y = pltpu.roll(jnp.concatenate([lo, hi], axis=1), 0, 1, stride=1, stride_axis=0)
            skew_ref[:, h * BLOCK_Q:(h + 1) * BLOCK_Q] = y[:, BLOCK_Q:]
        offsets = [CMP_BLOCK - 1 + CMP_STRIDE * c for c in range(per_tile - 1)]
        rows = [skew_ref[k:k + 1, :] for k in offsets] + [last_row]
        tbl_ref[pl.ds(pl.multiple_of((nq - 1 - d) * 8, 8), 8), :] = jnp.concatenate(rows, axis=0)
        return skew_ref[CMP_STRIDE - 1:CMP_STRIDE, :]

    lax.fori_loop(0, nq, body, jnp.zeros((1, GROUP_LANES), F32))


def _gather_heads_to_lanes(q_ref):
    q = jnp.concatenate([q_ref[h * HEAD_DIM:(h + 1) * HEAD_DIM, :] for h in range(HEADS_PER_GROUP)], axis=1)
    return q * jnp.asarray(HEAD_DIM ** -0.5, q.dtype)


def _heads_to_columns(o_t):
    stacked = jnp.concatenate([o_t[:, h * BLOCK_Q:(h + 1) * BLOCK_Q] for h in range(HEADS_PER_GROUP)], axis=0)
    return stacked.T


def _identity_tile():
    r = lax.broadcasted_iota(jnp.int32, (BLOCK_Q, BLOCK_Q), 0)
    c = lax.broadcasted_iota(jnp.int32, (BLOCK_Q, BLOCK_Q), 1)
    return jnp.where(r == c, 1.0, 0.0).astype(BF16)


def _biased_scores(eye, k_t, bias_tile, q_t):
    lhs = jnp.concatenate([eye, k_t], axis=1)
    rhs = jnp.concatenate([bias_tile, q_t], axis=0)
    return jnp.dot(lhs, rhs, preferred_element_type=F32)


def _softmax_probs(m, s_tiles):
    ms, alphas, ps = [], [], [[] for _ in s_tiles]
    for h in range(HEADS_PER_GROUP):
        lanes = slice(h * BLOCK_Q, (h + 1) * BLOCK_Q)
        cols = [s[:, lanes] for s in s_tiles]
        m_new = m[:, lanes]
        for c in cols:
            m_new = jnp.maximum(m_new, jnp.max(c, axis=0, keepdims=True))
        ms.append(m_new)
        alphas.append(jnp.exp(m[:, lanes] - m_new))
        for j, c in enumerate(cols):
            ps[j].append(jnp.exp((c - m_new).astype(BF16)))
    p = jnp.concatenate([jnp.concatenate(pj, axis=1) for pj in ps], axis=0)
    return jnp.concatenate(ms, axis=1), jnp.concatenate(alphas, axis=1), p


def _softmax_update(carry, s_tiles, v_aug):
    m, acc = carry
    m, alpha, p = _softmax_probs(m, s_tiles)
    return m, alpha * acc + jnp.dot(v_aug, p, preferred_element_type=F32)


def _pipelined_attention(n_steps, scores_fn, values_fn, init, s_scr, p_scr):
    m0, acc0 = init

    def put_scores(n):
        for j, s in enumerate(scores_fn(n)):
            s_scr[j] = s

    def fold(acc, alpha, n, p):
        return alpha * acc + jnp.dot(values_fn(n), p, preferred_element_type=F32)

    put_scores(0)
    p_scr[...] = jnp.zeros_like(p_scr)

    def body(k, carry):
        m, acc, alpha_prev = carry
        a = 2 * k
        acc = fold(acc, alpha_prev, a - 1, p_scr[...])
        s_b = scores_fn(a + 1)
        m, alpha_a, p_a = _softmax_probs(m, [s_scr.at[j] for j in range(s_scr.shape[0])])
        acc = fold(acc, alpha_a, a, p_a)
        put_scores(a + 2)
        m, alpha_b, p_b = _softmax_probs(m, s_b)
        p_scr[...] = p_b
        return m, acc, alpha_b

    trips = (n_steps + 1) // 2
    m, acc, alpha_last = lax.fori_loop(0, trips, body, (m0, acc0, jnp.ones((1, GROUP_LANES), F32)))
    return m, fold(acc, alpha_last, 2 * trips - 1, p_scr[...])


V_AUG_ROWS = HEAD_DIM + 16


def _normalized(acc):
    return acc[:HEAD_DIM] * (1.0 / acc[HEAD_DIM:HEAD_DIM + 1])


def _swa_kernel(q_ref, kprev_ref, kcur_ref, vprev_ref, vcur_ref, sink_ref, vb_ref, o_ref, tbl_ref):
    b, i = pl.program_id(1), pl.program_id(2)

    @pl.when((b == 0) & (i == 0))
    def _():
        _build_skew_table(vb_ref, tbl_ref, 3)

    q_t = _gather_heads_to_lanes(q_ref)
    eye = _identity_tile()
    acc0 = jnp.where(lax.broadcasted_iota(jnp.int32, (V_AUG_ROWS, GROUP_LANES), 0) < HEAD_DIM, 0.0, 1.0)
    s_cur = _biased_scores(eye, kcur_ref[...], tbl_ref[1], q_t)
    s_prev = _biased_scores(eye, kprev_ref[...], tbl_ref[jnp.where(i > 0, 2, 0)], q_t)
    v2 = jnp.concatenate([vcur_ref[...], vprev_ref[...]], axis=1)
    _, acc = _softmax_update((sink_ref[...], acc0), [s_cur, s_prev], v2)
    o_ref[...] = _heads_to_columns(_normalized(acc)).astype(o_ref.dtype)


def _swa_attention(h_t, kk, v_aug, sink_rows, vb, batch, seq):
    nq = seq // BLOCK_Q
    tokens = batch * seq
    return pl.pallas_call(
        _swa_kernel,
        grid=(GROUPS, batch, nq),
        in_specs=[
            pl.BlockSpec((GROUP_COLS, BLOCK_Q), lambda g, b, i: (g, b * nq + i)),
            pl.BlockSpec((None, None, BLOCK_Q, HEAD_DIM), lambda g, b, i: (0, g, b * nq + jnp.maximum(i - 1, 0), 0)),
            pl.BlockSpec((None, None, BLOCK_Q, HEAD_DIM), lambda g, b, i: (0, g, b * nq + i, 0)),
            pl.BlockSpec((None, None, None, V_AUG_ROWS, BLOCK_Q), lambda g, b, i: (g, b, jnp.maximum(i - 1, 0), 0, 0)),
            pl.BlockSpec((None, None, None, V_AUG_ROWS, BLOCK_Q), lambda g, b, i: (g, b, i, 0, 0)),
            pl.BlockSpec((None, 1, GROUP_LANES), lambda g, b, i: (g, 0, 0)),
            pl.BlockSpec((None, HEADS_PER_GROUP, 4, BLOCK_Q), lambda g, b, i: (g, 0, 0, 0)),
        ],
        out_specs=pl.BlockSpec((BLOCK_Q, GROUP_COLS), lambda g, b, i: (b * nq + i, g)),
        out_shape=jax.ShapeDtypeStruct((tokens, SWA_HEADS * HEAD_DIM), BF16),
        scratch_shapes=[pltpu.VMEM((3, BLOCK_Q, GROUP_LANES), BF16)],
        compiler_params=_params(3),
        name="swa_attention",
    )(h_t, kk, kk, v_aug, v_aug, sink_rows, vb)


def _compress_kernel(c_ref, pos_ref, w1_ref, w2_ref, w2t_ref, cn_ref, ct_ref):
    c = c_ref[...].astype(F32)
    top = (c + pos_ref[0]).astype(BF16)
    bot = (c + pos_ref[1]).astype(BF16)
    a = jnp.dot(top, w1_ref[0], preferred_element_type=F32)
    bm = jnp.dot(bot, w1_ref[1], preferred_element_type=F32)
    n = a.shape[0]
    pre = a + pltpu.roll(bm, n - 1, 0)
    hid = jax.nn.gelu(pre).astype(BF16)
    cn_ref[...] = jnp.dot(hid, w2_ref[...], preferred_element_type=F32).astype(cn_ref.dtype)
    ct_ref[...] = lax.dot_general(w2t_ref[...], hid, (((1,), (1,)), ((), ())),
                                  preferred_element_type=F32).astype(ct_ref.dtype)


def _compress(chunks, pos, w1, w2, w2t):
    _, g, b, ncp, width = chunks.shape
    hidden = w1.shape[-1]
    return pl.pallas_call(
        _compress_kernel,
        grid=(2, g, b),
        in_specs=[
            pl.BlockSpec((None, None, None, ncp, width), lambda t, g, b: (t, g, b, 0, 0)),
            pl.BlockSpec((None, 2, 1, width), lambda t, g, b: (t, 0, 0, 0)),
            pl.BlockSpec((None, 2, width, hidden), lambda t, g, b: (t, 0, 0, 0)),
            pl.BlockSpec((None, hidden, HEAD_DIM), lambda t, g, b: (t, 0, 0)),
            pl.BlockSpec((None, HEAD_DIM, hidden), lambda t, g, b: (t, 0, 0)),
        ],
        out_specs=[
            pl.BlockSpec((None, None, None, ncp, HEAD_DIM), lambda t, g, b: (t, g, b, 0, 0)),
            pl.BlockSpec((None, None, None, HEAD_DIM, ncp), lambda t, g, b: (t, g, b, 0, 0)),
        ],
        out_shape=[jax.ShapeDtypeStruct((2, g, b, ncp, HEAD_DIM), BF16),
                   jax.ShapeDtypeStruct((2, g, b, HEAD_DIM, ncp), BF16)],
        compiler_params=_params(3),
        name="nsa_compress",
    )(chunks, pos, w1, w2, w2t)


def _cmp_kernel(q_ref, kc_ref, vct_ref, vb_ref, ov_ref, oc_ref, sel_ref, bias_ref, skew_ref, *, nq):
    b, i = pl.program_id(1), pl.program_id(2)

    @pl.when((b == 0) & (i == 0))
    def _():
        _build_cmp_bias_table(vb_ref, bias_ref, skew_ref, nq)

    ncp = kc_ref.shape[0]
    nsel = sel_ref.shape[0]
    q_t = _gather_heads_to_lanes(q_ref)
    row0 = pl.multiple_of((nq - 1 - i) * 8, 8)
    s = jnp.dot(kc_ref[...], q_t, preferred_element_type=F32) + bias_ref[pl.ds(row0, ncp), :]
    c_io = lax.broadcasted_iota(jnp.int32, (ncp, GROUP_LANES), 0)
    pos = i * BLOCK_Q + (lax.broadcasted_iota(jnp.int32, (ncp, GROUP_LANES), 1) & (BLOCK_Q - 1))
    valid = pos - c_io * CMP_STRIDE - (CMP_BLOCK - 1) >= 0
    s = jnp.where(valid, s, NEG_INF)
    m = jnp.max(s, axis=0, keepdims=True)
    p = jnp.where(valid, jnp.exp(s - m), 0.0)
    l = jnp.sum(p, axis=0, keepdims=True)
    p = p * (1.0 / jnp.where(l > 0.0, l, 1.0))
    oc_ref[...] = jnp.dot(vct_ref[...], p.astype(BF16), preferred_element_type=F32).astype(oc_ref.dtype)

    psum = p[:, 0:BLOCK_Q]
    for h in range(1, HEADS_PER_GROUP):
        psum = psum + p[:, h * BLOCK_Q:(h + 1) * BLOCK_Q]
    hi = psum.astype(BF16)
    lo = (psum - hi.astype(F32)).astype(BF16)
    ov = ov_ref[...]
    score = jnp.dot(ov, hi, preferred_element_type=F32) + jnp.dot(ov, lo, preferred_element_type=F32)

    j_io = lax.broadcasted_iota(jnp.int32, (nsel, BLOCK_Q), 0)
    qpos = i * BLOCK_Q + lax.broadcasted_iota(jnp.int32, (nsel, BLOCK_Q), 1)
    causal = j_io * SEL_BLOCK <= qpos
    back = qpos // SEL_BLOCK - j_io
    forced = (j_io == 0) | ((back >= 0) & (back < SEL_LOCAL))
    score = jnp.where(causal, jnp.where(forced, FORCE_SCORE, score), -1.0)
    rank = jnp.zeros((nsel, BLOCK_Q), jnp.int32)
    for r in range(nsel):
        row = score[r:r + 1, :]
        ge = jnp.where(row >= score, 1, 0)
        gt = jnp.where(row > score, 1, 0)
        rank = rank + jnp.where(j_io > r, ge, gt)
    sel_ref[...] = ((rank < min(SEL_TOPK, nsel)) & causal).astype(sel_ref.dtype)


def _cmp_attention(h_t, cn, ct, vb, overlap_t, batch, seq):
    nq = seq // BLOCK_Q
    ncp = seq // CMP_STRIDE
    nsel = seq // SEL_BLOCK
    q_blk0 = SWA_HEADS * HEAD_DIM // GROUP_COLS
    return pl.pallas_call(
        functools.partial(_cmp_kernel, nq=nq),
        grid=(GROUPS, batch, nq),
        in_specs=[
            pl.BlockSpec((GROUP_COLS, BLOCK_Q), lambda g, b, i: (q_blk0 + g, b * nq + i)),
            pl.BlockSpec((None, None, None, ncp, HEAD_DIM), lambda g, b, i: (0, g, b, 0, 0)),
            pl.BlockSpec((None, None, None, HEAD_DIM, ncp), lambda g, b, i: (1, g, b, 0, 0)),
            pl.BlockSpec((None, HEADS_PER_GROUP, nq + 2, BLOCK_Q), lambda g, b, i: (g, 0, 0, 0)),
            pl.BlockSpec((nsel, ncp), lambda g, b, i: (0, 0)),
        ],
        out_specs=[
            pl.BlockSpec((None, None, None, HEAD_DIM, GROUP_LANES), lambda g, b, i: (b, g, i, 0, 0)),
            pl.BlockSpec((None, None, nsel, BLOCK_Q), lambda g, b, i: (b, g, 0, i)),
        ],
        out_shape=[jax.ShapeDtypeStruct((batch, GROUPS, nq, HEAD_DIM, GROUP_LANES), BF16),
                   jax.ShapeDtypeStruct((batch, GROUPS, nsel, seq), F32)],
        scratch_shapes=[pltpu.VMEM(((2 * nq - 1) * 8, GROUP_LANES), F32), pltpu.VMEM((BLOCK_Q, GROUP_LANES), F32)],
        compiler_params=_params(3),
        name="nsa_cmp_select",
    )(h_t, cn, ct, vb, overlap_t)


N_WIN_PAIRS = -(-(NSA_WINDOW - 1) // (2 * BLOCK_Q)) + 1
N_WIN_SLOTS = 2 * N_WIN_PAIRS + 1
SEL_ROWS = 16


def _selwin_kernel(q_ref, ksel_ref, kwin_ref, vsel_ref, vwin_ref, sel_ref, oc_ref, gate_ref, vbs_ref, vbw_ref,
                   o_ref, tsel_ref, twin_ref, s_scr, p_scr, *, nq):
    b, i = pl.program_id(1), pl.program_id(2)

    @pl.when((b == 0) & (i == 0))
    def _():
        _build_skew_table(vbs_ref, tsel_ref, nq + 1)
        _build_skew_table(vbw_ref, twin_ref, N_WIN_SLOTS)

    q_t = _gather_heads_to_lanes(q_ref)
    eye = _identity_tile()
    pair = 2 * BLOCK_Q
    init = (jnp.full((1, GROUP_LANES), NEG_INF, F32), jnp.zeros((V_AUG_ROWS, GROUP_LANES), F32))
    p_diag = i // 2
    blocks_per_tile = BLOCK_Q // SEL_BLOCK

    def pair_of_step(n):
        return jnp.clip(p_diag - n, 0, p_diag)

    def table_slots(n, n_steps):
        slot = i - 2 * pair_of_step(n) + 1
        live = (n >= 0) & (n < n_steps)
        return [jnp.where(live, slot - j, 0) for j in range(2)]

    def sel_scores(n):
        p = pair_of_step(n)
        slots = table_slots(n, p_diag + 1)
        neg = jnp.where(sel_ref[p] > 0.5, 0.0, NEG_INF)
        s_tiles = []
        for j in range(2):
            k_t = ksel_ref[pl.ds(pl.multiple_of(p * pair + j * BLOCK_Q, BLOCK_Q), BLOCK_Q), :]
            rows = neg[j * blocks_per_tile:(j + 1) * blocks_per_tile]
            rows = jnp.concatenate([jnp.concatenate([rows] * HEADS_PER_GROUP, axis=1),
                                    jnp.zeros((SEL_ROWS - blocks_per_tile, GROUP_LANES), F32)], axis=0)
            q_ext = jnp.concatenate([q_t, rows.astype(BF16)], axis=0)
            s_tiles.append(_biased_scores(eye, k_t, tsel_ref[slots[j]], q_ext))
        return s_tiles

    _, acc_s = _pipelined_attention(p_diag + 1, sel_scores, lambda n: vsel_ref[pair_of_step(n)], init,
                                    s_scr, p_scr)

    n_win = jnp.minimum(p_diag + 1, N_WIN_PAIRS)

    def win_scores(n):
        p = pair_of_step(n)
        slots = table_slots(n, n_win)
        s_tiles = []
        for j in range(2):
            k_t = kwin_ref[pl.ds(pl.multiple_of(p * pair + j * BLOCK_Q, BLOCK_Q), BLOCK_Q), :]
            s_tiles.append(_biased_scores(eye, k_t, twin_ref[slots[j]], q_t))
        return s_tiles

    _, acc_w = _pipelined_attention(n_win, win_scores, lambda n: vwin_ref[pair_of_step(n)], init, s_scr, p_scr)

    def gate_row(branch):
        g = jnp.concatenate([gate_ref[branch, h:h + 1, :] for h in range(HEADS_PER_GROUP)], axis=1)
        return jax.nn.sigmoid(g.astype(F32))

    out = (gate_row(0) * oc_ref[...].astype(F32) + gate_row(1) * _normalized(acc_s)
           + gate_row(2) * _normalized(acc_w))
    o_ref[...] = _heads_to_columns(out).astype(o_ref.dtype)


def _selwin_attention(h_t, ksel_aug, kk, vsel_aug, vwin_aug, sel, oc_t, gates_t, vb_sel, vb_win, batch, seq):
    nq = seq // BLOCK_Q
    tokens = batch * seq
    q_blk0 = SWA_HEADS * HEAD_DIM // GROUP_COLS
    return pl.pallas_call(
        functools.partial(_selwin_kernel, nq=nq),
        grid=(GROUPS, batch, nq),
        in_specs=[
            pl.BlockSpec((GROUP_COLS, BLOCK_Q), lambda g, b, i: (q_blk0 + g, b * nq + i)),
            pl.BlockSpec((None, seq, HEAD_DIM + SEL_ROWS), lambda g, b, i: (g, b, 0)),
            pl.BlockSpec((None, None, seq, HEAD_DIM), lambda g, b, i: (3, g, b, 0)),
            pl.BlockSpec((None, None, nq // 2, V_AUG_ROWS, 2 * BLOCK_Q), lambda g, b, i: (g, b, 0, 0, 0)),
            pl.BlockSpec((None, None, nq // 2, V_AUG_ROWS, 2 * BLOCK_Q), lambda g, b, i: (g, b, 0, 0, 0)),
            pl.BlockSpec((None, None, nq // 2, 4, BLOCK_Q), lambda g, b, i: (b, g, 0, 0, i)),
            pl.BlockSpec((None, None, None, HEAD_DIM, GROUP_LANES), lambda g, b, i: (b, g, i, 0, 0)),
            pl.BlockSpec((None, 3, HEADS_PER_GROUP, BLOCK_Q), lambda g, b, i: (g, 0, 0, b * nq + i)),
            pl.BlockSpec((None, HEADS_PER_GROUP, nq + 2, BLOCK_Q), lambda g, b, i: (g, 0, 0, 0)),
            pl.BlockSpec((None, HEADS_PER_GROUP, N_WIN_SLOTS + 1, BLOCK_Q), lambda g, b, i: (g, 0, 0, 0)),
        ],
        out_specs=pl.BlockSpec((BLOCK_Q, GROUP_COLS), lambda g, b, i: (b * nq + i, g)),
        out_shape=jax.ShapeDtypeStruct((tokens, NSA_HEADS * HEAD_DIM), BF16),
        scratch_shapes=[pltpu.VMEM((nq + 1, BLOCK_Q, GROUP_LANES), BF16),
                        pltpu.VMEM((N_WIN_SLOTS, BLOCK_Q, GROUP_LANES), BF16),
                        pltpu.VMEM((2, BLOCK_Q, GROUP_LANES), F32),
                        pltpu.VMEM((2 * BLOCK_Q, GROUP_LANES), BF16)],
        compiler_params=_params(3),
        name="nsa_sel_win",
    )(h_t, ksel_aug, kk, vsel_aug, vwin_aug, sel, oc_t, gates_t, vb_sel, vb_win)


def _layer_norm(y, g_ref, b_ref):
    mu = jnp.mean(y, axis=-1, keepdims=True)
    yc = y - mu
    var = jnp.mean(yc * yc, axis=-1, keepdims=True)
    return yc * lax.rsqrt(var + LN_EPS) * g_ref[...] + b_ref[...]


def _merge_kernel(x_ref, ya_ref, yb_ref, wga_ref, wgb_ref, wa_ref, wb_ref, o_ref):
    x = x_ref[...]
    ga = jax.nn.sigmoid(jnp.dot(x, wga_ref[...], preferred_element_type=F32))
    gb = jax.nn.sigmoid(jnp.dot(x, wgb_ref[...], preferred_element_type=F32))
    a = jnp.dot(ya_ref[...], wa_ref[...], preferred_element_type=F32)
    bb = jnp.dot(yb_ref[...], wb_ref[...], preferred_element_type=F32)
    o_ref[...] = (ga * a + gb * bb).astype(o_ref.dtype)


def _merge(x_bf, y_a, y_b, w_ga, w_gb, w_a, w_b, *, tm, tn):
    tokens, d = x_bf.shape
    ya_cols, yb_cols = y_a.shape[1], y_b.shape[1]
    return pl.pallas_call(
        _merge_kernel,
        grid=(tokens // tm, d // tn),
        in_specs=[
            pl.BlockSpec((tm, d), lambda i, j: (i, 0)),
            pl.BlockSpec((tm, ya_cols), lambda i, j: (i, 0)),
            pl.BlockSpec((tm, yb_cols), lambda i, j: (i, 0)),
            pl.BlockSpec((d, tn), lambda i, j: (0, j)),
            pl.BlockSpec((d, tn), lambda i, j: (0, j)),
            pl.BlockSpec((ya_cols, tn), lambda i, j: (0, j)),
            pl.BlockSpec((yb_cols, tn), lambda i, j: (0, j)),
        ],
        out_specs=pl.BlockSpec((tm, tn), lambda i, j: (i, j)),
        out_shape=jax.ShapeDtypeStruct((tokens, d), BF16),
        compiler_params=_params(2),
        name="branch_merge",
    )(x_bf, y_a, y_b, w_ga, w_gb, w_a, w_b)


def _mix_ln_kernel(m_ref, x_ref, w_ref, g_ref, b_ref, o_ref):
    mix = jnp.dot(m_ref[...], w_ref[...], preferred_element_type=F32)
    o_ref[...] = _layer_norm(DN_ALPHA * x_ref[...] + mix, g_ref, b_ref)


def _mix_ln(merged, x, w_mix, ln_g, ln_b, *, tm):
    tokens, d = x.shape
    return pl.pallas_call(
        _mix_ln_kernel,
        grid=(tokens // tm,),
        in_specs=[
            pl.BlockSpec((tm, d), lambda i: (i, 0)),
            pl.BlockSpec((tm, d), lambda i: (i, 0)),
            pl.BlockSpec((d, d), lambda i: (0, 0)),
            pl.BlockSpec((1, d), lambda i: (0, 0)),
            pl.BlockSpec((1, d), lambda i: (0, 0)),
        ],
        out_specs=pl.BlockSpec((tm, d), lambda i: (i, 0)),
        out_shape=jax.ShapeDtypeStruct((tokens, d), F32),
        compiler_params=_params(1),
        name="mix_out_ln1",
    )(merged, x, w_mix, ln_g, ln_b)


def _xa_kernel(h_ref, wq_ref, k_ref, v_ref, wo_ref, g_ref, b_ref, o_ref):
    h = h_ref[...]
    q = jnp.dot(h.astype(BF16), wq_ref[...], preferred_element_type=F32) * (XA_HEAD_DIM ** -0.5)
    q = q.astype(BF16)
    outs = []
    for hd in range(XA_HEADS):
        cols = slice(hd * XA_HEAD_DIM, (hd + 1) * XA_HEAD_DIM)
        s = lax.dot_general(q[:, cols], k_ref[:, cols], (((1,), (1,)), ((), ())), preferred_element_type=F32)
        p = jnp.exp(s - jnp.max(s, axis=-1, keepdims=True))
        l = jnp.sum(p, axis=-1, keepdims=True)
        o = jnp.dot(p.astype(BF16), v_ref[:, cols], preferred_element_type=F32)
        outs.append(o * (1.0 / l))
    o = jnp.concatenate(outs, axis=1).astype(BF16)
    xa = jnp.dot(o, wo_ref[...], preferred_element_type=F32)
    o_ref[...] = _layer_norm(DN_ALPHA * h + xa, g_ref, b_ref)


def _cross_attention_ln(h, kv_mem, w_q, w_o, ln_g, ln_b, batch, seq, *, tm):
    tokens, d = h.shape
    mem_len = kv_mem.shape[0] // batch
    xa_dim = XA_HEADS * XA_HEAD_DIM
    nt = seq // tm
    return pl.pallas_call(
        _xa_kernel,
        grid=(batch, nt),
        in_specs=[
            pl.BlockSpec((tm, d), lambda b, i: (b * nt + i, 0)),
            pl.BlockSpec((d, xa_dim), lambda b, i: (0, 0)),
            pl.BlockSpec((mem_len, xa_dim), lambda b, i: (b, 0)),
            pl.BlockSpec((mem_len, xa_dim), lambda b, i: (b, 1)),
            pl.BlockSpec((xa_dim, d), lambda b, i: (0, 0)),
            pl.BlockSpec((1, d), lambda b, i: (0, 0)),
            pl.BlockSpec((1, d), lambda b, i: (0, 0)),
        ],
        out_specs=pl.BlockSpec((tm, d), lambda b, i: (b * nt + i, 0)),
        out_shape=jax.ShapeDtypeStruct((tokens, d), F32),
        compiler_params=_params(2),
        name="cross_attention_ln2",
    )(h, w_q, kv_mem, kv_mem, w_o, ln_g, ln_b)


def _mlp_kernel(h_ref, w1_ref, w2_ref, g_ref, b_ref, o_ref, hb_ref):
    j = pl.program_id(1)

    @pl.when(j == 0)
    def _():
        hb_ref[...] = h_ref[...].astype(BF16)
        o_ref[...] = jnp.zeros_like(o_ref)

    u = jnp.dot(hb_ref[...], w1_ref[...], preferred_element_type=F32)
    u = jnp.square(jnp.maximum(u, 0.0)).astype(BF16)
    o_ref[...] += jnp.dot(u, w2_ref[...], preferred_element_type=F32)

    @pl.when(j == pl.num_programs(1) - 1)
    def _():
        o_ref[...] = _layer_norm(DN_ALPHA * h_ref[...] + o_ref[...], g_ref, b_ref)


def _mlp_ln(h, w1, w2, ln_g, ln_b, *, tm, tf):
    tokens, d = h.shape
    d_ff = w1.shape[1]
    return pl.pallas_call(
        _mlp_kernel,
        grid=(tokens // tm, d_ff // tf),
        in_specs=[
            pl.BlockSpec((tm, d), lambda i, j: (i, 0)),
            pl.BlockSpec((d, tf), lambda i, j: (0, j)),
            pl.BlockSpec((tf, d), lambda i, j: (j, 0)),
            pl.BlockSpec((1, d), lambda i, j: (0, 0)),
            pl.BlockSpec((1, d), lambda i, j: (0, 0)),
        ],
        out_specs=pl.BlockSpec((tm, d), lambda i, j: (i, 0)),
        out_shape=jax.ShapeDtypeStruct((tokens, d), F32),
        scratch_shapes=[pltpu.VMEM((tm, d), BF16)],
        compiler_params=_params(2),
        name="mlp_ln3",
    )(h, w1, w2, ln_g, ln_b)


def _overlap_t(seq):
    ncp, nsel = seq // CMP_STRIDE, seq // SEL_BLOCK
    c_start = np.arange(ncp)[None, :] * CMP_STRIDE
    s_start = np.arange(nsel)[:, None] * SEL_BLOCK
    ov = (c_start < s_start + SEL_BLOCK) & (c_start + CMP_BLOCK > s_start) & (np.arange(ncp)[None, :] < ncp - 1)
    return jnp.asarray(ov, BF16)


def kernel(x, mem, w_in, attn_sinks, rel_bias_table, cmp_pos_k, cmp_w1_k, cmp_w2_k, cmp_pos_v, cmp_w1_v, cmp_w2_v,
           w_branch_swa, w_branch_nsa, w_mix_out, ln1_g, ln1_b, xa_w_q, xa_w_kv, xa_w_o, ln2_g, ln2_b,
           mlp_w1, mlp_w2, ln3_g, ln3_b):
    batch, seq, d = x.shape
    tokens = batch * seq
    nq = seq // BLOCK_Q
    assert w_in.shape[0] == 1, "one layer"
    assert seq % (8 * BLOCK_Q) == 0 and d % 512 == 0

    w = w_in[0]
    sizes = (SWA_HEADS * HEAD_DIM, 128, 128, NSA_HEADS * HEAD_DIM, 128, 128, 128, 128, 128, 128, 3 * NSA_HEADS, d, d)
    offs = np.concatenate([[0], np.cumsum(sizes)])
    (w_qa, w_ka, w_va, w_qb, w_kc, w_vc, w_ks, w_vs, w_kw, w_vw, w_gn, w_ga, w_gb) = [
        w[:, offs[n]:offs[n + 1]] for n in range(len(sizes))]
    w_rows = jnp.concatenate([w_ka, w_kc, w_ks, w_kw, w_vc], axis=1).astype(BF16)
    w_gn_pad = jnp.pad(w_gn, ((0, 0), (0, 128 - w_gn.shape[1])))
    w_cols_t = jnp.concatenate([w_qa, w_qb, w_va, w_vs, w_vw, w_gn_pad], axis=1).T.astype(BF16)

    x2 = x.reshape(tokens, d)
    x_bf = x2.astype(BF16)

    h_rows = _matmul(x_bf, w_rows, nt=False, tm=min(2048, tokens), tn=w_rows.shape[1])
    h_t = _matmul(w_cols_t, x_bf, nt=True, tm=1280, tn=min(1024, tokens))

    kk = h_rows[:, :512].reshape(tokens, 4, GROUPS, HEAD_DIM).transpose(1, 2, 0, 3)
    v_cmp = h_rows[:, 512:].reshape(tokens, GROUPS, HEAD_DIM).transpose(1, 0, 2)

    def value_tiles(row0, width):
        v = h_t[row0:row0 + 128].reshape(GROUPS, HEAD_DIM, batch, seq // width, width).transpose(0, 2, 3, 1, 4)
        ones = jnp.ones(v.shape[:3] + (V_AUG_ROWS - HEAD_DIM, width), BF16)
        return jnp.concatenate([v, ones], axis=3)

    vswa_aug = value_tiles(2048, BLOCK_Q)
    vsel_aug, vwin_aug = value_tiles(2176, 2 * BLOCK_Q), value_tiles(2304, 2 * BLOCK_Q)
    gates_t = h_t[2432:2432 + 3 * NSA_HEADS].reshape(GROUPS, HEADS_PER_GROUP, 3, tokens).transpose(0, 2, 1, 3)
    block_of_key = (jnp.arange(tokens) % BLOCK_Q) // SEL_BLOCK
    onehot = (block_of_key[:, None] == jnp.arange(SEL_ROWS)[None, :]).astype(BF16)
    ksel_aug = jnp.concatenate([kk[2], jnp.broadcast_to(onehot, (GROUPS, tokens, SEL_ROWS))], axis=2)

    def grouped(vb, rows):
        return vb[:, :rows].reshape(GROUPS, HEADS_PER_GROUP, rows, BLOCK_Q)

    rel_swa, rel_nsa = rel_bias_table[:, :SWA_HEADS], rel_bias_table[:, SWA_HEADS:]
    vb_swa = grouped(_bias_by_distance(rel_swa, seq, window=SWA_WINDOW, negative=NEG_INF), 4)
    vb_sel = grouped(_bias_by_distance(rel_nsa, seq, negative=NEG_INF), nq + 2)
    vb_win = grouped(_bias_by_distance(rel_nsa, seq, window=NSA_WINDOW, negative=NEG_INF), N_WIN_SLOTS + 1)
    vb_cmp = grouped(_bias_by_distance(rel_nsa, seq), nq + 2)
    sink_rows = jnp.repeat(attn_sinks[0].astype(F32), BLOCK_Q).reshape(GROUPS, 1, GROUP_LANES)

    y_a = _swa_attention(h_t, kk, vswa_aug, sink_rows, vb_swa, batch, seq)

    ncp = seq // CMP_STRIDE
    chunk_w = CMP_STRIDE * HEAD_DIM
    chunks = jnp.stack([kk[1], v_cmp]).reshape(2, GROUPS, batch, ncp, chunk_w)
    pos = jnp.stack([cmp_pos_k[0], cmp_pos_v[0]]).astype(F32).reshape(2, 2, 1, chunk_w)
    w1 = jnp.stack([cmp_w1_k[0], cmp_w1_v[0]]).astype(BF16).reshape(2, 2, chunk_w, -1)
    w2 = jnp.stack([cmp_w2_k[0], cmp_w2_v[0]]).astype(BF16)
    cn, ct = _compress(chunks, pos, w1, w2, w2.transpose(0, 2, 1))
    oc_t, sel = _cmp_attention(h_t, cn, ct, vb_cmp, _overlap_t(seq), batch, seq)
    sel = sel.reshape(batch, GROUPS, nq // 2, 4, seq)
    y_b = _selwin_attention(h_t, ksel_aug, kk, vsel_aug, vwin_aug, sel, oc_t, gates_t, vb_sel, vb_win, batch, seq)

    merged = _merge(x_bf, y_a, y_b, w_ga.astype(BF16), w_gb.astype(BF16),
                    w_branch_swa[0].astype(BF16), w_branch_nsa[0].astype(BF16), tm=min(1024, tokens), tn=512)
    h1 = _mix_ln(merged, x2, w_mix_out[0].astype(BF16), ln1_g, ln1_b, tm=512)

    mem_bf = mem.reshape(-1, d).astype(BF16)
    kv_mem = _matmul(mem_bf, xa_w_kv[0].astype(BF16), nt=False, tm=mem_bf.shape[0], tn=512)
    h2 = _cross_attention_ln(h1, kv_mem, xa_w_q[0].astype(BF16), xa_w_o[0].astype(BF16), ln2_g, ln2_b,
                             batch, seq, tm=512)

    h3 = _mlp_ln(h2, mlp_w1[0].astype(BF16), mlp_w2[0].astype(BF16), ln3_g, ln3_b, tm=512, tf=1024)
    return h3.reshape(batch, seq, d)
```

```python
import functools
import math

import numpy as np
import jax
import jax.numpy as jnp
from jax import lax
from jax.experimental import pallas as pl
from jax.experimental.pallas import tpu as pltpu

F32 = jnp.float32
BF16 = jnp.bfloat16

HEAD_DIM = 64
BLOCK_Q = 128
SWA_HEADS = 16
SWA_KV_HEADS = 2
SWA_WINDOW = 128
NSA_HEADS = 16
NSA_KV_HEADS = 2
CMP_BLOCK = 32
CMP_STRIDE = 16
SEL_BLOCK = 64
SEL_TOPK = 16
SEL_LOCAL = 2
NSA_WINDOW = 512
REL_BUCKETS = 32
REL_MAX_DIST = 4096
XA_HEADS = 4
XA_HEAD_DIM = 128
DEPTH = 1
DN_ALPHA = (2.0 * DEPTH) ** 0.25
LN_EPS = 1e-5
NEG_INF = -1e30
FORCE_SCORE = 1e4

GROUPS = 2
HEADS_PER_GROUP = 8
GROUP_LANES = HEADS_PER_GROUP * BLOCK_Q
GROUP_COLS = HEADS_PER_GROUP * HEAD_DIM

V7X_VMEM_LIMIT_BYTES = 56 * 1024 * 1024


def _params(n_axes):
    return pltpu.CompilerParams(dimension_semantics=("arbitrary",) * n_axes,
                                vmem_limit_bytes=V7X_VMEM_LIMIT_BYTES)


def _mm_kernel(a_ref, b_ref, o_ref, *, nt):
    if nt:
        out = lax.dot_general(a_ref[...], b_ref[...], (((1,), (1,)), ((), ())), preferred_element_type=F32)
    else:
        out = jnp.dot(a_ref[...], b_ref[...], preferred_element_type=F32)
    o_ref[...] = out.astype(o_ref.dtype)


def _matmul(a, b, *, nt, tm, tn, out_dtype=BF16):
    m, k = a.shape
    n = b.shape[0] if nt else b.shape[1]
    b_spec = pl.BlockSpec((tn, k), lambda i, j: (j, 0)) if nt else pl.BlockSpec((k, tn), lambda i, j: (0, j))
    return pl.pallas_call(
        functools.partial(_mm_kernel, nt=nt),
        grid=(m // tm, n // tn),
        in_specs=[pl.BlockSpec((tm, k), lambda i, j: (i, 0)), b_spec],
        out_specs=pl.BlockSpec((tm, tn), lambda i, j: (i, j)),
        out_shape=jax.ShapeDtypeStruct((m, n), out_dtype),
        compiler_params=_params(2),
        name="proj_nt" if nt else "proj_nn",
    )(a, b)


def _rel_bucket(dist):
    exact = REL_BUCKETS // 2
    d = jnp.maximum(dist, 0)
    log_ratio = jnp.log(jnp.maximum(d, 1).astype(F32) / exact) / math.log(REL_MAX_DIST / exact)
    large = jnp.minimum(exact + (log_ratio * (REL_BUCKETS - exact)).astype(jnp.int32), REL_BUCKETS - 1)
    return jnp.where(d < exact, d, large)


def _bias_by_distance(rel_table, seq, *, window=None, negative=0.0):
    dist = jnp.arange(seq)
    by_dist = rel_table[_rel_bucket(dist)].astype(F32).T
    if window is not None:
        by_dist = jnp.where(dist < window, by_dist, NEG_INF)
    padded = jnp.pad(by_dist, ((0, 0), (2 * BLOCK_Q, 0)), constant_values=negative)
    return padded.reshape(rel_table.shape[1], seq // BLOCK_Q + 2, BLOCK_Q)


def _build_skew_table(vb_ref, tbl_ref, n_tiles):
    def body(d, carry):
        for h in range(HEADS_PER_GROUP):
            lo = jnp.broadcast_to(vb_ref[h, pl.ds(d, 1), :], (BLOCK_Q, BLOCK_Q))
            hi = jnp.broadcast_to(vb_ref[h, pl.ds(d + 1, 1), :], (BLOCK_Q, BLOCK_Q))
            y = pltpu.roll(jnp.concatenate([lo, hi], axis=1), 0, 1, stride=1, stride_axis=0)
            tbl_ref[d, :, h * BLOCK_Q:(h + 1) * BLOCK_Q] = y[:, BLOCK_Q:].astype(tbl_ref.dtype)
        return carry

    lax.fori_loop(0, n_tiles, body, 0)


def _build_cmp_bias_table(vb_ref, tbl_ref, skew_ref, nq):
    per_tile = BLOCK_Q // CMP_STRIDE
    assert per_tile == 8 and CMP_BLOCK - 1 + CMP_STRIDE * (per_tile - 1) - BLOCK_Q == CMP_STRIDE - 1
    tbl_ref[pl.ds(nq * 8, (nq - 1) * 8), :] = jnp.zeros(((nq - 1) * 8, GROUP_LANES), F32)

    def body(d, last_row):
        for h in range(HEADS_PER_GROUP):
            lo = jnp.broadcast_to(vb_ref[h, pl.ds(d + 1, 1), :], (BLOCK_Q, BLOCK_Q))
            hi = jnp.broadcast_to(vb_ref[h, pl.ds(d + 2, 1), :], (BLOCK_Q, BLOCK_Q))
            y = pltpu.roll(jnp.concatenate([lo, hi], axis=1), 0, 1, stride=1, stride_axis=0)
            skew_ref[:, h * BLOCK_Q:(h + 1) * BLOCK_Q] = y[:, BLOCK_Q:]
        offsets = [CMP_BLOCK - 1 + CMP_STRIDE * c for c in range(per_tile - 1)]
        rows = [skew_ref[k:k + 1, :] for k in offsets] + [last_row]
        tbl_ref[pl.ds(pl.multiple_of((nq - 1 - d) * 8, 8), 8), :] = jnp.concatenate(rows, axis=0)
        return skew_ref[CMP_STRIDE - 1:CMP_STRIDE, :]

    lax.fori_loop(0, nq, body, jnp.zeros((1, GROUP_LANES), F32))


def _gather_heads_to_lanes(q_ref):
    q = jnp.concatenate([q_ref[h * HEAD_DIM:(h + 1) * HEAD_DIM, :] for h in range(HEADS_PER_GROUP)], axis=1)
    return q * jnp.asarray(HEAD_DIM ** -0.5, q.dtype)


def _heads_to_columns(o_t):
    stacked = jnp.concatenate([o_t[:, h * BLOCK_Q:(h + 1) * BLOCK_Q] for h in range(HEADS_PER_GROUP)], axis=0)
    return stacked.T


def _identity_tile():
    r = lax.broadcasted_iota(jnp.int32, (BLOCK_Q, BLOCK_Q), 0)
    c = lax.broadcasted_iota(jnp.int32, (BLOCK_Q, BLOCK_Q), 1)
    return jnp.where(r == c, 1.0, 0.0).astype(BF16)


def _biased_scores(eye, k_t, bias_tile, q_t):
    lhs = jnp.concatenate([eye, k_t], axis=1)
    rhs = jnp.concatenate([bias_tile, q_t], axis=0)
    return jnp.dot(lhs, rhs, preferred_element_type=F32)


def _softmax_probs(m, s_tiles):
    ms, alphas, ps = [], [], [[] for _ in s_tiles]
    for h in range(HEADS_PER_GROUP):
        lanes = slice(h * BLOCK_Q, (h + 1) * BLOCK_Q)
        cols = [s[:, lanes] for s in s_tiles]
        m_new = m[:, lanes]
        for c in cols:
            m_new = jnp.maximum(m_new, jnp.max(c, axis=0, keepdims=True))
        ms.append(m_new)
        alphas.append(jnp.exp(m[:, lanes] - m_new))
        for j, c in enumerate(cols):
            ps[j].append(jnp.exp((c - m_new).astype(BF16)))
    p = jnp.concatenate([jnp.concatenate(pj, axis=1) for pj in ps], axis=0)
    return jnp.concatenate(ms, axis=1), jnp.concatenate(alphas, axis=1), p


def _softmax_update(carry, s_tiles, v_aug):
    m, acc = carry
    m, alpha, p = _softmax_probs(m, s_tiles)
    return m, alpha * acc + jnp.dot(v_aug, p, preferred_element_type=F32)


def _pipelined_attention(n_steps, scores_fn, values_fn, init, s_scr, p_scr):
    m0, acc0 = init

    def put_scores(n):
        for j, s in enumerate(scores_fn(n)):
            s_scr[j] = s

    def fold(acc, alpha, n, p):
        return alpha * acc + jnp.dot(values_fn(n), p, preferred_element_type=F32)

    put_scores(0)
    p_scr[...] = jnp.zeros_like(p_scr)

    def body(k, carry):
        m, acc, alpha_prev = carry
        a = 2 * k
        acc = fold(acc, alpha_prev, a - 1, p_scr[...])
        s_b = scores_fn(a + 1)
        m, alpha_a, p_a = _softmax_probs(m, [s_scr.at[j] for j in range(s_scr.shape[0])])
        acc = fold(acc, alpha_a, a, p_a)
        put_scores(a + 2)
        m, alpha_b, p_b = _softmax_probs(m, s_b)
        p_scr[...] = p_b
        return m, acc, alpha_b

    trips = (n_steps + 1) // 2
    m, acc, alpha_last = lax.fori_loop(0, trips, body, (m0, acc0, jnp.ones((1, GROUP_LANES), F32)))
    return m, fold(acc, alpha_last, 2 * trips - 1, p_scr[...])


V_AUG_ROWS = HEAD_DIM + 16


def _normalized(acc):
    return acc[:HEAD_DIM] * (1.0 / acc[HEAD_DIM:HEAD_DIM + 1])


def _swa_kernel(q_ref, kprev_ref, kcur_ref, vprev_ref, vcur_ref, sink_ref, vb_ref, o_ref, tbl_ref):
    b, i = pl.program_id(1), pl.program_id(2)

    @pl.when((b == 0) & (i == 0))
    def _():
        _build_skew_table(vb_ref, tbl_ref, 3)

    q_t = _gather_heads_to_lanes(q_ref)
    eye = _identity_tile()
    acc0 = jnp.where(lax.broadcasted_iota(jnp.int32, (V_AUG_ROWS, GROUP_LANES), 0) < HEAD_DIM, 0.0, 1.0)
    s_cur = _biased_scores(eye, kcur_ref[...], tbl_ref[1], q_t)
    s_prev = _biased_scores(eye, kprev_ref[...], tbl_ref[jnp.where(i > 0, 2, 0)], q_t)
    v2 = jnp.concatenate([vcur_ref[...], vprev_ref[...]], axis=1)
    _, acc = _softmax_update((sink_ref[...], acc0), [s_cur, s_prev], v2)
    o_ref[...] = _heads_to_columns(_normalized(acc)).astype(o_ref.dtype)


def _swa_attention(h_t, kk, v_aug, sink_rows, vb, batch, seq):
    nq = seq // BLOCK_Q
    tokens = batch * seq
    return pl.pallas_call(
        _swa_kernel,
        grid=(GROUPS, batch, nq),
        in_specs=[
            pl.BlockSpec((GROUP_COLS, BLOCK_Q), lambda g, b, i: (g, b * nq + i)),
            pl.BlockSpec((None, None, BLOCK_Q, HEAD_DIM), lambda g, b, i: (0, g, b * nq + jnp.maximum(i - 1, 0), 0)),
            pl.BlockSpec((None, None, BLOCK_Q, HEAD_DIM), lambda g, b, i: (0, g, b * nq + i, 0)),
            pl.BlockSpec((None, None, None, V_AUG_ROWS, BLOCK_Q), lambda g, b, i: (g, b, jnp.maximum(i - 1, 0), 0, 0)),
            pl.BlockSpec((None, None, None, V_AUG_ROWS, BLOCK_Q), lambda g, b, i: (g, b, i, 0, 0)),
            pl.BlockSpec((None, 1, GROUP_LANES), lambda g, b, i: (g, 0, 0)),
            pl.BlockSpec((None, HEADS_PER_GROUP, 4, BLOCK_Q), lambda g, b, i: (g, 0, 0, 0)),
        ],
        out_specs=pl.BlockSpec((BLOCK_Q, GROUP_COLS), lambda g, b, i: (b * nq + i, g)),
        out_shape=jax.ShapeDtypeStruct((tokens, SWA_HEADS * HEAD_DIM), BF16),
        scratch_shapes=[pltpu.VMEM((3, BLOCK_Q, GROUP_LANES), BF16)],
        compiler_params=_params(3),
        name="swa_attention",
    )(h_t, kk, kk, v_aug, v_aug, sink_rows, vb)


def _compress_kernel(c_ref, pos_ref, w1_ref, w2_ref, w2t_ref, cn_ref, ct_ref):
    c = c_ref[...].astype(F32)
    top = (c + pos_ref[0]).astype(BF16)
    bot = (c + pos_ref[1]).astype(BF16)
    a = jnp.dot(top, w1_ref[0], preferred_element_type=F32)
    bm = jnp.dot(bot, w1_ref[1], preferred_element_type=F32)
    n = a.shape[0]
    pre = a + pltpu.roll(bm, n - 1, 0)
    hid = jax.nn.gelu(pre).astype(BF16)
    cn_ref[...] = jnp.dot(hid, w2_ref[...], preferred_element_type=F32).astype(cn_ref.dtype)
    ct_ref[...] = lax.dot_general(w2t_ref[...], hid, (((1,), (1,)), ((), ())),
                                  preferred_element_type=F32).astype(ct_ref.dtype)


def _compress(chunks, pos, w1, w2, w2t):
    _, g, b, ncp, width = chunks.shape
    hidden = w1.shape[-1]
    return pl.pallas_call(
        _compress_kernel,
        grid=(2, g, b),
        in_specs=[
            pl.BlockSpec((None, None, None, ncp, width), lambda t, g, b: (t, g, b, 0, 0)),
            pl.BlockSpec((None, 2, 1, width), lambda t, g, b: (t, 0, 0, 0)),
            pl.BlockSpec((None, 2, width, hidden), lambda t, g, b: (t, 0, 0, 0)),
            pl.BlockSpec((None, hidden, HEAD_DIM), lambda t, g, b: (t, 0, 0)),
            pl.BlockSpec((None, HEAD_DIM, hidden), lambda t, g, b: (t, 0, 0)),
        ],
        out_specs=[
            pl.BlockSpec((None, None, None, ncp, HEAD_DIM), lambda t, g, b: (t, g, b, 0, 0)),
            pl.BlockSpec((None, None, None, HEAD_DIM, ncp), lambda t, g, b: (t, g, b, 0, 0)),
        ],
        out_shape=[jax.ShapeDtypeStruct((2, g, b, ncp, HEAD_DIM), BF16),
                   jax.ShapeDtypeStruct((2, g, b, HEAD_DIM, ncp), BF16)],
        compiler_params=_params(3),
        name="nsa_compress",
    )(chunks, pos, w1, w2, w2t)


def _cmp_kernel(q_ref, kc_ref, vct_ref, vb_ref, ov_ref, oc_ref, sel_ref, bias_ref, skew_ref, *, nq):
    b, i = pl.program_id(1), pl.program_id(2)

    @pl.when((b == 0) & (i == 0))
    def _():
        _build_cmp_bias_table(vb_ref, bias_ref, skew_ref, nq)

    ncp = kc_ref.shape[0]
    nsel = sel_ref.shape[0]
    q_t = _gather_heads_to_lanes(q_ref)
    row0 = pl.multiple_of((nq - 1 - i) * 8, 8)
    s = jnp.dot(kc_ref[...], q_t, preferred_element_type=F32) + bias_ref[pl.ds(row0, ncp), :]
    c_io = lax.broadcasted_iota(jnp.int32, (ncp, GROUP_LANES), 0)
    pos = i * BLOCK_Q + (lax.broadcasted_iota(jnp.int32, (ncp, GROUP_LANES), 1) & (BLOCK_Q - 1))
    valid = pos - c_io * CMP_STRIDE - (CMP_BLOCK - 1) >= 0
    s = jnp.where(valid, s, NEG_INF)
    m = jnp.max(s, axis=0, keepdims=True)
    p = jnp.where(valid, jnp.exp(s - m), 0.0)
    l = jnp.sum(p, axis=0, keepdims=True)
    p = p * (1.0 / jnp.where(l > 0.0, l, 1.0))
    oc_ref[...] = jnp.dot(vct_ref[...], p.astype(BF16), preferred_element_type=F32).astype(oc_ref.dtype)

    psum = p[:, 0:BLOCK_Q]
    for h in range(1, HEADS_PER_GROUP):
        psum = psum + p[:, h * BLOCK_Q:(h + 1) * BLOCK_Q]
    hi = psum.astype(BF16)
    lo = (psum - hi.astype(F32)).astype(BF16)
    ov = ov_ref[...]
    score = jnp.dot(ov, hi, preferred_element_type=F32) + jnp.dot(ov, lo, preferred_element_type=F32)

    j_io = lax.broadcasted_iota(jnp.int32, (nsel, BLOCK_Q), 0)
    qpos = i * BLOCK_Q + lax.broadcasted_iota(jnp.int32, (nsel, BLOCK_Q), 1)
    causal = j_io * SEL_BLOCK <= qpos
    back = qpos // SEL_BLOCK - j_io
    forced = (j_io == 0) | ((back >= 0) & (back < SEL_LOCAL))
    score = jnp.where(causal, jnp.where(forced, FORCE_SCORE, score), -1.0)
    rank = jnp.zeros((nsel, BLOCK_Q), jnp.int32)
    for r in range(nsel):
        row = score[r:r + 1, :]
        ge = jnp.where(row >= score, 1, 0)
        gt = jnp.where(row > score, 1, 0)
        rank = rank + jnp.where(j_io > r, ge, gt)
    sel_ref[...] = ((rank < min(SEL_TOPK, nsel)) & causal).astype(sel_ref.dtype)


def _cmp_attention(h_t, cn, ct, vb, overlap_t, batch, seq):
    nq = seq // BLOCK_Q
    ncp = seq // CMP_STRIDE
    nsel = seq // SEL_BLOCK
    q_blk0 = SWA_HEADS * HEAD_DIM // GROUP_COLS
    return pl.pallas_call(
        functools.partial(_cmp_kernel, nq=nq),
        grid=(GROUPS, batch, nq),
        in_specs=[
            pl.BlockSpec((GROUP_COLS, BLOCK_Q), lambda g, b, i: (q_blk0 + g, b * nq + i)),
            pl.BlockSpec((None, None, None, ncp, HEAD_DIM), lambda g, b, i: (0, g, b, 0, 0)),
            pl.BlockSpec((None, None, None, HEAD_DIM, ncp), lambda g, b, i: (1, g, b, 0, 0)),
            pl.BlockSpec((None, HEADS_PER_GROUP, nq + 2, BLOCK_Q), lambda g, b, i: (g, 0, 0, 0)),
            pl.BlockSpec((nsel, ncp), lambda g, b, i: (0, 0)),
        ],
        out_specs=[
            pl.BlockSpec((None, None, None, HEAD_DIM, GROUP_LANES), lambda g, b, i: (b, g, i, 0, 0)),
            pl.BlockSpec((None, None, nsel, BLOCK_Q), lambda g, b, i: (b, g, 0, i)),
        ],
        out_shape=[jax.ShapeDtypeStruct((batch, GROUPS, nq, HEAD_DIM, GROUP_LANES), BF16),
                   jax.ShapeDtypeStruct((batch, GROUPS, nsel, seq), F32)],
        scratch_shapes=[pltpu.VMEM(((2 * nq - 1) * 8, GROUP_LANES), F32), pltpu.VMEM((BLOCK_Q, GROUP_LANES), F32)],
        compiler_params=_params(3),
        name="nsa_cmp_select",
    )(h_t, cn, ct, vb, overlap_t)


N_WIN_PAIRS = -(-(NSA_WINDOW - 1) // (2 * BLOCK_Q)) + 1
N_WIN_SLOTS = 2 * N_WIN_PAIRS + 1
SEL_ROWS = 16


def _selwin_kernel(q_ref, ksel_ref, kwin_ref, vsel_ref, vwin_ref, sel_ref, oc_ref, gate_ref, vbs_ref, vbw_ref,
                   o_ref, tsel_ref, twin_ref, s_scr, p_scr, *, nq):
    b, i = pl.program_id(1), pl.program_id(2)

    @pl.when((b == 0) & (i == 0))
    def _():
        _build_skew_table(vbs_ref, tsel_ref, nq + 1)
        _build_skew_table(vbw_ref, twin_ref, N_WIN_SLOTS)

    q_t = _gather_heads_to_lanes(q_ref)
    eye = _identity_tile()
    pair = 2 * BLOCK_Q
    init = (jnp.full((1, GROUP_LANES), NEG_INF, F32), jnp.zeros((V_AUG_ROWS, GROUP_LANES), F32))
    p_diag = i // 2
    blocks_per_tile = BLOCK_Q // SEL_BLOCK

    def pair_of_step(n):
        return jnp.clip(p_diag - n, 0, p_diag)

    def table_slots(n, n_steps):
        slot = i - 2 * pair_of_step(n) + 1
        live = (n >= 0) & (n < n_steps)
        return [jnp.where(live, slot - j, 0) for j in range(2)]

    def sel_scores(n):
        p = pair_of_step(n)
        slots = table_slots(n, p_diag + 1)
        neg = jnp.where(sel_ref[p] > 0.5, 0.0, NEG_INF)
        s_tiles = []
        for j in range(2):
            k_t = ksel_ref[pl.ds(pl.multiple_of(p * pair + j * BLOCK_Q, BLOCK_Q), BLOCK_Q), :]
            rows = neg[j * blocks_per_tile:(j + 1) * blocks_per_tile]
            rows = jnp.concatenate([jnp.concatenate([rows] * HEADS_PER_GROUP, axis=1),
                                    jnp.zeros((SEL_ROWS - blocks_per_tile, GROUP_LANES), F32)], axis=0)
            q_ext = jnp.concatenate([q_t, rows.astype(BF16)], axis=0)
            s_tiles.append(_biased_scores(eye, k_t, tsel_ref[slots[j]], q_ext))
        return s_tiles

    def win_scores(n):
        p = pair_of_step(n)
        slots = table_slots(n, p_diag + 1)
        s_tiles = []
        for j in range(2):
            k_t = kwin_ref[pl.ds(pl.multiple_of(p * pair + j * BLOCK_Q, BLOCK_Q), BLOCK_Q), :]
            s_tiles.append(_biased_scores(eye, k_t, twin_ref[slots[j]], q_t))
        return s_tiles

    carry_w = init
    for n in range(N_WIN_PAIRS):
        carry_w = _softmax_update(carry_w, win_scores(n), vwin_ref[pair_of_step(n)])
    acc_w = carry_w[1]

    _, acc_s = _pipelined_attention(p_diag + 1, sel_scores, lambda n: vsel_ref[pair_of_step(n)], init,
                                    s_scr, p_scr)

    def gate_row(branch):
        g = jnp.concatenate([gate_ref[branch, h:h + 1, :] for h in range(HEADS_PER_GROUP)], axis=1)
        return jax.nn.sigmoid(g.astype(F32))

    out = (gate_row(0) * oc_ref[...].astype(F32) + gate_row(1) * _normalized(acc_s)
           + gate_row(2) * _normalized(acc_w))
    o_ref[...] = _heads_to_columns(out).astype(o_ref.dtype)


def _selwin_attention(h_t, ksel_aug, kk, vsel_aug, vwin_aug, sel, oc_t, gates_t, vb_sel, vb_win, batch, seq):
    nq = seq // BLOCK_Q
    tokens = batch * seq
    q_blk0 = SWA_HEADS * HEAD_DIM // GROUP_COLS
    return pl.pallas_call(
        functools.partial(_selwin_kernel, nq=nq),
        grid=(GROUPS, batch, nq),
        in_specs=[
            pl.BlockSpec((GROUP_COLS, BLOCK_Q), lambda g, b, i: (q_blk0 + g, b * nq + i)),
            pl.BlockSpec((None, seq, HEAD_DIM + SEL_ROWS), lambda g, b, i: (g, b, 0)),
            pl.BlockSpec((None, None, seq, HEAD_DIM), lambda g, b, i: (3, g, b, 0)),
            pl.BlockSpec((None, None, nq // 2, V_AUG_ROWS, 2 * BLOCK_Q), lambda g, b, i: (g, b, 0, 0, 0)),
            pl.BlockSpec((None, None, nq // 2, V_AUG_ROWS, 2 * BLOCK_Q), lambda g, b, i: (g, b, 0, 0, 0)),
            pl.BlockSpec((None, None, nq // 2, 4, BLOCK_Q), lambda g, b, i: (b, g, 0, 0, i)),
            pl.BlockSpec((None, None, None, HEAD_DIM, GROUP_LANES), lambda g, b, i: (b, g, i, 0, 0)),
            pl.BlockSpec((None, 3, HEADS_PER_GROUP, BLOCK_Q), lambda g, b, i: (g, 0, 0, b * nq + i)),
            pl.BlockSpec((None, HEADS_PER_GROUP, nq + 2, BLOCK_Q), lambda g, b, i: (g, 0, 0, 0)),
            pl.BlockSpec((None, HEADS_PER_GROUP, N_WIN_SLOTS + 1, BLOCK_Q), lambda g, b, i: (g, 0, 0, 0)),
        ],
        out_specs=pl.BlockSpec((BLOCK_Q, GROUP_COLS), lambda g, b, i: (b * nq + i, g)),
        out_shape=jax.ShapeDtypeStruct((tokens, NSA_HEADS * HEAD_DIM), BF16),
        scratch_shapes=[pltpu.VMEM((nq + 1, BLOCK_Q, GROUP_LANES), BF16),
                        pltpu.VMEM((N_WIN_SLOTS, BLOCK_Q, GROUP_LANES), BF16),
                        pltpu.VMEM((2, BLOCK_Q, GROUP_LANES), F32),
                        pltpu.VMEM((2 * BLOCK_Q, GROUP_LANES), BF16)],
        compiler_params=_params(3),
        name="nsa_sel_win",
    )(h_t, ksel_aug, kk, vsel_aug, vwin_aug, sel, oc_t, gates_t, vb_sel, vb_win)


def _layer_norm(y, g_ref, b_ref):
    mu = jnp.mean(y, axis=-1, keepdims=True)
    yc = y - mu
    var = jnp.mean(yc * yc, axis=-1, keepdims=True)
    return yc * lax.rsqrt(var + LN_EPS) * g_ref[...] + b_ref[...]


def _merge_kernel(x_ref, ya_ref, yb_ref, wga_ref, wgb_ref, wa_ref, wb_ref, o_ref):
    x = x_ref[...]
    ga = jax.nn.sigmoid(jnp.dot(x, wga_ref[...], preferred_element_type=F32))
    gb = jax.nn.sigmoid(jnp.dot(x, wgb_ref[...], preferred_element_type=F32))
    a = jnp.dot(ya_ref[...], wa_ref[...], preferred_element_type=F32)
    bb = jnp.dot(yb_ref[...], wb_ref[...], preferred_element_type=F32)
    o_ref[...] = (ga * a + gb * bb).astype(o_ref.dtype)


def _merge(x_bf, y_a, y_b, w_ga, w_gb, w_a, w_b, *, tm, tn):
    tokens, d = x_bf.shape
    ya_cols, yb_cols = y_a.shape[1], y_b.shape[1]
    return pl.pallas_call(
        _merge_kernel,
        grid=(tokens // tm, d // tn),
        in_specs=[
            pl.BlockSpec((tm, d), lambda i, j: (i, 0)),
            pl.BlockSpec((tm, ya_cols), lambda i, j: (i, 0)),
            pl.BlockSpec((tm, yb_cols), lambda i, j: (i, 0)),
            pl.BlockSpec((d, tn), lambda i, j: (0, j)),
            pl.BlockSpec((d, tn), lambda i, j: (0, j)),
            pl.BlockSpec((ya_cols, tn), lambda i, j: (0, j)),
            pl.BlockSpec((yb_cols, tn), lambda i, j: (0, j)),
        ],
        out_specs=pl.BlockSpec((tm, tn), lambda i, j: (i, j)),
        out_shape=jax.ShapeDtypeStruct((tokens, d), BF16),
        compiler_params=_params(2),
        name="branch_merge",
    )(x_bf, y_a, y_b, w_ga, w_gb, w_a, w_b)


def _mix_ln_kernel(m_ref, x_ref, w_ref, g_ref, b_ref, o_ref):
    mix = jnp.dot(m_ref[...], w_ref[...], preferred_element_type=F32)
    o_ref[...] = _layer_norm(DN_ALPHA * x_ref[...] + mix, g_ref, b_ref)


def _mix_ln(merged, x, w_mix, ln_g, ln_b, *, tm):
    tokens, d = x.shape
    return pl.pallas_call(
        _mix_ln_kernel,
        grid=(tokens // tm,),
        in_specs=[
            pl.BlockSpec((tm, d), lambda i: (i, 0)),
            pl.BlockSpec((tm, d), lambda i: (i, 0)),
            pl.BlockSpec((d, d), lambda i: (0, 0)),
            pl.BlockSpec((1, d), lambda i: (0, 0)),
            pl.BlockSpec((1, d), lambda i: (0, 0)),
        ],
        out_specs=pl.BlockSpec((tm, d), lambda i: (i, 0)),
        out_shape=jax.ShapeDtypeStruct((tokens, d), F32),
        compiler_params=_params(1),
        name="mix_out_ln1",
    )(merged, x, w_mix, ln_g, ln_b)


def _xa_kernel(h_ref, wq_ref, k_ref, v_ref, wo_ref, g_ref, b_ref, o_ref):
    h = h_ref[...]
    q = jnp.dot(h.astype(BF16), wq_ref[...], preferred_element_type=F32) * (XA_HEAD_DIM ** -0.5)
    q = q.astype(BF16)
    outs = []
    for hd in range(XA_HEADS):
        cols = slice(hd * XA_HEAD_DIM, (hd + 1) * XA_HEAD_DIM)
        s = lax.dot_general(q[:, cols], k_ref[:, cols], (((1,), (1,)), ((), ())), preferred_element_type=F32)
        p = jnp.exp(s - jnp.max(s, axis=-1, keepdims=True))
        l = jnp.sum(p, axis=-1, keepdims=True)
        o = jnp.dot(p.astype(BF16), v_ref[:, cols], preferred_element_type=F32)
        outs.append(o * (1.0 / l))
    o = jnp.concatenate(outs, axis=1).astype(BF16)
    xa = jnp.dot(o, wo_ref[...], preferred_element_type=F32)
    o_ref[...] = _layer_norm(DN_ALPHA * h + xa, g_ref, b_ref)


def _cross_attention_ln(h, kv_mem, w_q, w_o, ln_g, ln_b, batch, seq, *, tm):
    tokens, d = h.shape
    mem_len = kv_mem.shape[0] // batch
    xa_dim = XA_HEADS * XA_HEAD_DIM
    nt = seq // tm
    return pl.pallas_call(
        _xa_kernel,
        grid=(batch, nt),
        in_specs=[
            pl.BlockSpec((tm, d), lambda b, i: (b * nt + i, 0)),
            pl.BlockSpec((d, xa_dim), lambda b, i: (0, 0)),
            pl.BlockSpec((mem_len, xa_dim), lambda b, i: (b, 0)),
            pl.BlockSpec((mem_len, xa_dim), lambda b, i: (b, 1)),
            pl.BlockSpec((xa_dim, d), lambda b, i: (0, 0)),
            pl.BlockSpec((1, d), lambda b, i: (0, 0)),
            pl.BlockSpec((1, d), lambda b, i: (0, 0)),
        ],
        out_specs=pl.BlockSpec((tm, d), lambda b, i: (b * nt + i, 0)),
        out_shape=jax.ShapeDtypeStruct((tokens, d), F32),
        compiler_params=_params(2),
        name="cross_attention_ln2",
    )(h, w_q, kv_mem, kv_mem, w_o, ln_g, ln_b)


def _mlp_kernel(h_ref, w1_ref, w2_ref, g_ref, b_ref, o_ref, hb_ref):
    j = pl.program_id(1)

    @pl.when(j == 0)
    def _():
        hb_ref[...] = h_ref[...].astype(BF16)
        o_ref[...] = jnp.zeros_like(o_ref)

    u = jnp.dot(hb_ref[...], w1_ref[...], preferred_element_type=F32)
    u = jnp.square(jnp.maximum(u, 0.0)).astype(BF16)
    o_ref[...] += jnp.dot(u, w2_ref[...], preferred_element_type=F32)

    @pl.when(j == pl.num_programs(1) - 1)
    def _():
        o_ref[...] = _layer_norm(DN_ALPHA * h_ref[...] + o_ref[...], g_ref, b_ref)


def _mlp_ln(h, w1, w2, ln_g, ln_b, *, tm, tf):
    tokens, d = h.shape
    d_ff = w1.shape[1]
    return pl.pallas_call(
        _mlp_kernel,
        grid=(tokens // tm, d_ff // tf),
        in_specs=[
            pl.BlockSpec((tm, d), lambda i, j: (i, 0)),
            pl.BlockSpec((d, tf), lambda i, j: (0, j)),
            pl.BlockSpec((tf, d), lambda i, j: (j, 0)),
            pl.BlockSpec((1, d), lambda i, j: (0, 0)),
            pl.BlockSpec((1, d), lambda i, j: (0, 0)),
        ],
        out_specs=pl.BlockSpec((tm, d), lambda i, j: (i, 0)),
        out_shape=jax.ShapeDtypeStruct((tokens, d), F32),
        scratch_shapes=[pltpu.VMEM((tm, d), BF16)],
        compiler_params=_params(2),
        name="mlp_ln3",
    )(h, w1, w2, ln_g, ln_b)


def _overlap_t(seq):
    ncp, nsel = seq // CMP_STRIDE, seq // SEL_BLOCK
    c_start = np.arange(ncp)[None, :] * CMP_STRIDE
    s_start = np.arange(nsel)[:, None] * SEL_BLOCK
    ov = (c_start < s_start + SEL_BLOCK) & (c_start + CMP_BLOCK > s_start) & (np.arange(ncp)[None, :] < ncp - 1)
    return jnp.asarray(ov, BF16)


def kernel(x, mem, w_in, attn_sinks, rel_bias_table, cmp_pos_k, cmp_w1_k, cmp_w2_k, cmp_pos_v, cmp_w1_v, cmp_w2_v,
           w_branch_swa, w_branch_nsa, w_mix_out, ln1_g, ln1_b, xa_w_q, xa_w_kv, xa_w_o, ln2_g, ln2_b,
           mlp_w1, mlp_w2, ln3_g, ln3_b):
    batch, seq, d = x.shape
    tokens = batch * seq
    nq = seq // BLOCK_Q
    assert w_in.shape[0] == 1, "one layer"
    assert seq % (8 * BLOCK_Q) == 0 and d % 512 == 0

    w = w_in[0]
    sizes = (SWA_HEADS * HEAD_DIM, 128, 128, NSA_HEADS * HEAD_DIM, 128, 128, 128, 128, 128, 128, 3 * NSA_HEADS, d, d)
    offs = np.concatenate([[0], np.cumsum(sizes)])
    (w_qa, w_ka, w_va, w_qb, w_kc, w_vc, w_ks, w_vs, w_kw, w_vw, w_gn, w_ga, w_gb) = [
        w[:, offs[n]:offs[n + 1]] for n in range(len(sizes))]
    w_rows = jnp.concatenate([w_ka, w_kc, w_ks, w_kw, w_vc], axis=1).astype(BF16)
    w_gn_pad = jnp.pad(w_gn, ((0, 0), (0, 128 - w_gn.shape[1])))
    w_cols_t = jnp.concatenate([w_qa, w_qb, w_va, w_vs, w_vw, w_gn_pad], axis=1).T.astype(BF16)

    x2 = x.reshape(tokens, d)
    x_bf = x2.astype(BF16)

    h_rows = _matmul(x_bf, w_rows, nt=False, tm=min(2048, tokens), tn=w_rows.shape[1])
    h_t = _matmul(w_cols_t, x_bf, nt=True, tm=1280, tn=min(1024, tokens))

    kk = h_rows[:, :512].reshape(tokens, 4, GROUPS, HEAD_DIM).transpose(1, 2, 0, 3)
    v_cmp = h_rows[:, 512:].reshape(tokens, GROUPS, HEAD_DIM).transpose(1, 0, 2)

    def value_tiles(row0, width):
        v = h_t[row0:row0 + 128].reshape(GROUPS, HEAD_DIM, batch, seq // width, width).transpose(0, 2, 3, 1, 4)
        ones = jnp.ones(v.shape[:3] + (V_AUG_ROWS - HEAD_DIM, width), BF16)
        return jnp.concatenate([v, ones], axis=3)

    vswa_aug = value_tiles(2048, BLOCK_Q)
    vsel_aug, vwin_aug = value_tiles(2176, 2 * BLOCK_Q), value_tiles(2304, 2 * BLOCK_Q)
    gates_t = h_t[2432:2432 + 3 * NSA_HEADS].reshape(GROUPS, HEADS_PER_GROUP, 3, tokens).transpose(0, 2, 1, 3)
    block_of_key = (jnp.arange(tokens) % BLOCK_Q) // SEL_BLOCK
    onehot = (block_of_key[:, None] == jnp.arange(SEL_ROWS)[None, :]).astype(BF16)
    ksel_aug = jnp.concatenate([kk[2], jnp.broadcast_to(onehot, (GROUPS, tokens, SEL_ROWS))], axis=2)

    def grouped(vb, rows):
        return vb[:, :rows].reshape(GROUPS, HEADS_PER_GROUP, rows, BLOCK_Q)

    rel_swa, rel_nsa = rel_bias_table[:, :SWA_HEADS], rel_bias_table[:, SWA_HEADS:]
    vb_swa = grouped(_bias_by_distance(rel_swa, seq, window=SWA_WINDOW, negative=NEG_INF), 4)
    vb_sel = grouped(_bias_by_distance(rel_nsa, seq, negative=NEG_INF), nq + 2)
    vb_win = grouped(_bias_by_distance(rel_nsa, seq, window=NSA_WINDOW, negative=NEG_INF), N_WIN_SLOTS + 1)
    vb_cmp = grouped(_bias_by_distance(rel_nsa, seq), nq + 2)
    sink_rows = jnp.repeat(attn_sinks[0].astype(F32), BLOCK_Q).reshape(GROUPS, 1, GROUP_LANES)

    y_a = _swa_attention(h_t, kk, vswa_aug, sink_rows, vb_swa, batch, seq)

    ncp = seq // CMP_STRIDE
    chunk_w = CMP_STRIDE * HEAD_DIM
    chunks = jnp.stack([kk[1], v_cmp]).reshape(2, GROUPS, batch, ncp, chunk_w)
    pos = jnp.stack([cmp_pos_k[0], cmp_pos_v[0]]).astype(F32).reshape(2, 2, 1, chunk_w)
    w1 = jnp.stack([cmp_w1_k[0], cmp_w1_v[0]]).astype(BF16).reshape(2, 2, chunk_w, -1)
    w2 = jnp.stack([cmp_w2_k[0], cmp_w2_v[0]]).astype(BF16)
    cn, ct = _compress(chunks, pos, w1, w2, w2.transpose(0, 2, 1))
    oc_t, sel = _cmp_attention(h_t, cn, ct, vb_cmp, _overlap_t(seq), batch, seq)
    sel = sel.reshape(batch, GROUPS, nq // 2, 4, seq)
    y_b = _selwin_attention(h_t, ksel_aug, kk, vsel_aug, vwin_aug, sel, oc_t, gates_t, vb_sel, vb_win, batch, seq)

    merged = _merge(x_bf, y_a, y_b, w_ga.astype(BF16), w_gb.astype(BF16),
                    w_branch_swa[0].astype(BF16), w_branch_nsa[0].astype(BF16), tm=min(1024, tokens), tn=512)
    h1 = _mix_ln(merged, x2, w_mix_out[0].astype(BF16), ln1_g, ln1_b, tm=512)

    mem_bf = mem.reshape(-1, d).astype(BF16)
    kv_mem = _matmul(mem_bf, xa_w_kv[0].astype(BF16), nt=False, tm=mem_bf.shape[0], tn=512)
    h2 = _cross_attention_ln(h1, kv_mem, xa_w_q[0].astype(BF16), xa_w_o[0].astype(BF16), ln2_g, ln2_b,
                             batch, seq, tm=512)

    h3 = _mlp_ln(h2, mlp_w1[0].astype(BF16), mlp_w2[0].astype(BF16), ln3_g, ln3_b, tm=512, tf=1024)
    return h3.reshape(batch, seq, d)
```

```python
import functools
import math

import numpy as np
import jax
import jax.numpy as jnp
from jax import lax
from jax.experimental import pallas as pl
from jax.experimental.pallas import tpu as pltpu

F32 = jnp.float32
BF16 = jnp.bfloat16

HEAD_DIM = 64
BLOCK_Q = 128
SWA_HEADS = 16
SWA_KV_HEADS = 2
SWA_WINDOW = 128
NSA_HEADS = 16
NSA_KV_HEADS = 2
CMP_BLOCK = 32
CMP_STRIDE = 16
SEL_BLOCK = 64
SEL_TOPK = 16
SEL_LOCAL = 2
NSA_WINDOW = 512
REL_BUCKETS = 32
REL_MAX_DIST = 4096
XA_HEADS = 4
XA_HEAD_DIM = 128
DEPTH = 1
DN_ALPHA = (2.0 * DEPTH) ** 0.25
LN_EPS = 1e-5
NEG_INF = -1e30
FORCE_SCORE = 1e4

GROUPS = 2
HEADS_PER_GROUP = 8
GROUP_LANES = HEADS_PER_GROUP * BLOCK_Q
GROUP_COLS = HEADS_PER_GROUP * HEAD_DIM

V7X_VMEM_LIMIT_BYTES = 56 * 1024 * 1024


def _params(n_axes):
    return pltpu.CompilerParams(dimension_semantics=("arbitrary",) * n_axes,
                                vmem_limit_bytes=V7X_VMEM_LIMIT_BYTES)


def _mm_kernel(a_ref, b_ref, o_ref, *, nt):
    if nt:
        out = lax.dot_general(a_ref[...], b_ref[...], (((1,), (1,)), ((), ())), preferred_element_type=F32)
    else:
        out = jnp.dot(a_ref[...], b_ref[...], preferred_element_type=F32)
    o_ref[...] = out.astype(o_ref.dtype)


def _matmul(a, b, *, nt, tm, tn, out_dtype=BF16):
    m, k = a.shape
    n = b.shape[0] if nt else b.shape[1]
    b_spec = pl.BlockSpec((tn, k), lambda i, j: (j, 0)) if nt else pl.BlockSpec((k, tn), lambda i, j: (0, j))
    return pl.pallas_call(
        functools.partial(_mm_kernel, nt=nt),
        grid=(m // tm, n // tn),
        in_specs=[pl.BlockSpec((tm, k), lambda i, j: (i, 0)), b_spec],
        out_specs=pl.BlockSpec((tm, tn), lambda i, j: (i, j)),
        out_shape=jax.ShapeDtypeStruct((m, n), out_dtype),
        compiler_params=_params(2),
        name="proj_nt" if nt else "proj_nn",
    )(a, b)


def _rel_bucket(dist):
    exact = REL_BUCKETS // 2
    d = jnp.maximum(dist, 0)
    log_ratio = jnp.log(jnp.maximum(d, 1).astype(F32) / exact) / math.log(REL_MAX_DIST / exact)
    large = jnp.minimum(exact + (log_ratio * (REL_BUCKETS - exact)).astype(jnp.int32), REL_BUCKETS - 1)
    return jnp.where(d < exact, d, large)


def _bias_by_distance(rel_table, seq, *, window=None, negative=0.0):
    dist = jnp.arange(seq)
    by_dist = rel_table[_rel_bucket(dist)].astype(F32).T
    if window is not None:
        by_dist = jnp.where(dist < window, by_dist, NEG_INF)
    padded = jnp.pad(by_dist, ((0, 0), (2 * BLOCK_Q, 0)), constant_values=negative)
    return padded.reshape(rel_table.shape[1], seq // BLOCK_Q + 2, BLOCK_Q)


def _build_skew_table(vb_ref, tbl_ref, n_tiles):
    def body(d, carry):
        for h in range(HEADS_PER_GROUP):
            lo = jnp.broadcast_to(vb_ref[h, pl.ds(d, 1), :], (BLOCK_Q, BLOCK_Q))
            hi = jnp.broadcast_to(vb_ref[h, pl.ds(d + 1, 1), :], (BLOCK_Q, BLOCK_Q))
            y = pltpu.roll(jnp.concatenate([lo, hi], axis=1), 0, 1, stride=1, stride_axis=0)
            tbl_ref[d, :, h * BLOCK_Q:(h + 1) * BLOCK_Q] = y[:, BLOCK_Q:].astype(tbl_ref.dtype)
        return carry

    lax.fori_loop(0, n_tiles, body, 0)


def _build_cmp_bias_table(vb_ref, tbl_ref, skew_ref, nq):
    per_tile = BLOCK_Q // CMP_STRIDE
    assert per_tile == 8 and CMP_BLOCK - 1 + CMP_STRIDE * (per_tile - 1) - BLOCK_Q == CMP_STRIDE - 1
    tbl_ref[pl.ds(nq * 8, (nq - 1) * 8), :] = jnp.zeros(((nq - 1) * 8, GROUP_LANES), F32)

    def body(d, last_row):
        for h in range(HEADS_PER_GROUP):
            lo = jnp.broadcast_to(vb_ref[h, pl.ds(d + 1, 1), :], (BLOCK_Q, BLOCK_Q))
            hi = jnp.broadcast_to(vb_ref[h, pl.ds(d + 2, 1), :], (BLOCK_Q, BLOCK_Q))
            y = pltpu.roll(jnp.concatenate([lo, hi], axis=1), 0, 1, stride=1, stride_axis=0)
            skew_ref[:, h * BLOCK_Q:(h + 1) * BLOCK_Q] = y[:, BLOCK_Q:]
        offsets = [CMP_BLOCK - 1 + CMP_STRIDE * c for c in range(per_tile - 1)]
        rows = [skew_ref[k:k + 1, :] for k in offsets] + [last_row]
        tbl_ref[pl.ds(pl.multiple_of((nq - 1 - d) * 8, 8), 8), :] = jnp.concatenate(rows, axis=0)
        return skew_ref[CMP_STRIDE - 1:CMP_STRIDE, :]

    lax.fori_loop(0, nq, body, jnp.zeros((1, GROUP_LANES), F32))


def _gather_heads_to_lanes(q_ref):
    q = jnp.concatenate([q_ref[h * HEAD_DIM:(h + 1) * HEAD_DIM, u * BLOCK_Q:(u + 1) * BLOCK_Q]
                         for u in range(q_ref.shape[1] // BLOCK_Q) for h in range(HEADS_PER_GROUP)], axis=1)
    return q * jnp.asarray(HEAD_DIM ** -0.5, q.dtype)


def _heads_to_columns(o_t):
    stacked = jnp.concatenate([o_t[:, h * BLOCK_Q:(h + 1) * BLOCK_Q] for h in range(HEADS_PER_GROUP)], axis=0)
    return stacked.T


def _identity_tile():
    r = lax.broadcasted_iota(jnp.int32, (BLOCK_Q, BLOCK_Q), 0)
    c = lax.broadcasted_iota(jnp.int32, (BLOCK_Q, BLOCK_Q), 1)
    return jnp.where(r == c, 1.0, 0.0).astype(BF16)


def _biased_scores(eye, k_t, bias_tile, q_t):
    lhs = jnp.concatenate([eye, k_t], axis=1)
    rhs = jnp.concatenate([bias_tile, q_t], axis=0)
    return jnp.dot(lhs, rhs, preferred_element_type=F32)


def _softmax_probs(m, s_tiles):
    ms, alphas, ps = [], [], [[] for _ in s_tiles]
    for h in range(m.shape[1] // BLOCK_Q):
        lanes = slice(h * BLOCK_Q, (h + 1) * BLOCK_Q)
        cols = [s[:, lanes] for s in s_tiles]
        m_new = m[:, lanes]
        for c in cols:
            m_new = jnp.maximum(m_new, jnp.max(c, axis=0, keepdims=True))
        ms.append(m_new)
        alphas.append(jnp.exp(m[:, lanes] - m_new))
        for j, c in enumerate(cols):
            ps[j].append(jnp.exp((c - m_new).astype(BF16)))
    p = jnp.concatenate([jnp.concatenate(pj, axis=1) for pj in ps], axis=0)
    return jnp.concatenate(ms, axis=1), jnp.concatenate(alphas, axis=1), p


def _softmax_update(carry, s_tiles, v_aug):
    m, acc = carry
    m, alpha, p = _softmax_probs(m, s_tiles)
    return m, alpha * acc + jnp.dot(v_aug, p, preferred_element_type=F32)


def _pipelined_attention(n_steps, scores_fn, values_fn, init, s_scr, p_scr, acc_scr, *, fill):
    m0, acc0 = init

    def put_scores(n):
        for j, s in enumerate(scores_fn(n)):
            s_scr[j] = s

    def fold(alpha, n, p):
        acc_scr[...] = alpha * acc_scr[...] + jnp.dot(values_fn(n), p, preferred_element_type=F32)

    if fill:
        put_scores(0)
        p_scr[...] = jnp.zeros_like(p_scr)
        acc_scr[...] = acc0
        return None

    def body(k, carry):
        m, alpha_prev = carry
        a = 2 * k
        fold(alpha_prev, a - 1, p_scr[...])
        s_b = scores_fn(a + 1)
        m, alpha_a, p_a = _softmax_probs(m, [s_scr.at[j] for j in range(s_scr.shape[0])])
        put_scores(a + 2)
        fold(alpha_a, a, p_a)
        m, alpha_b, p_b = _softmax_probs(m, s_b)
        p_scr[...] = p_b
        return m, alpha_b

    trips = (n_steps + 1) // 2
    m, alpha_last = lax.fori_loop(0, trips, body, (m0, jnp.ones_like(m0)))
    fold(alpha_last, 2 * trips - 1, p_scr[...])
    return m, acc_scr[...]


V_AUG_ROWS = HEAD_DIM + 16


def _normalized(acc):
    return acc[:HEAD_DIM] * (1.0 / acc[HEAD_DIM:HEAD_DIM + 1])


def _swa_kernel(q_ref, kprev_ref, kcur_ref, vprev_ref, vcur_ref, sink_ref, vb_ref, o_ref, tbl_ref):
    b, i = pl.program_id(1), pl.program_id(2)

    @pl.when((b == 0) & (i == 0))
    def _():
        _build_skew_table(vb_ref, tbl_ref, 3)

    q_t = _gather_heads_to_lanes(q_ref)
    eye = _identity_tile()
    acc0 = jnp.where(lax.broadcasted_iota(jnp.int32, (V_AUG_ROWS, GROUP_LANES), 0) < HEAD_DIM, 0.0, 1.0)
    s_cur = _biased_scores(eye, kcur_ref[...], tbl_ref[1], q_t)
    s_prev = _biased_scores(eye, kprev_ref[...], tbl_ref[jnp.where(i > 0, 2, 0)], q_t)
    v2 = jnp.concatenate([vcur_ref[...], vprev_ref[...]], axis=1)
    _, acc = _softmax_update((sink_ref[...], acc0), [s_cur, s_prev], v2)
    o_ref[...] = _heads_to_columns(_normalized(acc)).astype(o_ref.dtype)


def _swa_attention(h_t, kk, v_aug, sink_rows, vb, batch, seq):
    nq = seq // BLOCK_Q
    tokens = batch * seq
    return pl.pallas_call(
        _swa_kernel,
        grid=(GROUPS, batch, nq),
        in_specs=[
            pl.BlockSpec((GROUP_COLS, BLOCK_Q), lambda g, b, i: (g, b * nq + i)),
            pl.BlockSpec((None, None, BLOCK_Q, HEAD_DIM), lambda g, b, i: (0, g, b * nq + jnp.maximum(i - 1, 0), 0)),
            pl.BlockSpec((None, None, BLOCK_Q, HEAD_DIM), lambda g, b, i: (0, g, b * nq + i, 0)),
            pl.BlockSpec((None, None, None, V_AUG_ROWS, BLOCK_Q), lambda g, b, i: (g, b, jnp.maximum(i - 1, 0), 0, 0)),
            pl.BlockSpec((None, None, None, V_AUG_ROWS, BLOCK_Q), lambda g, b, i: (g, b, i, 0, 0)),
            pl.BlockSpec((None, 1, GROUP_LANES), lambda g, b, i: (g, 0, 0)),
            pl.BlockSpec((None, HEADS_PER_GROUP, 4, BLOCK_Q), lambda g, b, i: (g, 0, 0, 0)),
        ],
        out_specs=pl.BlockSpec((BLOCK_Q, GROUP_COLS), lambda g, b, i: (b * nq + i, g)),
        out_shape=jax.ShapeDtypeStruct((tokens, SWA_HEADS * HEAD_DIM), BF16),
        scratch_shapes=[pltpu.VMEM((3, BLOCK_Q, GROUP_LANES), BF16)],
        compiler_params=_params(3),
        name="swa_attention",
    )(h_t, kk, kk, v_aug, v_aug, sink_rows, vb)


def _compress_kernel(c_ref, pos_ref, w1_ref, w2_ref, w2t_ref, cn_ref, ct_ref):
    c = c_ref[...].astype(F32)
    top = (c + pos_ref[0]).astype(BF16)
    bot = (c + pos_ref[1]).astype(BF16)
    a = jnp.dot(top, w1_ref[0], preferred_element_type=F32)
    bm = jnp.dot(bot, w1_ref[1], preferred_element_type=F32)
    n = a.shape[0]
    pre = a + pltpu.roll(bm, n - 1, 0)
    hid = jax.nn.gelu(pre).astype(BF16)
    cn_ref[...] = jnp.dot(hid, w2_ref[...], preferred_element_type=F32).astype(cn_ref.dtype)
    ct_ref[...] = lax.dot_general(w2t_ref[...], hid, (((1,), (1,)), ((), ())),
                                  preferred_element_type=F32).astype(ct_ref.dtype)


def _compress(chunks, pos, w1, w2, w2t):
    _, g, b, ncp, width = chunks.shape
    hidden = w1.shape[-1]
    return pl.pallas_call(
        _compress_kernel,
        grid=(2, g, b),
        in_specs=[
            pl.BlockSpec((None, None, None, ncp, width), lambda t, g, b: (t, g, b, 0, 0)),
            pl.BlockSpec((None, 2, 1, width), lambda t, g, b: (t, 0, 0, 0)),
            pl.BlockSpec((None, 2, width, hidden), lambda t, g, b: (t, 0, 0, 0)),
            pl.BlockSpec((None, hidden, HEAD_DIM), lambda t, g, b: (t, 0, 0)),
            pl.BlockSpec((None, HEAD_DIM, hidden), lambda t, g, b: (t, 0, 0)),
        ],
        out_specs=[
            pl.BlockSpec((None, None, None, ncp, HEAD_DIM), lambda t, g, b: (t, g, b, 0, 0)),
            pl.BlockSpec((None, None, None, HEAD_DIM, ncp), lambda t, g, b: (t, g, b, 0, 0)),
        ],
        out_shape=[jax.ShapeDtypeStruct((2, g, b, ncp, HEAD_DIM), BF16),
                   jax.ShapeDtypeStruct((2, g, b, HEAD_DIM, ncp), BF16)],
        compiler_params=_params(3),
        name="nsa_compress",
    )(chunks, pos, w1, w2, w2t)


def _cmp_kernel(q_ref, kc_ref, vct_ref, vb_ref, ov_ref, oc_ref, sel_ref, bias_ref, skew_ref, *, nq):
    b, i = pl.program_id(1), pl.program_id(2)

    @pl.when((b == 0) & (i == 0))
    def _():
        _build_cmp_bias_table(vb_ref, bias_ref, skew_ref, nq)

    ncp = kc_ref.shape[0]
    nsel = sel_ref.shape[0]
    q_t = _gather_heads_to_lanes(q_ref)
    row0 = pl.multiple_of((nq - 1 - i) * 8, 8)
    s = jnp.dot(kc_ref[...], q_t, preferred_element_type=F32) + bias_ref[pl.ds(row0, ncp), :]
    c_io = lax.broadcasted_iota(jnp.int32, (ncp, GROUP_LANES), 0)
    pos = i * BLOCK_Q + (lax.broadcasted_iota(jnp.int32, (ncp, GROUP_LANES), 1) & (BLOCK_Q - 1))
    valid = pos - c_io * CMP_STRIDE - (CMP_BLOCK - 1) >= 0
    s = jnp.where(valid, s, NEG_INF)
    m = jnp.max(s, axis=0, keepdims=True)
    p = jnp.where(valid, jnp.exp(s - m), 0.0)
    l = jnp.sum(p, axis=0, keepdims=True)
    p = p * (1.0 / jnp.where(l > 0.0, l, 1.0))
    oc_ref[...] = jnp.dot(vct_ref[...], p.astype(BF16), preferred_element_type=F32).astype(oc_ref.dtype)

    psum = p[:, 0:BLOCK_Q]
    for h in range(1, HEADS_PER_GROUP):
        psum = psum + p[:, h * BLOCK_Q:(h + 1) * BLOCK_Q]
    hi = psum.astype(BF16)
    lo = (psum - hi.astype(F32)).astype(BF16)
    ov = ov_ref[...]
    score = jnp.dot(ov, hi, preferred_element_type=F32) + jnp.dot(ov, lo, preferred_element_type=F32)

    j_io = lax.broadcasted_iota(jnp.int32, (nsel, BLOCK_Q), 0)
    qpos = i * BLOCK_Q + lax.broadcasted_iota(jnp.int32, (nsel, BLOCK_Q), 1)
    causal = j_io * SEL_BLOCK <= qpos
    back = qpos // SEL_BLOCK - j_io
    forced = (j_io == 0) | ((back >= 0) & (back < SEL_LOCAL))
    score = jnp.where(causal, jnp.where(forced, FORCE_SCORE, score), -1.0)
    rank = jnp.zeros((nsel, BLOCK_Q), jnp.int32)
    for r in range(nsel):
        row = score[r:r + 1, :]
        ge = jnp.where(row >= score, 1, 0)
        gt = jnp.where(row > score, 1, 0)
        rank = rank + jnp.where(j_io > r, ge, gt)
    sel_ref[...] = ((rank < min(SEL_TOPK, nsel)) & causal).astype(sel_ref.dtype)


def _cmp_attention(h_t, cn, ct, vb, overlap_t, batch, seq):
    nq = seq // BLOCK_Q
    ncp = seq // CMP_STRIDE
    nsel = seq // SEL_BLOCK
    q_blk0 = SWA_HEADS * HEAD_DIM // GROUP_COLS
    return pl.pallas_call(
        functools.partial(_cmp_kernel, nq=nq),
        grid=(GROUPS, batch, nq),
        in_specs=[
            pl.BlockSpec((GROUP_COLS, BLOCK_Q), lambda g, b, i: (q_blk0 + g, b * nq + i)),
            pl.BlockSpec((None, None, None, ncp, HEAD_DIM), lambda g, b, i: (0, g, b, 0, 0)),
            pl.BlockSpec((None, None, None, HEAD_DIM, ncp), lambda g, b, i: (1, g, b, 0, 0)),
            pl.BlockSpec((None, HEADS_PER_GROUP, nq + 2, BLOCK_Q), lambda g, b, i: (g, 0, 0, 0)),
            pl.BlockSpec((nsel, ncp), lambda g, b, i: (0, 0)),
        ],
        out_specs=[
            pl.BlockSpec((None, None, None, HEAD_DIM, GROUP_LANES), lambda g, b, i: (b, g, i, 0, 0)),
            pl.BlockSpec((None, None, nsel, BLOCK_Q), lambda g, b, i: (b, g, 0, i)),
        ],
        out_shape=[jax.ShapeDtypeStruct((batch, GROUPS, nq, HEAD_DIM, GROUP_LANES), BF16),
                   jax.ShapeDtypeStruct((batch, GROUPS, nsel, seq), F32)],
        scratch_shapes=[pltpu.VMEM(((2 * nq - 1) * 8, GROUP_LANES), F32), pltpu.VMEM((BLOCK_Q, GROUP_LANES), F32)],
        compiler_params=_params(3),
        name="nsa_cmp_select",
    )(h_t, cn, ct, vb, overlap_t)


SUBTILES = 2
N_WIN_PAIRS = -(-(NSA_WINDOW - 1) // (SUBTILES * BLOCK_Q)) + 1
N_WIN_SLOTS = SUBTILES * N_WIN_PAIRS + 1
SEL_ROWS = 16


def _selwin_kernel(q_ref, ksel_ref, kwin_ref, vsel_ref, vwin_ref, sel_ref, oc_ref, gate_ref, vbs_ref, vbw_ref,
                   o_ref, tsel_ref, twin_ref, s_scr, p_scr, acc_scr, *, nq):
    b, blk = pl.program_id(1), pl.program_id(2)

    @pl.when((b == 0) & (blk == 0))
    def _():
        _build_skew_table(vbs_ref, tsel_ref, nq + 1)
        _build_skew_table(vbw_ref, twin_ref, N_WIN_SLOTS)

    q_t = _gather_heads_to_lanes(q_ref)
    lanes = q_t.shape[1]
    eye = _identity_tile()
    pair = SUBTILES * BLOCK_Q
    init = (jnp.full((1, lanes), NEG_INF, F32), jnp.zeros((V_AUG_ROWS, lanes), F32))
    blocks_per_tile = BLOCK_Q // SEL_BLOCK

    def pair_of_step(n):
        return jnp.clip(blk - n, 0, blk)

    def bias_tile(tbl_ref, n, j):
        live = (n >= 0) & (n <= blk)
        return jnp.concatenate([tbl_ref[jnp.where(live, 2 * n + u - j + 1, 0)] for u in range(SUBTILES)], axis=1)

    def sel_scores(n):
        p = pair_of_step(n)
        neg = jnp.where(sel_ref[p] > 0.5, 0.0, NEG_INF)
        s_tiles = []
        for j in range(SUBTILES):
            k_t = ksel_ref[pl.ds(pl.multiple_of(p * pair + j * BLOCK_Q, BLOCK_Q), BLOCK_Q), :]
            rows = neg[j * blocks_per_tile:(j + 1) * blocks_per_tile]
            rows = jnp.concatenate([rows[:, u * BLOCK_Q:(u + 1) * BLOCK_Q]
                                    for u in range(SUBTILES) for _ in range(HEADS_PER_GROUP)], axis=1)
            rows = jnp.concatenate([rows, jnp.zeros((SEL_ROWS - blocks_per_tile, lanes), F32)], axis=0)
            q_ext = jnp.concatenate([q_t, rows.astype(BF16)], axis=0)
            s_tiles.append(_biased_scores(eye, k_t, bias_tile(tsel_ref, n, j), q_ext))
        return s_tiles

    def win_scores(n):
        p = pair_of_step(n)
        s_tiles = []
        for j in range(SUBTILES):
            k_t = kwin_ref[pl.ds(pl.multiple_of(p * pair + j * BLOCK_Q, BLOCK_Q), BLOCK_Q), :]
            s_tiles.append(_biased_scores(eye, k_t, bias_tile(twin_ref, n, j), q_t))
        return s_tiles

    def sel_pipeline(fill):
        return _pipelined_attention(blk + 1, sel_scores, lambda n: vsel_ref[pair_of_step(n)], init,
                                    s_scr, p_scr, acc_scr, fill=fill)

    m_w, acc_w = init
    s_w = [win_scores(n) for n in range(min(2, N_WIN_PAIRS))]
    for n in range(N_WIN_PAIRS):
        m_w, alpha, p = _softmax_probs(m_w, s_w[n])
        if n + 2 < N_WIN_PAIRS:
            s_w.append(win_scores(n + 2))
        if n == max(N_WIN_PAIRS - 2, 0):
            sel_pipeline(fill=True)
        acc_w = alpha * acc_w + jnp.dot(vwin_ref[pair_of_step(n)], p, preferred_element_type=F32)

    _, acc_s = sel_pipeline(fill=False)

    def gate_row(branch):
        g = jnp.concatenate([gate_ref[branch, h:h + 1, u * BLOCK_Q:(u + 1) * BLOCK_Q]
                             for u in range(SUBTILES) for h in range(HEADS_PER_GROUP)], axis=1)
        return jax.nn.sigmoid(g.astype(F32))

    o_c = jnp.concatenate([oc_ref[u] for u in range(SUBTILES)], axis=1).astype(F32)
    out = gate_row(0) * o_c + gate_row(1) * _normalized(acc_s) + gate_row(2) * _normalized(acc_w)
    for u in range(SUBTILES):
        o_ref[u * BLOCK_Q:(u + 1) * BLOCK_Q, :] = _heads_to_columns(
            out[:, u * GROUP_LANES:(u + 1) * GROUP_LANES]).astype(o_ref.dtype)


def _selwin_attention(h_t, ksel_aug, kk, vsel_aug, vwin_aug, sel, oc_t, gates_t, vb_sel, vb_win, batch, seq):
    nq = seq // BLOCK_Q
    nblk = nq // SUBTILES
    tokens = batch * seq
    q_blk0 = SWA_HEADS * HEAD_DIM // GROUP_COLS
    pair = SUBTILES * BLOCK_Q
    lanes = SUBTILES * GROUP_LANES
    return pl.pallas_call(
        functools.partial(_selwin_kernel, nq=nq),
        grid=(GROUPS, batch, nblk),
        in_specs=[
            pl.BlockSpec((GROUP_COLS, pair), lambda g, b, i: (q_blk0 + g, b * nblk + i)),
            pl.BlockSpec((None, seq, HEAD_DIM + SEL_ROWS), lambda g, b, i: (g, b, 0)),
            pl.BlockSpec((None, None, seq, HEAD_DIM), lambda g, b, i: (3, g, b, 0)),
            pl.BlockSpec((None, None, nblk, V_AUG_ROWS, pair), lambda g, b, i: (g, b, 0, 0, 0)),
            pl.BlockSpec((None, None, nblk, V_AUG_ROWS, pair), lambda g, b, i: (g, b, 0, 0, 0)),
            pl.BlockSpec((None, None, nblk, pair // SEL_BLOCK, pair), lambda g, b, i: (b, g, 0, 0, i)),
            pl.BlockSpec((None, None, SUBTILES, HEAD_DIM, GROUP_LANES), lambda g, b, i: (b, g, i, 0, 0)),
            pl.BlockSpec((None, 3, HEADS_PER_GROUP, pair), lambda g, b, i: (g, 0, 0, b * nblk + i)),
            pl.BlockSpec((None, HEADS_PER_GROUP, nq + 2, BLOCK_Q), lambda g, b, i: (g, 0, 0, 0)),
            pl.BlockSpec((None, HEADS_PER_GROUP, N_WIN_SLOTS + 1, BLOCK_Q), lambda g, b, i: (g, 0, 0, 0)),
        ],
        out_specs=pl.BlockSpec((pair, GROUP_COLS), lambda g, b, i: (b * nblk + i, g)),
        out_shape=jax.ShapeDtypeStruct((tokens, NSA_HEADS * HEAD_DIM), BF16),
        scratch_shapes=[pltpu.VMEM((nq + 1, BLOCK_Q, GROUP_LANES), BF16),
                        pltpu.VMEM((N_WIN_SLOTS, BLOCK_Q, GROUP_LANES), BF16),
                        pltpu.VMEM((SUBTILES, BLOCK_Q, lanes), F32),
                        pltpu.VMEM((pair, lanes), BF16),
                        pltpu.VMEM((V_AUG_ROWS, lanes), F32)],
        compiler_params=_params(3),
        name="nsa_sel_win",
    )(h_t, ksel_aug, kk, vsel_aug, vwin_aug, sel, oc_t, gates_t, vb_sel, vb_win)


def _layer_norm(y, g_ref, b_ref):
    mu = jnp.mean(y, axis=-1, keepdims=True)
    yc = y - mu
    var = jnp.mean(yc * yc, axis=-1, keepdims=True)
    return yc * lax.rsqrt(var + LN_EPS) * g_ref[...] + b_ref[...]


def _merge_kernel(x_ref, ya_ref, yb_ref, wga_ref, wgb_ref, wa_ref, wb_ref, o_ref):
    x = x_ref[...]
    ga = jax.nn.sigmoid(jnp.dot(x, wga_ref[...], preferred_element_type=F32))
    gb = jax.nn.sigmoid(jnp.dot(x, wgb_ref[...], preferred_element_type=F32))
    a = jnp.dot(ya_ref[...], wa_ref[...], preferred_element_type=F32)
    bb = jnp.dot(yb_ref[...], wb_ref[...], preferred_element_type=F32)
    o_ref[...] = (ga * a + gb * bb).astype(o_ref.dtype)


def _merge(x_bf, y_a, y_b, w_ga, w_gb, w_a, w_b, *, tm, tn):
    tokens, d = x_bf.shape
    ya_cols, yb_cols = y_a.shape[1], y_b.shape[1]
    return pl.pallas_call(
        _merge_kernel,
        grid=(tokens // tm, d // tn),
        in_specs=[
            pl.BlockSpec((tm, d), lambda i, j: (i, 0)),
            pl.BlockSpec((tm, ya_cols), lambda i, j: (i, 0)),
            pl.BlockSpec((tm, yb_cols), lambda i, j: (i, 0)),
            pl.BlockSpec((d, tn), lambda i, j: (0, j)),
            pl.BlockSpec((d, tn), lambda i, j: (0, j)),
            pl.BlockSpec((ya_cols, tn), lambda i, j: (0, j)),
            pl.BlockSpec((yb_cols, tn), lambda i, j: (0, j)),
        ],
        out_specs=pl.BlockSpec((tm, tn), lambda i, j: (i, j)),
        out_shape=jax.ShapeDtypeStruct((tokens, d), BF16),
        compiler_params=_params(2),
        name="branch_merge",
    )(x_bf, y_a, y_b, w_ga, w_gb, w_a, w_b)


def _mix_ln_kernel(m_ref, x_ref, w_ref, g_ref, b_ref, o_ref):
    mix = jnp.dot(m_ref[...], w_ref[...], preferred_element_type=F32)
    o_ref[...] = _layer_norm(DN_ALPHA * x_ref[...] + mix, g_ref, b_ref)


def _mix_ln(merged, x, w_mix, ln_g, ln_b, *, tm):
    tokens, d = x.shape
    return pl.pallas_call(
        _mix_ln_kernel,
        grid=(tokens // tm,),
        in_specs=[
            pl.BlockSpec((tm, d), lambda i: (i, 0)),
            pl.BlockSpec((tm, d), lambda i: (i, 0)),
            pl.BlockSpec((d, d), lambda i: (0, 0)),
            pl.BlockSpec((1, d), lambda i: (0, 0)),
            pl.BlockSpec((1, d), lambda i: (0, 0)),
        ],
        out_specs=pl.BlockSpec((tm, d), lambda i: (i, 0)),
        out_shape=jax.ShapeDtypeStruct((tokens, d), F32),
        compiler_params=_params(1),
        name="mix_out_ln1",
    )(merged, x, w_mix, ln_g, ln_b)


def _xa_kernel(h_ref, wq_ref, k_ref, v_ref, wo_ref, g_ref, b_ref, o_ref):
    h = h_ref[...]
    q = jnp.dot(h.astype(BF16), wq_ref[...], preferred_element_type=F32) * (XA_HEAD_DIM ** -0.5)
    q = q.astype(BF16)
    outs = []
    for hd in range(XA_HEADS):
        cols = slice(hd * XA_HEAD_DIM, (hd + 1) * XA_HEAD_DIM)
        s = lax.dot_general(q[:, cols], k_ref[:, cols], (((1,), (1,)), ((), ())), preferred_element_type=F32)
        p = jnp.exp(s - jnp.max(s, axis=-1, keepdims=True))
        l = jnp.sum(p, axis=-1, keepdims=True)
        o = jnp.dot(p.astype(BF16), v_ref[:, cols], preferred_element_type=F32)
        outs.append(o * (1.0 / l))
    o = jnp.concatenate(outs, axis=1).astype(BF16)
    xa = jnp.dot(o, wo_ref[...], preferred_element_type=F32)
    o_ref[...] = _layer_norm(DN_ALPHA * h + xa, g_ref, b_ref)


def _cross_attention_ln(h, kv_mem, w_q, w_o, ln_g, ln_b, batch, seq, *, tm):
    tokens, d = h.shape
    mem_len = kv_mem.shape[0] // batch
    xa_dim = XA_HEADS * XA_HEAD_DIM
    nt = seq // tm
    return pl.pallas_call(
        _xa_kernel,
        grid=(batch, nt),
        in_specs=[
            pl.BlockSpec((tm, d), lambda b, i: (b * nt + i, 0)),
            pl.BlockSpec((d, xa_dim), lambda b, i: (0, 0)),
            pl.BlockSpec((mem_len, xa_dim), lambda b, i: (b, 0)),
            pl.BlockSpec((mem_len, xa_dim), lambda b, i: (b, 1)),
            pl.BlockSpec((xa_dim, d), lambda b, i: (0, 0)),
            pl.BlockSpec((1, d), lambda b, i: (0, 0)),
            pl.BlockSpec((1, d), lambda b, i: (0, 0)),
        ],
        out_specs=pl.BlockSpec((tm, d), lambda b, i: (b * nt + i, 0)),
        out_shape=jax.ShapeDtypeStruct((tokens, d), F32),
        compiler_params=_params(2),
        name="cross_attention_ln2",
    )(h, w_q, kv_mem, kv_mem, w_o, ln_g, ln_b)


def _mlp_kernel(h_ref, w1_ref, w2_ref, g_ref, b_ref, o_ref, hb_ref):
    j = pl.program_id(1)

    @pl.when(j == 0)
    def _():
        hb_ref[...] = h_ref[...].astype(BF16)
        o_ref[...] = jnp.zeros_like(o_ref)

    u = jnp.dot(hb_ref[...], w1_ref[...], preferred_element_type=F32)
    u = jnp.square(jnp.maximum(u, 0.0)).astype(BF16)
    o_ref[...] += jnp.dot(u, w2_ref[...], preferred_element_type=F32)

    @pl.when(j == pl.num_programs(1) - 1)
    def _():
        o_ref[...] = _layer_norm(DN_ALPHA * h_ref[...] + o_ref[...], g_ref, b_ref)


def _mlp_ln(h, w1, w2, ln_g, ln_b, *, tm, tf):
    tokens, d = h.shape
    d_ff = w1.shape[1]
    return pl.pallas_call(
        _mlp_kernel,
        grid=(tokens // tm, d_ff // tf),
        in_specs=[
            pl.BlockSpec((tm, d), lambda i, j: (i, 0)),
            pl.BlockSpec((d, tf), lambda i, j: (0, j)),
            pl.BlockSpec((tf, d), lambda i, j: (j, 0)),
            pl.BlockSpec((1, d), lambda i, j: (0, 0)),
            pl.BlockSpec((1, d), lambda i, j: (0, 0)),
        ],
        out_specs=pl.BlockSpec((tm, d), lambda i, j: (i, 0)),
        out_shape=jax.ShapeDtypeStruct((tokens, d), F32),
        scratch_shapes=[pltpu.VMEM((tm, d), BF16)],
        compiler_params=_params(2),
        name="mlp_ln3",
    )(h, w1, w2, ln_g, ln_b)


def _overlap_t(seq):
    ncp, nsel = seq // CMP_STRIDE, seq // SEL_BLOCK
    c_start = np.arange(ncp)[None, :] * CMP_STRIDE
    s_start = np.arange(nsel)[:, None] * SEL_BLOCK
    ov = (c_start < s_start + SEL_BLOCK) & (c_start + CMP_BLOCK > s_start) & (np.arange(ncp)[None, :] < ncp - 1)
    return jnp.asarray(ov, BF16)


def kernel(x, mem, w_in, attn_sinks, rel_bias_table, cmp_pos_k, cmp_w1_k, cmp_w2_k, cmp_pos_v, cmp_w1_v, cmp_w2_v,
           w_branch_swa, w_branch_nsa, w_mix_out, ln1_g, ln1_b, xa_w_q, xa_w_kv, xa_w_o, ln2_g, ln2_b,
           mlp_w1, mlp_w2, ln3_g, ln3_b):
    batch, seq, d = x.shape
    tokens = batch * seq
    nq = seq // BLOCK_Q
    assert w_in.shape[0] == 1, "one layer"
    assert seq % (8 * BLOCK_Q) == 0 and d % 512 == 0

    w = w_in[0]
    sizes = (SWA_HEADS * HEAD_DIM, 128, 128, NSA_HEADS * HEAD_DIM, 128, 128, 128, 128, 128, 128, 3 * NSA_HEADS, d, d)
    offs = np.concatenate([[0], np.cumsum(sizes)])
    (w_qa, w_ka, w_va, w_qb, w_kc, w_vc, w_ks, w_vs, w_kw, w_vw, w_gn, w_ga, w_gb) = [
        w[:, offs[n]:offs[n + 1]] for n in range(len(sizes))]
    w_rows = jnp.concatenate([w_ka, w_kc, w_ks, w_kw, w_vc], axis=1).astype(BF16)
    w_gn_pad = jnp.pad(w_gn, ((0, 0), (0, 128 - w_gn.shape[1])))
    w_cols_t = jnp.concatenate([w_qa, w_qb, w_va, w_vs, w_vw, w_gn_pad], axis=1).T.astype(BF16)

    x2 = x.reshape(tokens, d)
    x_bf = x2.astype(BF16)

    h_rows = _matmul(x_bf, w_rows, nt=False, tm=min(2048, tokens), tn=w_rows.shape[1])
    h_t = _matmul(w_cols_t, x_bf, nt=True, tm=1280, tn=min(1024, tokens))

    kk = h_rows[:, :512].reshape(tokens, 4, GROUPS, HEAD_DIM).transpose(1, 2, 0, 3)
    v_cmp = h_rows[:, 512:].reshape(tokens, GROUPS, HEAD_DIM).transpose(1, 0, 2)

    def value_tiles(row0, width):
        v = h_t[row0:row0 + 128].reshape(GROUPS, HEAD_DIM, batch, seq // width, width).transpose(0, 2, 3, 1, 4)
        ones = jnp.ones(v.shape[:3] + (V_AUG_ROWS - HEAD_DIM, width), BF16)
        return jnp.concatenate([v, ones], axis=3)

    vswa_aug = value_tiles(2048, BLOCK_Q)
    vsel_aug, vwin_aug = value_tiles(2176, 2 * BLOCK_Q), value_tiles(2304, 2 * BLOCK_Q)
    gates_t = h_t[2432:2432 + 3 * NSA_HEADS].reshape(GROUPS, HEADS_PER_GROUP, 3, tokens).transpose(0, 2, 1, 3)
    block_of_key = (jnp.arange(tokens) % BLOCK_Q) // SEL_BLOCK
    onehot = (block_of_key[:, None] == jnp.arange(SEL_ROWS)[None, :]).astype(BF16)
    ksel_aug = jnp.concatenate([kk[2], jnp.broadcast_to(onehot, (GROUPS, tokens, SEL_ROWS))], axis=2)

    def grouped(vb, rows):
        return vb[:, :rows].reshape(GROUPS, HEADS_PER_GROUP, rows, BLOCK_Q)

    rel_swa, rel_nsa = rel_bias_table[:, :SWA_HEADS], rel_bias_table[:, SWA_HEADS:]
    vb_swa = grouped(_bias_by_distance(rel_swa, seq, window=SWA_WINDOW, negative=NEG_INF), 4)
    vb_sel = grouped(_bias_by_distance(rel_nsa, seq, negative=NEG_INF), nq + 2)
    vb_win = grouped(_bias_by_distance(rel_nsa, seq, window=NSA_WINDOW, negative=NEG_INF), N_WIN_SLOTS + 1)
    vb_cmp = grouped(_bias_by_distance(rel_nsa, seq), nq + 2)
    sink_rows = jnp.repeat(attn_sinks[0].astype(F32), BLOCK_Q).reshape(GROUPS, 1, GROUP_LANES)

    y_a = _swa_attention(h_t, kk, vswa_aug, sink_rows, vb_swa, batch, seq)

    ncp = seq // CMP_STRIDE
    chunk_w = CMP_STRIDE * HEAD_DIM
    chunks = jnp.stack([kk[1], v_cmp]).reshape(2, GROUPS, batch, ncp, chunk_w)
    pos = jnp.stack([cmp_pos_k[0], cmp_pos_v[0]]).astype(F32).reshape(2, 2, 1, chunk_w)
    w1 = jnp.stack([cmp_w1_k[0], cmp_w1_v[0]]).astype(BF16).reshape(2, 2, chunk_w, -1)
    w2 = jnp.stack([cmp_w2_k[0], cmp_w2_v[0]]).astype(BF16)
    cn, ct = _compress(chunks, pos, w1, w2, w2.transpose(0, 2, 1))
    oc_t, sel = _cmp_attention(h_t, cn, ct, vb_cmp, _overlap_t(seq), batch, seq)
    sel = sel.reshape(batch, GROUPS, nq // 2, 4, seq)
    y_b = _selwin_attention(h_t, ksel_aug, kk, vsel_aug, vwin_aug, sel, oc_t, gates_t, vb_sel, vb_win, batch, seq)

    merged = _merge(x_bf, y_a, y_b, w_ga.astype(BF16), w_gb.astype(BF16),
                    w_branch_swa[0].astype(BF16), w_branch_nsa[0].astype(BF16), tm=min(1024, tokens), tn=512)
    h1 = _mix_ln(merged, x2, w_mix_out[0].astype(BF16), ln1_g, ln1_b, tm=512)

    mem_bf = mem.reshape(-1, d).astype(BF16)
    kv_mem = _matmul(mem_bf, xa_w_kv[0].astype(BF16), nt=False, tm=mem_bf.shape[0], tn=512)
    h2 = _cross_attention_ln(h1, kv_mem, xa_w_q[0].astype(BF16), xa_w_o[0].astype(BF16), ln2_g, ln2_b,
                             batch, seq, tm=512)

    h3 = _mlp_ln(h2, mlp_w1[0].astype(BF16), mlp_w2[0].astype(BF16), ln3_g, ln3_b, tm=512, tf=1024)
    return h3.reshape(batch, seq, d)
```

```python
import functools
import math

import numpy as np
import jax
import jax.numpy as jnp
from jax import lax
from jax.experimental import pallas as pl
from jax.experimental.pallas import tpu as pltpu

F32 = jnp.float32
BF16 = jnp.bfloat16

HEAD_DIM = 64
BLOCK_Q = 128
SWA_HEADS = 16
SWA_KV_HEADS = 2
SWA_WINDOW = 128
NSA_HEADS = 16
NSA_KV_HEADS = 2
CMP_BLOCK = 32
CMP_STRIDE = 16
SEL_BLOCK = 64
SEL_TOPK = 16
SEL_LOCAL = 2
NSA_WINDOW = 512
REL_BUCKETS = 32
REL_MAX_DIST = 4096
XA_HEADS = 4
XA_HEAD_DIM = 128
DEPTH = 1
DN_ALPHA = (2.0 * DEPTH) ** 0.25
LN_EPS = 1e-5
NEG_INF = -1e30
FORCE_SCORE = 1e4

GROUPS = 2
HEADS_PER_GROUP = 8
GROUP_LANES = HEADS_PER_GROUP * BLOCK_Q
GROUP_COLS = HEADS_PER_GROUP * HEAD_DIM
SUBTILES = 2

V7X_VMEM_LIMIT_BYTES = 56 * 1024 * 1024


def _params(n_axes):
    return pltpu.CompilerParams(dimension_semantics=("arbitrary",) * n_axes,
                                vmem_limit_bytes=V7X_VMEM_LIMIT_BYTES)


def _mm_kernel(a_ref, b_ref, o_ref, *, nt):
    if nt:
        out = lax.dot_general(a_ref[...], b_ref[...], (((1,), (1,)), ((), ())), preferred_element_type=F32)
    else:
        out = jnp.dot(a_ref[...], b_ref[...], preferred_element_type=F32)
    o_ref[...] = out.astype(o_ref.dtype)


def _matmul(a, b, *, nt, tm, tn, out_dtype=BF16):
    m, k = a.shape
    n = b.shape[0] if nt else b.shape[1]
    b_spec = pl.BlockSpec((tn, k), lambda i, j: (j, 0)) if nt else pl.BlockSpec((k, tn), lambda i, j: (0, j))
    return pl.pallas_call(
        functools.partial(_mm_kernel, nt=nt),
        grid=(m // tm, n // tn),
        in_specs=[pl.BlockSpec((tm, k), lambda i, j: (i, 0)), b_spec],
        out_specs=pl.BlockSpec((tm, tn), lambda i, j: (i, j)),
        out_shape=jax.ShapeDtypeStruct((m, n), out_dtype),
        compiler_params=_params(2),
        name="proj_nt" if nt else "proj_nn",
    )(a, b)


def _rel_bucket(dist):
    exact = REL_BUCKETS // 2
    d = jnp.maximum(dist, 0)
    log_ratio = jnp.log(jnp.maximum(d, 1).astype(F32) / exact) / math.log(REL_MAX_DIST / exact)
    large = jnp.minimum(exact + (log_ratio * (REL_BUCKETS - exact)).astype(jnp.int32), REL_BUCKETS - 1)
    return jnp.where(d < exact, d, large)


def _bias_by_distance(by_dist, rows, *, window=None):
    heads, seq = by_dist.shape
    if window is not None:
        by_dist = jnp.where(jnp.arange(seq) < window, by_dist, NEG_INF)
    padded = jnp.pad(by_dist, ((0, 0), (2 * BLOCK_Q, 0)), constant_values=NEG_INF)[:, :rows * BLOCK_Q]
    return padded.reshape(heads // HEADS_PER_GROUP, HEADS_PER_GROUP, rows, BLOCK_Q)


def _build_skew_table(vb_ref, tbl_ref, n_tiles):
    def body(d, carry):
        for h in range(HEADS_PER_GROUP):
            lo = jnp.broadcast_to(vb_ref[h, pl.ds(d, 1), :], (BLOCK_Q, BLOCK_Q))
            hi = jnp.broadcast_to(vb_ref[h, pl.ds(d + 1, 1), :], (BLOCK_Q, BLOCK_Q))
            y = pltpu.roll(jnp.concatenate([lo, hi], axis=1), 0, 1, stride=1, stride_axis=0)
            tbl_ref[d, :, h * BLOCK_Q:(h + 1) * BLOCK_Q] = y[:, BLOCK_Q:].astype(tbl_ref.dtype)
        return carry

    lax.fori_loop(0, n_tiles, body, 0)


def _build_cmp_bias_table(vb_ref, tbl_ref, skew_ref, nq):
    per_tile = BLOCK_Q // CMP_STRIDE
    assert per_tile == 8 and CMP_BLOCK - 1 + CMP_STRIDE * (per_tile - 1) - BLOCK_Q == CMP_STRIDE - 1
    tbl_ref[pl.ds(nq * 8, (nq - 1) * 8), :] = jnp.full(((nq - 1) * 8, GROUP_LANES), NEG_INF, F32)

    def body(d, last_row):
        for h in range(HEADS_PER_GROUP):
            lo = jnp.broadcast_to(vb_ref[h, pl.ds(d + 1, 1), :], (BLOCK_Q, BLOCK_Q))
            hi = jnp.broadcast_to(vb_ref[h, pl.ds(d + 2, 1), :], (BLOCK_Q, BLOCK_Q))
            y = pltpu.roll(jnp.concatenate([lo, hi], axis=1), 0, 1, stride=1, stride_axis=0)
            skew_ref[:, h * BLOCK_Q:(h + 1) * BLOCK_Q] = y[:, BLOCK_Q:]
        offsets = [CMP_BLOCK - 1 + CMP_STRIDE * c for c in range(per_tile - 1)]
        rows = [skew_ref[k:k + 1, :] for k in offsets] + [last_row]
        tbl_ref[pl.ds(pl.multiple_of((nq - 1 - d) * 8, 8), 8), :] = jnp.concatenate(rows, axis=0)
        return skew_ref[CMP_STRIDE - 1:CMP_STRIDE, :]

    lax.fori_loop(0, nq, body, jnp.full((1, GROUP_LANES), NEG_INF, F32))


def _gather_heads_to_lanes(q_ref):
    q = jnp.concatenate([q_ref[h * HEAD_DIM:(h + 1) * HEAD_DIM, u * BLOCK_Q:(u + 1) * BLOCK_Q]
                         for u in range(q_ref.shape[1] // BLOCK_Q) for h in range(HEADS_PER_GROUP)], axis=1)
    return q * jnp.asarray(HEAD_DIM ** -0.5, q.dtype)


def _heads_to_columns(o_t):
    stacked = jnp.concatenate([o_t[:, h * BLOCK_Q:(h + 1) * BLOCK_Q] for h in range(HEADS_PER_GROUP)], axis=0)
    return stacked.T


def _identity_tile():
    r = lax.broadcasted_iota(jnp.int32, (BLOCK_Q, BLOCK_Q), 0)
    c = lax.broadcasted_iota(jnp.int32, (BLOCK_Q, BLOCK_Q), 1)
    return jnp.where(r == c, 1.0, 0.0).astype(BF16)


def _biased_scores(eye, k_t, bias_tile, q_t):
    lhs = jnp.concatenate([eye, k_t], axis=1)
    rhs = jnp.concatenate([bias_tile, q_t], axis=0)
    return jnp.dot(lhs, rhs, preferred_element_type=F32)


def _softmax_probs(m, s_tiles):
    ms, alphas, ps = [], [], [[] for _ in s_tiles]
    for h in range(m.shape[1] // BLOCK_Q):
        lanes = slice(h * BLOCK_Q, (h + 1) * BLOCK_Q)
        cols = [s[:, lanes] for s in s_tiles]
        m_new = m[:, lanes]
        for c in cols:
            m_new = jnp.maximum(m_new, jnp.max(c, axis=0, keepdims=True))
        ms.append(m_new)
        alphas.append(jnp.exp(m[:, lanes] - m_new))
        for j, c in enumerate(cols):
            ps[j].append(jnp.exp((c - m_new).astype(BF16)))
    p = jnp.concatenate([jnp.concatenate(pj, axis=1) for pj in ps], axis=0)
    return jnp.concatenate(ms, axis=1), jnp.concatenate(alphas, axis=1), p


def _softmax_update(carry, s_tiles, v_aug):
    m, acc = carry
    m, alpha, p = _softmax_probs(m, s_tiles)
    return m, alpha * acc + jnp.dot(v_aug, p, preferred_element_type=F32)


def _pipelined_attention(n_steps, scores_fn, values_fn, init, s_scr, p_scr, acc_scr, *, fill):
    m0, acc0 = init

    def put_scores(n):
        for j, s in enumerate(scores_fn(n)):
            s_scr[j] = s

    def fold(alpha, n, p):
        acc_scr[...] = alpha * acc_scr[...] + jnp.dot(values_fn(n), p, preferred_element_type=F32)

    if fill:
        put_scores(0)
        p_scr[...] = jnp.zeros_like(p_scr)
        acc_scr[...] = acc0
        return None

    def body(k, carry):
        m, alpha_prev = carry
        a = 2 * k
        fold(alpha_prev, a - 1, p_scr[...])
        s_b = scores_fn(a + 1)
        m, alpha_a, p_a = _softmax_probs(m, [s_scr.at[j] for j in range(s_scr.shape[0])])
        put_scores(a + 2)
        fold(alpha_a, a, p_a)
        m, alpha_b, p_b = _softmax_probs(m, s_b)
        p_scr[...] = p_b
        return m, alpha_b

    trips = (n_steps + 1) // 2
    m, alpha_last = lax.fori_loop(0, trips, body, (m0, jnp.ones_like(m0)))
    fold(alpha_last, 2 * trips - 1, p_scr[...])
    return m, acc_scr[...]


V_AUG_ROWS = HEAD_DIM + 16


def _normalized(acc):
    return acc[:HEAD_DIM] * (1.0 / acc[HEAD_DIM:HEAD_DIM + 1])


def _swa_kernel(q_ref, kprev_ref, kcur_ref, vprev_ref, vcur_ref, sink_ref, vb_ref, o_ref, tbl_ref):
    b, i = pl.program_id(1), pl.program_id(2)

    @pl.when((b == 0) & (i == 0))
    def _():
        _build_skew_table(vb_ref, tbl_ref, 3)

    eye = _identity_tile()
    acc0 = jnp.where(lax.broadcasted_iota(jnp.int32, (V_AUG_ROWS, GROUP_LANES), 0) < HEAD_DIM, 0.0, 1.0)
    for u in range(SWA_TILES):
        q_t = _gather_heads_to_lanes(q_ref.at[:, u * BLOCK_Q:(u + 1) * BLOCK_Q])
        k_cur, v_cur = kcur_ref[u * BLOCK_Q:(u + 1) * BLOCK_Q, :], vcur_ref[u]
        if u == 0:
            k_prev, v_prev, prev_slot = kprev_ref[...], vprev_ref[...], jnp.where(i > 0, 2, 0)
        else:
            k_prev, v_prev, prev_slot = kcur_ref[(u - 1) * BLOCK_Q:u * BLOCK_Q, :], vcur_ref[u - 1], 2
        s_cur = _biased_scores(eye, k_cur, tbl_ref[1], q_t)
        s_prev = _biased_scores(eye, k_prev, tbl_ref[prev_slot], q_t)
        v2 = jnp.concatenate([v_cur, v_prev], axis=1)
        _, acc = _softmax_update((sink_ref[...], acc0), [s_cur, s_prev], v2)
        o_ref[u * BLOCK_Q:(u + 1) * BLOCK_Q, :] = _heads_to_columns(_normalized(acc)).astype(o_ref.dtype)


SWA_TILES = 4


def _swa_attention(h_t, kk, v_aug, sink_rows, vb, batch, seq):
    nq = seq // BLOCK_Q
    nblk = nq // SWA_TILES
    width = SWA_TILES * BLOCK_Q
    tokens = batch * seq

    def prev_tile(i):
        return jnp.maximum(i * SWA_TILES - 1, 0)

    return pl.pallas_call(
        _swa_kernel,
        grid=(GROUPS, batch, nblk),
        in_specs=[
            pl.BlockSpec((GROUP_COLS, width), lambda g, b, i: (g, b * nblk + i)),
            pl.BlockSpec((None, None, BLOCK_Q, HEAD_DIM), lambda g, b, i: (0, g, b * nq + prev_tile(i), 0)),
            pl.BlockSpec((None, None, width, HEAD_DIM), lambda g, b, i: (0, g, b * nblk + i, 0)),
            pl.BlockSpec((None, None, None, V_AUG_ROWS, BLOCK_Q), lambda g, b, i: (g, b, prev_tile(i), 0, 0)),
            pl.BlockSpec((None, None, SWA_TILES, V_AUG_ROWS, BLOCK_Q), lambda g, b, i: (g, b, i, 0, 0)),
            pl.BlockSpec((None, 1, GROUP_LANES), lambda g, b, i: (g, 0, 0)),
            pl.BlockSpec((None, HEADS_PER_GROUP, 4, BLOCK_Q), lambda g, b, i: (g, 0, 0, 0)),
        ],
        out_specs=pl.BlockSpec((width, GROUP_COLS), lambda g, b, i: (b * nblk + i, g)),
        out_shape=jax.ShapeDtypeStruct((tokens, SWA_HEADS * HEAD_DIM), BF16),
        scratch_shapes=[pltpu.VMEM((3, BLOCK_Q, GROUP_LANES), BF16)],
        compiler_params=_params(3),
        name="swa_attention",
    )(h_t, kk, kk, v_aug, v_aug, sink_rows, vb)


def _compress_kernel(c_ref, pos_ref, w1_ref, w2_ref, w2t_ref, cn_ref, ct_ref):
    c = c_ref[...].astype(F32)
    top = (c + pos_ref[0]).astype(BF16)
    bot = (c + pos_ref[1]).astype(BF16)
    a = jnp.dot(top, w1_ref[0], preferred_element_type=F32)
    bm = jnp.dot(bot, w1_ref[1], preferred_element_type=F32)
    n = a.shape[0]
    pre = a + pltpu.roll(bm, n - 1, 0)
    hid = jax.nn.gelu(pre).astype(BF16)
    cn_ref[...] = jnp.dot(hid, w2_ref[...], preferred_element_type=F32).astype(cn_ref.dtype)
    ct_ref[...] = lax.dot_general(w2t_ref[...], hid, (((1,), (1,)), ((), ())),
                                  preferred_element_type=F32).astype(ct_ref.dtype)


def _compress(chunks, pos, w1, w2, w2t):
    _, g, b, ncp, width = chunks.shape
    hidden = w1.shape[-1]
    return pl.pallas_call(
        _compress_kernel,
        grid=(2, g, b),
        in_specs=[
            pl.BlockSpec((None, None, None, ncp, width), lambda t, g, b: (t, g, b, 0, 0)),
            pl.BlockSpec((None, 2, 1, width), lambda t, g, b: (t, 0, 0, 0)),
            pl.BlockSpec((None, 2, width, hidden), lambda t, g, b: (t, 0, 0, 0)),
            pl.BlockSpec((None, hidden, HEAD_DIM), lambda t, g, b: (t, 0, 0)),
            pl.BlockSpec((None, HEAD_DIM, hidden), lambda t, g, b: (t, 0, 0)),
        ],
        out_specs=[
            pl.BlockSpec((None, None, None, ncp, HEAD_DIM), lambda t, g, b: (t, g, b, 0, 0)),
            pl.BlockSpec((None, None, None, HEAD_DIM, ncp), lambda t, g, b: (t, g, b, 0, 0)),
        ],
        out_shape=[jax.ShapeDtypeStruct((2, g, b, ncp, HEAD_DIM), BF16),
                   jax.ShapeDtypeStruct((2, g, b, HEAD_DIM, ncp), BF16)],
        compiler_params=_params(3),
        name="nsa_compress",
    )(chunks, pos, w1, w2, w2t)


def _cmp_kernel(q_ref, kc_ref, vct_ref, vb_ref, ov_ref, oc_ref, sel_ref, bias_ref, skew_ref, *, nq):
    b, i = pl.program_id(1), pl.program_id(2)

    @pl.when((b == 0) & (i == 0))
    def _():
        _build_cmp_bias_table(vb_ref, bias_ref, skew_ref, nq)

    ncp = kc_ref.shape[0]
    nsel = sel_ref.shape[0]
    n_tiles = ncp // BLOCK_Q
    q_t = _gather_heads_to_lanes(q_ref)
    width = SUBTILES * BLOCK_Q
    eye = _identity_tile()

    def bias_tile(t):
        rows = [pl.multiple_of((nq - 1 - (SUBTILES * i + u)) * 8 + t * BLOCK_Q, 8) for u in range(SUBTILES)]
        return jnp.concatenate([bias_ref[pl.ds(r, BLOCK_Q), :].astype(BF16) for r in rows], axis=1)

    s_tiles = [_biased_scores(eye, kc_ref[t * BLOCK_Q:(t + 1) * BLOCK_Q, :], bias_tile(t), q_t)
               for t in range(n_tiles)]
    q_pos = i * width + lax.broadcasted_iota(jnp.int32, (1, width), 1)
    sees_any = q_pos >= CMP_BLOCK - 1
    psum = [[jnp.zeros((BLOCK_Q, BLOCK_Q), F32) for _ in range(n_tiles)] for _ in range(SUBTILES)]
    probs = [[] for _ in range(n_tiles)]
    for u in range(SUBTILES):
        for h in range(HEADS_PER_GROUP):
            lanes = slice((u * HEADS_PER_GROUP + h) * BLOCK_Q, (u * HEADS_PER_GROUP + h + 1) * BLOCK_Q)
            cols = [s[:, lanes] for s in s_tiles]
            m = functools.reduce(jnp.maximum, [jnp.max(c, axis=0, keepdims=True) for c in cols])
            ps = [jnp.exp(c - m) for c in cols]
            l = functools.reduce(jnp.add, [jnp.sum(p, axis=0, keepdims=True) for p in ps])
            inv = jnp.where(sees_any[:, u * BLOCK_Q:(u + 1) * BLOCK_Q], 1.0 / l, 0.0)
            for t in range(n_tiles):
                p = ps[t] * inv
                psum[u][t] = psum[u][t] + p
                probs[t].append(p.astype(BF16))
    p_all = jnp.concatenate([jnp.concatenate(pt, axis=1) for pt in probs], axis=0)
    o_c = jnp.dot(vct_ref[...], p_all, preferred_element_type=F32).astype(oc_ref.dtype)
    for u in range(SUBTILES):
        oc_ref[u] = o_c[:, u * GROUP_LANES:(u + 1) * GROUP_LANES]

    psum = jnp.concatenate([jnp.concatenate(pu, axis=0) for pu in psum], axis=1)
    hi = psum.astype(BF16)
    lo = (psum - hi.astype(F32)).astype(BF16)
    ov = ov_ref[...]
    score = jnp.dot(ov, hi, preferred_element_type=F32) + jnp.dot(ov, lo, preferred_element_type=F32)

    j_io = lax.broadcasted_iota(jnp.int32, (nsel, width), 0)
    qpos = i * width + lax.broadcasted_iota(jnp.int32, (nsel, width), 1)
    causal = j_io * SEL_BLOCK <= qpos
    back = qpos // SEL_BLOCK - j_io
    forced = (j_io == 0) | ((back >= 0) & (back < SEL_LOCAL))
    score = jnp.where(causal, jnp.where(forced, FORCE_SCORE, score), -1.0)
    slab_rows = lax.broadcasted_iota(jnp.int32, (8, width), 0)
    slabs = [score[8 * g:8 * (g + 1), :] for g in range(nsel // 8)]
    ranks = [jnp.zeros((8, width), F32) for _ in slabs]
    for r in range(nsel):
        row = jnp.broadcast_to(score[r:r + 1, :], (8, width))
        for g, slab in enumerate(slabs):
            if g > r // 8:
                ahead = row >= slab
            elif g < r // 8:
                ahead = row > slab
            else:
                ranks[g] = ranks[g] + jnp.where(slab_rows > r % 8, jnp.where(row >= slab, 1.0, 0.0),
                                                jnp.where(row > slab, 1.0, 0.0))
                continue
            ranks[g] = ranks[g] + jnp.where(ahead, 1.0, 0.0)
    rank = jnp.concatenate(ranks, axis=0)
    sel_ref[...] = jnp.where((rank < min(SEL_TOPK, nsel)) & causal, 1.0, 0.0).astype(sel_ref.dtype)


def _cmp_attention(h_t, cn, ct, vb, overlap_t, batch, seq):
    nq = seq // BLOCK_Q
    ncp = seq // CMP_STRIDE
    nsel = seq // SEL_BLOCK
    q_blk0 = SWA_HEADS * HEAD_DIM // GROUP_COLS
    nblk = nq // SUBTILES
    width = SUBTILES * BLOCK_Q
    return pl.pallas_call(
        functools.partial(_cmp_kernel, nq=nq),
        grid=(GROUPS, batch, nblk),
        in_specs=[
            pl.BlockSpec((GROUP_COLS, width), lambda g, b, i: (q_blk0 + g, b * nblk + i)),
            pl.BlockSpec((None, None, None, ncp, HEAD_DIM), lambda g, b, i: (0, g, b, 0, 0)),
            pl.BlockSpec((None, None, None, HEAD_DIM, ncp), lambda g, b, i: (1, g, b, 0, 0)),
            pl.BlockSpec((None, HEADS_PER_GROUP, nq + 2, BLOCK_Q), lambda g, b, i: (g, 0, 0, 0)),
            pl.BlockSpec((nsel, ncp), lambda g, b, i: (0, 0)),
        ],
        out_specs=[
            pl.BlockSpec((None, None, SUBTILES, HEAD_DIM, GROUP_LANES), lambda g, b, i: (b, g, i, 0, 0)),
            pl.BlockSpec((None, None, nsel, width), lambda g, b, i: (b, g, 0, i)),
        ],
        out_shape=[jax.ShapeDtypeStruct((batch, GROUPS, nq, HEAD_DIM, GROUP_LANES), BF16),
                   jax.ShapeDtypeStruct((batch, GROUPS, nsel, seq), F32)],
        scratch_shapes=[pltpu.VMEM(((2 * nq - 1) * 8, GROUP_LANES), F32), pltpu.VMEM((BLOCK_Q, GROUP_LANES), F32)],
        compiler_params=_params(3),
        name="nsa_cmp_select",
    )(h_t, cn, ct, vb, overlap_t)


N_WIN_PAIRS = -(-(NSA_WINDOW - 1) // (SUBTILES * BLOCK_Q)) + 1
N_WIN_SLOTS = SUBTILES * N_WIN_PAIRS + 1
SEL_ROWS = 16


def _selwin_kernel(q_ref, ksel_ref, kwin_ref, vsel_ref, vwin_ref, sel_ref, oc_ref, gate_ref, vbs_ref, vbw_ref,
                   o_ref, tsel_ref, twin_ref, s_scr, p_scr, acc_scr, *, nq):
    b, blk = pl.program_id(1), pl.program_id(2)

    @pl.when((b == 0) & (blk == 0))
    def _():
        _build_skew_table(vbs_ref, tsel_ref, nq + 1)
        _build_skew_table(vbw_ref, twin_ref, N_WIN_SLOTS)

    q_t = _gather_heads_to_lanes(q_ref)
    lanes = q_t.shape[1]
    eye = _identity_tile()
    pair = SUBTILES * BLOCK_Q
    init = (jnp.full((1, lanes), NEG_INF, F32), jnp.zeros((V_AUG_ROWS, lanes), F32))
    blocks_per_tile = BLOCK_Q // SEL_BLOCK

    def pair_of_step(n):
        return jnp.clip(blk - n, 0, blk)

    def bias_tile(tbl_ref, n, j):
        live = (n >= 0) & (n <= blk)
        return jnp.concatenate([tbl_ref[jnp.where(live, 2 * n + u - j + 1, 0)] for u in range(SUBTILES)], axis=1)

    def sel_scores(n):
        p = pair_of_step(n)
        neg = jnp.where(sel_ref[p] > 0.5, 0.0, NEG_INF)
        s_tiles = []
        for j in range(SUBTILES):
            k_t = ksel_ref[pl.ds(pl.multiple_of(p * pair + j * BLOCK_Q, BLOCK_Q), BLOCK_Q), :]
            rows = neg[j * blocks_per_tile:(j + 1) * blocks_per_tile]
            rows = jnp.concatenate([rows[:, u * BLOCK_Q:(u + 1) * BLOCK_Q]
                                    for u in range(SUBTILES) for _ in range(HEADS_PER_GROUP)], axis=1)
            rows = jnp.concatenate([rows, jnp.zeros((SEL_ROWS - blocks_per_tile, lanes), F32)], axis=0)
            q_ext = jnp.concatenate([q_t, rows.astype(BF16)], axis=0)
            s_tiles.append(_biased_scores(eye, k_t, bias_tile(tsel_ref, n, j), q_ext))
        return s_tiles

    def win_scores(n):
        p = pair_of_step(n)
        s_tiles = []
        for j in range(SUBTILES):
            k_t = kwin_ref[pl.ds(pl.multiple_of(p * pair + j * BLOCK_Q, BLOCK_Q), BLOCK_Q), :]
            s_tiles.append(_biased_scores(eye, k_t, bias_tile(twin_ref, n, j), q_t))
        return s_tiles

    def sel_pipeline(fill):
        return _pipelined_attention(blk + 1, sel_scores, lambda n: vsel_ref[pair_of_step(n)], init,
                                    s_scr, p_scr, acc_scr, fill=fill)

    m_w, acc_w = init
    s_w = [win_scores(n) for n in range(min(2, N_WIN_PAIRS))]
    for n in range(N_WIN_PAIRS):
        m_w, alpha, p = _softmax_probs(m_w, s_w[n])
        if n + 2 < N_WIN_PAIRS:
            s_w.append(win_scores(n + 2))
        if n == max(N_WIN_PAIRS - 2, 0):
            sel_pipeline(fill=True)
        acc_w = alpha * acc_w + jnp.dot(vwin_ref[pair_of_step(n)], p, preferred_element_type=F32)

    _, acc_s = sel_pipeline(fill=False)

    def gate_row(branch):
        g = jnp.concatenate([gate_ref[branch, h:h + 1, u * BLOCK_Q:(u + 1) * BLOCK_Q]
                             for u in range(SUBTILES) for h in range(HEADS_PER_GROUP)], axis=1)
        return jax.nn.sigmoid(g.astype(F32))

    o_c = jnp.concatenate([oc_ref[u] for u in range(SUBTILES)], axis=1).astype(F32)
    out = gate_row(0) * o_c + gate_row(1) * _normalized(acc_s) + gate_row(2) * _normalized(acc_w)
    for u in range(SUBTILES):
        o_ref[u * BLOCK_Q:(u + 1) * BLOCK_Q, :] = _heads_to_columns(
            out[:, u * GROUP_LANES:(u + 1) * GROUP_LANES]).astype(o_ref.dtype)


def _selwin_attention(h_t, ksel_aug, kk, vsel_aug, vwin_aug, sel, oc_t, gates_t, vb_sel, vb_win, batch, seq):
    nq = seq // BLOCK_Q
    nblk = nq // SUBTILES
    tokens = batch * seq
    q_blk0 = SWA_HEADS * HEAD_DIM // GROUP_COLS
    pair = SUBTILES * BLOCK_Q
    lanes = SUBTILES * GROUP_LANES
    return pl.pallas_call(
        functools.partial(_selwin_kernel, nq=nq),
        grid=(GROUPS, batch, nblk),
        in_specs=[
            pl.BlockSpec((GROUP_COLS, pair), lambda g, b, i: (q_blk0 + g, b * nblk + i)),
            pl.BlockSpec((None, seq, HEAD_DIM + SEL_ROWS), lambda g, b, i: (g, b, 0)),
            pl.BlockSpec((None, None, seq, HEAD_DIM), lambda g, b, i: (3, g, b, 0)),
            pl.BlockSpec((None, None, nblk, V_AUG_ROWS, pair), lambda g, b, i: (g, b, 0, 0, 0)),
            pl.BlockSpec((None, None, nblk, V_AUG_ROWS, pair), lambda g, b, i: (g, b, 0, 0, 0)),
            pl.BlockSpec((None, None, nblk, pair // SEL_BLOCK, pair), lambda g, b, i: (b, g, 0, 0, i)),
            pl.BlockSpec((None, None, SUBTILES, HEAD_DIM, GROUP_LANES), lambda g, b, i: (b, g, i, 0, 0)),
            pl.BlockSpec((None, 3, HEADS_PER_GROUP, pair), lambda g, b, i: (g, 0, 0, b * nblk + i)),
            pl.BlockSpec((None, HEADS_PER_GROUP, nq + 2, BLOCK_Q), lambda g, b, i: (g, 0, 0, 0)),
            pl.BlockSpec((None, HEADS_PER_GROUP, N_WIN_SLOTS + 1, BLOCK_Q), lambda g, b, i: (g, 0, 0, 0)),
        ],
        out_specs=pl.BlockSpec((pair, GROUP_COLS), lambda g, b, i: (b * nblk + i, g)),
        out_shape=jax.ShapeDtypeStruct((tokens, NSA_HEADS * HEAD_DIM), BF16),
        scratch_shapes=[pltpu.VMEM((nq + 1, BLOCK_Q, GROUP_LANES), BF16),
                        pltpu.VMEM((N_WIN_SLOTS, BLOCK_Q, GROUP_LANES), BF16),
                        pltpu.VMEM((SUBTILES, BLOCK_Q, lanes), F32),
                        pltpu.VMEM((pair, lanes), BF16),
                        pltpu.VMEM((V_AUG_ROWS, lanes), F32)],
        compiler_params=_params(3),
        name="nsa_sel_win",
    )(h_t, ksel_aug, kk, vsel_aug, vwin_aug, sel, oc_t, gates_t, vb_sel, vb_win)


def _layer_norm(y, g_ref, b_ref):
    mu = jnp.mean(y, axis=-1, keepdims=True)
    yc = y - mu
    var = jnp.mean(yc * yc, axis=-1, keepdims=True)
    return yc * lax.rsqrt(var + LN_EPS) * g_ref[...] + b_ref[...]


def _merge_kernel(x_ref, ya_ref, yb_ref, wga_ref, wgb_ref, wa_ref, wb_ref, o_ref):
    x = x_ref[...]
    ga = jax.nn.sigmoid(jnp.dot(x, wga_ref[...], preferred_element_type=F32))
    gb = jax.nn.sigmoid(jnp.dot(x, wgb_ref[...], preferred_element_type=F32))
    a = jnp.dot(ya_ref[...], wa_ref[...], preferred_element_type=F32)
    bb = jnp.dot(yb_ref[...], wb_ref[...], preferred_element_type=F32)
    o_ref[...] = (ga * a + gb * bb).astype(o_ref.dtype)


def _merge(x_bf, y_a, y_b, w_ga, w_gb, w_a, w_b, *, tm, tn):
    tokens, d = x_bf.shape
    ya_cols, yb_cols = y_a.shape[1], y_b.shape[1]
    return pl.pallas_call(
        _merge_kernel,
        grid=(tokens // tm, d // tn),
        in_specs=[
            pl.BlockSpec((tm, d), lambda i, j: (i, 0)),
            pl.BlockSpec((tm, ya_cols), lambda i, j: (i, 0)),
            pl.BlockSpec((tm, yb_cols), lambda i, j: (i, 0)),
            pl.BlockSpec((d, tn), lambda i, j: (0, j)),
            pl.BlockSpec((d, tn), lambda i, j: (0, j)),
            pl.BlockSpec((ya_cols, tn), lambda i, j: (0, j)),
            pl.BlockSpec((yb_cols, tn), lambda i, j: (0, j)),
        ],
        out_specs=pl.BlockSpec((tm, tn), lambda i, j: (i, j)),
        out_shape=jax.ShapeDtypeStruct((tokens, d), BF16),
        compiler_params=_params(2),
        name="branch_merge",
    )(x_bf, y_a, y_b, w_ga, w_gb, w_a, w_b)


def _mix_ln_kernel(m_ref, x_ref, w_ref, g_ref, b_ref, o_ref):
    mix = jnp.dot(m_ref[...], w_ref[...], preferred_element_type=F32)
    o_ref[...] = _layer_norm(DN_ALPHA * x_ref[...] + mix, g_ref, b_ref)


def _mix_ln(merged, x, w_mix, ln_g, ln_b, *, tm):
    tokens, d = x.shape
    return pl.pallas_call(
        _mix_ln_kernel,
        grid=(tokens // tm,),
        in_specs=[
            pl.BlockSpec((tm, d), lambda i: (i, 0)),
            pl.BlockSpec((tm, d), lambda i: (i, 0)),
            pl.BlockSpec((d, d), lambda i: (0, 0)),
            pl.BlockSpec((1, d), lambda i: (0, 0)),
            pl.BlockSpec((1, d), lambda i: (0, 0)),
        ],
        out_specs=pl.BlockSpec((tm, d), lambda i: (i, 0)),
        out_shape=jax.ShapeDtypeStruct((tokens, d), F32),
        compiler_params=_params(1),
        name="mix_out_ln1",
    )(merged, x, w_mix, ln_g, ln_b)


def _xa_kernel(h_ref, wq_ref, k_ref, v_ref, wo_ref, g_ref, b_ref, o_ref):
    h = h_ref[...]
    q = jnp.dot(h.astype(BF16), wq_ref[...], preferred_element_type=F32) * (XA_HEAD_DIM ** -0.5)
    q = q.astype(BF16)
    outs = []
    for hd in range(XA_HEADS):
        cols = slice(hd * XA_HEAD_DIM, (hd + 1) * XA_HEAD_DIM)
        s = lax.dot_general(q[:, cols], k_ref[:, cols], (((1,), (1,)), ((), ())), preferred_element_type=F32)
        p = jnp.exp(s - jnp.max(s, axis=-1, keepdims=True))
        l = jnp.sum(p, axis=-1, keepdims=True)
        o = jnp.dot(p.astype(BF16), v_ref[:, cols], preferred_element_type=F32)
        outs.append(o * (1.0 / l))
    o = jnp.concatenate(outs, axis=1).astype(BF16)
    xa = jnp.dot(o, wo_ref[...], preferred_element_type=F32)
    o_ref[...] = _layer_norm(DN_ALPHA * h + xa, g_ref, b_ref)


def _cross_attention_ln(h, kv_mem, w_q, w_o, ln_g, ln_b, batch, seq, *, tm):
    tokens, d = h.shape
    mem_len = kv_mem.shape[0] // batch
    xa_dim = XA_HEADS * XA_HEAD_DIM
    nt = seq // tm
    return pl.pallas_call(
        _xa_kernel,
        grid=(batch, nt),
        in_specs=[
            pl.BlockSpec((tm, d), lambda b, i: (b * nt + i, 0)),
            pl.BlockSpec((d, xa_dim), lambda b, i: (0, 0)),
            pl.BlockSpec((mem_len, xa_dim), lambda b, i: (b, 0)),
            pl.BlockSpec((mem_len, xa_dim), lambda b, i: (b, 1)),
            pl.BlockSpec((xa_dim, d), lambda b, i: (0, 0)),
            pl.BlockSpec((1, d), lambda b, i: (0, 0)),
            pl.BlockSpec((1, d), lambda b, i: (0, 0)),
        ],
        out_specs=pl.BlockSpec((tm, d), lambda b, i: (b * nt + i, 0)),
        out_shape=jax.ShapeDtypeStruct((tokens, d), F32),
        compiler_params=_params(2),
        name="cross_attention_ln2",
    )(h, w_q, kv_mem, kv_mem, w_o, ln_g, ln_b)


def _mlp_kernel(h_ref, w1_ref, w2_ref, g_ref, b_ref, o_ref, hb_ref):
    j = pl.program_id(1)

    @pl.when(j == 0)
    def _():
        hb_ref[...] = h_ref[...].astype(BF16)
        o_ref[...] = jnp.zeros_like(o_ref)

    u = jnp.dot(hb_ref[...], w1_ref[...], preferred_element_type=F32)
    u = jnp.square(jnp.maximum(u, 0.0)).astype(BF16)
    o_ref[...] += jnp.dot(u, w2_ref[...], preferred_element_type=F32)

    @pl.when(j == pl.num_programs(1) - 1)
    def _():
        o_ref[...] = _layer_norm(DN_ALPHA * h_ref[...] + o_ref[...], g_ref, b_ref)


def _mlp_ln(h, w1, w2, ln_g, ln_b, *, tm, tf):
    tokens, d = h.shape
    d_ff = w1.shape[1]
    return pl.pallas_call(
        _mlp_kernel,
        grid=(tokens // tm, d_ff // tf),
        in_specs=[
            pl.BlockSpec((tm, d), lambda i, j: (i, 0)),
            pl.BlockSpec((d, tf), lambda i, j: (0, j)),
            pl.BlockSpec((tf, d), lambda i, j: (j, 0)),
            pl.BlockSpec((1, d), lambda i, j: (0, 0)),
            pl.BlockSpec((1, d), lambda i, j: (0, 0)),
        ],
        out_specs=pl.BlockSpec((tm, d), lambda i, j: (i, 0)),
        out_shape=jax.ShapeDtypeStruct((tokens, d), F32),
        scratch_shapes=[pltpu.VMEM((tm, d), BF16)],
        compiler_params=_params(2),
        name="mlp_ln3",
    )(h, w1, w2, ln_g, ln_b)


def _overlap_t(seq):
    ncp, nsel = seq // CMP_STRIDE, seq // SEL_BLOCK
    c_start = np.arange(ncp)[None, :] * CMP_STRIDE
    s_start = np.arange(nsel)[:, None] * SEL_BLOCK
    ov = (c_start < s_start + SEL_BLOCK) & (c_start + CMP_BLOCK > s_start) & (np.arange(ncp)[None, :] < ncp - 1)
    return jnp.asarray(ov, BF16)


def kernel(x, mem, w_in, attn_sinks, rel_bias_table, cmp_pos_k, cmp_w1_k, cmp_w2_k, cmp_pos_v, cmp_w1_v, cmp_w2_v,
           w_branch_swa, w_branch_nsa, w_mix_out, ln1_g, ln1_b, xa_w_q, xa_w_kv, xa_w_o, ln2_g, ln2_b,
           mlp_w1, mlp_w2, ln3_g, ln3_b):
    batch, seq, d = x.shape
    tokens = batch * seq
    nq = seq // BLOCK_Q
    assert w_in.shape[0] == 1, "one layer"
    assert seq % (8 * BLOCK_Q) == 0 and d % 512 == 0

    w = w_in[0]
    sizes = (SWA_HEADS * HEAD_DIM, 128, 128, NSA_HEADS * HEAD_DIM, 128, 128, 128, 128, 128, 128, 3 * NSA_HEADS, d, d)
    offs = np.concatenate([[0], np.cumsum(sizes)])
    (w_qa, w_ka, w_va, w_qb, w_kc, w_vc, w_ks, w_vs, w_kw, w_vw, w_gn, w_ga, w_gb) = [
        w[:, offs[n]:offs[n + 1]] for n in range(len(sizes))]
    w_rows = jnp.concatenate([w_ka, w_kc, w_ks, w_kw, w_vc], axis=1).astype(BF16)
    w_gn_pad = jnp.pad(w_gn, ((0, 0), (0, 128 - w_gn.shape[1])))
    w_cols_t = jnp.concatenate([w_qa, w_qb, w_va, w_vs, w_vw, w_gn_pad], axis=1).T.astype(BF16)

    x2 = x.reshape(tokens, d)
    x_bf = x2.astype(BF16)

    h_rows = _matmul(x_bf, w_rows, nt=False, tm=min(2048, tokens), tn=w_rows.shape[1])
    h_t = _matmul(w_cols_t, x_bf, nt=True, tm=1280, tn=min(1024, tokens))

    kk = h_rows[:, :512].reshape(tokens, 4, GROUPS, HEAD_DIM).transpose(1, 2, 0, 3)
    v_cmp = h_rows[:, 512:].reshape(tokens, GROUPS, HEAD_DIM).transpose(1, 0, 2)

    def value_tiles(row0, width):
        v = h_t[row0:row0 + 128].reshape(GROUPS, HEAD_DIM, batch, seq // width, width).transpose(0, 2, 3, 1, 4)
        ones = jnp.ones(v.shape[:3] + (V_AUG_ROWS - HEAD_DIM, width), BF16)
        return jnp.concatenate([v, ones], axis=3)

    vswa_aug = value_tiles(2048, BLOCK_Q)
    vsel_aug, vwin_aug = value_tiles(2176, 2 * BLOCK_Q), value_tiles(2304, 2 * BLOCK_Q)
    gates_t = h_t[2432:2432 + 3 * NSA_HEADS].reshape(GROUPS, HEADS_PER_GROUP, 3, tokens).transpose(0, 2, 1, 3)
    block_of_key = (jnp.arange(tokens) % BLOCK_Q) // SEL_BLOCK
    onehot = (block_of_key[:, None] == jnp.arange(SEL_ROWS)[None, :]).astype(BF16)
    ksel_aug = jnp.concatenate([kk[2], jnp.broadcast_to(onehot, (GROUPS, tokens, SEL_ROWS))], axis=2)

    by_dist = rel_bias_table[_rel_bucket(jnp.arange(seq))].astype(F32).T
    vb_swa = _bias_by_distance(by_dist[:SWA_HEADS], 4, window=SWA_WINDOW)
    vb_sel = _bias_by_distance(by_dist[SWA_HEADS:], nq + 2)
    vb_win = _bias_by_distance(by_dist[SWA_HEADS:], N_WIN_SLOTS + 1, window=NSA_WINDOW)
    vb_cmp = vb_sel
    sink_rows = jnp.repeat(attn_sinks[0].astype(F32), BLOCK_Q).reshape(GROUPS, 1, GROUP_LANES)

    y_a = _swa_attention(h_t, kk, vswa_aug, sink_rows, vb_swa, batch, seq)

    ncp = seq // CMP_STRIDE
    chunk_w = CMP_STRIDE * HEAD_DIM
    chunks = jnp.stack([kk[1], v_cmp]).reshape(2, GROUPS, batch, ncp, chunk_w)
    pos = jnp.stack([cmp_pos_k[0], cmp_pos_v[0]]).astype(F32).reshape(2, 2, 1, chunk_w)
    w1 = jnp.stack([cmp_w1_k[0], cmp_w1_v[0]]).astype(BF16).reshape(2, 2, chunk_w, -1)
    w2 = jnp.stack([cmp_w2_k[0], cmp_w2_v[0]]).astype(BF16)
    cn, ct = _compress(chunks, pos, w1, w2, w2.transpose(0, 2, 1))
    oc_t, sel = _cmp_attention(h_t, cn, ct, vb_cmp, _overlap_t(seq), batch, seq)
    sel = sel.reshape(batch, GROUPS, nq // 2, 4, seq)
    y_b = _selwin_attention(h_t, ksel_aug, kk, vsel_aug, vwin_aug, sel, oc_t, gates_t, vb_sel, vb_win, batch, seq)

    merged = _merge(x_bf, y_a, y_b, w_ga.astype(BF16), w_gb.astype(BF16),
                    w_branch_swa[0].astype(BF16), w_branch_nsa[0].astype(BF16), tm=min(1024, tokens), tn=512)
    h1 = _mix_ln(merged, x2, w_mix_out[0].astype(BF16), ln1_g, ln1_b, tm=512)

    mem_bf = mem.reshape(-1, d).astype(BF16)
    kv_mem = _matmul(mem_bf, xa_w_kv[0].astype(BF16), nt=False, tm=mem_bf.shape[0], tn=512)
    h2 = _cross_attention_ln(h1, kv_mem, xa_w_q[0].astype(BF16), xa_w_o[0].astype(BF16), ln2_g, ln2_b,
                             batch, seq, tm=512)

    h3 = _mlp_ln(h2, mlp_w1[0].astype(BF16), mlp_w2[0].astype(BF16), ln3_g, ln3_b, tm=512, tf=1024)
    return h3.reshape(batch, seq, d)
```

```python
import functools
import math

import numpy as np
import jax
import jax.numpy as jnp
from jax import lax
from jax.experimental import pallas as pl
from jax.experimental.pallas import tpu as pltpu

F32 = jnp.float32
BF16 = jnp.bfloat16

HEAD_DIM = 64
BLOCK_Q = 128
SWA_HEADS = 16
SWA_KV_HEADS = 2
SWA_WINDOW = 128
NSA_HEADS = 16
NSA_KV_HEADS = 2
CMP_BLOCK = 32
CMP_STRIDE = 16
SEL_BLOCK = 64
SEL_TOPK = 16
SEL_LOCAL = 2
NSA_WINDOW = 512
REL_BUCKETS = 32
REL_MAX_DIST = 4096
XA_HEADS = 4
XA_HEAD_DIM = 128
DEPTH = 1
DN_ALPHA = (2.0 * DEPTH) ** 0.25
LN_EPS = 1e-5
NEG_INF = -1e30
FORCE_SCORE = 1e4

GROUPS = 2
HEADS_PER_GROUP = 8
GROUP_LANES = HEADS_PER_GROUP * BLOCK_Q
GROUP_COLS = HEADS_PER_GROUP * HEAD_DIM
SUBTILES = 2

V7X_VMEM_LIMIT_BYTES = 56 * 1024 * 1024


def _params(n_axes):
    return pltpu.CompilerParams(dimension_semantics=("arbitrary",) * n_axes,
                                vmem_limit_bytes=V7X_VMEM_LIMIT_BYTES)


def _mm_kernel(a_ref, b_ref, o_ref, *, nt):
    if nt:
        out = lax.dot_general(a_ref[...], b_ref[...], (((1,), (1,)), ((), ())), preferred_element_type=F32)
    else:
        out = jnp.dot(a_ref[...], b_ref[...], preferred_element_type=F32)
    o_ref[...] = out.astype(o_ref.dtype)


def _matmul(a, b, *, nt, tm, tn, out_dtype=BF16):
    m, k = a.shape
    n = b.shape[0] if nt else b.shape[1]
    b_spec = pl.BlockSpec((tn, k), lambda i, j: (j, 0)) if nt else pl.BlockSpec((k, tn), lambda i, j: (0, j))
    return pl.pallas_call(
        functools.partial(_mm_kernel, nt=nt),
        grid=(m // tm, n // tn),
        in_specs=[pl.BlockSpec((tm, k), lambda i, j: (i, 0)), b_spec],
        out_specs=pl.BlockSpec((tm, tn), lambda i, j: (i, j)),
        out_shape=jax.ShapeDtypeStruct((m, n), out_dtype),
        compiler_params=_params(2),
        name="proj_nt" if nt else "proj_nn",
    )(a, b)


def _rel_bucket(dist):
    exact = REL_BUCKETS // 2
    d = jnp.maximum(dist, 0)
    log_ratio = jnp.log(jnp.maximum(d, 1).astype(F32) / exact) / math.log(REL_MAX_DIST / exact)
    large = jnp.minimum(exact + (log_ratio * (REL_BUCKETS - exact)).astype(jnp.int32), REL_BUCKETS - 1)
    return jnp.where(d < exact, d, large)


def _bias_by_distance(by_dist, rows, *, window=None):
    heads, seq = by_dist.shape
    if window is not None:
        by_dist = jnp.where(jnp.arange(seq) < window, by_dist, NEG_INF)
    padded = jnp.pad(by_dist, ((0, 0), (2 * BLOCK_Q, 0)), constant_values=NEG_INF)[:, :rows * BLOCK_Q]
    return padded.reshape(heads // HEADS_PER_GROUP, HEADS_PER_GROUP, rows, BLOCK_Q)


def _build_skew_table(vb_ref, tbl_ref, n_tiles):
    def body(d, carry):
        for h in range(HEADS_PER_GROUP):
            lo = jnp.broadcast_to(vb_ref[h, pl.ds(d, 1), :], (BLOCK_Q, BLOCK_Q))
            hi = jnp.broadcast_to(vb_ref[h, pl.ds(d + 1, 1), :], (BLOCK_Q, BLOCK_Q))
            y = pltpu.roll(jnp.concatenate([lo, hi], axis=1), 0, 1, stride=1, stride_axis=0)
            tbl_ref[d, :, h * BLOCK_Q:(h + 1) * BLOCK_Q] = y[:, BLOCK_Q:].astype(tbl_ref.dtype)
        return carry

    lax.fori_loop(0, n_tiles, body, 0)


def _build_cmp_bias_table(vb_ref, tbl_ref, skew_ref, nq):
    per_tile = BLOCK_Q // CMP_STRIDE
    assert per_tile == 8 and CMP_BLOCK - 1 + CMP_STRIDE * (per_tile - 1) - BLOCK_Q == CMP_STRIDE - 1
    tbl_ref[pl.ds(nq * 8, (nq - 1) * 8), :] = jnp.full(((nq - 1) * 8, GROUP_LANES), NEG_INF, F32)

    def body(d, last_row):
        for h in range(HEADS_PER_GROUP):
            lo = jnp.broadcast_to(vb_ref[h, pl.ds(d + 1, 1), :], (BLOCK_Q, BLOCK_Q))
            hi = jnp.broadcast_to(vb_ref[h, pl.ds(d + 2, 1), :], (BLOCK_Q, BLOCK_Q))
            y = pltpu.roll(jnp.concatenate([lo, hi], axis=1), 0, 1, stride=1, stride_axis=0)
            skew_ref[:, h * BLOCK_Q:(h + 1) * BLOCK_Q] = y[:, BLOCK_Q:]
        offsets = [CMP_BLOCK - 1 + CMP_STRIDE * c for c in range(per_tile - 1)]
        rows = [skew_ref[k:k + 1, :] for k in offsets] + [last_row]
        tbl_ref[pl.ds(pl.multiple_of((nq - 1 - d) * 8, 8), 8), :] = jnp.concatenate(rows, axis=0)
        return skew_ref[CMP_STRIDE - 1:CMP_STRIDE, :]

    lax.fori_loop(0, nq, body, jnp.full((1, GROUP_LANES), NEG_INF, F32))


def _gather_heads_to_lanes(q_ref):
    q = jnp.concatenate([q_ref[h * HEAD_DIM:(h + 1) * HEAD_DIM, u * BLOCK_Q:(u + 1) * BLOCK_Q]
                         for u in range(q_ref.shape[1] // BLOCK_Q) for h in range(HEADS_PER_GROUP)], axis=1)
    return q * jnp.asarray(HEAD_DIM ** -0.5, q.dtype)


def _heads_to_columns(o_t):
    stacked = jnp.concatenate([o_t[:, h * BLOCK_Q:(h + 1) * BLOCK_Q] for h in range(HEADS_PER_GROUP)], axis=0)
    return stacked.T


def _identity_tile():
    r = lax.broadcasted_iota(jnp.int32, (BLOCK_Q, BLOCK_Q), 0)
    c = lax.broadcasted_iota(jnp.int32, (BLOCK_Q, BLOCK_Q), 1)
    return jnp.where(r == c, 1.0, 0.0).astype(BF16)


def _biased_scores(eye, k_t, bias_tile, q_t):
    lhs = jnp.concatenate([eye, k_t], axis=1)
    rhs = jnp.concatenate([bias_tile, q_t], axis=0)
    return jnp.dot(lhs, rhs, preferred_element_type=F32)


def _softmax_probs(m, s_tiles):
    ms, alphas, ps = [], [], [[] for _ in s_tiles]
    for h in range(m.shape[1] // BLOCK_Q):
        lanes = slice(h * BLOCK_Q, (h + 1) * BLOCK_Q)
        cols = [s[:, lanes] for s in s_tiles]
        m_new = m[:, lanes]
        for c in cols:
            m_new = jnp.maximum(m_new, jnp.max(c, axis=0, keepdims=True))
        ms.append(m_new)
        alphas.append(jnp.exp(m[:, lanes] - m_new))
        for j, c in enumerate(cols):
            ps[j].append(jnp.exp((c - m_new).astype(BF16)))
    p = jnp.concatenate([jnp.concatenate(pj, axis=1) for pj in ps], axis=0)
    return jnp.concatenate(ms, axis=1), jnp.concatenate(alphas, axis=1), p


def _softmax_update(carry, s_tiles, v_aug):
    m, acc = carry
    m, alpha, p = _softmax_probs(m, s_tiles)
    return m, alpha * acc + jnp.dot(v_aug, p, preferred_element_type=F32)


def _pipelined_attention(n_steps, scores_fn, values_fn, init, s_scr, p_scr, acc_scr, *, fill):
    m0, acc0 = init

    def put_scores(n):
        for j, s in enumerate(scores_fn(n)):
            s_scr[j] = s

    def fold(alpha, n, p):
        acc_scr[...] = alpha * acc_scr[...] + jnp.dot(values_fn(n), p, preferred_element_type=F32)

    if fill:
        put_scores(0)
        p_scr[...] = jnp.zeros_like(p_scr)
        acc_scr[...] = acc0
        return None

    def body(k, carry):
        m, alpha_prev = carry
        a = 2 * k
        fold(alpha_prev, a - 1, p_scr[...])
        s_b = scores_fn(a + 1)
        m, alpha_a, p_a = _softmax_probs(m, [s_scr.at[j] for j in range(s_scr.shape[0])])
        put_scores(a + 2)
        fold(alpha_a, a, p_a)
        m, alpha_b, p_b = _softmax_probs(m, s_b)
        p_scr[...] = p_b
        return m, alpha_b

    trips = (n_steps + 1) // 2
    m, alpha_last = lax.fori_loop(0, trips, body, (m0, jnp.ones_like(m0)))
    fold(alpha_last, 2 * trips - 1, p_scr[...])
    return m, acc_scr[...]


V_AUG_ROWS = HEAD_DIM + 16


def _normalized(acc):
    return acc[:HEAD_DIM] * (1.0 / acc[HEAD_DIM:HEAD_DIM + 1])


def _swa_kernel(q_ref, kprev_ref, kcur_ref, vprev_ref, vcur_ref, sink_ref, vb_ref, o_ref, tbl_ref):
    b, i = pl.program_id(1), pl.program_id(2)

    @pl.when((b == 0) & (i == 0))
    def _():
        _build_skew_table(vb_ref, tbl_ref, 3)

    eye = _identity_tile()
    acc0 = jnp.where(lax.broadcasted_iota(jnp.int32, (V_AUG_ROWS, GROUP_LANES), 0) < HEAD_DIM, 0.0, 1.0)
    for u in range(SWA_TILES):
        q_t = _gather_heads_to_lanes(q_ref.at[:, u * BLOCK_Q:(u + 1) * BLOCK_Q])
        k_cur, v_cur = kcur_ref[u * BLOCK_Q:(u + 1) * BLOCK_Q, :], vcur_ref[u]
        if u == 0:
            k_prev, v_prev, prev_slot = kprev_ref[...], vprev_ref[...], jnp.where(i > 0, 2, 0)
        else:
            k_prev, v_prev, prev_slot = kcur_ref[(u - 1) * BLOCK_Q:u * BLOCK_Q, :], vcur_ref[u - 1], 2
        s_cur = _biased_scores(eye, k_cur, tbl_ref[1], q_t)
        s_prev = _biased_scores(eye, k_prev, tbl_ref[prev_slot], q_t)
        v2 = jnp.concatenate([v_cur, v_prev], axis=1)
        _, acc = _softmax_update((sink_ref[...], acc0), [s_cur, s_prev], v2)
        o_ref[u * BLOCK_Q:(u + 1) * BLOCK_Q, :] = _heads_to_columns(_normalized(acc)).astype(o_ref.dtype)


SWA_TILES = 4


def _swa_attention(h_t, kk, v_aug, sink_rows, vb, batch, seq):
    nq = seq // BLOCK_Q
    nblk = nq // SWA_TILES
    width = SWA_TILES * BLOCK_Q
    tokens = batch * seq

    def prev_tile(i):
        return jnp.maximum(i * SWA_TILES - 1, 0)

    return pl.pallas_call(
        _swa_kernel,
        grid=(GROUPS, batch, nblk),
        in_specs=[
            pl.BlockSpec((GROUP_COLS, width), lambda g, b, i: (g, b * nblk + i)),
            pl.BlockSpec((None, None, BLOCK_Q, HEAD_DIM), lambda g, b, i: (0, g, b * nq + prev_tile(i), 0)),
            pl.BlockSpec((None, None, width, HEAD_DIM), lambda g, b, i: (0, g, b * nblk + i, 0)),
            pl.BlockSpec((None, None, None, V_AUG_ROWS, BLOCK_Q), lambda g, b, i: (g, b, prev_tile(i), 0, 0)),
            pl.BlockSpec((None, None, SWA_TILES, V_AUG_ROWS, BLOCK_Q), lambda g, b, i: (g, b, i, 0, 0)),
            pl.BlockSpec((None, 1, GROUP_LANES), lambda g, b, i: (g, 0, 0)),
            pl.BlockSpec((None, HEADS_PER_GROUP, 4, BLOCK_Q), lambda g, b, i: (g, 0, 0, 0)),
        ],
        out_specs=pl.BlockSpec((width, GROUP_COLS), lambda g, b, i: (b * nblk + i, g)),
        out_shape=jax.ShapeDtypeStruct((tokens, SWA_HEADS * HEAD_DIM), BF16),
        scratch_shapes=[pltpu.VMEM((3, BLOCK_Q, GROUP_LANES), BF16)],
        compiler_params=_params(3),
        name="swa_attention",
    )(h_t, kk, kk, v_aug, v_aug, sink_rows, vb)


def _compress_kernel(c_ref, pos_ref, w1_ref, w2_ref, w2t_ref, cn_ref, ct_ref):
    c = c_ref[...].astype(F32)
    top = (c + pos_ref[0]).astype(BF16)
    bot = (c + pos_ref[1]).astype(BF16)
    a = jnp.dot(top, w1_ref[0], preferred_element_type=F32)
    bm = jnp.dot(bot, w1_ref[1], preferred_element_type=F32)
    n = a.shape[0]
    pre = a + pltpu.roll(bm, n - 1, 0)
    hid = jax.nn.gelu(pre).astype(BF16)
    cn_ref[...] = jnp.dot(hid, w2_ref[...], preferred_element_type=F32).astype(cn_ref.dtype)
    ct_ref[...] = lax.dot_general(w2t_ref[...], hid, (((1,), (1,)), ((), ())),
                                  preferred_element_type=F32).astype(ct_ref.dtype)


def _compress(chunks, pos, w1, w2, w2t):
    _, g, b, ncp, width = chunks.shape
    hidden = w1.shape[-1]
    return pl.pallas_call(
        _compress_kernel,
        grid=(2, g, b),
        in_specs=[
            pl.BlockSpec((None, None, None, ncp, width), lambda t, g, b: (t, g, b, 0, 0)),
            pl.BlockSpec((None, 2, 1, width), lambda t, g, b: (t, 0, 0, 0)),
            pl.BlockSpec((None, 2, width, hidden), lambda t, g, b: (t, 0, 0, 0)),
            pl.BlockSpec((None, hidden, HEAD_DIM), lambda t, g, b: (t, 0, 0)),
            pl.BlockSpec((None, HEAD_DIM, hidden), lambda t, g, b: (t, 0, 0)),
        ],
        out_specs=[
            pl.BlockSpec((None, None, None, ncp, HEAD_DIM), lambda t, g, b: (t, g, b, 0, 0)),
            pl.BlockSpec((None, None, None, HEAD_DIM, ncp), lambda t, g, b: (t, g, b, 0, 0)),
        ],
        out_shape=[jax.ShapeDtypeStruct((2, g, b, ncp, HEAD_DIM), BF16),
                   jax.ShapeDtypeStruct((2, g, b, HEAD_DIM, ncp), BF16)],
        compiler_params=_params(3),
        name="nsa_compress",
    )(chunks, pos, w1, w2, w2t)


def _cmp_kernel(q_ref, kc_ref, vct_ref, vb_ref, ov_ref, oc_ref, sel_ref, bias_ref, skew_ref, *, nq):
    b, i = pl.program_id(1), pl.program_id(2)

    @pl.when((b == 0) & (i == 0))
    def _():
        _build_cmp_bias_table(vb_ref, bias_ref, skew_ref, nq)

    ncp = kc_ref.shape[0]
    nsel = sel_ref.shape[0]
    n_tiles = ncp // BLOCK_Q
    q_t = _gather_heads_to_lanes(q_ref)
    width = SUBTILES * BLOCK_Q
    eye = _identity_tile()

    def bias_tile(t):
        rows = [pl.multiple_of((nq - 1 - (SUBTILES * i + u)) * 8 + t * BLOCK_Q, 8) for u in range(SUBTILES)]
        return jnp.concatenate([bias_ref[pl.ds(r, BLOCK_Q), :].astype(BF16) for r in rows], axis=1)

    s_tiles = [_biased_scores(eye, kc_ref[t * BLOCK_Q:(t + 1) * BLOCK_Q, :], bias_tile(t), q_t)
               for t in range(n_tiles)]
    q_pos = i * width + lax.broadcasted_iota(jnp.int32, (1, width), 1)
    sees_any = q_pos >= CMP_BLOCK - 1
    psum = [[jnp.zeros((BLOCK_Q, BLOCK_Q), F32) for _ in range(n_tiles)] for _ in range(SUBTILES)]
    probs = [[] for _ in range(n_tiles)]
    for u in range(SUBTILES):
        for h in range(HEADS_PER_GROUP):
            lanes = slice((u * HEADS_PER_GROUP + h) * BLOCK_Q, (u * HEADS_PER_GROUP + h + 1) * BLOCK_Q)
            cols = [s[:, lanes] for s in s_tiles]
            m = functools.reduce(jnp.maximum, [jnp.max(c, axis=0, keepdims=True) for c in cols])
            ps = [jnp.exp(c - m) for c in cols]
            l = functools.reduce(jnp.add, [jnp.sum(p, axis=0, keepdims=True) for p in ps])
            inv = jnp.where(sees_any[:, u * BLOCK_Q:(u + 1) * BLOCK_Q], 1.0 / l, 0.0)
            for t in range(n_tiles):
                p = ps[t] * inv
                psum[u][t] = psum[u][t] + p
                probs[t].append(p.astype(BF16))
    p_all = jnp.concatenate([jnp.concatenate(pt, axis=1) for pt in probs], axis=0)
    o_c = jnp.dot(vct_ref[...], p_all, preferred_element_type=F32).astype(oc_ref.dtype)
    for u in range(SUBTILES):
        oc_ref[u] = o_c[:, u * GROUP_LANES:(u + 1) * GROUP_LANES]

    psum = jnp.concatenate([jnp.concatenate(pu, axis=0) for pu in psum], axis=1)
    hi = psum.astype(BF16)
    lo = (psum - hi.astype(F32)).astype(BF16)
    ov = ov_ref[...]
    score = jnp.dot(ov, hi, preferred_element_type=F32) + jnp.dot(ov, lo, preferred_element_type=F32)

    j_io = lax.broadcasted_iota(jnp.int32, (nsel, width), 0)
    qpos = i * width + lax.broadcasted_iota(jnp.int32, (nsel, width), 1)
    causal = j_io * SEL_BLOCK <= qpos
    back = qpos // SEL_BLOCK - j_io
    forced = (j_io == 0) | ((back >= 0) & (back < SEL_LOCAL))
    score = jnp.where(causal, jnp.where(forced, FORCE_SCORE, score), -1.0)
    slab_rows = lax.broadcasted_iota(jnp.int32, (8, width), 0)
    slabs = [score[8 * g:8 * (g + 1), :] for g in range(nsel // 8)]
    ranks = [jnp.zeros((8, width), F32) for _ in slabs]
    for r in range(nsel):
        row = jnp.broadcast_to(score[r:r + 1, :], (8, width))
        for g, slab in enumerate(slabs):
            if g > r // 8:
                ahead = row >= slab
            elif g < r // 8:
                ahead = row > slab
            else:
                ranks[g] = ranks[g] + jnp.where(slab_rows > r % 8, jnp.where(row >= slab, 1.0, 0.0),
                                                jnp.where(row > slab, 1.0, 0.0))
                continue
            ranks[g] = ranks[g] + jnp.where(ahead, 1.0, 0.0)
    rank = jnp.concatenate(ranks, axis=0)
    sel_ref[...] = jnp.where((rank < min(SEL_TOPK, nsel)) & causal, 1.0, 0.0).astype(sel_ref.dtype)


def _cmp_attention(h_t, cn, ct, vb, overlap_t, batch, seq):
    nq = seq // BLOCK_Q
    ncp = seq // CMP_STRIDE
    nsel = seq // SEL_BLOCK
    q_blk0 = SWA_HEADS * HEAD_DIM // GROUP_COLS
    nblk = nq // SUBTILES
    width = SUBTILES * BLOCK_Q
    return pl.pallas_call(
        functools.partial(_cmp_kernel, nq=nq),
        grid=(GROUPS, batch, nblk),
        in_specs=[
            pl.BlockSpec((GROUP_COLS, width), lambda g, b, i: (q_blk0 + g, b * nblk + i)),
            pl.BlockSpec((None, None, None, ncp, HEAD_DIM), lambda g, b, i: (0, g, b, 0, 0)),
            pl.BlockSpec((None, None, None, HEAD_DIM, ncp), lambda g, b, i: (1, g, b, 0, 0)),
            pl.BlockSpec((None, HEADS_PER_GROUP, nq + 2, BLOCK_Q), lambda g, b, i: (g, 0, 0, 0)),
            pl.BlockSpec((nsel, ncp), lambda g, b, i: (0, 0)),
        ],
        out_specs=[
            pl.BlockSpec((None, None, SUBTILES, HEAD_DIM, GROUP_LANES), lambda g, b, i: (b, g, i, 0, 0)),
            pl.BlockSpec((None, None, nsel, width), lambda g, b, i: (b, g, 0, i)),
        ],
        out_shape=[jax.ShapeDtypeStruct((batch, GROUPS, nq, HEAD_DIM, GROUP_LANES), BF16),
                   jax.ShapeDtypeStruct((batch, GROUPS, nsel, seq), F32)],
        scratch_shapes=[pltpu.VMEM(((2 * nq - 1) * 8, GROUP_LANES), F32), pltpu.VMEM((BLOCK_Q, GROUP_LANES), F32)],
        compiler_params=_params(3),
        name="nsa_cmp_select",
    )(h_t, cn, ct, vb, overlap_t)


N_WIN_PAIRS = -(-(NSA_WINDOW - 1) // (SUBTILES * BLOCK_Q)) + 1
N_WIN_SLOTS = SUBTILES * N_WIN_PAIRS + 1
SEL_ROWS = 16


def _selwin_kernel(q_ref, ksel_ref, kwin_ref, vsel_ref, vwin_ref, sel_ref, oc_ref, gate_ref, vbs_ref, vbw_ref,
                   o_ref, tsel_ref, twin_ref, s_scr, p_scr, acc_scr, *, nq):
    b, blk = pl.program_id(1), pl.program_id(2)

    @pl.when((b == 0) & (blk == 0))
    def _():
        _build_skew_table(vbs_ref, tsel_ref, nq + 1)
        _build_skew_table(vbw_ref, twin_ref, N_WIN_SLOTS)

    q_t = _gather_heads_to_lanes(q_ref)
    lanes = q_t.shape[1]
    eye = _identity_tile()
    pair = SUBTILES * BLOCK_Q
    init = (jnp.full((1, lanes), NEG_INF, F32), jnp.zeros((V_AUG_ROWS, lanes), F32))
    blocks_per_tile = BLOCK_Q // SEL_BLOCK

    def pair_of_step(n):
        return jnp.clip(blk - n, 0, blk)

    def bias_tile(tbl_ref, n, j):
        live = (n >= 0) & (n <= blk)
        return jnp.concatenate([tbl_ref[jnp.where(live, 2 * n + u - j + 1, 0)] for u in range(SUBTILES)], axis=1)

    def sel_scores(n):
        p = pair_of_step(n)
        neg = jnp.where(sel_ref[p] > 0.5, 0.0, NEG_INF)
        s_tiles = []
        for j in range(SUBTILES):
            k_t = ksel_ref[pl.ds(pl.multiple_of(p * pair + j * BLOCK_Q, BLOCK_Q), BLOCK_Q), :]
            rows = neg[j * blocks_per_tile:(j + 1) * blocks_per_tile]
            rows = jnp.concatenate([rows[:, u * BLOCK_Q:(u + 1) * BLOCK_Q]
                                    for u in range(SUBTILES) for _ in range(HEADS_PER_GROUP)], axis=1)
            rows = jnp.concatenate([rows, jnp.zeros((SEL_ROWS - blocks_per_tile, lanes), F32)], axis=0)
            q_ext = jnp.concatenate([q_t, rows.astype(BF16)], axis=0)
            s_tiles.append(_biased_scores(eye, k_t, bias_tile(tsel_ref, n, j), q_ext))
        return s_tiles

    def win_scores(n):
        p = pair_of_step(n)
        s_tiles = []
        for j in range(SUBTILES):
            k_t = kwin_ref[pl.ds(pl.multiple_of(p * pair + j * BLOCK_Q, BLOCK_Q), BLOCK_Q), :]
            s_tiles.append(_biased_scores(eye, k_t, bias_tile(twin_ref, n, j), q_t))
        return s_tiles

    def sel_pipeline(fill):
        return _pipelined_attention(blk + 1, sel_scores, lambda n: vsel_ref[pair_of_step(n)], init,
                                    s_scr, p_scr, acc_scr, fill=fill)

    m_w, acc_w = init
    s_w = [win_scores(n) for n in range(min(2, N_WIN_PAIRS))]
    for n in range(N_WIN_PAIRS):
        m_w, alpha, p = _softmax_probs(m_w, s_w[n])
        if n + 2 < N_WIN_PAIRS:
            s_w.append(win_scores(n + 2))
        if n == max(N_WIN_PAIRS - 2, 0):
            sel_pipeline(fill=True)
        acc_w = alpha * acc_w + jnp.dot(vwin_ref[pair_of_step(n)], p, preferred_element_type=F32)

    _, acc_s = sel_pipeline(fill=False)

    def gate_row(branch):
        g = jnp.concatenate([gate_ref[branch, h:h + 1, u * BLOCK_Q:(u + 1) * BLOCK_Q]
                             for u in range(SUBTILES) for h in range(HEADS_PER_GROUP)], axis=1)
        return jax.nn.sigmoid(g.astype(F32))

    o_c = jnp.concatenate([oc_ref[u] for u in range(SUBTILES)], axis=1).astype(F32)
    out = gate_row(0) * o_c + gate_row(1) * _normalized(acc_s) + gate_row(2) * _normalized(acc_w)
    for u in range(SUBTILES):
        o_ref[u * BLOCK_Q:(u + 1) * BLOCK_Q, :] = _heads_to_columns(
            out[:, u * GROUP_LANES:(u + 1) * GROUP_LANES]).astype(o_ref.dtype)


def _selwin_attention(h_t, ksel_aug, kk, vsel_aug, vwin_aug, sel, oc_t, gates_t, vb_sel, vb_win, batch, seq):
    nq = seq // BLOCK_Q
    nblk = nq // SUBTILES
    tokens = batch * seq
    q_blk0 = SWA_HEADS * HEAD_DIM // GROUP_COLS
    pair = SUBTILES * BLOCK_Q
    lanes = SUBTILES * GROUP_LANES
    return pl.pallas_call(
        functools.partial(_selwin_kernel, nq=nq),
        grid=(GROUPS, batch, nblk),
        in_specs=[
            pl.BlockSpec((GROUP_COLS, pair), lambda g, b, i: (q_blk0 + g, b * nblk + i)),
            pl.BlockSpec((None, seq, HEAD_DIM + SEL_ROWS), lambda g, b, i: (g, b, 0)),
            pl.BlockSpec((None, None, seq, HEAD_DIM), lambda g, b, i: (3, g, b, 0)),
            pl.BlockSpec((None, None, nblk, V_AUG_ROWS, pair), lambda g, b, i: (g, b, 0, 0, 0)),
            pl.BlockSpec((None, None, nblk, V_AUG_ROWS, pair), lambda g, b, i: (g, b, 0, 0, 0)),
            pl.BlockSpec((None, None, nblk, pair // SEL_BLOCK, pair), lambda g, b, i: (b, g, 0, 0, i)),
            pl.BlockSpec((None, None, SUBTILES, HEAD_DIM, GROUP_LANES), lambda g, b, i: (b, g, i, 0, 0)),
            pl.BlockSpec((None, 3, HEADS_PER_GROUP, pair), lambda g, b, i: (g, 0, 0, b * nblk + i)),
            pl.BlockSpec((None, HEADS_PER_GROUP, nq + 2, BLOCK_Q), lambda g, b, i: (g, 0, 0, 0)),
            pl.BlockSpec((None, HEADS_PER_GROUP, N_WIN_SLOTS + 1, BLOCK_Q), lambda g, b, i: (g, 0, 0, 0)),
        ],
        out_specs=pl.BlockSpec((pair, GROUP_COLS), lambda g, b, i: (b * nblk + i, g)),
        out_shape=jax.ShapeDtypeStruct((tokens, NSA_HEADS * HEAD_DIM), BF16),
        scratch_shapes=[pltpu.VMEM((nq + 1, BLOCK_Q, GROUP_LANES), BF16),
                        pltpu.VMEM((N_WIN_SLOTS, BLOCK_Q, GROUP_LANES), BF16),
                        pltpu.VMEM((SUBTILES, BLOCK_Q, lanes), F32),
                        pltpu.VMEM((pair, lanes), BF16),
                        pltpu.VMEM((V_AUG_ROWS, lanes), F32)],
        compiler_params=_params(3),
        name="nsa_sel_win",
    )(h_t, ksel_aug, kk, vsel_aug, vwin_aug, sel, oc_t, gates_t, vb_sel, vb_win)


ROW_SPLITS = 2


def _layer_norm(y, g_ref, b_ref):
    mu = jnp.mean(y, axis=-1, keepdims=True)
    yc = y - mu
    var = jnp.mean(yc * yc, axis=-1, keepdims=True)
    return yc * lax.rsqrt(var + LN_EPS) * g_ref[...] + b_ref[...]


def _merge_kernel(x_ref, ya_ref, yb_ref, wga_ref, wgb_ref, wa_ref, wb_ref, o_ref):
    x = x_ref[...]
    ga = jax.nn.sigmoid(jnp.dot(x, wga_ref[...], preferred_element_type=F32))
    gb = jax.nn.sigmoid(jnp.dot(x, wgb_ref[...], preferred_element_type=F32))
    a = jnp.dot(ya_ref[...], wa_ref[...], preferred_element_type=F32)
    bb = jnp.dot(yb_ref[...], wb_ref[...], preferred_element_type=F32)
    o_ref[...] = (ga * a + gb * bb).astype(o_ref.dtype)


def _merge(x_bf, y_a, y_b, w_ga, w_gb, w_a, w_b, *, tm, tn):
    tokens, d = x_bf.shape
    ya_cols, yb_cols = y_a.shape[1], y_b.shape[1]
    return pl.pallas_call(
        _merge_kernel,
        grid=(tokens // tm, d // tn),
        in_specs=[
            pl.BlockSpec((tm, d), lambda i, j: (i, 0)),
            pl.BlockSpec((tm, ya_cols), lambda i, j: (i, 0)),
            pl.BlockSpec((tm, yb_cols), lambda i, j: (i, 0)),
            pl.BlockSpec((d, tn), lambda i, j: (0, j)),
            pl.BlockSpec((d, tn), lambda i, j: (0, j)),
            pl.BlockSpec((ya_cols, tn), lambda i, j: (0, j)),
            pl.BlockSpec((yb_cols, tn), lambda i, j: (0, j)),
        ],
        out_specs=pl.BlockSpec((tm, tn), lambda i, j: (i, j)),
        out_shape=jax.ShapeDtypeStruct((tokens, d), BF16),
        compiler_params=_params(2),
        name="branch_merge",
    )(x_bf, y_a, y_b, w_ga, w_gb, w_a, w_b)


def _mix_ln_kernel(m_ref, x_ref, w_ref, g_ref, b_ref, o_ref):
    rows = m_ref.shape[0] // ROW_SPLITS
    for r in range(ROW_SPLITS):
        sl = slice(r * rows, (r + 1) * rows)
        mix = jnp.dot(m_ref[sl, :], w_ref[...], preferred_element_type=F32)
        o_ref[sl, :] = _layer_norm(DN_ALPHA * x_ref[sl, :] + mix, g_ref, b_ref)


def _mix_ln(merged, x, w_mix, ln_g, ln_b, *, tm):
    tokens, d = x.shape
    return pl.pallas_call(
        _mix_ln_kernel,
        grid=(tokens // tm,),
        in_specs=[
            pl.BlockSpec((tm, d), lambda i: (i, 0)),
            pl.BlockSpec((tm, d), lambda i: (i, 0)),
            pl.BlockSpec((d, d), lambda i: (0, 0)),
            pl.BlockSpec((1, d), lambda i: (0, 0)),
            pl.BlockSpec((1, d), lambda i: (0, 0)),
        ],
        out_specs=pl.BlockSpec((tm, d), lambda i: (i, 0)),
        out_shape=jax.ShapeDtypeStruct((tokens, d), F32),
        compiler_params=_params(1),
        name="mix_out_ln1",
    )(merged, x, w_mix, ln_g, ln_b)


def _xa_kernel(h_ref, wq_ref, k_ref, v_ref, wo_ref, g_ref, b_ref, o_ref):
    rows = h_ref.shape[0] // ROW_SPLITS
    subs = [slice(r * rows, (r + 1) * rows) for r in range(ROW_SPLITS)]
    qs = [(jnp.dot(h_ref[sl, :].astype(BF16), wq_ref[...], preferred_element_type=F32)
           * (XA_HEAD_DIM ** -0.5)).astype(BF16) for sl in subs]
    outs = [[] for _ in subs]
    for hd in range(XA_HEADS):
        cols = slice(hd * XA_HEAD_DIM, (hd + 1) * XA_HEAD_DIM)
        ss = [lax.dot_general(q[:, cols], k_ref[:, cols], (((1,), (1,)), ((), ())), preferred_element_type=F32)
              for q in qs]
        for r, s in enumerate(ss):
            p = jnp.exp(s - jnp.max(s, axis=-1, keepdims=True))
            l = jnp.sum(p, axis=-1, keepdims=True)
            o = jnp.dot(p.astype(BF16), v_ref[:, cols], preferred_element_type=F32)
            outs[r].append(o * (1.0 / l))
    xas = [jnp.dot(jnp.concatenate(o, axis=1).astype(BF16), wo_ref[...], preferred_element_type=F32) for o in outs]
    for sl, xa in zip(subs, xas):
        o_ref[sl, :] = _layer_norm(DN_ALPHA * h_ref[sl, :] + xa, g_ref, b_ref)


def _cross_attention_ln(h, kv_mem, w_q, w_o, ln_g, ln_b, batch, seq, *, tm):
    tokens, d = h.shape
    mem_len = kv_mem.shape[0] // batch
    xa_dim = XA_HEADS * XA_HEAD_DIM
    nt = seq // tm
    return pl.pallas_call(
        _xa_kernel,
        grid=(batch, nt),
        in_specs=[
            pl.BlockSpec((tm, d), lambda b, i: (b * nt + i, 0)),
            pl.BlockSpec((d, xa_dim), lambda b, i: (0, 0)),
            pl.BlockSpec((mem_len, xa_dim), lambda b, i: (b, 0)),
            pl.BlockSpec((mem_len, xa_dim), lambda b, i: (b, 1)),
            pl.BlockSpec((xa_dim, d), lambda b, i: (0, 0)),
            pl.BlockSpec((1, d), lambda b, i: (0, 0)),
            pl.BlockSpec((1, d), lambda b, i: (0, 0)),
        ],
        out_specs=pl.BlockSpec((tm, d), lambda b, i: (b * nt + i, 0)),
        out_shape=jax.ShapeDtypeStruct((tokens, d), F32),
        compiler_params=_params(2),
        name="cross_attention_ln2",
    )(h, w_q, kv_mem, kv_mem, w_o, ln_g, ln_b)


def _mlp_kernel(h_ref, w1_ref, w2_ref, g_ref, b_ref, o_ref, hb_ref):
    j = pl.program_id(1)

    @pl.when(j == 0)
    def _():
        hb_ref[...] = h_ref[...].astype(BF16)
        o_ref[...] = jnp.zeros_like(o_ref)

    rows = hb_ref.shape[0] // ROW_SPLITS
    subs = [slice(r * rows, (r + 1) * rows) for r in range(ROW_SPLITS)]
    ups = [jnp.dot(hb_ref[sl, :], w1_ref[...], preferred_element_type=F32) for sl in subs]
    for sl, u in zip(subs, ups):
        u = jnp.square(jnp.maximum(u, 0.0)).astype(BF16)
        o_ref[sl, :] += jnp.dot(u, w2_ref[...], preferred_element_type=F32)

    @pl.when(j == pl.num_programs(1) - 1)
    def _():
        o_ref[...] = _layer_norm(DN_ALPHA * h_ref[...] + o_ref[...], g_ref, b_ref)


def _mlp_ln(h, w1, w2, ln_g, ln_b, *, tm, tf):
    tokens, d = h.shape
    d_ff = w1.shape[1]
    return pl.pallas_call(
        _mlp_kernel,
        grid=(tokens // tm, d_ff // tf),
        in_specs=[
            pl.BlockSpec((tm, d), lambda i, j: (i, 0)),
            pl.BlockSpec((d, tf), lambda i, j: (0, j)),
            pl.BlockSpec((tf, d), lambda i, j: (j, 0)),
            pl.BlockSpec((1, d), lambda i, j: (0, 0)),
            pl.BlockSpec((1, d), lambda i, j: (0, 0)),
        ],
        out_specs=pl.BlockSpec((tm, d), lambda i, j: (i, 0)),
        out_shape=jax.ShapeDtypeStruct((tokens, d), F32),
        scratch_shapes=[pltpu.VMEM((tm, d), BF16)],
        compiler_params=_params(2),
        name="mlp_ln3",
    )(h, w1, w2, ln_g, ln_b)


def _overlap_t(seq):
    ncp, nsel = seq // CMP_STRIDE, seq // SEL_BLOCK
    c_start = np.arange(ncp)[None, :] * CMP_STRIDE
    s_start = np.arange(nsel)[:, None] * SEL_BLOCK
    ov = (c_start < s_start + SEL_BLOCK) & (c_start + CMP_BLOCK > s_start) & (np.arange(ncp)[None, :] < ncp - 1)
    return jnp.asarray(ov, BF16)


def kernel(x, mem, w_in, attn_sinks, rel_bias_table, cmp_pos_k, cmp_w1_k, cmp_w2_k, cmp_pos_v, cmp_w1_v, cmp_w2_v,
           w_branch_swa, w_branch_nsa, w_mix_out, ln1_g, ln1_b, xa_w_q, xa_w_kv, xa_w_o, ln2_g, ln2_b,
           mlp_w1, mlp_w2, ln3_g, ln3_b):
    batch, seq, d = x.shape
    tokens = batch * seq
    nq = seq // BLOCK_Q
    assert w_in.shape[0] == 1, "one layer"
    assert seq % (8 * BLOCK_Q) == 0 and d % 512 == 0

    w = w_in[0]
    sizes = (SWA_HEADS * HEAD_DIM, 128, 128, NSA_HEADS * HEAD_DIM, 128, 128, 128, 128, 128, 128, 3 * NSA_HEADS, d, d)
    offs = np.concatenate([[0], np.cumsum(sizes)])
    (w_qa, w_ka, w_va, w_qb, w_kc, w_vc, w_ks, w_vs, w_kw, w_vw, w_gn, w_ga, w_gb) = [
        w[:, offs[n]:offs[n + 1]] for n in range(len(sizes))]
    w_rows = jnp.concatenate([w_ka, w_kc, w_ks, w_kw, w_vc], axis=1).astype(BF16)
    w_gn_pad = jnp.pad(w_gn, ((0, 0), (0, 128 - w_gn.shape[1])))
    w_cols_t = jnp.concatenate([w_qa, w_qb, w_va, w_vs, w_vw, w_gn_pad], axis=1).T.astype(BF16)

    x2 = x.reshape(tokens, d)
    x_bf = x2.astype(BF16)

    h_rows = _matmul(x_bf, w_rows, nt=False, tm=min(2048, tokens), tn=w_rows.shape[1])
    h_t = _matmul(w_cols_t, x_bf, nt=True, tm=1280, tn=min(1024, tokens))

    kk = h_rows[:, :512].reshape(tokens, 4, GROUPS, HEAD_DIM).transpose(1, 2, 0, 3)
    v_cmp = h_rows[:, 512:].reshape(tokens, GROUPS, HEAD_DIM).transpose(1, 0, 2)

    def value_tiles(row0, width):
        v = h_t[row0:row0 + 128].reshape(GROUPS, HEAD_DIM, batch, seq // width, width).transpose(0, 2, 3, 1, 4)
        ones = jnp.ones(v.shape[:3] + (V_AUG_ROWS - HEAD_DIM, width), BF16)
        return jnp.concatenate([v, ones], axis=3)

    vswa_aug = value_tiles(2048, BLOCK_Q)
    vsel_aug, vwin_aug = value_tiles(2176, 2 * BLOCK_Q), value_tiles(2304, 2 * BLOCK_Q)
    gates_t = h_t[2432:2432 + 3 * NSA_HEADS].reshape(GROUPS, HEADS_PER_GROUP, 3, tokens).transpose(0, 2, 1, 3)
    block_of_key = (jnp.arange(tokens) % BLOCK_Q) // SEL_BLOCK
    onehot = (block_of_key[:, None] == jnp.arange(SEL_ROWS)[None, :]).astype(BF16)
    ksel_aug = jnp.concatenate([kk[2], jnp.broadcast_to(onehot, (GROUPS, tokens, SEL_ROWS))], axis=2)

    by_dist = rel_bias_table[_rel_bucket(jnp.arange(seq))].astype(F32).T
    vb_swa = _bias_by_distance(by_dist[:SWA_HEADS], 4, window=SWA_WINDOW)
    vb_sel = _bias_by_distance(by_dist[SWA_HEADS:], nq + 2)
    vb_win = _bias_by_distance(by_dist[SWA_HEADS:], N_WIN_SLOTS + 1, window=NSA_WINDOW)
    vb_cmp = vb_sel
    sink_rows = jnp.repeat(attn_sinks[0].astype(F32), BLOCK_Q).reshape(GROUPS, 1, GROUP_LANES)

    y_a = _swa_attention(h_t, kk, vswa_aug, sink_rows, vb_swa, batch, seq)

    ncp = seq // CMP_STRIDE
    chunk_w = CMP_STRIDE * HEAD_DIM
    chunks = jnp.stack([kk[1], v_cmp]).reshape(2, GROUPS, batch, ncp, chunk_w)
    pos = jnp.stack([cmp_pos_k[0], cmp_pos_v[0]]).astype(F32).reshape(2, 2, 1, chunk_w)
    w1 = jnp.stack([cmp_w1_k[0], cmp_w1_v[0]]).astype(BF16).reshape(2, 2, chunk_w, -1)
    w2 = jnp.stack([cmp_w2_k[0], cmp_w2_v[0]]).astype(BF16)
    cn, ct = _compress(chunks, pos, w1, w2, w2.transpose(0, 2, 1))
    oc_t, sel = _cmp_attention(h_t, cn, ct, vb_cmp, _overlap_t(seq), batch, seq)
    sel = sel.reshape(batch, GROUPS, nq // 2, 4, seq)
    y_b = _selwin_attention(h_t, ksel_aug, kk, vsel_aug, vwin_aug, sel, oc_t, gates_t, vb_sel, vb_win, batch, seq)

    merged = _merge(x_bf, y_a, y_b, w_ga.astype(BF16), w_gb.astype(BF16),
                    w_branch_swa[0].astype(BF16), w_branch_nsa[0].astype(BF16), tm=min(1024, tokens), tn=512)
    h1 = _mix_ln(merged, x2, w_mix_out[0].astype(BF16), ln1_g, ln1_b, tm=512)

    mem_bf = mem.reshape(-1, d).astype(BF16)
    kv_mem = _matmul(mem_bf, xa_w_kv[0].astype(BF16), nt=False, tm=mem_bf.shape[0], tn=512)
    h2 = _cross_attention_ln(h1, kv_mem, xa_w_q[0].astype(BF16), xa_w_o[0].astype(BF16), ln2_g, ln2_b,
                             batch, seq, tm=512)

    h3 = _mlp_ln(h2, mlp_w1[0].astype(BF16), mlp_w2[0].astype(BF16), ln3_g, ln3_b, tm=1024, tf=512)
    return h3.reshape(batch, seq, d)
```

```python
import functools
import math

import numpy as np
import jax
import jax.numpy as jnp
from jax import lax
from jax.experimental import pallas as pl
from jax.experimental.pallas import tpu as pltpu

F32 = jnp.float32
BF16 = jnp.bfloat16

HEAD_DIM = 64
BLOCK_Q = 128
SWA_HEADS = 16
SWA_KV_HEADS = 2
SWA_WINDOW = 128
NSA_HEADS = 16
NSA_KV_HEADS = 2
CMP_BLOCK = 32
CMP_STRIDE = 16
SEL_BLOCK = 64
SEL_TOPK = 16
SEL_LOCAL = 2
NSA_WINDOW = 512
REL_BUCKETS = 32
REL_MAX_DIST = 4096
XA_HEADS = 4
XA_HEAD_DIM = 128
DEPTH = 1
DN_ALPHA = (2.0 * DEPTH) ** 0.25
LN_EPS = 1e-5
NEG_INF = -1e30
FORCE_SCORE = 1e4

GROUPS = 2
HEADS_PER_GROUP = 8
GROUP_LANES = HEADS_PER_GROUP * BLOCK_Q
GROUP_COLS = HEADS_PER_GROUP * HEAD_DIM
SUBTILES = 2

V7X_VMEM_LIMIT_BYTES = 56 * 1024 * 1024


def _params(n_axes):
    return pltpu.CompilerParams(dimension_semantics=("arbitrary",) * n_axes,
                                vmem_limit_bytes=V7X_VMEM_LIMIT_BYTES)


def _mm_kernel(a_ref, b_ref, o_ref, *, nt):
    if nt:
        out = lax.dot_general(a_ref[...], b_ref[...], (((1,), (1,)), ((), ())), preferred_element_type=F32)
    else:
        out = jnp.dot(a_ref[...], b_ref[...], preferred_element_type=F32)
    o_ref[...] = out.astype(o_ref.dtype)


def _matmul(a, b, *, nt, tm, tn, out_dtype=BF16):
    m, k = a.shape
    n = b.shape[0] if nt else b.shape[1]
    b_spec = pl.BlockSpec((tn, k), lambda i, j: (j, 0)) if nt else pl.BlockSpec((k, tn), lambda i, j: (0, j))
    return pl.pallas_call(
        functools.partial(_mm_kernel, nt=nt),
        grid=(m // tm, n // tn),
        in_specs=[pl.BlockSpec((tm, k), lambda i, j: (i, 0)), b_spec],
        out_specs=pl.BlockSpec((tm, tn), lambda i, j: (i, j)),
        out_shape=jax.ShapeDtypeStruct((m, n), out_dtype),
        compiler_params=_params(2),
        name="proj_nt" if nt else "proj_nn",
    )(a, b)


def _rel_bucket(dist):
    exact = REL_BUCKETS // 2
    d = jnp.maximum(dist, 0)
    log_ratio = jnp.log(jnp.maximum(d, 1).astype(F32) / exact) / math.log(REL_MAX_DIST / exact)
    large = jnp.minimum(exact + (log_ratio * (REL_BUCKETS - exact)).astype(jnp.int32), REL_BUCKETS - 1)
    return jnp.where(d < exact, d, large)


def _bias_by_distance(by_dist, rows, *, window=None):
    heads, seq = by_dist.shape
    if window is not None:
        by_dist = jnp.where(jnp.arange(seq) < window, by_dist, NEG_INF)
    padded = jnp.pad(by_dist, ((0, 0), (2 * BLOCK_Q, 0)), constant_values=NEG_INF)[:, :rows * BLOCK_Q]
    return padded.reshape(heads // HEADS_PER_GROUP, HEADS_PER_GROUP, rows, BLOCK_Q)


def _build_skew_table(vb_ref, tbl_ref, n_tiles):
    def body(d, carry):
        for h in range(HEADS_PER_GROUP):
            lo = jnp.broadcast_to(vb_ref[h, pl.ds(d, 1), :], (BLOCK_Q, BLOCK_Q))
            hi = jnp.broadcast_to(vb_ref[h, pl.ds(d + 1, 1), :], (BLOCK_Q, BLOCK_Q))
            y = pltpu.roll(jnp.concatenate([lo, hi], axis=1), 0, 1, stride=1, stride_axis=0)
            tbl_ref[d, :, h * BLOCK_Q:(h + 1) * BLOCK_Q] = y[:, BLOCK_Q:].astype(tbl_ref.dtype)
        return carry

    lax.fori_loop(0, n_tiles, body, 0)


def _build_cmp_bias_table(vb_ref, tbl_ref, skew_ref, nq):
    per_tile = BLOCK_Q // CMP_STRIDE
    assert per_tile == 8 and CMP_BLOCK - 1 + CMP_STRIDE * (per_tile - 1) - BLOCK_Q == CMP_STRIDE - 1
    tbl_ref[pl.ds(nq * 8, (nq - 1) * 8), :] = jnp.full(((nq - 1) * 8, GROUP_LANES), NEG_INF, F32)

    def body(d, last_row):
        for h in range(HEADS_PER_GROUP):
            lo = jnp.broadcast_to(vb_ref[h, pl.ds(d + 1, 1), :], (BLOCK_Q, BLOCK_Q))
            hi = jnp.broadcast_to(vb_ref[h, pl.ds(d + 2, 1), :], (BLOCK_Q, BLOCK_Q))
            y = pltpu.roll(jnp.concatenate([lo, hi], axis=1), 0, 1, stride=1, stride_axis=0)
            skew_ref[:, h * BLOCK_Q:(h + 1) * BLOCK_Q] = y[:, BLOCK_Q:]
        offsets = [CMP_BLOCK - 1 + CMP_STRIDE * c for c in range(per_tile - 1)]
        rows = [skew_ref[k:k + 1, :] for k in offsets] + [last_row]
        tbl_ref[pl.ds(pl.multiple_of((nq - 1 - d) * 8, 8), 8), :] = jnp.concatenate(rows, axis=0)
        return skew_ref[CMP_STRIDE - 1:CMP_STRIDE, :]

    lax.fori_loop(0, nq, body, jnp.full((1, GROUP_LANES), NEG_INF, F32))


def _gather_heads_to_lanes(q_ref):
    q = jnp.concatenate([q_ref[h * HEAD_DIM:(h + 1) * HEAD_DIM, u * BLOCK_Q:(u + 1) * BLOCK_Q]
                         for u in range(q_ref.shape[1] // BLOCK_Q) for h in range(HEADS_PER_GROUP)], axis=1)
    return q * jnp.asarray(HEAD_DIM ** -0.5, q.dtype)


def _heads_to_columns(o_t):
    stacked = jnp.concatenate([o_t[:, h * BLOCK_Q:(h + 1) * BLOCK_Q] for h in range(HEADS_PER_GROUP)], axis=0)
    return stacked.T


def _identity_tile():
    r = lax.broadcasted_iota(jnp.int32, (BLOCK_Q, BLOCK_Q), 0)
    c = lax.broadcasted_iota(jnp.int32, (BLOCK_Q, BLOCK_Q), 1)
    return jnp.where(r == c, 1.0, 0.0).astype(BF16)


def _biased_scores(eye, k_t, bias_tile, q_t):
    lhs = jnp.concatenate([eye, k_t], axis=1)
    rhs = jnp.concatenate([bias_tile, q_t], axis=0)
    return jnp.dot(lhs, rhs, preferred_element_type=F32)


def _softmax_probs(m, s_tiles):
    ms, alphas, ps = [], [], [[] for _ in s_tiles]
    for h in range(m.shape[1] // BLOCK_Q):
        lanes = slice(h * BLOCK_Q, (h + 1) * BLOCK_Q)
        cols = [s[:, lanes] for s in s_tiles]
        m_new = m[:, lanes]
        for c in cols:
            m_new = jnp.maximum(m_new, jnp.max(c, axis=0, keepdims=True))
        ms.append(m_new)
        alphas.append(jnp.exp(m[:, lanes] - m_new))
        for j, c in enumerate(cols):
            ps[j].append(jnp.exp((c - m_new).astype(BF16)))
    p = jnp.concatenate([jnp.concatenate(pj, axis=1) for pj in ps], axis=0)
    return jnp.concatenate(ms, axis=1), jnp.concatenate(alphas, axis=1), p


def _softmax_update(carry, s_tiles, v_aug):
    m, acc = carry
    m, alpha, p = _softmax_probs(m, s_tiles)
    return m, alpha * acc + jnp.dot(v_aug, p, preferred_element_type=F32)


def _pipelined_attention(n_steps, scores_fn, values_fn, init, s_scr, p_scr, acc_scr, *, fill):
    m0, acc0 = init

    def put_scores(n):
        for j, s in enumerate(scores_fn(n)):
            s_scr[j] = s

    def fold(alpha, n, p):
        acc_scr[...] = alpha * acc_scr[...] + jnp.dot(values_fn(n), p, preferred_element_type=F32)

    if fill:
        put_scores(0)
        p_scr[...] = jnp.zeros_like(p_scr)
        acc_scr[...] = acc0
        return None

    def body(k, carry):
        m, alpha_a, alpha_b = carry
        a = 2 * k
        s_b = scores_fn(a + 1)
        fold(alpha_a, a - 2, p_scr[0])
        fold(alpha_b, a - 1, p_scr[1])
        m, alpha_a, p_a = _softmax_probs(m, [s_scr.at[j] for j in range(s_scr.shape[0])])
        put_scores(a + 2)
        m, alpha_b, p_b = _softmax_probs(m, s_b)
        p_scr[0] = p_a
        p_scr[1] = p_b
        return m, alpha_a, alpha_b

    trips = (n_steps + 1) // 2
    ones = jnp.ones_like(m0)
    m, alpha_a, alpha_b = lax.fori_loop(0, trips, body, (m0, ones, ones))
    fold(alpha_a, 2 * trips - 2, p_scr[0])
    fold(alpha_b, 2 * trips - 1, p_scr[1])
    return m, acc_scr[...]


SLAB = 2 * BLOCK_Q
LOOKAHEAD = 12


def _slabwise_softmax_step(m, acc_ref, lhs_tiles, rhs_fn, v_fn):
    n_tiles = len(lhs_tiles)
    items = [(sl, j) for sl in range(m.shape[1] // SLAB) for j in range(n_tiles)]

    def scores(k):
        sl, j = items[k]
        return jnp.dot(lhs_tiles[j], rhs_fn(j, sl), preferred_element_type=F32)

    pending = [scores(k) for k in range(min(LOOKAHEAD, len(items)))]
    new_m = []
    for k, (sl, j) in enumerate(items):
        lanes = slice(sl * SLAB, (sl + 1) * SLAB)
        if j == 0:
            m_s, acc_s = m[:, lanes], acc_ref[:, lanes]
        s = pending.pop(0)
        if k + LOOKAHEAD < len(items):
            pending.append(scores(k + LOOKAHEAD))
        m_next = jnp.maximum(m_s, jnp.max(s, axis=0, keepdims=True))
        p = jnp.exp((s - m_next).astype(BF16))
        acc_s = jnp.exp(m_s - m_next) * acc_s + jnp.dot(v_fn(j), p, preferred_element_type=F32)
        m_s = m_next
        if j == n_tiles - 1:
            acc_ref[:, lanes] = acc_s
            new_m.append(m_s)
    return jnp.concatenate(new_m, axis=1)


V_AUG_ROWS = HEAD_DIM + 16


def _normalized(acc):
    return acc[:HEAD_DIM] * (1.0 / acc[HEAD_DIM:HEAD_DIM + 1])


def _swa_kernel(q_ref, kprev_ref, kcur_ref, vprev_ref, vcur_ref, sink_ref, vb_ref, o_ref, tbl_ref):
    b, i = pl.program_id(1), pl.program_id(2)

    @pl.when((b == 0) & (i == 0))
    def _():
        _build_skew_table(vb_ref, tbl_ref, 3)

    eye = _identity_tile()
    acc0 = jnp.where(lax.broadcasted_iota(jnp.int32, (V_AUG_ROWS, GROUP_LANES), 0) < HEAD_DIM, 0.0, 1.0)
    for u in range(SWA_TILES):
        q_t = _gather_heads_to_lanes(q_ref.at[:, u * BLOCK_Q:(u + 1) * BLOCK_Q])
        k_cur, v_cur = kcur_ref[u * BLOCK_Q:(u + 1) * BLOCK_Q, :], vcur_ref[u]
        if u == 0:
            k_prev, v_prev, prev_slot = kprev_ref[...], vprev_ref[...], jnp.where(i > 0, 2, 0)
        else:
            k_prev, v_prev, prev_slot = kcur_ref[(u - 1) * BLOCK_Q:u * BLOCK_Q, :], vcur_ref[u - 1], 2
        s_cur = _biased_scores(eye, k_cur, tbl_ref[1], q_t)
        s_prev = _biased_scores(eye, k_prev, tbl_ref[prev_slot], q_t)
        v2 = jnp.concatenate([v_cur, v_prev], axis=1)
        _, acc = _softmax_update((sink_ref[...], acc0), [s_cur, s_prev], v2)
        o_ref[u * BLOCK_Q:(u + 1) * BLOCK_Q, :] = _heads_to_columns(_normalized(acc)).astype(o_ref.dtype)


SWA_TILES = 4


def _swa_attention(h_t, kk, v_aug, sink_rows, vb, batch, seq):
    nq = seq // BLOCK_Q
    nblk = nq // SWA_TILES
    width = SWA_TILES * BLOCK_Q
    tokens = batch * seq

    def prev_tile(i):
        return jnp.maximum(i * SWA_TILES - 1, 0)

    return pl.pallas_call(
        _swa_kernel,
        grid=(GROUPS, batch, nblk),
        in_specs=[
            pl.BlockSpec((GROUP_COLS, width), lambda g, b, i: (g, b * nblk + i)),
            pl.BlockSpec((None, None, BLOCK_Q, HEAD_DIM), lambda g, b, i: (0, g, b * nq + prev_tile(i), 0)),
            pl.BlockSpec((None, None, width, HEAD_DIM), lambda g, b, i: (0, g, b * nblk + i, 0)),
            pl.BlockSpec((None, None, None, V_AUG_ROWS, BLOCK_Q), lambda g, b, i: (g, b, prev_tile(i), 0, 0)),
            pl.BlockSpec((None, None, SWA_TILES, V_AUG_ROWS, BLOCK_Q), lambda g, b, i: (g, b, i, 0, 0)),
            pl.BlockSpec((None, 1, GROUP_LANES), lambda g, b, i: (g, 0, 0)),
            pl.BlockSpec((None, HEADS_PER_GROUP, 4, BLOCK_Q), lambda g, b, i: (g, 0, 0, 0)),
        ],
        out_specs=pl.BlockSpec((width, GROUP_COLS), lambda g, b, i: (b * nblk + i, g)),
        out_shape=jax.ShapeDtypeStruct((tokens, SWA_HEADS * HEAD_DIM), BF16),
        scratch_shapes=[pltpu.VMEM((3, BLOCK_Q, GROUP_LANES), BF16)],
        compiler_params=_params(3),
        name="swa_attention",
    )(h_t, kk, kk, v_aug, v_aug, sink_rows, vb)


def _compress_kernel(c_ref, pos_ref, w1_ref, w2_ref, w2t_ref, cn_ref, ct_ref):
    c = c_ref[...].astype(F32)
    top = (c + pos_ref[0]).astype(BF16)
    bot = (c + pos_ref[1]).astype(BF16)
    a = jnp.dot(top, w1_ref[0], preferred_element_type=F32)
    bm = jnp.dot(bot, w1_ref[1], preferred_element_type=F32)
    n = a.shape[0]
    pre = a + pltpu.roll(bm, n - 1, 0)
    hid = jax.nn.gelu(pre).astype(BF16)
    cn_ref[...] = jnp.dot(hid, w2_ref[...], preferred_element_type=F32).astype(cn_ref.dtype)
    ct_ref[...] = lax.dot_general(w2t_ref[...], hid, (((1,), (1,)), ((), ())),
                                  preferred_element_type=F32).astype(ct_ref.dtype)


def _compress(chunks, pos, w1, w2, w2t):
    _, g, b, ncp, width = chunks.shape
    hidden = w1.shape[-1]
    return pl.pallas_call(
        _compress_kernel,
        grid=(2, g, b),
        in_specs=[
            pl.BlockSpec((None, None, None, ncp, width), lambda t, g, b: (t, g, b, 0, 0)),
            pl.BlockSpec((None, 2, 1, width), lambda t, g, b: (t, 0, 0, 0)),
            pl.BlockSpec((None, 2, width, hidden), lambda t, g, b: (t, 0, 0, 0)),
            pl.BlockSpec((None, hidden, HEAD_DIM), lambda t, g, b: (t, 0, 0)),
            pl.BlockSpec((None, HEAD_DIM, hidden), lambda t, g, b: (t, 0, 0)),
        ],
        out_specs=[
            pl.BlockSpec((None, None, None, ncp, HEAD_DIM), lambda t, g, b: (t, g, b, 0, 0)),
            pl.BlockSpec((None, None, None, HEAD_DIM, ncp), lambda t, g, b: (t, g, b, 0, 0)),
        ],
        out_shape=[jax.ShapeDtypeStruct((2, g, b, ncp, HEAD_DIM), BF16),
                   jax.ShapeDtypeStruct((2, g, b, HEAD_DIM, ncp), BF16)],
        compiler_params=_params(3),
        name="nsa_compress",
    )(chunks, pos, w1, w2, w2t)


def _cmp_kernel(q_ref, kc_ref, vct_ref, vb_ref, ov_ref, oc_ref, sel_ref, bias_ref, skew_ref, *, nq):
    b, i = pl.program_id(1), pl.program_id(2)

    @pl.when((b == 0) & (i == 0))
    def _():
        _build_cmp_bias_table(vb_ref, bias_ref, skew_ref, nq)

    ncp = kc_ref.shape[0]
    nsel = sel_ref.shape[0]
    n_tiles = ncp // BLOCK_Q
    q_t = _gather_heads_to_lanes(q_ref)
    width = SUBTILES * BLOCK_Q
    eye = _identity_tile()

    def bias_tile(t):
        rows = [pl.multiple_of((nq - 1 - (SUBTILES * i + u)) * 8 + t * BLOCK_Q, 8) for u in range(SUBTILES)]
        return jnp.concatenate([bias_ref[pl.ds(r, BLOCK_Q), :].astype(BF16) for r in rows], axis=1)

    s_tiles = [_biased_scores(eye, kc_ref[t * BLOCK_Q:(t + 1) * BLOCK_Q, :], bias_tile(t), q_t)
               for t in range(n_tiles)]
    q_pos = i * width + lax.broadcasted_iota(jnp.int32, (1, width), 1)
    sees_any = q_pos >= CMP_BLOCK - 1
    psum = [[jnp.zeros((BLOCK_Q, BLOCK_Q), F32) for _ in range(n_tiles)] for _ in range(SUBTILES)]
    probs = [[] for _ in range(n_tiles)]
    for u in range(SUBTILES):
        for h in range(HEADS_PER_GROUP):
            lanes = slice((u * HEADS_PER_GROUP + h) * BLOCK_Q, (u * HEADS_PER_GROUP + h + 1) * BLOCK_Q)
            cols = [s[:, lanes] for s in s_tiles]
            m = functools.reduce(jnp.maximum, [jnp.max(c, axis=0, keepdims=True) for c in cols])
            ps = [jnp.exp(c - m) for c in cols]
            l = functools.reduce(jnp.add, [jnp.sum(p, axis=0, keepdims=True) for p in ps])
            inv = jnp.where(sees_any[:, u * BLOCK_Q:(u + 1) * BLOCK_Q], 1.0 / l, 0.0)
            for t in range(n_tiles):
                p = ps[t] * inv
                psum[u][t] = psum[u][t] + p
                probs[t].append(p.astype(BF16))
    p_all = jnp.concatenate([jnp.concatenate(pt, axis=1) for pt in probs], axis=0)
    o_c = jnp.dot(vct_ref[...], p_all, preferred_element_type=F32).astype(oc_ref.dtype)
    for u in range(SUBTILES):
        oc_ref[u] = o_c[:, u * GROUP_LANES:(u + 1) * GROUP_LANES]

    psum = jnp.concatenate([jnp.concatenate(pu, axis=0) for pu in psum], axis=1)
    hi = psum.astype(BF16)
    lo = (psum - hi.astype(F32)).astype(BF16)
    ov = ov_ref[...]
    score = jnp.dot(ov, hi, preferred_element_type=F32) + jnp.dot(ov, lo, preferred_element_type=F32)

    j_io = lax.broadcasted_iota(jnp.int32, (nsel, width), 0)
    qpos = i * width + lax.broadcasted_iota(jnp.int32, (nsel, width), 1)
    causal = j_io * SEL_BLOCK <= qpos
    back = qpos // SEL_BLOCK - j_io
    forced = (j_io == 0) | ((back >= 0) & (back < SEL_LOCAL))
    score = jnp.where(causal, jnp.where(forced, FORCE_SCORE, score), -1.0)
    slab_rows = lax.broadcasted_iota(jnp.int32, (8, width), 0)
    slabs = [score[8 * g:8 * (g + 1), :] for g in range(nsel // 8)]
    ranks = [jnp.zeros((8, width), F32) for _ in slabs]
    for r in range(nsel):
        row = jnp.broadcast_to(score[r:r + 1, :], (8, width))
        for g, slab in enumerate(slabs):
            if g > r // 8:
                ahead = row >= slab
            elif g < r // 8:
                ahead = row > slab
            else:
                ranks[g] = ranks[g] + jnp.where(slab_rows > r % 8, jnp.where(row >= slab, 1.0, 0.0),
                                                jnp.where(row > slab, 1.0, 0.0))
                continue
            ranks[g] = ranks[g] + jnp.where(ahead, 1.0, 0.0)
    rank = jnp.concatenate(ranks, axis=0)
    sel_ref[...] = jnp.where((rank < min(SEL_TOPK, nsel)) & causal, 1.0, 0.0).astype(sel_ref.dtype)


def _cmp_attention(h_t, cn, ct, vb, overlap_t, batch, seq):
    nq = seq // BLOCK_Q
    ncp = seq // CMP_STRIDE
    nsel = seq // SEL_BLOCK
    q_blk0 = SWA_HEADS * HEAD_DIM // GROUP_COLS
    nblk = nq // SUBTILES
    width = SUBTILES * BLOCK_Q
    return pl.pallas_call(
        functools.partial(_cmp_kernel, nq=nq),
        grid=(GROUPS, batch, nblk),
        in_specs=[
            pl.BlockSpec((GROUP_COLS, width), lambda g, b, i: (q_blk0 + g, b * nblk + i)),
            pl.BlockSpec((None, None, None, ncp, HEAD_DIM), lambda g, b, i: (0, g, b, 0, 0)),
            pl.BlockSpec((None, None, None, HEAD_DIM, ncp), lambda g, b, i: (1, g, b, 0, 0)),
            pl.BlockSpec((None, HEADS_PER_GROUP, nq + 2, BLOCK_Q), lambda g, b, i: (g, 0, 0, 0)),
            pl.BlockSpec((nsel, ncp), lambda g, b, i: (0, 0)),
        ],
        out_specs=[
            pl.BlockSpec((None, None, SUBTILES, HEAD_DIM, GROUP_LANES), lambda g, b, i: (b, g, i, 0, 0)),
            pl.BlockSpec((None, None, nsel, width), lambda g, b, i: (b, g, 0, i)),
        ],
        out_shape=[jax.ShapeDtypeStruct((batch, GROUPS, nq, HEAD_DIM, GROUP_LANES), BF16),
                   jax.ShapeDtypeStruct((batch, GROUPS, nsel, seq), F32)],
        scratch_shapes=[pltpu.VMEM(((2 * nq - 1) * 8, GROUP_LANES), F32), pltpu.VMEM((BLOCK_Q, GROUP_LANES), F32)],
        compiler_params=_params(3),
        name="nsa_cmp_select",
    )(h_t, cn, ct, vb, overlap_t)


N_WIN_PAIRS = -(-(NSA_WINDOW - 1) // (SUBTILES * BLOCK_Q)) + 1
N_WIN_SLOTS = SUBTILES * N_WIN_PAIRS + 1
SEL_ROWS = 16
SEL_STEPS = 2


def _selwin_kernel(q_ref, ksel_ref, kwin_ref, vsel_ref, vwin_ref, sel_ref, oc_ref, gate_ref, vbs_ref, vbw_ref,
                   o_ref, tsel_ref, twin_ref, accs_ref, accw_ref, *, nq):
    b, blk = pl.program_id(1), pl.program_id(2)

    @pl.when((b == 0) & (blk == 0))
    def _():
        _build_skew_table(vbs_ref, tsel_ref, nq + 1)
        _build_skew_table(vbw_ref, twin_ref, N_WIN_SLOTS)

    q_t = _gather_heads_to_lanes(q_ref)
    lanes = q_t.shape[1]
    eye = _identity_tile()
    pair = SUBTILES * BLOCK_Q
    m0 = jnp.full((1, lanes), NEG_INF, F32)
    accs_ref[...] = jnp.zeros_like(accs_ref)
    accw_ref[...] = jnp.zeros_like(accw_ref)
    blocks_per_tile = BLOCK_Q // SEL_BLOCK
    slabs_per_tile = GROUP_LANES // SLAB

    def attend(m, steps, k_ref, tbl_ref, v_ref, acc_ref, q_ext_fn):
        tiles = [(n, jnp.clip(blk - n, 0, blk), j) for n in steps for j in range(SUBTILES)]
        lhs = [jnp.concatenate([eye, k_ref[pl.ds(pl.multiple_of(p * pair + j * BLOCK_Q, BLOCK_Q), BLOCK_Q), :]],
                               axis=1) for _, p, j in tiles]
        q_exts = [q_ext_fn(p, j) for _, p, j in tiles]

        def rhs(t, slab):
            n, _, j = tiles[t]
            u, part = slab // slabs_per_tile, slab % slabs_per_tile
            slot = jnp.where(n <= blk, 2 * n + u - j + 1, 0)
            bias = tbl_ref[slot, :, part * SLAB:(part + 1) * SLAB]
            return jnp.concatenate([bias, q_exts[t][:, slab * SLAB:(slab + 1) * SLAB]], axis=0)

        def values(t):
            _, p, j = tiles[t]
            return v_ref[p, :, j * BLOCK_Q:(j + 1) * BLOCK_Q]

        return _slabwise_softmax_step(m, acc_ref, lhs, rhs, values)

    attend(m0, list(range(N_WIN_PAIRS)), kwin_ref, twin_ref, vwin_ref, accw_ref, lambda p, j: q_t)

    def sel_q_ext(p, j):
        neg = jnp.where(sel_ref[p, j * blocks_per_tile:(j + 1) * blocks_per_tile, :] > 0.5, 0.0, NEG_INF)
        rows = jnp.concatenate([neg[:, u * BLOCK_Q:(u + 1) * BLOCK_Q]
                                for u in range(SUBTILES) for _ in range(HEADS_PER_GROUP)], axis=1)
        rows = jnp.concatenate([rows, jnp.zeros((SEL_ROWS - blocks_per_tile, lanes), F32)], axis=0)
        return jnp.concatenate([q_t, rows.astype(BF16)], axis=0)

    def sel_body(k, m):
        return attend(m, [SEL_STEPS * k + r for r in range(SEL_STEPS)], ksel_ref, tsel_ref, vsel_ref, accs_ref,
                      sel_q_ext)

    lax.fori_loop(0, (blk + SEL_STEPS) // SEL_STEPS, sel_body, m0)

    def gate_row(branch):
        g = jnp.concatenate([gate_ref[branch, h:h + 1, u * BLOCK_Q:(u + 1) * BLOCK_Q]
                             for u in range(SUBTILES) for h in range(HEADS_PER_GROUP)], axis=1)
        return jax.nn.sigmoid(g.astype(F32))

    o_c = jnp.concatenate([oc_ref[u] for u in range(SUBTILES)], axis=1).astype(F32)
    out = (gate_row(0) * o_c + gate_row(1) * _normalized(accs_ref[...])
           + gate_row(2) * _normalized(accw_ref[...]))
    for u in range(SUBTILES):
        o_ref[u * BLOCK_Q:(u + 1) * BLOCK_Q, :] = _heads_to_columns(
            out[:, u * GROUP_LANES:(u + 1) * GROUP_LANES]).astype(o_ref.dtype)


def _selwin_attention(h_t, ksel_aug, kk, vsel_aug, vwin_aug, sel, oc_t, gates_t, vb_sel, vb_win, batch, seq):
    nq = seq // BLOCK_Q
    nblk = nq // SUBTILES
    tokens = batch * seq
    q_blk0 = SWA_HEADS * HEAD_DIM // GROUP_COLS
    pair = SUBTILES * BLOCK_Q
    lanes = SUBTILES * GROUP_LANES
    return pl.pallas_call(
        functools.partial(_selwin_kernel, nq=nq),
        grid=(GROUPS, batch, nblk),
        in_specs=[
            pl.BlockSpec((GROUP_COLS, pair), lambda g, b, i: (q_blk0 + g, b * nblk + i)),
            pl.BlockSpec((None, seq, HEAD_DIM + SEL_ROWS), lambda g, b, i: (g, b, 0)),
            pl.BlockSpec((None, None, seq, HEAD_DIM), lambda g, b, i: (3, g, b, 0)),
            pl.BlockSpec((None, None, nblk, V_AUG_ROWS, pair), lambda g, b, i: (g, b, 0, 0, 0)),
            pl.BlockSpec((None, None, nblk, V_AUG_ROWS, pair), lambda g, b, i: (g, b, 0, 0, 0)),
            pl.BlockSpec((None, None, nblk, pair // SEL_BLOCK, pair), lambda g, b, i: (b, g, 0, 0, i)),
            pl.BlockSpec((None, None, SUBTILES, HEAD_DIM, GROUP_LANES), lambda g, b, i: (b, g, i, 0, 0)),
            pl.BlockSpec((None, 3, HEADS_PER_GROUP, pair), lambda g, b, i: (g, 0, 0, b * nblk + i)),
            pl.BlockSpec((None, HEADS_PER_GROUP, nq + 2, BLOCK_Q), lambda g, b, i: (g, 0, 0, 0)),
            pl.BlockSpec((None, HEADS_PER_GROUP, N_WIN_SLOTS + 1, BLOCK_Q), lambda g, b, i: (g, 0, 0, 0)),
        ],
        out_specs=pl.BlockSpec((pair, GROUP_COLS), lambda g, b, i: (b * nblk + i, g)),
        out_shape=jax.ShapeDtypeStruct((tokens, NSA_HEADS * HEAD_DIM), BF16),
        scratch_shapes=[pltpu.VMEM((nq + 1, BLOCK_Q, GROUP_LANES), BF16),
                        pltpu.VMEM((N_WIN_SLOTS, BLOCK_Q, GROUP_LANES), BF16),
                        pltpu.VMEM((V_AUG_ROWS, lanes), F32),
                        pltpu.VMEM((V_AUG_ROWS, lanes), F32)],
        compiler_params=_params(3),
        name="nsa_sel_win",
    )(h_t, ksel_aug, kk, vsel_aug, vwin_aug, sel, oc_t, gates_t, vb_sel, vb_win)


ROW_SPLITS = 2


def _layer_norm(y, g_ref, b_ref):
    mu = jnp.mean(y, axis=-1, keepdims=True)
    yc = y - mu
    var = jnp.mean(yc * yc, axis=-1, keepdims=True)
    return yc * lax.rsqrt(var + LN_EPS) * g_ref[...] + b_ref[...]


def _merge_kernel(x_ref, ya_ref, yb_ref, wga_ref, wgb_ref, wa_ref, wb_ref, o_ref):
    x = x_ref[...]
    ga = jax.nn.sigmoid(jnp.dot(x, wga_ref[...], preferred_element_type=F32))
    gb = jax.nn.sigmoid(jnp.dot(x, wgb_ref[...], preferred_element_type=F32))
    a = jnp.dot(ya_ref[...], wa_ref[...], preferred_element_type=F32)
    bb = jnp.dot(yb_ref[...], wb_ref[...], preferred_element_type=F32)
    o_ref[...] = (ga * a + gb * bb).astype(o_ref.dtype)


def _merge(x_bf, y_a, y_b, w_ga, w_gb, w_a, w_b, *, tm, tn):
    tokens, d = x_bf.shape
    ya_cols, yb_cols = y_a.shape[1], y_b.shape[1]
    return pl.pallas_call(
        _merge_kernel,
        grid=(tokens // tm, d // tn),
        in_specs=[
            pl.BlockSpec((tm, d), lambda i, j: (i, 0)),
            pl.BlockSpec((tm, ya_cols), lambda i, j: (i, 0)),
            pl.BlockSpec((tm, yb_cols), lambda i, j: (i, 0)),
            pl.BlockSpec((d, tn), lambda i, j: (0, j)),
            pl.BlockSpec((d, tn), lambda i, j: (0, j)),
            pl.BlockSpec((ya_cols, tn), lambda i, j: (0, j)),
            pl.BlockSpec((yb_cols, tn), lambda i, j: (0, j)),
        ],
        out_specs=pl.BlockSpec((tm, tn), lambda i, j: (i, j)),
        out_shape=jax.ShapeDtypeStruct((tokens, d), BF16),
        compiler_params=_params(2),
        name="branch_merge",
    )(x_bf, y_a, y_b, w_ga, w_gb, w_a, w_b)


def _mix_ln_kernel(m_ref, x_ref, w_ref, g_ref, b_ref, o_ref):
    rows = m_ref.shape[0] // ROW_SPLITS
    for r in range(ROW_SPLITS):
        sl = slice(r * rows, (r + 1) * rows)
        mix = jnp.dot(m_ref[sl, :], w_ref[...], preferred_element_type=F32)
        o_ref[sl, :] = _layer_norm(DN_ALPHA * x_ref[sl, :] + mix, g_ref, b_ref)


def _mix_ln(merged, x, w_mix, ln_g, ln_b, *, tm):
    tokens, d = x.shape
    return pl.pallas_call(
        _mix_ln_kernel,
        grid=(tokens // tm,),
        in_specs=[
            pl.BlockSpec((tm, d), lambda i: (i, 0)),
            pl.BlockSpec((tm, d), lambda i: (i, 0)),
            pl.BlockSpec((d, d), lambda i: (0, 0)),
            pl.BlockSpec((1, d), lambda i: (0, 0)),
            pl.BlockSpec((1, d), lambda i: (0, 0)),
        ],
        out_specs=pl.BlockSpec((tm, d), lambda i: (i, 0)),
        out_shape=jax.ShapeDtypeStruct((tokens, d), F32),
        compiler_params=_params(1),
        name="mix_out_ln1",
    )(merged, x, w_mix, ln_g, ln_b)


def _xa_kernel(h_ref, wq_ref, k_ref, v_ref, wo_ref, g_ref, b_ref, o_ref):
    rows = h_ref.shape[0] // ROW_SPLITS
    subs = [slice(r * rows, (r + 1) * rows) for r in range(ROW_SPLITS)]
    qs = [(jnp.dot(h_ref[sl, :].astype(BF16), wq_ref[...], preferred_element_type=F32)
           * (XA_HEAD_DIM ** -0.5)).astype(BF16) for sl in subs]
    outs = [[] for _ in subs]
    for hd in range(XA_HEADS):
        cols = slice(hd * XA_HEAD_DIM, (hd + 1) * XA_HEAD_DIM)
        ss = [lax.dot_general(q[:, cols], k_ref[:, cols], (((1,), (1,)), ((), ())), preferred_element_type=F32)
              for q in qs]
        for r, s in enumerate(ss):
            p = jnp.exp(s - jnp.max(s, axis=-1, keepdims=True))
            l = jnp.sum(p, axis=-1, keepdims=True)
            o = jnp.dot(p.astype(BF16), v_ref[:, cols], preferred_element_type=F32)
            outs[r].append(o * (1.0 / l))
    xas = [jnp.dot(jnp.concatenate(o, axis=1).astype(BF16), wo_ref[...], preferred_element_type=F32) for o in outs]
    for sl, xa in zip(subs, xas):
        o_ref[sl, :] = _layer_norm(DN_ALPHA * h_ref[sl, :] + xa, g_ref, b_ref)


def _cross_attention_ln(h, kv_mem, w_q, w_o, ln_g, ln_b, batch, seq, *, tm):
    tokens, d = h.shape
    mem_len = kv_mem.shape[0] // batch
    xa_dim = XA_HEADS * XA_HEAD_DIM
    nt = seq // tm
    return pl.pallas_call(
        _xa_kernel,
        grid=(batch, nt),
        in_specs=[
            pl.BlockSpec((tm, d), lambda b, i: (b * nt + i, 0)),
            pl.BlockSpec((d, xa_dim), lambda b, i: (0, 0)),
            pl.BlockSpec((mem_len, xa_dim), lambda b, i: (b, 0)),
            pl.BlockSpec((mem_len, xa_dim), lambda b, i: (b, 1)),
            pl.BlockSpec((xa_dim, d), lambda b, i: (0, 0)),
            pl.BlockSpec((1, d), lambda b, i: (0, 0)),
            pl.BlockSpec((1, d), lambda b, i: (0, 0)),
        ],
        out_specs=pl.BlockSpec((tm, d), lambda b, i: (b * nt + i, 0)),
        out_shape=jax.ShapeDtypeStruct((tokens, d), F32),
        compiler_params=_params(2),
        name="cross_attention_ln2",
    )(h, w_q, kv_mem, kv_mem, w_o, ln_g, ln_b)


def _mlp_kernel(h_ref, w1_ref, w2_ref, g_ref, b_ref, o_ref, hb_ref):
    j = pl.program_id(1)

    @pl.when(j == 0)
    def _():
        hb_ref[...] = h_ref[...].astype(BF16)
        o_ref[...] = jnp.zeros_like(o_ref)

    rows = hb_ref.shape[0] // ROW_SPLITS
    subs = [slice(r * rows, (r + 1) * rows) for r in range(ROW_SPLITS)]
    ups = [jnp.dot(hb_ref[sl, :], w1_ref[...], preferred_element_type=F32) for sl in subs]
    for sl, u in zip(subs, ups):
        u = jnp.square(jnp.maximum(u, 0.0)).astype(BF16)
        o_ref[sl, :] += jnp.dot(u, w2_ref[...], preferred_element_type=F32)

    @pl.when(j == pl.num_programs(1) - 1)
    def _():
        o_ref[...] = _layer_norm(DN_ALPHA * h_ref[...] + o_ref[...], g_ref, b_ref)


def _mlp_ln(h, w1, w2, ln_g, ln_b, *, tm, tf):
    tokens, d = h.shape
    d_ff = w1.shape[1]
    return pl.pallas_call(
        _mlp_kernel,
        grid=(tokens // tm, d_ff // tf),
        in_specs=[
            pl.BlockSpec((tm, d), lambda i, j: (i, 0)),
            pl.BlockSpec((d, tf), lambda i, j: (0, j)),
            pl.BlockSpec((tf, d), lambda i, j: (j, 0)),
            pl.BlockSpec((1, d), lambda i, j: (0, 0)),
            pl.BlockSpec((1, d), lambda i, j: (0, 0)),
        ],
        out_specs=pl.BlockSpec((tm, d), lambda i, j: (i, 0)),
        out_shape=jax.ShapeDtypeStruct((tokens, d), F32),
        scratch_shapes=[pltpu.VMEM((tm, d), BF16)],
        compiler_params=_params(2),
        name="mlp_ln3",
    )(h, w1, w2, ln_g, ln_b)


def _overlap_t(seq):
    ncp, nsel = seq // CMP_STRIDE, seq // SEL_BLOCK
    c_start = np.arange(ncp)[None, :] * CMP_STRIDE
    s_start = np.arange(nsel)[:, None] * SEL_BLOCK
    ov = (c_start < s_start + SEL_BLOCK) & (c_start + CMP_BLOCK > s_start) & (np.arange(ncp)[None, :] < ncp - 1)
    return jnp.asarray(ov, BF16)


def kernel(x, mem, w_in, attn_sinks, rel_bias_table, cmp_pos_k, cmp_w1_k, cmp_w2_k, cmp_pos_v, cmp_w1_v, cmp_w2_v,
           w_branch_swa, w_branch_nsa, w_mix_out, ln1_g, ln1_b, xa_w_q, xa_w_kv, xa_w_o, ln2_g, ln2_b,
           mlp_w1, mlp_w2, ln3_g, ln3_b):
    batch, seq, d = x.shape
    tokens = batch * seq
    nq = seq // BLOCK_Q
    assert w_in.shape[0] == 1, "one layer"
    assert seq % (8 * BLOCK_Q) == 0 and d % 512 == 0

    w = w_in[0]
    sizes = (SWA_HEADS * HEAD_DIM, 128, 128, NSA_HEADS * HEAD_DIM, 128, 128, 128, 128, 128, 128, 3 * NSA_HEADS, d, d)
    offs = np.concatenate([[0], np.cumsum(sizes)])
    (w_qa, w_ka, w_va, w_qb, w_kc, w_vc, w_ks, w_vs, w_kw, w_vw, w_gn, w_ga, w_gb) = [
        w[:, offs[n]:offs[n + 1]] for n in range(len(sizes))]
    w_rows = jnp.concatenate([w_ka, w_kc, w_ks, w_kw, w_vc], axis=1).astype(BF16)
    w_gn_pad = jnp.pad(w_gn, ((0, 0), (0, 128 - w_gn.shape[1])))
    w_cols_t = jnp.concatenate([w_qa, w_qb, w_va, w_vs, w_vw, w_gn_pad], axis=1).T.astype(BF16)

    x2 = x.reshape(tokens, d)
    x_bf = x2.astype(BF16)

    h_rows = _matmul(x_bf, w_rows, nt=False, tm=min(2048, tokens), tn=w_rows.shape[1])
    h_t = _matmul(w_cols_t, x_bf, nt=True, tm=1280, tn=min(1024, tokens))

    kk = h_rows[:, :512].reshape(tokens, 4, GROUPS, HEAD_DIM).transpose(1, 2, 0, 3)
    v_cmp = h_rows[:, 512:].reshape(tokens, GROUPS, HEAD_DIM).transpose(1, 0, 2)

    def value_tiles(row0, width):
        v = h_t[row0:row0 + 128].reshape(GROUPS, HEAD_DIM, batch, seq // width, width).transpose(0, 2, 3, 1, 4)
        ones = jnp.ones(v.shape[:3] + (V_AUG_ROWS - HEAD_DIM, width), BF16)
        return jnp.concatenate([v, ones], axis=3)

    vswa_aug = value_tiles(2048, BLOCK_Q)
    vsel_aug, vwin_aug = value_tiles(2176, 2 * BLOCK_Q), value_tiles(2304, 2 * BLOCK_Q)
    gates_t = h_t[2432:2432 + 3 * NSA_HEADS].reshape(GROUPS, HEADS_PER_GROUP, 3, tokens).transpose(0, 2, 1, 3)
    block_of_key = (jnp.arange(tokens) % BLOCK_Q) // SEL_BLOCK
    onehot = (block_of_key[:, None] == jnp.arange(SEL_ROWS)[None, :]).astype(BF16)
    ksel_aug = jnp.concatenate([kk[2], jnp.broadcast_to(onehot, (GROUPS, tokens, SEL_ROWS))], axis=2)

    by_dist = rel_bias_table[_rel_bucket(jnp.arange(seq))].astype(F32).T
    vb_swa = _bias_by_distance(by_dist[:SWA_HEADS], 4, window=SWA_WINDOW)
    vb_sel = _bias_by_distance(by_dist[SWA_HEADS:], nq + 2)
    vb_win = _bias_by_distance(by_dist[SWA_HEADS:], N_WIN_SLOTS + 1, window=NSA_WINDOW)
    vb_cmp = vb_sel
    sink_rows = jnp.repeat(attn_sinks[0].astype(F32), BLOCK_Q).reshape(GROUPS, 1, GROUP_LANES)

    y_a = _swa_attention(h_t, kk, vswa_aug, sink_rows, vb_swa, batch, seq)

    ncp = seq // CMP_STRIDE
    chunk_w = CMP_STRIDE * HEAD_DIM
    chunks = jnp.stack([kk[1], v_cmp]).reshape(2, GROUPS, batch, ncp, chunk_w)
    pos = jnp.stack([cmp_pos_k[0], cmp_pos_v[0]]).astype(F32).reshape(2, 2, 1, chunk_w)
    w1 = jnp.stack([cmp_w1_k[0], cmp_w1_v[0]]).astype(BF16).reshape(2, 2, chunk_w, -1)
    w2 = jnp.stack([cmp_w2_k[0], cmp_w2_v[0]]).astype(BF16)
    cn, ct = _compress(chunks, pos, w1, w2, w2.transpose(0, 2, 1))
    oc_t, sel = _cmp_attention(h_t, cn, ct, vb_cmp, _overlap_t(seq), batch, seq)
    sel = sel.reshape(batch, GROUPS, nq // 2, 4, seq)
    y_b = _selwin_attention(h_t, ksel_aug, kk, vsel_aug, vwin_aug, sel, oc_t, gates_t, vb_sel, vb_win, batch, seq)

    merged = _merge(x_bf, y_a, y_b, w_ga.astype(BF16), w_gb.astype(BF16),
                    w_branch_swa[0].astype(BF16), w_branch_nsa[0].astype(BF16), tm=min(1024, tokens), tn=512)
    h1 = _mix_ln(merged, x2, w_mix_out[0].astype(BF16), ln1_g, ln1_b, tm=512)

    mem_bf = mem.reshape(-1, d).astype(BF16)
    kv_mem = _matmul(mem_bf, xa_w_kv[0].astype(BF16), nt=False, tm=mem_bf.shape[0], tn=512)
    h2 = _cross_attention_ln(h1, kv_mem, xa_w_q[0].astype(BF16), xa_w_o[0].astype(BF16), ln2_g, ln2_b,
                             batch, seq, tm=512)

    h3 = _mlp_ln(h2, mlp_w1[0].astype(BF16), mlp_w2[0].astype(BF16), ln3_g, ln3_b, tm=1024, tf=512)
    return h3.reshape(batch, seq, d)
```

```python
import functools
import math

import numpy as np
import jax
import jax.numpy as jnp
from jax import lax
from jax.experimental import pallas as pl
from jax.experimental.pallas import tpu as pltpu

F32 = jnp.float32
BF16 = jnp.bfloat16

HEAD_DIM = 64
BLOCK_Q = 128
SWA_HEADS = 16
SWA_KV_HEADS = 2
SWA_WINDOW = 128
NSA_HEADS = 16
NSA_KV_HEADS = 2
CMP_BLOCK = 32
CMP_STRIDE = 16
SEL_BLOCK = 64
SEL_TOPK = 16
SEL_LOCAL = 2
NSA_WINDOW = 512
REL_BUCKETS = 32
REL_MAX_DIST = 4096
XA_HEADS = 4
XA_HEAD_DIM = 128
DEPTH = 1
DN_ALPHA = (2.0 * DEPTH) ** 0.25
LN_EPS = 1e-5
NEG_INF = -1e30
FORCE_SCORE = 1e4

GROUPS = 2
HEADS_PER_GROUP = 8
GROUP_LANES = HEADS_PER_GROUP * BLOCK_Q
GROUP_COLS = HEADS_PER_GROUP * HEAD_DIM
SUBTILES = 2

V7X_VMEM_LIMIT_BYTES = 56 * 1024 * 1024


def _params(n_axes):
    return pltpu.CompilerParams(dimension_semantics=("arbitrary",) * n_axes,
                                vmem_limit_bytes=V7X_VMEM_LIMIT_BYTES)


def _mm_kernel(a_ref, b_ref, o_ref, *, nt):
    if nt:
        out = lax.dot_general(a_ref[...], b_ref[...], (((1,), (1,)), ((), ())), preferred_element_type=F32)
    else:
        out = jnp.dot(a_ref[...], b_ref[...], preferred_element_type=F32)
    o_ref[...] = out.astype(o_ref.dtype)


def _matmul(a, b, *, nt, tm, tn, out_dtype=BF16):
    m, k = a.shape
    n = b.shape[0] if nt else b.shape[1]
    b_spec = pl.BlockSpec((tn, k), lambda i, j: (j, 0)) if nt else pl.BlockSpec((k, tn), lambda i, j: (0, j))
    return pl.pallas_call(
        functools.partial(_mm_kernel, nt=nt),
        grid=(m // tm, n // tn),
        in_specs=[pl.BlockSpec((tm, k), lambda i, j: (i, 0)), b_spec],
        out_specs=pl.BlockSpec((tm, tn), lambda i, j: (i, j)),
        out_shape=jax.ShapeDtypeStruct((m, n), out_dtype),
        compiler_params=_params(2),
        name="proj_nt" if nt else "proj_nn",
    )(a, b)


def _rel_bucket(dist):
    exact = REL_BUCKETS // 2
    d = jnp.maximum(dist, 0)
    log_ratio = jnp.log(jnp.maximum(d, 1).astype(F32) / exact) / math.log(REL_MAX_DIST / exact)
    large = jnp.minimum(exact + (log_ratio * (REL_BUCKETS - exact)).astype(jnp.int32), REL_BUCKETS - 1)
    return jnp.where(d < exact, d, large)


def _bias_by_distance(by_dist, rows, *, window=None):
    heads, seq = by_dist.shape
    if window is not None:
        by_dist = jnp.where(jnp.arange(seq) < window, by_dist, NEG_INF)
    padded = jnp.pad(by_dist, ((0, 0), (2 * BLOCK_Q, 0)), constant_values=NEG_INF)[:, :rows * BLOCK_Q]
    return padded.reshape(heads // HEADS_PER_GROUP, HEADS_PER_GROUP, rows, BLOCK_Q)


def _build_skew_table(vb_ref, tbl_ref, n_tiles):
    def body(d, carry):
        for h in range(HEADS_PER_GROUP):
            lo = jnp.broadcast_to(vb_ref[h, pl.ds(d, 1), :], (BLOCK_Q, BLOCK_Q))
            hi = jnp.broadcast_to(vb_ref[h, pl.ds(d + 1, 1), :], (BLOCK_Q, BLOCK_Q))
            y = pltpu.roll(jnp.concatenate([lo, hi], axis=1), 0, 1, stride=1, stride_axis=0)
            tbl_ref[d, :, h * BLOCK_Q:(h + 1) * BLOCK_Q] = y[:, BLOCK_Q:].astype(tbl_ref.dtype)
        return carry

    lax.fori_loop(0, n_tiles, body, 0)


def _build_cmp_bias_table(vb_ref, tbl_ref, skew_ref, nq):
    per_tile = BLOCK_Q // CMP_STRIDE
    assert per_tile == 8 and CMP_BLOCK - 1 + CMP_STRIDE * (per_tile - 1) - BLOCK_Q == CMP_STRIDE - 1
    tbl_ref[pl.ds(nq * 8, (nq - 1) * 8), :] = jnp.full(((nq - 1) * 8, GROUP_LANES), NEG_INF, F32)

    def body(d, last_row):
        for h in range(HEADS_PER_GROUP):
            lo = jnp.broadcast_to(vb_ref[h, pl.ds(d + 1, 1), :], (BLOCK_Q, BLOCK_Q))
            hi = jnp.broadcast_to(vb_ref[h, pl.ds(d + 2, 1), :], (BLOCK_Q, BLOCK_Q))
            y = pltpu.roll(jnp.concatenate([lo, hi], axis=1), 0, 1, stride=1, stride_axis=0)
            skew_ref[:, h * BLOCK_Q:(h + 1) * BLOCK_Q] = y[:, BLOCK_Q:]
        offsets = [CMP_BLOCK - 1 + CMP_STRIDE * c for c in range(per_tile - 1)]
        rows = [skew_ref[k:k + 1, :] for k in offsets] + [last_row]
        tbl_ref[pl.ds(pl.multiple_of((nq - 1 - d) * 8, 8), 8), :] = jnp.concatenate(rows, axis=0)
        return skew_ref[CMP_STRIDE - 1:CMP_STRIDE, :]

    lax.fori_loop(0, nq, body, jnp.full((1, GROUP_LANES), NEG_INF, F32))


def _gather_heads_to_lanes(q_ref):
    q = jnp.concatenate([q_ref[h * HEAD_DIM:(h + 1) * HEAD_DIM, u * BLOCK_Q:(u + 1) * BLOCK_Q]
                         for u in range(q_ref.shape[1] // BLOCK_Q) for h in range(HEADS_PER_GROUP)], axis=1)
    return q * jnp.asarray(HEAD_DIM ** -0.5, q.dtype)


def _heads_to_columns(o_t):
    stacked = jnp.concatenate([o_t[:, h * BLOCK_Q:(h + 1) * BLOCK_Q] for h in range(HEADS_PER_GROUP)], axis=0)
    return stacked.T


def _identity_tile():
    r = lax.broadcasted_iota(jnp.int32, (BLOCK_Q, BLOCK_Q), 0)
    c = lax.broadcasted_iota(jnp.int32, (BLOCK_Q, BLOCK_Q), 1)
    return jnp.where(r == c, 1.0, 0.0).astype(BF16)


def _biased_scores(eye, k_t, bias_tile, q_t):
    lhs = jnp.concatenate([eye, k_t], axis=1)
    rhs = jnp.concatenate([bias_tile, q_t], axis=0)
    return jnp.dot(lhs, rhs, preferred_element_type=F32)


def _softmax_probs(m, s_tiles):
    ms, alphas, ps = [], [], [[] for _ in s_tiles]
    for h in range(m.shape[1] // BLOCK_Q):
        lanes = slice(h * BLOCK_Q, (h + 1) * BLOCK_Q)
        cols = [s[:, lanes] for s in s_tiles]
        m_new = m[:, lanes]
        for c in cols:
            m_new = jnp.maximum(m_new, jnp.max(c, axis=0, keepdims=True))
        ms.append(m_new)
        alphas.append(jnp.exp(m[:, lanes] - m_new))
        for j, c in enumerate(cols):
            ps[j].append(jnp.exp((c - m_new).astype(BF16)))
    p = jnp.concatenate([jnp.concatenate(pj, axis=1) for pj in ps], axis=0)
    return jnp.concatenate(ms, axis=1), jnp.concatenate(alphas, axis=1), p


def _softmax_update(carry, s_tiles, v_aug):
    m, acc = carry
    m, alpha, p = _softmax_probs(m, s_tiles)
    return m, alpha * acc + jnp.dot(v_aug, p, preferred_element_type=F32)


def _pipelined_attention(n_steps, scores_fn, values_fn, init, s_scr, p_scr, acc_scr, *, fill):
    m0, acc0 = init

    def put_scores(n):
        for j, s in enumerate(scores_fn(n)):
            s_scr[j] = s

    def fold(alpha, n, p):
        acc_scr[...] = alpha * acc_scr[...] + jnp.dot(values_fn(n), p, preferred_element_type=F32)

    if fill:
        put_scores(0)
        p_scr[...] = jnp.zeros_like(p_scr)
        acc_scr[...] = acc0
        return None

    def body(k, carry):
        m, alpha_a, alpha_b = carry
        a = 2 * k
        s_b = scores_fn(a + 1)
        fold(alpha_a, a - 2, p_scr[0])
        fold(alpha_b, a - 1, p_scr[1])
        m, alpha_a, p_a = _softmax_probs(m, [s_scr.at[j] for j in range(s_scr.shape[0])])
        put_scores(a + 2)
        m, alpha_b, p_b = _softmax_probs(m, s_b)
        p_scr[0] = p_a
        p_scr[1] = p_b
        return m, alpha_a, alpha_b

    trips = (n_steps + 1) // 2
    ones = jnp.ones_like(m0)
    m, alpha_a, alpha_b = lax.fori_loop(0, trips, body, (m0, ones, ones))
    fold(alpha_a, 2 * trips - 2, p_scr[0])
    fold(alpha_b, 2 * trips - 1, p_scr[1])
    return m, acc_scr[...]


SLAB = 2 * BLOCK_Q
LOOKAHEAD = 12


def _slabwise_softmax_step(m, acc_ref, lhs_tiles, rhs_fn, v_fn):
    n_tiles = len(lhs_tiles)
    items = [(sl, j) for sl in range(m.shape[1] // SLAB) for j in range(n_tiles)]

    def scores(k):
        sl, j = items[k]
        lhs = lhs_tiles[j](sl) if callable(lhs_tiles[j]) else lhs_tiles[j]
        return jnp.dot(lhs, rhs_fn(j, sl), preferred_element_type=F32)

    pending = [scores(k) for k in range(min(LOOKAHEAD, len(items)))]
    new_m = []
    for k, (sl, j) in enumerate(items):
        lanes = slice(sl * SLAB, (sl + 1) * SLAB)
        if j == 0:
            m_s, acc_s = m[:, lanes], acc_ref[:, lanes]
        s = pending.pop(0)
        if k + LOOKAHEAD < len(items):
            pending.append(scores(k + LOOKAHEAD))
        m_next = jnp.maximum(m_s, jnp.max(s, axis=0, keepdims=True))
        p = jnp.exp((s - m_next).astype(BF16))
        acc_s = jnp.exp(m_s - m_next) * acc_s + jnp.dot(v_fn(j, sl), p, preferred_element_type=F32)
        m_s = m_next
        if j == n_tiles - 1:
            acc_ref[:, lanes] = acc_s
            new_m.append(m_s)
    return jnp.concatenate(new_m, axis=1)


V_AUG_ROWS = HEAD_DIM + 16


def _normalized(acc):
    return acc[:HEAD_DIM] * (1.0 / acc[HEAD_DIM:HEAD_DIM + 1])


def _swa_kernel(q_ref, kprev_ref, kcur_ref, vprev_ref, vcur_ref, sink_ref, vb_ref, o_ref, tbl_ref, acc_ref):
    b, i = pl.program_id(1), pl.program_id(2)

    @pl.when((b == 0) & (i == 0))
    def _():
        _build_skew_table(vb_ref, tbl_ref, 3)

    eye = _identity_tile()
    q_t = _gather_heads_to_lanes(q_ref)
    lanes = q_t.shape[1]
    slabs_per_tile = GROUP_LANES // SLAB
    m0 = jnp.concatenate([sink_ref[...]] * SWA_TILES, axis=1)
    acc_ref[...] = jnp.where(lax.broadcasted_iota(jnp.int32, (V_AUG_ROWS, lanes), 0) < HEAD_DIM, 0.0, 1.0)

    def key_tile(j):
        def at(slab):
            u = slab // slabs_per_tile - j
            k_t = kprev_ref[...] if u < 0 else kcur_ref[u * BLOCK_Q:(u + 1) * BLOCK_Q, :]
            return jnp.concatenate([eye, k_t], axis=1)
        return at

    def rhs(j, slab):
        u, part = slab // slabs_per_tile, slab % slabs_per_tile
        slot = 1 + j if (u > 0 or j == 0) else jnp.where(i > 0, 2, 0)
        return jnp.concatenate([tbl_ref[slot, :, part * SLAB:(part + 1) * SLAB],
                                q_t[:, slab * SLAB:(slab + 1) * SLAB]], axis=0)

    def values(j, slab):
        u = slab // slabs_per_tile - j
        return vprev_ref[...] if u < 0 else vcur_ref[u]

    _slabwise_softmax_step(m0, acc_ref, [key_tile(0), key_tile(1)], rhs, values)
    out = _normalized(acc_ref[...])
    for u in range(SWA_TILES):
        o_ref[u * BLOCK_Q:(u + 1) * BLOCK_Q, :] = _heads_to_columns(
            out[:, u * GROUP_LANES:(u + 1) * GROUP_LANES]).astype(o_ref.dtype)


SWA_TILES = 4


def _swa_attention(h_t, kk, v_aug, sink_rows, vb, batch, seq):
    nq = seq // BLOCK_Q
    nblk = nq // SWA_TILES
    width = SWA_TILES * BLOCK_Q
    tokens = batch * seq

    def prev_tile(i):
        return jnp.maximum(i * SWA_TILES - 1, 0)

    return pl.pallas_call(
        _swa_kernel,
        grid=(GROUPS, batch, nblk),
        in_specs=[
            pl.BlockSpec((GROUP_COLS, width), lambda g, b, i: (g, b * nblk + i)),
            pl.BlockSpec((None, None, BLOCK_Q, HEAD_DIM), lambda g, b, i: (0, g, b * nq + prev_tile(i), 0)),
            pl.BlockSpec((None, None, width, HEAD_DIM), lambda g, b, i: (0, g, b * nblk + i, 0)),
            pl.BlockSpec((None, None, None, V_AUG_ROWS, BLOCK_Q), lambda g, b, i: (g, b, prev_tile(i), 0, 0)),
            pl.BlockSpec((None, None, SWA_TILES, V_AUG_ROWS, BLOCK_Q), lambda g, b, i: (g, b, i, 0, 0)),
            pl.BlockSpec((None, 1, GROUP_LANES), lambda g, b, i: (g, 0, 0)),
            pl.BlockSpec((None, HEADS_PER_GROUP, 4, BLOCK_Q), lambda g, b, i: (g, 0, 0, 0)),
        ],
        out_specs=pl.BlockSpec((width, GROUP_COLS), lambda g, b, i: (b * nblk + i, g)),
        out_shape=jax.ShapeDtypeStruct((tokens, SWA_HEADS * HEAD_DIM), BF16),
        scratch_shapes=[pltpu.VMEM((3, BLOCK_Q, GROUP_LANES), BF16),
                        pltpu.VMEM((V_AUG_ROWS, SWA_TILES * GROUP_LANES), F32)],
        compiler_params=_params(3),
        name="swa_attention",
    )(h_t, kk, kk, v_aug, v_aug, sink_rows, vb)


def _compress_kernel(c_ref, pos_ref, w1_ref, w2_ref, w2t_ref, cn_ref, ct_ref):
    c = c_ref[...].astype(F32)
    top = (c + pos_ref[0]).astype(BF16)
    bot = (c + pos_ref[1]).astype(BF16)
    a = jnp.dot(top, w1_ref[0], preferred_element_type=F32)
    bm = jnp.dot(bot, w1_ref[1], preferred_element_type=F32)
    n = a.shape[0]
    pre = a + pltpu.roll(bm, n - 1, 0)
    hid = jax.nn.gelu(pre).astype(BF16)
    cn_ref[...] = jnp.dot(hid, w2_ref[...], preferred_element_type=F32).astype(cn_ref.dtype)
    ct_ref[...] = lax.dot_general(w2t_ref[...], hid, (((1,), (1,)), ((), ())),
                                  preferred_element_type=F32).astype(ct_ref.dtype)


def _compress(chunks, pos, w1, w2, w2t):
    _, g, b, ncp, width = chunks.shape
    hidden = w1.shape[-1]
    return pl.pallas_call(
        _compress_kernel,
        grid=(2, g, b),
        in_specs=[
            pl.BlockSpec((None, None, None, ncp, width), lambda t, g, b: (t, g, b, 0, 0)),
            pl.BlockSpec((None, 2, 1, width), lambda t, g, b: (t, 0, 0, 0)),
            pl.BlockSpec((None, 2, width, hidden), lambda t, g, b: (t, 0, 0, 0)),
            pl.BlockSpec((None, hidden, HEAD_DIM), lambda t, g, b: (t, 0, 0)),
            pl.BlockSpec((None, HEAD_DIM, hidden), lambda t, g, b: (t, 0, 0)),
        ],
        out_specs=[
            pl.BlockSpec((None, None, None, ncp, HEAD_DIM), lambda t, g, b: (t, g, b, 0, 0)),
            pl.BlockSpec((None, None, None, HEAD_DIM, ncp), lambda t, g, b: (t, g, b, 0, 0)),
        ],
        out_shape=[jax.ShapeDtypeStruct((2, g, b, ncp, HEAD_DIM), BF16),
                   jax.ShapeDtypeStruct((2, g, b, HEAD_DIM, ncp), BF16)],
        compiler_params=_params(3),
        name="nsa_compress",
    )(chunks, pos, w1, w2, w2t)


def _cmp_kernel(q_ref, kc_ref, vct_ref, vb_ref, ov_ref, oc_ref, sel_ref, bias_ref, skew_ref, *, nq):
    b, i = pl.program_id(1), pl.program_id(2)

    @pl.when((b == 0) & (i == 0))
    def _():
        _build_cmp_bias_table(vb_ref, bias_ref, skew_ref, nq)

    ncp = kc_ref.shape[0]
    nsel = sel_ref.shape[0]
    n_tiles = ncp // BLOCK_Q
    q_t = _gather_heads_to_lanes(q_ref)
    width = SUBTILES * BLOCK_Q
    eye = _identity_tile()

    slabs_per_tile = GROUP_LANES // SLAB
    lhs = [jnp.concatenate([eye, kc_ref[t * BLOCK_Q:(t + 1) * BLOCK_Q, :]], axis=1) for t in range(n_tiles)]
    items = [(sl, t) for sl in range(SUBTILES * slabs_per_tile) for t in range(n_tiles)]

    def scores(k):
        sl, t = items[k]
        u, part = sl // slabs_per_tile, sl % slabs_per_tile
        row0 = pl.multiple_of((nq - 1 - (SUBTILES * i + u)) * 8 + t * BLOCK_Q, 8)
        bias = bias_ref[pl.ds(row0, BLOCK_Q), part * SLAB:(part + 1) * SLAB].astype(BF16)
        return jnp.dot(lhs[t], jnp.concatenate([bias, q_t[:, sl * SLAB:(sl + 1) * SLAB]], axis=0),
                       preferred_element_type=F32)

    q_pos = i * width + lax.broadcasted_iota(jnp.int32, (1, width), 1)
    sees_any = q_pos >= CMP_BLOCK - 1
    psum = [[jnp.zeros((BLOCK_Q, BLOCK_Q), F32) for _ in range(n_tiles)] for _ in range(SUBTILES)]
    pending = [scores(k) for k in range(min(LOOKAHEAD, len(items)))]
    for sl in range(SUBTILES * slabs_per_tile):
        u, part = sl // slabs_per_tile, sl % slabs_per_tile
        s = [pending.pop(0) for _ in range(n_tiles)]
        for k in range(sl * n_tiles + LOOKAHEAD, min((sl + 1) * n_tiles + LOOKAHEAD, len(items))):
            pending.append(scores(k))
        m = functools.reduce(jnp.maximum, [jnp.max(c, axis=0, keepdims=True) for c in s])
        e = [jnp.exp(c - m) for c in s]
        l = functools.reduce(jnp.add, [jnp.sum(c, axis=0, keepdims=True) for c in e])
        seen = sees_any[:, u * BLOCK_Q:(u + 1) * BLOCK_Q]
        inv = jnp.where(jnp.concatenate([seen] * (SLAB // BLOCK_Q), axis=1), 1.0 / l, 0.0)
        o_slab = jnp.zeros((HEAD_DIM, SLAB), F32)
        for t in range(n_tiles):
            p = e[t] * inv
            psum[u][t] = psum[u][t] + functools.reduce(
                jnp.add, [p[:, r * BLOCK_Q:(r + 1) * BLOCK_Q] for r in range(SLAB // BLOCK_Q)])
            o_slab = o_slab + jnp.dot(vct_ref[:, t * BLOCK_Q:(t + 1) * BLOCK_Q], p.astype(BF16),
                                      preferred_element_type=F32)
        oc_ref[u, :, part * SLAB:(part + 1) * SLAB] = o_slab.astype(oc_ref.dtype)

    psum = jnp.concatenate([jnp.concatenate(pu, axis=0) for pu in psum], axis=1)
    hi = psum.astype(BF16)
    lo = (psum - hi.astype(F32)).astype(BF16)
    ov = ov_ref[...]
    score = jnp.dot(ov, hi, preferred_element_type=F32) + jnp.dot(ov, lo, preferred_element_type=F32)

    j_io = lax.broadcasted_iota(jnp.int32, (nsel, width), 0)
    qpos = i * width + lax.broadcasted_iota(jnp.int32, (nsel, width), 1)
    causal = j_io * SEL_BLOCK <= qpos
    back = qpos // SEL_BLOCK - j_io
    forced = (j_io == 0) | ((back >= 0) & (back < SEL_LOCAL))
    score = jnp.where(causal, jnp.where(forced, FORCE_SCORE, score), -1.0)
    slab_rows = lax.broadcasted_iota(jnp.int32, (8, width), 0)
    slabs = [score[8 * g:8 * (g + 1), :] for g in range(nsel // 8)]
    ranks = [jnp.zeros((8, width), F32) for _ in slabs]
    for r in range(nsel):
        row = jnp.broadcast_to(score[r:r + 1, :], (8, width))
        for g, slab in enumerate(slabs):
            if g > r // 8:
                ahead = row >= slab
            elif g < r // 8:
                ahead = row > slab
            else:
                ranks[g] = ranks[g] + jnp.where(slab_rows > r % 8, jnp.where(row >= slab, 1.0, 0.0),
                                                jnp.where(row > slab, 1.0, 0.0))
                continue
            ranks[g] = ranks[g] + jnp.where(ahead, 1.0, 0.0)
    rank = jnp.concatenate(ranks, axis=0)
    sel_ref[...] = jnp.where((rank < min(SEL_TOPK, nsel)) & causal, 1.0, 0.0).astype(sel_ref.dtype)


def _cmp_attention(h_t, cn, ct, vb, overlap_t, batch, seq):
    nq = seq // BLOCK_Q
    ncp = seq // CMP_STRIDE
    nsel = seq // SEL_BLOCK
    q_blk0 = SWA_HEADS * HEAD_DIM // GROUP_COLS
    nblk = nq // SUBTILES
    width = SUBTILES * BLOCK_Q
    return pl.pallas_call(
        functools.partial(_cmp_kernel, nq=nq),
        grid=(GROUPS, batch, nblk),
        in_specs=[
            pl.BlockSpec((GROUP_COLS, width), lambda g, b, i: (q_blk0 + g, b * nblk + i)),
            pl.BlockSpec((None, None, None, ncp, HEAD_DIM), lambda g, b, i: (0, g, b, 0, 0)),
            pl.BlockSpec((None, None, None, HEAD_DIM, ncp), lambda g, b, i: (1, g, b, 0, 0)),
            pl.BlockSpec((None, HEADS_PER_GROUP, nq + 2, BLOCK_Q), lambda g, b, i: (g, 0, 0, 0)),
            pl.BlockSpec((nsel, ncp), lambda g, b, i: (0, 0)),
        ],
        out_specs=[
            pl.BlockSpec((None, None, SUBTILES, HEAD_DIM, GROUP_LANES), lambda g, b, i: (b, g, i, 0, 0)),
            pl.BlockSpec((None, None, nsel, width), lambda g, b, i: (b, g, 0, i)),
        ],
        out_shape=[jax.ShapeDtypeStruct((batch, GROUPS, nq, HEAD_DIM, GROUP_LANES), BF16),
                   jax.ShapeDtypeStruct((batch, GROUPS, nsel, seq), F32)],
        scratch_shapes=[pltpu.VMEM(((2 * nq - 1) * 8, GROUP_LANES), F32), pltpu.VMEM((BLOCK_Q, GROUP_LANES), F32)],
        compiler_params=_params(3),
        name="nsa_cmp_select",
    )(h_t, cn, ct, vb, overlap_t)


N_WIN_PAIRS = -(-(NSA_WINDOW - 1) // (SUBTILES * BLOCK_Q)) + 1
N_WIN_SLOTS = SUBTILES * N_WIN_PAIRS + 1
SEL_ROWS = 16
SEL_STEPS = 2


def _selwin_kernel(q_ref, ksel_ref, kwin_ref, vsel_ref, vwin_ref, sel_ref, oc_ref, gate_ref, vbs_ref, vbw_ref,
                   o_ref, tsel_ref, twin_ref, accs_ref, accw_ref, *, nq):
    b, blk = pl.program_id(1), pl.program_id(2)

    @pl.when((b == 0) & (blk == 0))
    def _():
        _build_skew_table(vbs_ref, tsel_ref, nq + 1)
        _build_skew_table(vbw_ref, twin_ref, N_WIN_SLOTS)

    q_t = _gather_heads_to_lanes(q_ref)
    lanes = q_t.shape[1]
    eye = _identity_tile()
    pair = SUBTILES * BLOCK_Q
    m0 = jnp.full((1, lanes), NEG_INF, F32)
    accs_ref[...] = jnp.zeros_like(accs_ref)
    accw_ref[...] = jnp.zeros_like(accw_ref)
    blocks_per_tile = BLOCK_Q // SEL_BLOCK
    slabs_per_tile = GROUP_LANES // SLAB

    def attend(m, steps, k_ref, tbl_ref, v_ref, acc_ref, q_ext_fn):
        tiles = [(n, jnp.clip(blk - n, 0, blk), j) for n in steps for j in range(SUBTILES)]
        lhs = [jnp.concatenate([eye, k_ref[pl.ds(pl.multiple_of(p * pair + j * BLOCK_Q, BLOCK_Q), BLOCK_Q), :]],
                               axis=1) for _, p, j in tiles]
        q_exts = [q_ext_fn(p, j) for _, p, j in tiles]

        def rhs(t, slab):
            n, _, j = tiles[t]
            u, part = slab // slabs_per_tile, slab % slabs_per_tile
            slot = jnp.where(n <= blk, 2 * n + u - j + 1, 0)
            bias = tbl_ref[slot, :, part * SLAB:(part + 1) * SLAB]
            return jnp.concatenate([bias, q_exts[t][:, slab * SLAB:(slab + 1) * SLAB]], axis=0)

        def values(t, slab):
            _, p, j = tiles[t]
            return v_ref[p, :, j * BLOCK_Q:(j + 1) * BLOCK_Q]

        return _slabwise_softmax_step(m, acc_ref, lhs, rhs, values)

    attend(m0, list(range(N_WIN_PAIRS)), kwin_ref, twin_ref, vwin_ref, accw_ref, lambda p, j: q_t)

    def sel_q_ext(p, j):
        neg = jnp.where(sel_ref[p, j * blocks_per_tile:(j + 1) * blocks_per_tile, :] > 0.5, 0.0, NEG_INF)
        rows = jnp.concatenate([neg[:, u * BLOCK_Q:(u + 1) * BLOCK_Q]
                                for u in range(SUBTILES) for _ in range(HEADS_PER_GROUP)], axis=1)
        rows = jnp.concatenate([rows, jnp.zeros((SEL_ROWS - blocks_per_tile, lanes), F32)], axis=0)
        return jnp.concatenate([q_t, rows.astype(BF16)], axis=0)

    def sel_body(k, m):
        return attend(m, [SEL_STEPS * k + r for r in range(SEL_STEPS)], ksel_ref, tsel_ref, vsel_ref, accs_ref,
                      sel_q_ext)

    lax.fori_loop(0, (blk + SEL_STEPS) // SEL_STEPS, sel_body, m0)

    def gate_row(branch):
        g = jnp.concatenate([gate_ref[branch, h:h + 1, u * BLOCK_Q:(u + 1) * BLOCK_Q]
                             for u in range(SUBTILES) for h in range(HEADS_PER_GROUP)], axis=1)
        return jax.nn.sigmoid(g.astype(F32))

    o_c = jnp.concatenate([oc_ref[u] for u in range(SUBTILES)], axis=1).astype(F32)
    out = (gate_row(0) * o_c + gate_row(1) * _normalized(accs_ref[...])
           + gate_row(2) * _normalized(accw_ref[...]))
    for u in range(SUBTILES):
        o_ref[u * BLOCK_Q:(u + 1) * BLOCK_Q, :] = _heads_to_columns(
            out[:, u * GROUP_LANES:(u + 1) * GROUP_LANES]).astype(o_ref.dtype)


def _selwin_attention(h_t, ksel_aug, kk, vsel_aug, vwin_aug, sel, oc_t, gates_t, vb_sel, vb_win, batch, seq):
    nq = seq // BLOCK_Q
    nblk = nq // SUBTILES
    tokens = batch * seq
    q_blk0 = SWA_HEADS * HEAD_DIM // GROUP_COLS
    pair = SUBTILES * BLOCK_Q
    lanes = SUBTILES * GROUP_LANES
    return pl.pallas_call(
        functools.partial(_selwin_kernel, nq=nq),
        grid=(GROUPS, batch, nblk),
        in_specs=[
            pl.BlockSpec((GROUP_COLS, pair), lambda g, b, i: (q_blk0 + g, b * nblk + i)),
            pl.BlockSpec((None, seq, HEAD_DIM + SEL_ROWS), lambda g, b, i: (g, b, 0)),
            pl.BlockSpec((None, None, seq, HEAD_DIM), lambda g, b, i: (3, g, b, 0)),
            pl.BlockSpec((None, None, nblk, V_AUG_ROWS, pair), lambda g, b, i: (g, b, 0, 0, 0)),
            pl.BlockSpec((None, None, nblk, V_AUG_ROWS, pair), lambda g, b, i: (g, b, 0, 0, 0)),
            pl.BlockSpec((None, None, nblk, pair // SEL_BLOCK, pair), lambda g, b, i: (b, g, 0, 0, i)),
            pl.BlockSpec((None, None, SUBTILES, HEAD_DIM, GROUP_LANES), lambda g, b, i: (b, g, i, 0, 0)),
            pl.BlockSpec((None, 3, HEADS_PER_GROUP, pair), lambda g, b, i: (g, 0, 0, b * nblk + i)),
            pl.BlockSpec((None, HEADS_PER_GROUP, nq + 2, BLOCK_Q), lambda g, b, i: (g, 0, 0, 0)),
            pl.BlockSpec((None, HEADS_PER_GROUP, N_WIN_SLOTS + 1, BLOCK_Q), lambda g, b, i: (g, 0, 0, 0)),
        ],
        out_specs=pl.BlockSpec((pair, GROUP_COLS), lambda g, b, i: (b * nblk + i, g)),
        out_shape=jax.ShapeDtypeStruct((tokens, NSA_HEADS * HEAD_DIM), BF16),
        scratch_shapes=[pltpu.VMEM((nq + 1, BLOCK_Q, GROUP_LANES), BF16),
                        pltpu.VMEM((N_WIN_SLOTS, BLOCK_Q, GROUP_LANES), BF16),
                        pltpu.VMEM((V_AUG_ROWS, lanes), F32),
                        pltpu.VMEM((V_AUG_ROWS, lanes), F32)],
        compiler_params=_params(3),
        name="nsa_sel_win",
    )(h_t, ksel_aug, kk, vsel_aug, vwin_aug, sel, oc_t, gates_t, vb_sel, vb_win)


ROW_SPLITS = 2


def _layer_norm(y, g_ref, b_ref):
    mu = jnp.mean(y, axis=-1, keepdims=True)
    yc = y - mu
    var = jnp.mean(yc * yc, axis=-1, keepdims=True)
    return yc * lax.rsqrt(var + LN_EPS) * g_ref[...] + b_ref[...]


def _merge_kernel(x_ref, ya_ref, yb_ref, wga_ref, wgb_ref, wa_ref, wb_ref, o_ref):
    x = x_ref[...]
    ga = jax.nn.sigmoid(jnp.dot(x, wga_ref[...], preferred_element_type=F32))
    gb = jax.nn.sigmoid(jnp.dot(x, wgb_ref[...], preferred_element_type=F32))
    a = jnp.dot(ya_ref[...], wa_ref[...], preferred_element_type=F32)
    bb = jnp.dot(yb_ref[...], wb_ref[...], preferred_element_type=F32)
    o_ref[...] = (ga * a + gb * bb).astype(o_ref.dtype)


def _merge(x_bf, y_a, y_b, w_ga, w_gb, w_a, w_b, *, tm, tn):
    tokens, d = x_bf.shape
    ya_cols, yb_cols = y_a.shape[1], y_b.shape[1]
    return pl.pallas_call(
        _merge_kernel,
        grid=(tokens // tm, d // tn),
        in_specs=[
            pl.BlockSpec((tm, d), lambda i, j: (i, 0)),
            pl.BlockSpec((tm, ya_cols), lambda i, j: (i, 0)),
            pl.BlockSpec((tm, yb_cols), lambda i, j: (i, 0)),
            pl.BlockSpec((d, tn), lambda i, j: (0, j)),
            pl.BlockSpec((d, tn), lambda i, j: (0, j)),
            pl.BlockSpec((ya_cols, tn), lambda i, j: (0, j)),
            pl.BlockSpec((yb_cols, tn), lambda i, j: (0, j)),
        ],
        out_specs=pl.BlockSpec((tm, tn), lambda i, j: (i, j)),
        out_shape=jax.ShapeDtypeStruct((tokens, d), BF16),
        compiler_params=_params(2),
        name="branch_merge",
    )(x_bf, y_a, y_b, w_ga, w_gb, w_a, w_b)


def _mix_ln_kernel(m_ref, x_ref, w_ref, g_ref, b_ref, o_ref):
    rows = m_ref.shape[0] // ROW_SPLITS
    for r in range(ROW_SPLITS):
        sl = slice(r * rows, (r + 1) * rows)
        mix = jnp.dot(m_ref[sl, :], w_ref[...], preferred_element_type=F32)
        o_ref[sl, :] = _layer_norm(DN_ALPHA * x_ref[sl, :] + mix, g_ref, b_ref)


def _mix_ln(merged, x, w_mix, ln_g, ln_b, *, tm):
    tokens, d = x.shape
    return pl.pallas_call(
        _mix_ln_kernel,
        grid=(tokens // tm,),
        in_specs=[
            pl.BlockSpec((tm, d), lambda i: (i, 0)),
            pl.BlockSpec((tm, d), lambda i: (i, 0)),
            pl.BlockSpec((d, d), lambda i: (0, 0)),
            pl.BlockSpec((1, d), lambda i: (0, 0)),
            pl.BlockSpec((1, d), lambda i: (0, 0)),
        ],
        out_specs=pl.BlockSpec((tm, d), lambda i: (i, 0)),
        out_shape=jax.ShapeDtypeStruct((tokens, d), F32),
        compiler_params=_params(1),
        name="mix_out_ln1",
    )(merged, x, w_mix, ln_g, ln_b)


def _xa_kernel(h_ref, wq_ref, k_ref, v_ref, wo_ref, g_ref, b_ref, o_ref):
    h = h_ref[...]
    q = jnp.dot(h.astype(BF16), wq_ref[...], preferred_element_type=F32) * (XA_HEAD_DIM ** -0.5)
    q = q.astype(BF16)
    outs = []
    for hd in range(XA_HEADS):
        cols = slice(hd * XA_HEAD_DIM, (hd + 1) * XA_HEAD_DIM)
        s = lax.dot_general(q[:, cols], k_ref[:, cols], (((1,), (1,)), ((), ())), preferred_element_type=F32)
        p = jnp.exp(s - jnp.max(s, axis=-1, keepdims=True))
        l = jnp.sum(p, axis=-1, keepdims=True)
        o = jnp.dot(p.astype(BF16), v_ref[:, cols], preferred_element_type=F32)
        outs.append(o * (1.0 / l))
    o = jnp.concatenate(outs, axis=1).astype(BF16)
    xa = jnp.dot(o, wo_ref[...], preferred_element_type=F32)
    o_ref[...] = _layer_norm(DN_ALPHA * h + xa, g_ref, b_ref)


def _cross_attention_ln(h, kv_mem, w_q, w_o, ln_g, ln_b, batch, seq, *, tm):
    tokens, d = h.shape
    mem_len = kv_mem.shape[0] // batch
    xa_dim = XA_HEADS * XA_HEAD_DIM
    nt = seq // tm
    return pl.pallas_call(
        _xa_kernel,
        grid=(batch, nt),
        in_specs=[
            pl.BlockSpec((tm, d), lambda b, i: (b * nt + i, 0)),
            pl.BlockSpec((d, xa_dim), lambda b, i: (0, 0)),
            pl.BlockSpec((mem_len, xa_dim), lambda b, i: (b, 0)),
            pl.BlockSpec((mem_len, xa_dim), lambda b, i: (b, 1)),
            pl.BlockSpec((xa_dim, d), lambda b, i: (0, 0)),
            pl.BlockSpec((1, d), lambda b, i: (0, 0)),
            pl.BlockSpec((1, d), lambda b, i: (0, 0)),
        ],
        out_specs=pl.BlockSpec((tm, d), lambda b, i: (b * nt + i, 0)),
        out_shape=jax.ShapeDtypeStruct((tokens, d), F32),
        compiler_params=_params(2),
        name="cross_attention_ln2",
    )(h, w_q, kv_mem, kv_mem, w_o, ln_g, ln_b)


def _mlp_kernel(h_ref, w1_ref, w2_ref, g_ref, b_ref, o_ref, hb_ref):
    j = pl.program_id(1)

    @pl.when(j == 0)
    def _():
        hb_ref[...] = h_ref[...].astype(BF16)
        o_ref[...] = jnp.zeros_like(o_ref)

    rows = hb_ref.shape[0] // ROW_SPLITS
    subs = [slice(r * rows, (r + 1) * rows) for r in range(ROW_SPLITS)]
    ups = [jnp.dot(hb_ref[sl, :], w1_ref[...], preferred_element_type=F32) for sl in subs]
    for sl, u in zip(subs, ups):
        u = jnp.square(jnp.maximum(u, 0.0)).astype(BF16)
        o_ref[sl, :] += jnp.dot(u, w2_ref[...], preferred_element_type=F32)

    @pl.when(j == pl.num_programs(1) - 1)
    def _():
        o_ref[...] = _layer_norm(DN_ALPHA * h_ref[...] + o_ref[...], g_ref, b_ref)


def _mlp_ln(h, w1, w2, ln_g, ln_b, *, tm, tf):
    tokens, d = h.shape
    d_ff = w1.shape[1]
    return pl.pallas_call(
        _mlp_kernel,
        grid=(tokens // tm, d_ff // tf),
        in_specs=[
            pl.BlockSpec((tm, d), lambda i, j: (i, 0)),
            pl.BlockSpec((d, tf), lambda i, j: (0, j)),
            pl.BlockSpec((tf, d), lambda i, j: (j, 0)),
            pl.BlockSpec((1, d), lambda i, j: (0, 0)),
            pl.BlockSpec((1, d), lambda i, j: (0, 0)),
        ],
        out_specs=pl.BlockSpec((tm, d), lambda i, j: (i, 0)),
        out_shape=jax.ShapeDtypeStruct((tokens, d), F32),
        scratch_shapes=[pltpu.VMEM((tm, d), BF16)],
        compiler_params=_params(2),
        name="mlp_ln3",
    )(h, w1, w2, ln_g, ln_b)


def _overlap_t(seq):
    ncp, nsel = seq // CMP_STRIDE, seq // SEL_BLOCK
    c_start = np.arange(ncp)[None, :] * CMP_STRIDE
    s_start = np.arange(nsel)[:, None] * SEL_BLOCK
    ov = (c_start < s_start + SEL_BLOCK) & (c_start + CMP_BLOCK > s_start) & (np.arange(ncp)[None, :] < ncp - 1)
    return jnp.asarray(ov, BF16)


def kernel(x, mem, w_in, attn_sinks, rel_bias_table, cmp_pos_k, cmp_w1_k, cmp_w2_k, cmp_pos_v, cmp_w1_v, cmp_w2_v,
           w_branch_swa, w_branch_nsa, w_mix_out, ln1_g, ln1_b, xa_w_q, xa_w_kv, xa_w_o, ln2_g, ln2_b,
           mlp_w1, mlp_w2, ln3_g, ln3_b):
    batch, seq, d = x.shape
    tokens = batch * seq
    nq = seq // BLOCK_Q
    assert w_in.shape[0] == 1, "one layer"
    assert seq % (8 * BLOCK_Q) == 0 and d % 512 == 0

    w = w_in[0]
    sizes = (SWA_HEADS * HEAD_DIM, 128, 128, NSA_HEADS * HEAD_DIM, 128, 128, 128, 128, 128, 128, 3 * NSA_HEADS, d, d)
    offs = np.concatenate([[0], np.cumsum(sizes)])
    (w_qa, w_ka, w_va, w_qb, w_kc, w_vc, w_ks, w_vs, w_kw, w_vw, w_gn, w_ga, w_gb) = [
        w[:, offs[n]:offs[n + 1]] for n in range(len(sizes))]
    w_rows = jnp.concatenate([w_ka, w_kc, w_ks, w_kw, w_vc], axis=1).astype(BF16)
    w_gn_pad = jnp.pad(w_gn, ((0, 0), (0, 128 - w_gn.shape[1])))
    w_cols_t = jnp.concatenate([w_qa, w_qb, w_va, w_vs, w_vw, w_gn_pad], axis=1).T.astype(BF16)

    x2 = x.reshape(tokens, d)
    x_bf = x2.astype(BF16)

    h_rows = _matmul(x_bf, w_rows, nt=False, tm=min(2048, tokens), tn=w_rows.shape[1])
    h_t = _matmul(w_cols_t, x_bf, nt=True, tm=1280, tn=min(1024, tokens))

    kk = h_rows[:, :512].reshape(tokens, 4, GROUPS, HEAD_DIM).transpose(1, 2, 0, 3)
    v_cmp = h_rows[:, 512:].reshape(tokens, GROUPS, HEAD_DIM).transpose(1, 0, 2)

    def value_tiles(row0, width):
        v = h_t[row0:row0 + 128].reshape(GROUPS, HEAD_DIM, batch, seq // width, width).transpose(0, 2, 3, 1, 4)
        ones = jnp.ones(v.shape[:3] + (V_AUG_ROWS - HEAD_DIM, width), BF16)
        return jnp.concatenate([v, ones], axis=3)

    vswa_aug = value_tiles(2048, BLOCK_Q)
    vsel_aug, vwin_aug = value_tiles(2176, 2 * BLOCK_Q), value_tiles(2304, 2 * BLOCK_Q)
    gates_t = h_t[2432:2432 + 3 * NSA_HEADS].reshape(GROUPS, HEADS_PER_GROUP, 3, tokens).transpose(0, 2, 1, 3)
    block_of_key = (jnp.arange(tokens) % BLOCK_Q) // SEL_BLOCK
    onehot = (block_of_key[:, None] == jnp.arange(SEL_ROWS)[None, :]).astype(BF16)
    ksel_aug = jnp.concatenate([kk[2], jnp.broadcast_to(onehot, (GROUPS, tokens, SEL_ROWS))], axis=2)

    by_dist = rel_bias_table[_rel_bucket(jnp.arange(seq))].astype(F32).T
    vb_swa = _bias_by_distance(by_dist[:SWA_HEADS], 4, window=SWA_WINDOW)
    vb_sel = _bias_by_distance(by_dist[SWA_HEADS:], nq + 2)
    vb_win = _bias_by_distance(by_dist[SWA_HEADS:], N_WIN_SLOTS + 1, window=NSA_WINDOW)
    vb_cmp = vb_sel
    sink_rows = jnp.repeat(attn_sinks[0].astype(F32), BLOCK_Q).reshape(GROUPS, 1, GROUP_LANES)

    y_a = _swa_attention(h_t, kk, vswa_aug, sink_rows, vb_swa, batch, seq)

    ncp = seq // CMP_STRIDE
    chunk_w = CMP_STRIDE * HEAD_DIM
    chunks = jnp.stack([kk[1], v_cmp]).reshape(2, GROUPS, batch, ncp, chunk_w)
    pos = jnp.stack([cmp_pos_k[0], cmp_pos_v[0]]).astype(F32).reshape(2, 2, 1, chunk_w)
    w1 = jnp.stack([cmp_w1_k[0], cmp_w1_v[0]]).astype(BF16).reshape(2, 2, chunk_w, -1)
    w2 = jnp.stack([cmp_w2_k[0], cmp_w2_v[0]]).astype(BF16)
    cn, ct = _compress(chunks, pos, w1, w2, w2.transpose(0, 2, 1))
    oc_t, sel = _cmp_attention(h_t, cn, ct, vb_cmp, _overlap_t(seq), batch, seq)
    sel = sel.reshape(batch, GROUPS, nq // 2, 4, seq)
    y_b = _selwin_attention(h_t, ksel_aug, kk, vsel_aug, vwin_aug, sel, oc_t, gates_t, vb_sel, vb_win, batch, seq)

    merged = _merge(x_bf, y_a, y_b, w_ga.astype(BF16), w_gb.astype(BF16),
                    w_branch_swa[0].astype(BF16), w_branch_nsa[0].astype(BF16), tm=min(1024, tokens), tn=512)
    h1 = _mix_ln(merged, x2, w_mix_out[0].astype(BF16), ln1_g, ln1_b, tm=512)

    mem_bf = mem.reshape(-1, d).astype(BF16)
    kv_mem = _matmul(mem_bf, xa_w_kv[0].astype(BF16), nt=False, tm=mem_bf.shape[0], tn=512)
    h2 = _cross_attention_ln(h1, kv_mem, xa_w_q[0].astype(BF16), xa_w_o[0].astype(BF16), ln2_g, ln2_b,
                             batch, seq, tm=512)

    h3 = _mlp_ln(h2, mlp_w1[0].astype(BF16), mlp_w2[0].astype(BF16), ln3_g, ln3_b, tm=1024, tf=512)
    return h3.reshape(batch, seq, d)
```

```python
import functools
import math

import numpy as np
import jax
import jax.numpy as jnp
from jax import lax
from jax.experimental import pallas as pl
from jax.experimental.pallas import tpu as pltpu

F32 = jnp.float32
BF16 = jnp.bfloat16

HEAD_DIM = 64
BLOCK_Q = 128
SWA_HEADS = 16
SWA_KV_HEADS = 2
SWA_WINDOW = 128
NSA_HEADS = 16
NSA_KV_HEADS = 2
CMP_BLOCK = 32
CMP_STRIDE = 16
SEL_BLOCK = 64
SEL_TOPK = 16
SEL_LOCAL = 2
NSA_WINDOW = 512
REL_BUCKETS = 32
REL_MAX_DIST = 4096
XA_HEADS = 4
XA_HEAD_DIM = 128
DEPTH = 1
DN_ALPHA = (2.0 * DEPTH) ** 0.25
LN_EPS = 1e-5
NEG_INF = -1e30
FORCE_SCORE = 1e4

GROUPS = 2
HEADS_PER_GROUP = 8
GROUP_LANES = HEADS_PER_GROUP * BLOCK_Q
GROUP_COLS = HEADS_PER_GROUP * HEAD_DIM
SUBTILES = 2

V7X_VMEM_LIMIT_BYTES = 56 * 1024 * 1024


def _params(n_axes):
    return pltpu.CompilerParams(dimension_semantics=("arbitrary",) * n_axes,
                                vmem_limit_bytes=V7X_VMEM_LIMIT_BYTES)


def _mm_kernel(a_ref, b_ref, o_ref, *, nt):
    if nt:
        out = lax.dot_general(a_ref[...], b_ref[...], (((1,), (1,)), ((), ())), preferred_element_type=F32)
    else:
        out = jnp.dot(a_ref[...], b_ref[...], preferred_element_type=F32)
    o_ref[...] = out.astype(o_ref.dtype)


def _matmul(a, b, *, nt, tm, tn, out_dtype=BF16):
    m, k = a.shape
    n = b.shape[0] if nt else b.shape[1]
    b_spec = pl.BlockSpec((tn, k), lambda i, j: (j, 0)) if nt else pl.BlockSpec((k, tn), lambda i, j: (0, j))
    return pl.pallas_call(
        functools.partial(_mm_kernel, nt=nt),
        grid=(m // tm, n // tn),
        in_specs=[pl.BlockSpec((tm, k), lambda i, j: (i, 0)), b_spec],
        out_specs=pl.BlockSpec((tm, tn), lambda i, j: (i, j)),
        out_shape=jax.ShapeDtypeStruct((m, n), out_dtype),
        compiler_params=_params(2),
        name="proj_nt" if nt else "proj_nn",
    )(a, b)


def _rel_bucket(dist):
    exact = REL_BUCKETS // 2
    d = jnp.maximum(dist, 0)
    log_ratio = jnp.log(jnp.maximum(d, 1).astype(F32) / exact) / math.log(REL_MAX_DIST / exact)
    large = jnp.minimum(exact + (log_ratio * (REL_BUCKETS - exact)).astype(jnp.int32), REL_BUCKETS - 1)
    return jnp.where(d < exact, d, large)


def _bias_by_distance(by_dist, rows, *, window=None):
    heads, seq = by_dist.shape
    if window is not None:
        by_dist = jnp.where(jnp.arange(seq) < window, by_dist, NEG_INF)
    padded = jnp.pad(by_dist, ((0, 0), (2 * BLOCK_Q, 0)), constant_values=NEG_INF)[:, :rows * BLOCK_Q]
    return padded.reshape(heads // HEADS_PER_GROUP, HEADS_PER_GROUP, rows, BLOCK_Q)


def _build_skew_table(vb_ref, tbl_ref, n_tiles):
    def body(d, carry):
        for h in range(HEADS_PER_GROUP):
            lo = jnp.broadcast_to(vb_ref[h, pl.ds(d, 1), :], (BLOCK_Q, BLOCK_Q))
            hi = jnp.broadcast_to(vb_ref[h, pl.ds(d + 1, 1), :], (BLOCK_Q, BLOCK_Q))
            y = pltpu.roll(jnp.concatenate([lo, hi], axis=1), 0, 1, stride=1, stride_axis=0)
            tbl_ref[d, :, h * BLOCK_Q:(h + 1) * BLOCK_Q] = y[:, BLOCK_Q:].astype(tbl_ref.dtype)
        return carry

    lax.fori_loop(0, n_tiles, body, 0)


def _build_cmp_bias_table(vb_ref, tbl_ref, skew_ref, nq):
    per_tile = BLOCK_Q // CMP_STRIDE
    assert per_tile == 8 and CMP_BLOCK - 1 + CMP_STRIDE * (per_tile - 1) - BLOCK_Q == CMP_STRIDE - 1
    tbl_ref[pl.ds(nq * 8, (nq - 1) * 8), :] = jnp.full(((nq - 1) * 8, GROUP_LANES), NEG_INF, F32)

    def body(d, last_row):
        for h in range(HEADS_PER_GROUP):
            lo = jnp.broadcast_to(vb_ref[h, pl.ds(d + 1, 1), :], (BLOCK_Q, BLOCK_Q))
            hi = jnp.broadcast_to(vb_ref[h, pl.ds(d + 2, 1), :], (BLOCK_Q, BLOCK_Q))
            y = pltpu.roll(jnp.concatenate([lo, hi], axis=1), 0, 1, stride=1, stride_axis=0)
            skew_ref[:, h * BLOCK_Q:(h + 1) * BLOCK_Q] = y[:, BLOCK_Q:]
        offsets = [CMP_BLOCK - 1 + CMP_STRIDE * c for c in range(per_tile - 1)]
        rows = [skew_ref[k:k + 1, :] for k in offsets] + [last_row]
        tbl_ref[pl.ds(pl.multiple_of((nq - 1 - d) * 8, 8), 8), :] = jnp.concatenate(rows, axis=0)
        return skew_ref[CMP_STRIDE - 1:CMP_STRIDE, :]

    lax.fori_loop(0, nq, body, jnp.full((1, GROUP_LANES), NEG_INF, F32))


def _gather_heads_to_lanes(q_ref):
    q = jnp.concatenate([q_ref[h * HEAD_DIM:(h + 1) * HEAD_DIM, u * BLOCK_Q:(u + 1) * BLOCK_Q]
                         for u in range(q_ref.shape[1] // BLOCK_Q) for h in range(HEADS_PER_GROUP)], axis=1)
    return q * jnp.asarray(HEAD_DIM ** -0.5, q.dtype)


def _heads_to_columns(o_t):
    stacked = jnp.concatenate([o_t[:, h * BLOCK_Q:(h + 1) * BLOCK_Q] for h in range(HEADS_PER_GROUP)], axis=0)
    return stacked.T


def _identity_tile():
    r = lax.broadcasted_iota(jnp.int32, (BLOCK_Q, BLOCK_Q), 0)
    c = lax.broadcasted_iota(jnp.int32, (BLOCK_Q, BLOCK_Q), 1)
    return jnp.where(r == c, 1.0, 0.0).astype(BF16)


def _biased_scores(eye, k_t, bias_tile, q_t):
    lhs = jnp.concatenate([eye, k_t], axis=1)
    rhs = jnp.concatenate([bias_tile, q_t], axis=0)
    return jnp.dot(lhs, rhs, preferred_element_type=F32)


def _softmax_probs(m, s_tiles):
    ms, alphas, ps = [], [], [[] for _ in s_tiles]
    for h in range(m.shape[1] // BLOCK_Q):
        lanes = slice(h * BLOCK_Q, (h + 1) * BLOCK_Q)
        cols = [s[:, lanes] for s in s_tiles]
        m_new = m[:, lanes]
        for c in cols:
            m_new = jnp.maximum(m_new, jnp.max(c, axis=0, keepdims=True))
        ms.append(m_new)
        alphas.append(jnp.exp(m[:, lanes] - m_new))
        for j, c in enumerate(cols):
            ps[j].append(jnp.exp((c - m_new).astype(BF16)))
    p = jnp.concatenate([jnp.concatenate(pj, axis=1) for pj in ps], axis=0)
    return jnp.concatenate(ms, axis=1), jnp.concatenate(alphas, axis=1), p


def _softmax_update(carry, s_tiles, v_aug):
    m, acc = carry
    m, alpha, p = _softmax_probs(m, s_tiles)
    return m, alpha * acc + jnp.dot(v_aug, p, preferred_element_type=F32)


def _pipelined_attention(n_steps, scores_fn, values_fn, init, s_scr, p_scr, acc_scr, *, fill):
    m0, acc0 = init

    def put_scores(n):
        for j, s in enumerate(scores_fn(n)):
            s_scr[j] = s

    def fold(alpha, n, p):
        acc_scr[...] = alpha * acc_scr[...] + jnp.dot(values_fn(n), p, preferred_element_type=F32)

    if fill:
        put_scores(0)
        p_scr[...] = jnp.zeros_like(p_scr)
        acc_scr[...] = acc0
        return None

    def body(k, carry):
        m, alpha_a, alpha_b = carry
        a = 2 * k
        s_b = scores_fn(a + 1)
        fold(alpha_a, a - 2, p_scr[0])
        fold(alpha_b, a - 1, p_scr[1])
        m, alpha_a, p_a = _softmax_probs(m, [s_scr.at[j] for j in range(s_scr.shape[0])])
        put_scores(a + 2)
        m, alpha_b, p_b = _softmax_probs(m, s_b)
        p_scr[0] = p_a
        p_scr[1] = p_b
        return m, alpha_a, alpha_b

    trips = (n_steps + 1) // 2
    ones = jnp.ones_like(m0)
    m, alpha_a, alpha_b = lax.fori_loop(0, trips, body, (m0, ones, ones))
    fold(alpha_a, 2 * trips - 2, p_scr[0])
    fold(alpha_b, 2 * trips - 1, p_scr[1])
    return m, acc_scr[...]


SLAB = 2 * BLOCK_Q
LOOKAHEAD = 12


def _slabwise_softmax_step(m, acc_ref, lhs_tiles, rhs_fn, v_fn, masked_fn=None):
    per_slab = [[j for j in range(len(lhs_tiles)) if masked_fn is None or not masked_fn(j, sl)]
                for sl in range(m.shape[1] // SLAB)]
    items = [(sl, j) for sl, tiles in enumerate(per_slab) for j in tiles]

    def scores(k):
        sl, j = items[k]
        lhs = lhs_tiles[j](sl) if callable(lhs_tiles[j]) else lhs_tiles[j]
        return jnp.dot(lhs, rhs_fn(j, sl), preferred_element_type=F32)

    pending = [scores(k) for k in range(min(LOOKAHEAD, len(items)))]
    new_m = []
    for k, (sl, j) in enumerate(items):
        lanes = slice(sl * SLAB, (sl + 1) * SLAB)
        if j == per_slab[sl][0]:
            m_s, acc_s = m[:, lanes], acc_ref[:, lanes]
        s = pending.pop(0)
        if k + LOOKAHEAD < len(items):
            pending.append(scores(k + LOOKAHEAD))
        m_next = jnp.maximum(m_s, jnp.max(s, axis=0, keepdims=True))
        p = jnp.exp((s - m_next).astype(BF16))
        acc_s = jnp.exp(m_s - m_next) * acc_s + jnp.dot(v_fn(j, sl), p, preferred_element_type=F32)
        m_s = m_next
        if j == per_slab[sl][-1]:
            acc_ref[:, lanes] = acc_s
            new_m.append(m_s)
    return jnp.concatenate(new_m, axis=1)


V_AUG_ROWS = HEAD_DIM + 16


def _normalized(acc):
    return acc[:HEAD_DIM] * (1.0 / acc[HEAD_DIM:HEAD_DIM + 1])


def _swa_kernel(q_ref, kprev_ref, kcur_ref, vprev_ref, vcur_ref, sink_ref, vb_ref, o_ref, tbl_ref, acc_ref):
    b, i = pl.program_id(1), pl.program_id(2)

    @pl.when((b == 0) & (i == 0))
    def _():
        _build_skew_table(vb_ref, tbl_ref, 3)

    eye = _identity_tile()
    q_t = _gather_heads_to_lanes(q_ref)
    lanes = q_t.shape[1]
    slabs_per_tile = GROUP_LANES // SLAB
    m0 = jnp.concatenate([sink_ref[...]] * SWA_TILES, axis=1)
    acc_ref[...] = jnp.where(lax.broadcasted_iota(jnp.int32, (V_AUG_ROWS, lanes), 0) < HEAD_DIM, 0.0, 1.0)

    def key_tile(j):
        def at(slab):
            u = slab // slabs_per_tile - j
            k_t = kprev_ref[...] if u < 0 else kcur_ref[u * BLOCK_Q:(u + 1) * BLOCK_Q, :]
            return jnp.concatenate([eye, k_t], axis=1)
        return at

    def rhs(j, slab):
        u, part = slab // slabs_per_tile, slab % slabs_per_tile
        slot = 1 + j if (u > 0 or j == 0) else jnp.where(i > 0, 2, 0)
        return jnp.concatenate([tbl_ref[slot, :, part * SLAB:(part + 1) * SLAB],
                                q_t[:, slab * SLAB:(slab + 1) * SLAB]], axis=0)

    def values(j, slab):
        u = slab // slabs_per_tile - j
        return vprev_ref[...] if u < 0 else vcur_ref[u]

    _slabwise_softmax_step(m0, acc_ref, [key_tile(0), key_tile(1)], rhs, values)
    out = _normalized(acc_ref[...])
    for u in range(SWA_TILES):
        o_ref[u * BLOCK_Q:(u + 1) * BLOCK_Q, :] = _heads_to_columns(
            out[:, u * GROUP_LANES:(u + 1) * GROUP_LANES]).astype(o_ref.dtype)


SWA_TILES = 4


def _swa_attention(h_t, kk, v_aug, sink_rows, vb, batch, seq):
    nq = seq // BLOCK_Q
    nblk = nq // SWA_TILES
    width = SWA_TILES * BLOCK_Q
    tokens = batch * seq

    def prev_tile(i):
        return jnp.maximum(i * SWA_TILES - 1, 0)

    return pl.pallas_call(
        _swa_kernel,
        grid=(GROUPS, batch, nblk),
        in_specs=[
            pl.BlockSpec((GROUP_COLS, width), lambda g, b, i: (g, b * nblk + i)),
            pl.BlockSpec((None, None, BLOCK_Q, HEAD_DIM), lambda g, b, i: (0, g, b * nq + prev_tile(i), 0)),
            pl.BlockSpec((None, None, width, HEAD_DIM), lambda g, b, i: (0, g, b * nblk + i, 0)),
            pl.BlockSpec((None, None, None, V_AUG_ROWS, BLOCK_Q), lambda g, b, i: (g, b, prev_tile(i), 0, 0)),
            pl.BlockSpec((None, None, SWA_TILES, V_AUG_ROWS, BLOCK_Q), lambda g, b, i: (g, b, i, 0, 0)),
            pl.BlockSpec((None, 1, GROUP_LANES), lambda g, b, i: (g, 0, 0)),
            pl.BlockSpec((None, HEADS_PER_GROUP, 4, BLOCK_Q), lambda g, b, i: (g, 0, 0, 0)),
        ],
        out_specs=pl.BlockSpec((width, GROUP_COLS), lambda g, b, i: (b * nblk + i, g)),
        out_shape=jax.ShapeDtypeStruct((tokens, SWA_HEADS * HEAD_DIM), BF16),
        scratch_shapes=[pltpu.VMEM((3, BLOCK_Q, GROUP_LANES), BF16),
                        pltpu.VMEM((V_AUG_ROWS, SWA_TILES * GROUP_LANES), F32)],
        compiler_params=_params(3),
        name="swa_attention",
    )(h_t, kk, kk, v_aug, v_aug, sink_rows, vb)


def _compress_kernel(c_ref, pos_ref, w1_ref, w2_ref, w2t_ref, cn_ref, ct_ref):
    c = c_ref[...].astype(F32)
    top = (c + pos_ref[0]).astype(BF16)
    bot = (c + pos_ref[1]).astype(BF16)
    a = jnp.dot(top, w1_ref[0], preferred_element_type=F32)
    bm = jnp.dot(bot, w1_ref[1], preferred_element_type=F32)
    n = a.shape[0]
    pre = a + pltpu.roll(bm, n - 1, 0)
    hid = jax.nn.gelu(pre).astype(BF16)
    cn_ref[...] = jnp.dot(hid, w2_ref[...], preferred_element_type=F32).astype(cn_ref.dtype)
    ct_ref[...] = lax.dot_general(w2t_ref[...], hid, (((1,), (1,)), ((), ())),
                                  preferred_element_type=F32).astype(ct_ref.dtype)


def _compress(chunks, pos, w1, w2, w2t):
    _, g, b, ncp, width = chunks.shape
    hidden = w1.shape[-1]
    return pl.pallas_call(
        _compress_kernel,
        grid=(2, g, b),
        in_specs=[
            pl.BlockSpec((None, None, None, ncp, width), lambda t, g, b: (t, g, b, 0, 0)),
            pl.BlockSpec((None, 2, 1, width), lambda t, g, b: (t, 0, 0, 0)),
            pl.BlockSpec((None, 2, width, hidden), lambda t, g, b: (t, 0, 0, 0)),
            pl.BlockSpec((None, hidden, HEAD_DIM), lambda t, g, b: (t, 0, 0)),
            pl.BlockSpec((None, HEAD_DIM, hidden), lambda t, g, b: (t, 0, 0)),
        ],
        out_specs=[
            pl.BlockSpec((None, None, None, ncp, HEAD_DIM), lambda t, g, b: (t, g, b, 0, 0)),
            pl.BlockSpec((None, None, None, HEAD_DIM, ncp), lambda t, g, b: (t, g, b, 0, 0)),
        ],
        out_shape=[jax.ShapeDtypeStruct((2, g, b, ncp, HEAD_DIM), BF16),
                   jax.ShapeDtypeStruct((2, g, b, HEAD_DIM, ncp), BF16)],
        compiler_params=_params(3),
        name="nsa_compress",
    )(chunks, pos, w1, w2, w2t)


def _cmp_kernel(q_ref, kc_ref, vct_ref, vb_ref, ov_ref, oc_ref, sel_ref, bias_ref, skew_ref, *, nq):
    b, i = pl.program_id(1), pl.program_id(2)

    @pl.when((b == 0) & (i == 0))
    def _():
        _build_cmp_bias_table(vb_ref, bias_ref, skew_ref, nq)

    ncp = kc_ref.shape[0]
    nsel = sel_ref.shape[0]
    n_tiles = ncp // BLOCK_Q
    q_t = _gather_heads_to_lanes(q_ref)
    width = SUBTILES * BLOCK_Q
    eye = _identity_tile()

    slabs_per_tile = GROUP_LANES // SLAB
    lhs = [jnp.concatenate([eye, kc_ref[t * BLOCK_Q:(t + 1) * BLOCK_Q, :]], axis=1) for t in range(n_tiles)]
    items = [(sl, t) for sl in range(SUBTILES * slabs_per_tile) for t in range(n_tiles)]

    def scores(k):
        sl, t = items[k]
        u, part = sl // slabs_per_tile, sl % slabs_per_tile
        row0 = pl.multiple_of((nq - 1 - (SUBTILES * i + u)) * 8 + t * BLOCK_Q, 8)
        bias = bias_ref[pl.ds(row0, BLOCK_Q), part * SLAB:(part + 1) * SLAB].astype(BF16)
        return jnp.dot(lhs[t], jnp.concatenate([bias, q_t[:, sl * SLAB:(sl + 1) * SLAB]], axis=0),
                       preferred_element_type=F32)

    q_pos = i * width + lax.broadcasted_iota(jnp.int32, (1, width), 1)
    sees_any = q_pos >= CMP_BLOCK - 1
    psum = [[jnp.zeros((BLOCK_Q, BLOCK_Q), F32) for _ in range(n_tiles)] for _ in range(SUBTILES)]
    pending = [scores(k) for k in range(min(LOOKAHEAD, len(items)))]
    for sl in range(SUBTILES * slabs_per_tile):
        u, part = sl // slabs_per_tile, sl % slabs_per_tile
        s = [pending.pop(0) for _ in range(n_tiles)]
        for k in range(sl * n_tiles + LOOKAHEAD, min((sl + 1) * n_tiles + LOOKAHEAD, len(items))):
            pending.append(scores(k))
        m = functools.reduce(jnp.maximum, [jnp.max(c, axis=0, keepdims=True) for c in s])
        e = [jnp.exp(c - m) for c in s]
        l = functools.reduce(jnp.add, [jnp.sum(c, axis=0, keepdims=True) for c in e])
        seen = sees_any[:, u * BLOCK_Q:(u + 1) * BLOCK_Q]
        inv = jnp.where(jnp.concatenate([seen] * (SLAB // BLOCK_Q), axis=1), 1.0 / l, 0.0)
        o_slab = jnp.zeros((HEAD_DIM, SLAB), F32)
        for t in range(n_tiles):
            p = e[t] * inv
            psum[u][t] = psum[u][t] + functools.reduce(
                jnp.add, [p[:, r * BLOCK_Q:(r + 1) * BLOCK_Q] for r in range(SLAB // BLOCK_Q)])
            o_slab = o_slab + jnp.dot(vct_ref[:, t * BLOCK_Q:(t + 1) * BLOCK_Q], p.astype(BF16),
                                      preferred_element_type=F32)
        oc_ref[u, :, part * SLAB:(part + 1) * SLAB] = o_slab.astype(oc_ref.dtype)

    psum = jnp.concatenate([jnp.concatenate(pu, axis=0) for pu in psum], axis=1)
    hi = psum.astype(BF16)
    lo = (psum - hi.astype(F32)).astype(BF16)
    ov = ov_ref[...]
    score = jnp.dot(ov, hi, preferred_element_type=F32) + jnp.dot(ov, lo, preferred_element_type=F32)

    j_io = lax.broadcasted_iota(jnp.int32, (nsel, width), 0)
    qpos = i * width + lax.broadcasted_iota(jnp.int32, (nsel, width), 1)
    causal = j_io * SEL_BLOCK <= qpos
    back = qpos // SEL_BLOCK - j_io
    forced = (j_io == 0) | ((back >= 0) & (back < SEL_LOCAL))
    score = jnp.where(causal, jnp.where(forced, FORCE_SCORE, score), -1.0)
    slab_rows = lax.broadcasted_iota(jnp.int32, (8, width), 0)
    slabs = [score[8 * g:8 * (g + 1), :] for g in range(nsel // 8)]
    ranks = [jnp.zeros((8, width), F32) for _ in slabs]
    for r in range(nsel):
        row = jnp.broadcast_to(score[r:r + 1, :], (8, width))
        for g, slab in enumerate(slabs):
            if g > r // 8:
                ahead = row >= slab
            elif g < r // 8:
                ahead = row > slab
            else:
                ranks[g] = ranks[g] + jnp.where(slab_rows > r % 8, jnp.where(row >= slab, 1.0, 0.0),
                                                jnp.where(row > slab, 1.0, 0.0))
                continue
            ranks[g] = ranks[g] + jnp.where(ahead, 1.0, 0.0)
    rank = jnp.concatenate(ranks, axis=0)
    sel_ref[...] = jnp.where((rank < min(SEL_TOPK, nsel)) & causal, 1.0, 0.0).astype(sel_ref.dtype)


def _cmp_attention(h_t, cn, ct, vb, overlap_t, batch, seq):
    nq = seq // BLOCK_Q
    ncp = seq // CMP_STRIDE
    nsel = seq // SEL_BLOCK
    q_blk0 = SWA_HEADS * HEAD_DIM // GROUP_COLS
    nblk = nq // SUBTILES
    width = SUBTILES * BLOCK_Q
    return pl.pallas_call(
        functools.partial(_cmp_kernel, nq=nq),
        grid=(GROUPS, batch, nblk),
        in_specs=[
            pl.BlockSpec((GROUP_COLS, width), lambda g, b, i: (q_blk0 + g, b * nblk + i)),
            pl.BlockSpec((None, None, None, ncp, HEAD_DIM), lambda g, b, i: (0, g, b, 0, 0)),
            pl.BlockSpec((None, None, None, HEAD_DIM, ncp), lambda g, b, i: (1, g, b, 0, 0)),
            pl.BlockSpec((None, HEADS_PER_GROUP, nq + 2, BLOCK_Q), lambda g, b, i: (g, 0, 0, 0)),
            pl.BlockSpec((nsel, ncp), lambda g, b, i: (0, 0)),
        ],
        out_specs=[
            pl.BlockSpec((None, None, SUBTILES, HEAD_DIM, GROUP_LANES), lambda g, b, i: (b, g, i, 0, 0)),
            pl.BlockSpec((None, None, nsel, width), lambda g, b, i: (b, g, 0, i)),
        ],
        out_shape=[jax.ShapeDtypeStruct((batch, GROUPS, nq, HEAD_DIM, GROUP_LANES), BF16),
                   jax.ShapeDtypeStruct((batch, GROUPS, nsel, seq), F32)],
        scratch_shapes=[pltpu.VMEM(((2 * nq - 1) * 8, GROUP_LANES), F32), pltpu.VMEM((BLOCK_Q, GROUP_LANES), F32)],
        compiler_params=_params(3),
        name="nsa_cmp_select",
    )(h_t, cn, ct, vb, overlap_t)


N_WIN_PAIRS = -(-(NSA_WINDOW - 1) // (SUBTILES * BLOCK_Q)) + 1
N_WIN_SLOTS = SUBTILES * N_WIN_PAIRS + 1
SEL_ROWS = 16
SEL_STEPS = 2


def _selwin_kernel(q_ref, ksel_ref, kwin_ref, vsel_ref, vwin_ref, sel_ref, oc_ref, gate_ref, vbs_ref, vbw_ref,
                   o_ref, tsel_ref, twin_ref, accs_ref, accw_ref, *, nq):
    b, blk = pl.program_id(1), pl.program_id(2)

    @pl.when((b == 0) & (blk == 0))
    def _():
        _build_skew_table(vbs_ref, tsel_ref, nq + 1)
        _build_skew_table(vbw_ref, twin_ref, N_WIN_SLOTS)

    q_t = _gather_heads_to_lanes(q_ref)
    lanes = q_t.shape[1]
    eye = _identity_tile()
    pair = SUBTILES * BLOCK_Q
    m0 = jnp.full((1, lanes), NEG_INF, F32)
    accs_ref[...] = jnp.zeros_like(accs_ref)
    accw_ref[...] = jnp.zeros_like(accw_ref)
    blocks_per_tile = BLOCK_Q // SEL_BLOCK
    slabs_per_tile = GROUP_LANES // SLAB

    def attend(m, steps, k_ref, tbl_ref, v_ref, acc_ref, q_ext_fn, reach=None):
        tiles = [(n, jnp.clip(blk - n, 0, blk), j) for n in steps for j in range(SUBTILES)]
        lhs = [jnp.concatenate([eye, k_ref[pl.ds(pl.multiple_of(p * pair + j * BLOCK_Q, BLOCK_Q), BLOCK_Q), :]],
                               axis=1) for _, p, j in tiles]
        q_exts = [q_ext_fn(p, j) for _, p, j in tiles]

        def rhs(t, slab):
            n, _, j = tiles[t]
            u, part = slab // slabs_per_tile, slab % slabs_per_tile
            slot = jnp.where(n <= blk, 2 * n + u - j + 1, 0)
            bias = tbl_ref[slot, :, part * SLAB:(part + 1) * SLAB]
            return jnp.concatenate([bias, q_exts[t][:, slab * SLAB:(slab + 1) * SLAB]], axis=0)

        def values(t, slab):
            _, p, j = tiles[t]
            return v_ref[p, :, j * BLOCK_Q:(j + 1) * BLOCK_Q]

        def masked(t, slab):
            n, _, j = tiles[t]
            distance = 2 * n + slab // slabs_per_tile - j
            return distance < 0 or distance >= reach

        return _slabwise_softmax_step(m, acc_ref, lhs, rhs, values, masked if reach is not None else None)

    attend(m0, list(range(N_WIN_PAIRS)), kwin_ref, twin_ref, vwin_ref, accw_ref, lambda p, j: q_t,
           reach=-(-(NSA_WINDOW + BLOCK_Q - 1) // BLOCK_Q))

    def sel_q_ext(p, j):
        neg = jnp.where(sel_ref[p, j * blocks_per_tile:(j + 1) * blocks_per_tile, :] > 0.5, 0.0, NEG_INF)
        rows = jnp.concatenate([neg[:, u * BLOCK_Q:(u + 1) * BLOCK_Q]
                                for u in range(SUBTILES) for _ in range(HEADS_PER_GROUP)], axis=1)
        rows = jnp.concatenate([rows, jnp.zeros((SEL_ROWS - blocks_per_tile, lanes), F32)], axis=0)
        return jnp.concatenate([q_t, rows.astype(BF16)], axis=0)

    def sel_body(k, m):
        return attend(m, [SEL_STEPS * k + r for r in range(SEL_STEPS)], ksel_ref, tsel_ref, vsel_ref, accs_ref,
                      sel_q_ext)

    lax.fori_loop(0, (blk + SEL_STEPS) // SEL_STEPS, sel_body, m0)

    def gate_row(branch):
        g = jnp.concatenate([gate_ref[branch, h:h + 1, u * BLOCK_Q:(u + 1) * BLOCK_Q]
                             for u in range(SUBTILES) for h in range(HEADS_PER_GROUP)], axis=1)
        return jax.nn.sigmoid(g.astype(F32))

    o_c = jnp.concatenate([oc_ref[u] for u in range(SUBTILES)], axis=1).astype(F32)
    out = (gate_row(0) * o_c + gate_row(1) * _normalized(accs_ref[...])
           + gate_row(2) * _normalized(accw_ref[...]))
    for u in range(SUBTILES):
        o_ref[u * BLOCK_Q:(u + 1) * BLOCK_Q, :] = _heads_to_columns(
            out[:, u * GROUP_LANES:(u + 1) * GROUP_LANES]).astype(o_ref.dtype)


def _selwin_attention(h_t, ksel_aug, kk, vsel_aug, vwin_aug, sel, oc_t, gates_t, vb_sel, vb_win, batch, seq):
    nq = seq // BLOCK_Q
    nblk = nq // SUBTILES
    tokens = batch * seq
    q_blk0 = SWA_HEADS * HEAD_DIM // GROUP_COLS
    pair = SUBTILES * BLOCK_Q
    lanes = SUBTILES * GROUP_LANES
    return pl.pallas_call(
        functools.partial(_selwin_kernel, nq=nq),
        grid=(GROUPS, batch, nblk),
        in_specs=[
            pl.BlockSpec((GROUP_COLS, pair), lambda g, b, i: (q_blk0 + g, b * nblk + i)),
            pl.BlockSpec((None, seq, HEAD_DIM + SEL_ROWS), lambda g, b, i: (g, b, 0)),
            pl.BlockSpec((None, None, seq, HEAD_DIM), lambda g, b, i: (3, g, b, 0)),
            pl.BlockSpec((None, None, nblk, V_AUG_ROWS, pair), lambda g, b, i: (g, b, 0, 0, 0)),
            pl.BlockSpec((None, None, nblk, V_AUG_ROWS, pair), lambda g, b, i: (g, b, 0, 0, 0)),
            pl.BlockSpec((None, None, nblk, pair // SEL_BLOCK, pair), lambda g, b, i: (b, g, 0, 0, i)),
            pl.BlockSpec((None, None, SUBTILES, HEAD_DIM, GROUP_LANES), lambda g, b, i: (b, g, i, 0, 0)),
            pl.BlockSpec((None, 3, HEADS_PER_GROUP, pair), lambda g, b, i: (g, 0, 0, b * nblk + i)),
            pl.BlockSpec((None, HEADS_PER_GROUP, nq + 2, BLOCK_Q), lambda g, b, i: (g, 0, 0, 0)),
            pl.BlockSpec((None, HEADS_PER_GROUP, N_WIN_SLOTS + 1, BLOCK_Q), lambda g, b, i: (g, 0, 0, 0)),
        ],
        out_specs=pl.BlockSpec((pair, GROUP_COLS), lambda g, b, i: (b * nblk + i, g)),
        out_shape=jax.ShapeDtypeStruct((tokens, NSA_HEADS * HEAD_DIM), BF16),
        scratch_shapes=[pltpu.VMEM((nq + 1, BLOCK_Q, GROUP_LANES), BF16),
                        pltpu.VMEM((N_WIN_SLOTS, BLOCK_Q, GROUP_LANES), BF16),
                        pltpu.VMEM((V_AUG_ROWS, lanes), F32),
                        pltpu.VMEM((V_AUG_ROWS, lanes), F32)],
        compiler_params=_params(3),
        name="nsa_sel_win",
    )(h_t, ksel_aug, kk, vsel_aug, vwin_aug, sel, oc_t, gates_t, vb_sel, vb_win)


ROW_SPLITS = 2


def _layer_norm(y, g_ref, b_ref):
    mu = jnp.mean(y, axis=-1, keepdims=True)
    yc = y - mu
    var = jnp.mean(yc * yc, axis=-1, keepdims=True)
    return yc * lax.rsqrt(var + LN_EPS) * g_ref[...] + b_ref[...]


def _merge_kernel(x_ref, ya_ref, yb_ref, wga_ref, wgb_ref, wa_ref, wb_ref, o_ref):
    x = x_ref[...]
    ga = jax.nn.sigmoid(jnp.dot(x, wga_ref[...], preferred_element_type=F32))
    gb = jax.nn.sigmoid(jnp.dot(x, wgb_ref[...], preferred_element_type=F32))
    a = jnp.dot(ya_ref[...], wa_ref[...], preferred_element_type=F32)
    bb = jnp.dot(yb_ref[...], wb_ref[...], preferred_element_type=F32)
    o_ref[...] = (ga * a + gb * bb).astype(o_ref.dtype)


def _merge(x_bf, y_a, y_b, w_ga, w_gb, w_a, w_b, *, tm, tn):
    tokens, d = x_bf.shape
    ya_cols, yb_cols = y_a.shape[1], y_b.shape[1]
    return pl.pallas_call(
        _merge_kernel,
        grid=(tokens // tm, d // tn),
        in_specs=[
            pl.BlockSpec((tm, d), lambda i, j: (i, 0)),
            pl.BlockSpec((tm, ya_cols), lambda i, j: (i, 0)),
            pl.BlockSpec((tm, yb_cols), lambda i, j: (i, 0)),
            pl.BlockSpec((d, tn), lambda i, j: (0, j)),
            pl.BlockSpec((d, tn), lambda i, j: (0, j)),
            pl.BlockSpec((ya_cols, tn), lambda i, j: (0, j)),
            pl.BlockSpec((yb_cols, tn), lambda i, j: (0, j)),
        ],
        out_specs=pl.BlockSpec((tm, tn), lambda i, j: (i, j)),
        out_shape=jax.ShapeDtypeStruct((tokens, d), BF16),
        compiler_params=_params(2),
        name="branch_merge",
    )(x_bf, y_a, y_b, w_ga, w_gb, w_a, w_b)


def _mix_ln_kernel(m_ref, x_ref, w_ref, g_ref, b_ref, o_ref):
    rows = m_ref.shape[0] // ROW_SPLITS
    for r in range(ROW_SPLITS):
        sl = slice(r * rows, (r + 1) * rows)
        mix = jnp.dot(m_ref[sl, :], w_ref[...], preferred_element_type=F32)
        o_ref[sl, :] = _layer_norm(DN_ALPHA * x_ref[sl, :] + mix, g_ref, b_ref)


def _mix_ln(merged, x, w_mix, ln_g, ln_b, *, tm):
    tokens, d = x.shape
    return pl.pallas_call(
        _mix_ln_kernel,
        grid=(tokens // tm,),
        in_specs=[
            pl.BlockSpec((tm, d), lambda i: (i, 0)),
            pl.BlockSpec((tm, d), lambda i: (i, 0)),
            pl.BlockSpec((d, d), lambda i: (0, 0)),
            pl.BlockSpec((1, d), lambda i: (0, 0)),
            pl.BlockSpec((1, d), lambda i: (0, 0)),
        ],
        out_specs=pl.BlockSpec((tm, d), lambda i: (i, 0)),
        out_shape=jax.ShapeDtypeStruct((tokens, d), F32),
        compiler_params=_params(1),
        name="mix_out_ln1",
    )(merged, x, w_mix, ln_g, ln_b)


def _xa_kernel(h_ref, wq_ref, k_ref, v_ref, wo_ref, g_ref, b_ref, o_ref):
    h = h_ref[...]
    q = jnp.dot(h.astype(BF16), wq_ref[...], preferred_element_type=F32) * (XA_HEAD_DIM ** -0.5)
    q = q.astype(BF16)
    outs = []
    for hd in range(XA_HEADS):
        cols = slice(hd * XA_HEAD_DIM, (hd + 1) * XA_HEAD_DIM)
        s = lax.dot_general(q[:, cols], k_ref[:, cols], (((1,), (1,)), ((), ())), preferred_element_type=F32)
        p = jnp.exp(s - jnp.max(s, axis=-1, keepdims=True))
        l = jnp.sum(p, axis=-1, keepdims=True)
        o = jnp.dot(p.astype(BF16), v_ref[:, cols], preferred_element_type=F32)
        outs.append(o * (1.0 / l))
    o = jnp.concatenate(outs, axis=1).astype(BF16)
    xa = jnp.dot(o, wo_ref[...], preferred_element_type=F32)
    o_ref[...] = _layer_norm(DN_ALPHA * h + xa, g_ref, b_ref)


def _cross_attention_ln(h, kv_mem, w_q, w_o, ln_g, ln_b, batch, seq, *, tm):
    tokens, d = h.shape
    mem_len = kv_mem.shape[0] // batch
    xa_dim = XA_HEADS * XA_HEAD_DIM
    nt = seq // tm
    return pl.pallas_call(
        _xa_kernel,
        grid=(batch, nt),
        in_specs=[
            pl.BlockSpec((tm, d), lambda b, i: (b * nt + i, 0)),
            pl.BlockSpec((d, xa_dim), lambda b, i: (0, 0)),
            pl.BlockSpec((mem_len, xa_dim), lambda b, i: (b, 0)),
            pl.BlockSpec((mem_len, xa_dim), lambda b, i: (b, 1)),
            pl.BlockSpec((xa_dim, d), lambda b, i: (0, 0)),
            pl.BlockSpec((1, d), lambda b, i: (0, 0)),
            pl.BlockSpec((1, d), lambda b, i: (0, 0)),
        ],
        out_specs=pl.BlockSpec((tm, d), lambda b, i: (b * nt + i, 0)),
        out_shape=jax.ShapeDtypeStruct((tokens, d), F32),
        compiler_params=_params(2),
        name="cross_attention_ln2",
    )(h, w_q, kv_mem, kv_mem, w_o, ln_g, ln_b)


def _mlp_kernel(h_ref, w1_ref, w2_ref, g_ref, b_ref, o_ref, hb_ref):
    j = pl.program_id(1)

    @pl.when(j == 0)
    def _():
        hb_ref[...] = h_ref[...].astype(BF16)
        o_ref[...] = jnp.zeros_like(o_ref)

    rows = hb_ref.shape[0] // ROW_SPLITS
    subs = [slice(r * rows, (r + 1) * rows) for r in range(ROW_SPLITS)]
    ups = [jnp.dot(hb_ref[sl, :], w1_ref[...], preferred_element_type=F32) for sl in subs]
    for sl, u in zip(subs, ups):
        u = jnp.square(jnp.maximum(u, 0.0)).astype(BF16)
        o_ref[sl, :] += jnp.dot(u, w2_ref[...], preferred_element_type=F32)

    @pl.when(j == pl.num_programs(1) - 1)
    def _():
        o_ref[...] = _layer_norm(DN_ALPHA * h_ref[...] + o_ref[...], g_ref, b_ref)


def _mlp_ln(h, w1, w2, ln_g, ln_b, *, tm, tf):
    tokens, d = h.shape
    d_ff = w1.shape[1]
    return pl.pallas_call(
        _mlp_kernel,
        grid=(tokens // tm, d_ff // tf),
        in_specs=[
            pl.BlockSpec((tm, d), lambda i, j: (i, 0)),
            pl.BlockSpec((d, tf), lambda i, j: (0, j)),
            pl.BlockSpec((tf, d), lambda i, j: (j, 0)),
            pl.BlockSpec((1, d), lambda i, j: (0, 0)),
            pl.BlockSpec((1, d), lambda i, j: (0, 0)),
        ],
        out_specs=pl.BlockSpec((tm, d), lambda i, j: (i, 0)),
        out_shape=jax.ShapeDtypeStruct((tokens, d), F32),
        scratch_shapes=[pltpu.VMEM((tm, d), BF16)],
        compiler_params=_params(2),
        name="mlp_ln3",
    )(h, w1, w2, ln_g, ln_b)


def _overlap_t(seq):
    ncp, nsel = seq // CMP_STRIDE, seq // SEL_BLOCK
    c_start = np.arange(ncp)[None, :] * CMP_STRIDE
    s_start = np.arange(nsel)[:, None] * SEL_BLOCK
    ov = (c_start < s_start + SEL_BLOCK) & (c_start + CMP_BLOCK > s_start) & (np.arange(ncp)[None, :] < ncp - 1)
    return jnp.asarray(ov, BF16)


def kernel(x, mem, w_in, attn_sinks, rel_bias_table, cmp_pos_k, cmp_w1_k, cmp_w2_k, cmp_pos_v, cmp_w1_v, cmp_w2_v,
           w_branch_swa, w_branch_nsa, w_mix_out, ln1_g, ln1_b, xa_w_q, xa_w_kv, xa_w_o, ln2_g, ln2_b,
           mlp_w1, mlp_w2, ln3_g, ln3_b):
    batch, seq, d = x.shape
    tokens = batch * seq
    nq = seq // BLOCK_Q
    assert w_in.shape[0] == 1, "one layer"
    assert seq % (8 * BLOCK_Q) == 0 and d % 512 == 0

    w = w_in[0]
    sizes = (SWA_HEADS * HEAD_DIM, 128, 128, NSA_HEADS * HEAD_DIM, 128, 128, 128, 128, 128, 128, 3 * NSA_HEADS, d, d)
    offs = np.concatenate([[0], np.cumsum(sizes)])
    (w_qa, w_ka, w_va, w_qb, w_kc, w_vc, w_ks, w_vs, w_kw, w_vw, w_gn, w_ga, w_gb) = [
        w[:, offs[n]:offs[n + 1]] for n in range(len(sizes))]
    w_rows = jnp.concatenate([w_ka, w_kc, w_ks, w_kw, w_vc], axis=1).astype(BF16)
    w_gn_pad = jnp.pad(w_gn, ((0, 0), (0, 128 - w_gn.shape[1])))
    w_cols_t = jnp.concatenate([w_qa, w_qb, w_va, w_vs, w_vw, w_gn_pad], axis=1).T.astype(BF16)

    x2 = x.reshape(tokens, d)
    x_bf = x2.astype(BF16)

    h_rows = _matmul(x_bf, w_rows, nt=False, tm=min(2048, tokens), tn=w_rows.shape[1])
    h_t = _matmul(w_cols_t, x_bf, nt=True, tm=1280, tn=min(1024, tokens))

    kk = h_rows[:, :512].reshape(tokens, 4, GROUPS, HEAD_DIM).transpose(1, 2, 0, 3)
    v_cmp = h_rows[:, 512:].reshape(tokens, GROUPS, HEAD_DIM).transpose(1, 0, 2)

    def value_tiles(row0, width):
        v = h_t[row0:row0 + 128].reshape(GROUPS, HEAD_DIM, batch, seq // width, width).transpose(0, 2, 3, 1, 4)
        ones = jnp.ones(v.shape[:3] + (V_AUG_ROWS - HEAD_DIM, width), BF16)
        return jnp.concatenate([v, ones], axis=3)

    vswa_aug = value_tiles(2048, BLOCK_Q)
    vsel_aug, vwin_aug = value_tiles(2176, 2 * BLOCK_Q), value_tiles(2304, 2 * BLOCK_Q)
    gates_t = h_t[2432:2432 + 3 * NSA_HEADS].reshape(GROUPS, HEADS_PER_GROUP, 3, tokens).transpose(0, 2, 1, 3)
    block_of_key = (jnp.arange(tokens) % BLOCK_Q) // SEL_BLOCK
    onehot = (block_of_key[:, None] == jnp.arange(SEL_ROWS)[None, :]).astype(BF16)
    ksel_aug = jnp.concatenate([kk[2], jnp.broadcast_to(onehot, (GROUPS, tokens, SEL_ROWS))], axis=2)

    by_dist = rel_bias_table[_rel_bucket(jnp.arange(seq))].astype(F32).T
    vb_swa = _bias_by_distance(by_dist[:SWA_HEADS], 4, window=SWA_WINDOW)
    vb_sel = _bias_by_distance(by_dist[SWA_HEADS:], nq + 2)
    vb_win = _bias_by_distance(by_dist[SWA_HEADS:], N_WIN_SLOTS + 1, window=NSA_WINDOW)
    vb_cmp = vb_sel
    sink_rows = jnp.repeat(attn_sinks[0].astype(F32), BLOCK_Q).reshape(GROUPS, 1, GROUP_LANES)

    y_a = _swa_attention(h_t, kk, vswa_aug, sink_rows, vb_swa, batch, seq)

    ncp = seq // CMP_STRIDE
    chunk_w = CMP_STRIDE * HEAD_DIM
    chunks = jnp.stack([kk[1], v_cmp]).reshape(2, GROUPS, batch, ncp, chunk_w)
    pos = jnp.stack([cmp_pos_k[0], cmp_pos_v[0]]).astype(F32).reshape(2, 2, 1, chunk_w)
    w1 = jnp.stack([cmp_w1_k[0], cmp_w1_v[0]]).astype(BF16).reshape(2, 2, chunk_w, -1)
    w2 = jnp.stack([cmp_w2_k[0], cmp_w2_v[0]]).astype(BF16)
    cn, ct = _compress(chunks, pos, w1, w2, w2.transpose(0, 2, 1))
    oc_t, sel = _cmp_attention(h_t, cn, ct, vb_cmp, _overlap_t(seq), batch, seq)
    sel = sel.reshape(batch, GROUPS, nq // 2, 4, seq)
    y_b = _selwin_attention(h_t, ksel_aug, kk, vsel_aug, vwin_aug, sel, oc_t, gates_t, vb_sel, vb_win, batch, seq)

    merged = _merge(x_bf, y_a, y_b, w_ga.astype(BF16), w_gb.astype(BF16),
                    w_branch_swa[0].astype(BF16), w_branch_nsa[0].astype(BF16), tm=min(1024, tokens), tn=512)
    h1 = _mix_ln(merged, x2, w_mix_out[0].astype(BF16), ln1_g, ln1_b, tm=512)

    mem_bf = mem.reshape(-1, d).astype(BF16)
    kv_mem = _matmul(mem_bf, xa_w_kv[0].astype(BF16), nt=False, tm=mem_bf.shape[0], tn=512)
    h2 = _cross_attention_ln(h1, kv_mem, xa_w_q[0].astype(BF16), xa_w_o[0].astype(BF16), ln2_g, ln2_b,
                             batch, seq, tm=512)

    h3 = _mlp_ln(h2, mlp_w1[0].astype(BF16), mlp_w2[0].astype(BF16), ln3_g, ln3_b, tm=1024, tf=512)
    return h3.reshape(batch, seq, d)
```

```python
import functools
import math

import numpy as np
import jax
import jax.numpy as jnp
from jax import lax
from jax.experimental import pallas as pl
from jax.experimental.pallas import tpu as pltpu

F32 = jnp.float32
BF16 = jnp.bfloat16

HEAD_DIM = 64
BLOCK_Q = 128
SWA_HEADS = 16
SWA_KV_HEADS = 2
SWA_WINDOW = 128
NSA_HEADS = 16
NSA_KV_HEADS = 2
CMP_BLOCK = 32
CMP_STRIDE = 16
SEL_BLOCK = 64
SEL_TOPK = 16
SEL_LOCAL = 2
NSA_WINDOW = 512
REL_BUCKETS = 32
REL_MAX_DIST = 4096
XA_HEADS = 4
XA_HEAD_DIM = 128
DEPTH = 1
DN_ALPHA = (2.0 * DEPTH) ** 0.25
LN_EPS = 1e-5
NEG_INF = -1e30
FORCE_SCORE = 1e4

GROUPS = 2
HEADS_PER_GROUP = 8
GROUP_LANES = HEADS_PER_GROUP * BLOCK_Q
GROUP_COLS = HEADS_PER_GROUP * HEAD_DIM
SUBTILES = 2

V7X_VMEM_LIMIT_BYTES = 56 * 1024 * 1024


def _params(n_axes):
    return pltpu.CompilerParams(dimension_semantics=("arbitrary",) * n_axes,
                                vmem_limit_bytes=V7X_VMEM_LIMIT_BYTES)


def _mm_kernel(a_ref, b_ref, o_ref, *, nt):
    if nt:
        out = lax.dot_general(a_ref[...], b_ref[...], (((1,), (1,)), ((), ())), preferred_element_type=F32)
    else:
        out = jnp.dot(a_ref[...], b_ref[...], preferred_element_type=F32)
    o_ref[...] = out.astype(o_ref.dtype)


def _matmul(a, b, *, nt, tm, tn, out_dtype=BF16):
    m, k = a.shape
    n = b.shape[0] if nt else b.shape[1]
    b_spec = pl.BlockSpec((tn, k), lambda i, j: (j, 0)) if nt else pl.BlockSpec((k, tn), lambda i, j: (0, j))
    return pl.pallas_call(
        functools.partial(_mm_kernel, nt=nt),
        grid=(m // tm, n // tn),
        in_specs=[pl.BlockSpec((tm, k), lambda i, j: (i, 0)), b_spec],
        out_specs=pl.BlockSpec((tm, tn), lambda i, j: (i, j)),
        out_shape=jax.ShapeDtypeStruct((m, n), out_dtype),
        compiler_params=_params(2),
        name="proj_nt" if nt else "proj_nn",
    )(a, b)


def _rel_bucket(dist):
    exact = REL_BUCKETS // 2
    d = jnp.maximum(dist, 0)
    log_ratio = jnp.log(jnp.maximum(d, 1).astype(F32) / exact) / math.log(REL_MAX_DIST / exact)
    large = jnp.minimum(exact + (log_ratio * (REL_BUCKETS - exact)).astype(jnp.int32), REL_BUCKETS - 1)
    return jnp.where(d < exact, d, large)


def _bias_by_distance(by_dist, rows, *, window=None):
    heads, seq = by_dist.shape
    if window is not None:
        by_dist = jnp.where(jnp.arange(seq) < window, by_dist, NEG_INF)
    padded = jnp.pad(by_dist, ((0, 0), (2 * BLOCK_Q, 0)), constant_values=NEG_INF)[:, :rows * BLOCK_Q]
    return padded.reshape(heads // HEADS_PER_GROUP, HEADS_PER_GROUP, rows, BLOCK_Q)


def _build_skew_table(vb_ref, tbl_ref, n_tiles):
    def body(d, carry):
        for h in range(HEADS_PER_GROUP):
            lo = jnp.broadcast_to(vb_ref[h, pl.ds(d, 1), :], (BLOCK_Q, BLOCK_Q))
            hi = jnp.broadcast_to(vb_ref[h, pl.ds(d + 1, 1), :], (BLOCK_Q, BLOCK_Q))
            y = pltpu.roll(jnp.concatenate([lo, hi], axis=1), 0, 1, stride=1, stride_axis=0)
            tbl_ref[d, :, h * BLOCK_Q:(h + 1) * BLOCK_Q] = y[:, BLOCK_Q:].astype(tbl_ref.dtype)
        return carry

    lax.fori_loop(0, n_tiles, body, 0)


def _build_cmp_bias_table(vb_ref, tbl_ref, skew_ref, nq):
    per_tile = BLOCK_Q // CMP_STRIDE
    assert per_tile == 8 and CMP_BLOCK - 1 + CMP_STRIDE * (per_tile - 1) - BLOCK_Q == CMP_STRIDE - 1
    tbl_ref[pl.ds(nq * 8, nq * 8), :] = jnp.full((nq * 8, GROUP_LANES), NEG_INF, F32)

    def body(d, last_row):
        for h in range(HEADS_PER_GROUP):
            lo = jnp.broadcast_to(vb_ref[h, pl.ds(d + 1, 1), :], (BLOCK_Q, BLOCK_Q))
            hi = jnp.broadcast_to(vb_ref[h, pl.ds(d + 2, 1), :], (BLOCK_Q, BLOCK_Q))
            y = pltpu.roll(jnp.concatenate([lo, hi], axis=1), 0, 1, stride=1, stride_axis=0)
            skew_ref[:, h * BLOCK_Q:(h + 1) * BLOCK_Q] = y[:, BLOCK_Q:]
        offsets = [CMP_BLOCK - 1 + CMP_STRIDE * c for c in range(per_tile - 1)]
        rows = [skew_ref[k:k + 1, :] for k in offsets] + [last_row]
        tbl_ref[pl.ds(pl.multiple_of((nq - 1 - d) * 8, 8), 8), :] = jnp.concatenate(rows, axis=0)
        return skew_ref[CMP_STRIDE - 1:CMP_STRIDE, :]

    lax.fori_loop(0, nq, body, jnp.full((1, GROUP_LANES), NEG_INF, F32))


def _gather_heads_to_lanes(q_ref):
    q = jnp.concatenate([q_ref[h * HEAD_DIM:(h + 1) * HEAD_DIM, u * BLOCK_Q:(u + 1) * BLOCK_Q]
                         for u in range(q_ref.shape[1] // BLOCK_Q) for h in range(HEADS_PER_GROUP)], axis=1)
    return q * jnp.asarray(HEAD_DIM ** -0.5, q.dtype)


def _heads_to_columns(o_t):
    stacked = jnp.concatenate([o_t[:, h * BLOCK_Q:(h + 1) * BLOCK_Q] for h in range(HEADS_PER_GROUP)], axis=0)
    return stacked.T


def _identity_tile():
    r = lax.broadcasted_iota(jnp.int32, (BLOCK_Q, BLOCK_Q), 0)
    c = lax.broadcasted_iota(jnp.int32, (BLOCK_Q, BLOCK_Q), 1)
    return jnp.where(r == c, 1.0, 0.0).astype(BF16)


def _biased_scores(eye, k_t, bias_tile, q_t):
    lhs = jnp.concatenate([eye, k_t], axis=1)
    rhs = jnp.concatenate([bias_tile, q_t], axis=0)
    return jnp.dot(lhs, rhs, preferred_element_type=F32)


def _softmax_probs(m, s_tiles):
    ms, alphas, ps = [], [], [[] for _ in s_tiles]
    for h in range(m.shape[1] // BLOCK_Q):
        lanes = slice(h * BLOCK_Q, (h + 1) * BLOCK_Q)
        cols = [s[:, lanes] for s in s_tiles]
        m_new = m[:, lanes]
        for c in cols:
            m_new = jnp.maximum(m_new, jnp.max(c, axis=0, keepdims=True))
        ms.append(m_new)
        alphas.append(jnp.exp(m[:, lanes] - m_new))
        for j, c in enumerate(cols):
            ps[j].append(jnp.exp((c - m_new).astype(BF16)))
    p = jnp.concatenate([jnp.concatenate(pj, axis=1) for pj in ps], axis=0)
    return jnp.concatenate(ms, axis=1), jnp.concatenate(alphas, axis=1), p


def _softmax_update(carry, s_tiles, v_aug):
    m, acc = carry
    m, alpha, p = _softmax_probs(m, s_tiles)
    return m, alpha * acc + jnp.dot(v_aug, p, preferred_element_type=F32)


def _pipelined_attention(n_steps, scores_fn, values_fn, init, s_scr, p_scr, acc_scr, *, fill):
    m0, acc0 = init

    def put_scores(n):
        for j, s in enumerate(scores_fn(n)):
            s_scr[j] = s

    def fold(alpha, n, p):
        acc_scr[...] = alpha * acc_scr[...] + jnp.dot(values_fn(n), p, preferred_element_type=F32)

    if fill:
        put_scores(0)
        p_scr[...] = jnp.zeros_like(p_scr)
        acc_scr[...] = acc0
        return None

    def body(k, carry):
        m, alpha_a, alpha_b = carry
        a = 2 * k
        s_b = scores_fn(a + 1)
        fold(alpha_a, a - 2, p_scr[0])
        fold(alpha_b, a - 1, p_scr[1])
        m, alpha_a, p_a = _softmax_probs(m, [s_scr.at[j] for j in range(s_scr.shape[0])])
        put_scores(a + 2)
        m, alpha_b, p_b = _softmax_probs(m, s_b)
        p_scr[0] = p_a
        p_scr[1] = p_b
        return m, alpha_a, alpha_b

    trips = (n_steps + 1) // 2
    ones = jnp.ones_like(m0)
    m, alpha_a, alpha_b = lax.fori_loop(0, trips, body, (m0, ones, ones))
    fold(alpha_a, 2 * trips - 2, p_scr[0])
    fold(alpha_b, 2 * trips - 1, p_scr[1])
    return m, acc_scr[...]


SLAB = 2 * BLOCK_Q
LOOKAHEAD = 12


def _slabwise_softmax_step(m, acc_ref, lhs_tiles, rhs_fn, v_fn, masked_fn=None):
    per_slab = [[j for j in range(len(lhs_tiles)) if masked_fn is None or not masked_fn(j, sl)]
                for sl in range(m.shape[1] // SLAB)]
    items = [(sl, j) for sl, tiles in enumerate(per_slab) for j in tiles]

    def scores(k):
        sl, j = items[k]
        lhs = lhs_tiles[j](sl) if callable(lhs_tiles[j]) else lhs_tiles[j]
        return jnp.dot(lhs, rhs_fn(j, sl), preferred_element_type=F32)

    pending = [scores(k) for k in range(min(LOOKAHEAD, len(items)))]
    new_m = []
    for k, (sl, j) in enumerate(items):
        lanes = slice(sl * SLAB, (sl + 1) * SLAB)
        if j == per_slab[sl][0]:
            m_s, acc_s = m[:, lanes], acc_ref[:, lanes]
        s = pending.pop(0)
        if k + LOOKAHEAD < len(items):
            pending.append(scores(k + LOOKAHEAD))
        m_next = jnp.maximum(m_s, jnp.max(s, axis=0, keepdims=True))
        p = jnp.exp((s - m_next).astype(BF16))
        acc_s = jnp.exp(m_s - m_next) * acc_s + jnp.dot(v_fn(j, sl), p, preferred_element_type=F32)
        m_s = m_next
        if j == per_slab[sl][-1]:
            acc_ref[:, lanes] = acc_s
            new_m.append(m_s)
    return jnp.concatenate(new_m, axis=1)


V_AUG_ROWS = HEAD_DIM + 16


def _normalized(acc):
    return acc[:HEAD_DIM] * (1.0 / acc[HEAD_DIM:HEAD_DIM + 1])


def _swa_kernel(q_ref, kprev_ref, kcur_ref, vprev_ref, vcur_ref, sink_ref, vb_ref, o_ref, tbl_ref, acc_ref):
    b, i = pl.program_id(1), pl.program_id(2)

    @pl.when((b == 0) & (i == 0))
    def _():
        _build_skew_table(vb_ref, tbl_ref, 3)

    eye = _identity_tile()
    q_t = _gather_heads_to_lanes(q_ref)
    lanes = q_t.shape[1]
    slabs_per_tile = GROUP_LANES // SLAB
    m0 = jnp.concatenate([sink_ref[...]] * SWA_TILES, axis=1)
    acc_ref[...] = jnp.where(lax.broadcasted_iota(jnp.int32, (V_AUG_ROWS, lanes), 0) < HEAD_DIM, 0.0, 1.0)

    def key_tile(j):
        def at(slab):
            u = slab // slabs_per_tile - j
            k_t = kprev_ref[...] if u < 0 else kcur_ref[u * BLOCK_Q:(u + 1) * BLOCK_Q, :]
            return jnp.concatenate([eye, k_t], axis=1)
        return at

    def rhs(j, slab):
        u, part = slab // slabs_per_tile, slab % slabs_per_tile
        slot = 1 + j if (u > 0 or j == 0) else jnp.where(i > 0, 2, 0)
        return jnp.concatenate([tbl_ref[slot, :, part * SLAB:(part + 1) * SLAB],
                                q_t[:, slab * SLAB:(slab + 1) * SLAB]], axis=0)

    def values(j, slab):
        u = slab // slabs_per_tile - j
        return vprev_ref[...] if u < 0 else vcur_ref[u]

    _slabwise_softmax_step(m0, acc_ref, [key_tile(0), key_tile(1)], rhs, values)
    out = _normalized(acc_ref[...])
    for u in range(SWA_TILES):
        o_ref[u * BLOCK_Q:(u + 1) * BLOCK_Q, :] = _heads_to_columns(
            out[:, u * GROUP_LANES:(u + 1) * GROUP_LANES]).astype(o_ref.dtype)


SWA_TILES = 4


def _swa_attention(h_t, kk, v_aug, sink_rows, vb, batch, seq):
    nq = seq // BLOCK_Q
    nblk = nq // SWA_TILES
    width = SWA_TILES * BLOCK_Q
    tokens = batch * seq

    def prev_tile(i):
        return jnp.maximum(i * SWA_TILES - 1, 0)

    return pl.pallas_call(
        _swa_kernel,
        grid=(GROUPS, batch, nblk),
        in_specs=[
            pl.BlockSpec((GROUP_COLS, width), lambda g, b, i: (g, b * nblk + i)),
            pl.BlockSpec((None, None, BLOCK_Q, HEAD_DIM), lambda g, b, i: (0, g, b * nq + prev_tile(i), 0)),
            pl.BlockSpec((None, None, width, HEAD_DIM), lambda g, b, i: (0, g, b * nblk + i, 0)),
            pl.BlockSpec((None, None, None, V_AUG_ROWS, BLOCK_Q), lambda g, b, i: (g, b, prev_tile(i), 0, 0)),
            pl.BlockSpec((None, None, SWA_TILES, V_AUG_ROWS, BLOCK_Q), lambda g, b, i: (g, b, i, 0, 0)),
            pl.BlockSpec((None, 1, GROUP_LANES), lambda g, b, i: (g, 0, 0)),
            pl.BlockSpec((None, HEADS_PER_GROUP, 4, BLOCK_Q), lambda g, b, i: (g, 0, 0, 0)),
        ],
        out_specs=pl.BlockSpec((width, GROUP_COLS), lambda g, b, i: (b * nblk + i, g)),
        out_shape=jax.ShapeDtypeStruct((tokens, SWA_HEADS * HEAD_DIM), BF16),
        scratch_shapes=[pltpu.VMEM((3, BLOCK_Q, GROUP_LANES), BF16),
                        pltpu.VMEM((V_AUG_ROWS, SWA_TILES * GROUP_LANES), F32)],
        compiler_params=_params(3),
        name="swa_attention",
    )(h_t, kk, kk, v_aug, v_aug, sink_rows, vb)


def _compress_kernel(c_ref, pos_ref, w1_ref, w2_ref, w2t_ref, cn_ref, ct_ref):
    c = c_ref[...].astype(F32)
    top = (c + pos_ref[0]).astype(BF16)
    bot = (c + pos_ref[1]).astype(BF16)
    a = jnp.dot(top, w1_ref[0], preferred_element_type=F32)
    bm = jnp.dot(bot, w1_ref[1], preferred_element_type=F32)
    n = a.shape[0]
    pre = a + pltpu.roll(bm, n - 1, 0)
    hid = jax.nn.gelu(pre).astype(BF16)
    cn_ref[...] = jnp.dot(hid, w2_ref[...], preferred_element_type=F32).astype(cn_ref.dtype)
    ct_ref[...] = lax.dot_general(w2t_ref[...], hid, (((1,), (1,)), ((), ())),
                                  preferred_element_type=F32).astype(ct_ref.dtype)


def _compress(chunks, pos, w1, w2, w2t):
    _, g, b, ncp, width = chunks.shape
    hidden = w1.shape[-1]
    return pl.pallas_call(
        _compress_kernel,
        grid=(2, g, b),
        in_specs=[
            pl.BlockSpec((None, None, None, ncp, width), lambda t, g, b: (t, g, b, 0, 0)),
            pl.BlockSpec((None, 2, 1, width), lambda t, g, b: (t, 0, 0, 0)),
            pl.BlockSpec((None, 2, width, hidden), lambda t, g, b: (t, 0, 0, 0)),
            pl.BlockSpec((None, hidden, HEAD_DIM), lambda t, g, b: (t, 0, 0)),
            pl.BlockSpec((None, HEAD_DIM, hidden), lambda t, g, b: (t, 0, 0)),
        ],
        out_specs=[
            pl.BlockSpec((None, None, None, ncp, HEAD_DIM), lambda t, g, b: (t, g, b, 0, 0)),
            pl.BlockSpec((None, None, None, HEAD_DIM, ncp), lambda t, g, b: (t, g, b, 0, 0)),
        ],
        out_shape=[jax.ShapeDtypeStruct((2, g, b, ncp, HEAD_DIM), BF16),
                   jax.ShapeDtypeStruct((2, g, b, HEAD_DIM, ncp), BF16)],
        compiler_params=_params(3),
        name="nsa_compress",
    )(chunks, pos, w1, w2, w2t)


def _cmp_kernel(q_ref, kc_ref, vct_ref, vb_ref, ov_ref, oc_ref, sel_ref, bias_ref, skew_ref, bias16_ref, *, nq):
    b, i = pl.program_id(1), pl.program_id(2)

    @pl.when((b == 0) & (i == 0))
    def _():
        _build_cmp_bias_table(vb_ref, bias_ref, skew_ref, nq)
        rows = bias_ref.shape[0]
        bias16_ref[0] = bias_ref[...].astype(BF16)
        bias16_ref[1, 0:rows - 16, :] = bias_ref[8:rows - 8, :].astype(BF16)

    ncp = kc_ref.shape[0]
    nsel = sel_ref.shape[0]
    n_tiles = ncp // BLOCK_Q
    q_t = _gather_heads_to_lanes(q_ref)
    width = SUBTILES * BLOCK_Q
    eye = _identity_tile()

    slabs_per_tile = GROUP_LANES // SLAB
    lhs = [jnp.concatenate([eye, kc_ref[t * BLOCK_Q:(t + 1) * BLOCK_Q, :]], axis=1) for t in range(n_tiles)]
    items = [(sl, t) for sl in range(SUBTILES * slabs_per_tile) for t in range(n_tiles)]

    def scores(k):
        sl, t = items[k]
        u, part = sl // slabs_per_tile, sl % slabs_per_tile
        shifted = (nq - 1 - u) % 2
        row0 = pl.multiple_of((nq - 1 - (SUBTILES * i + u)) * 8 + t * BLOCK_Q - 8 * shifted, 16)
        bias = bias16_ref[shifted, pl.ds(row0, BLOCK_Q), part * SLAB:(part + 1) * SLAB]
        return jnp.dot(lhs[t], jnp.concatenate([bias, q_t[:, sl * SLAB:(sl + 1) * SLAB]], axis=0),
                       preferred_element_type=F32)

    q_pos = i * width + lax.broadcasted_iota(jnp.int32, (1, width), 1)
    sees_any = q_pos >= CMP_BLOCK - 1
    psum = [[jnp.zeros((BLOCK_Q, BLOCK_Q), F32) for _ in range(n_tiles)] for _ in range(SUBTILES)]
    pending = [scores(k) for k in range(min(LOOKAHEAD, len(items)))]
    for sl in range(SUBTILES * slabs_per_tile):
        u, part = sl // slabs_per_tile, sl % slabs_per_tile
        s = [pending.pop(0) for _ in range(n_tiles)]
        for k in range(sl * n_tiles + LOOKAHEAD, min((sl + 1) * n_tiles + LOOKAHEAD, len(items))):
            pending.append(scores(k))
        m = functools.reduce(jnp.maximum, [jnp.max(c, axis=0, keepdims=True) for c in s])
        e = [jnp.exp(c - m) for c in s]
        l = functools.reduce(jnp.add, [jnp.sum(c, axis=0, keepdims=True) for c in e])
        seen = sees_any[:, u * BLOCK_Q:(u + 1) * BLOCK_Q]
        inv = jnp.where(jnp.concatenate([seen] * (SLAB // BLOCK_Q), axis=1), 1.0 / l, 0.0)
        o_slab = jnp.zeros((HEAD_DIM, SLAB), F32)
        for t in range(n_tiles):
            p = e[t] * inv
            psum[u][t] = psum[u][t] + functools.reduce(
                jnp.add, [p[:, r * BLOCK_Q:(r + 1) * BLOCK_Q] for r in range(SLAB // BLOCK_Q)])
            o_slab = o_slab + jnp.dot(vct_ref[:, t * BLOCK_Q:(t + 1) * BLOCK_Q], p.astype(BF16),
                                      preferred_element_type=F32)
        oc_ref[u, :, part * SLAB:(part + 1) * SLAB] = o_slab.astype(oc_ref.dtype)

    psum = jnp.concatenate([jnp.concatenate(pu, axis=0) for pu in psum], axis=1)
    hi = psum.astype(BF16)
    lo = (psum - hi.astype(F32)).astype(BF16)
    ov = ov_ref[...]
    score = jnp.dot(ov, hi, preferred_element_type=F32) + jnp.dot(ov, lo, preferred_element_type=F32)

    j_io = lax.broadcasted_iota(jnp.int32, (nsel, width), 0)
    qpos = i * width + lax.broadcasted_iota(jnp.int32, (nsel, width), 1)
    causal = j_io * SEL_BLOCK <= qpos
    back = qpos // SEL_BLOCK - j_io
    forced = (j_io == 0) | ((back >= 0) & (back < SEL_LOCAL))
    score = jnp.where(causal, jnp.where(forced, FORCE_SCORE, score), -1.0)
    slab_rows = lax.broadcasted_iota(jnp.int32, (8, width), 0)
    slabs = [score[8 * g:8 * (g + 1), :] for g in range(nsel // 8)]
    ranks = [jnp.zeros((8, width), F32) for _ in slabs]
    for r in range(nsel):
        row = jnp.broadcast_to(score[r:r + 1, :], (8, width))
        for g, slab in enumerate(slabs):
            if g > r // 8:
                ahead = row >= slab
            elif g < r // 8:
                ahead = row > slab
            else:
                ranks[g] = ranks[g] + jnp.where(slab_rows > r % 8, jnp.where(row >= slab, 1.0, 0.0),
                                                jnp.where(row > slab, 1.0, 0.0))
                continue
            ranks[g] = ranks[g] + jnp.where(ahead, 1.0, 0.0)
    rank = jnp.concatenate(ranks, axis=0)
    sel_ref[...] = jnp.where((rank < min(SEL_TOPK, nsel)) & causal, 1.0, 0.0).astype(sel_ref.dtype)


def _cmp_attention(h_t, cn, ct, vb, overlap_t, batch, seq):
    nq = seq // BLOCK_Q
    ncp = seq // CMP_STRIDE
    nsel = seq // SEL_BLOCK
    q_blk0 = SWA_HEADS * HEAD_DIM // GROUP_COLS
    nblk = nq // SUBTILES
    width = SUBTILES * BLOCK_Q
    return pl.pallas_call(
        functools.partial(_cmp_kernel, nq=nq),
        grid=(GROUPS, batch, nblk),
        in_specs=[
            pl.BlockSpec((GROUP_COLS, width), lambda g, b, i: (q_blk0 + g, b * nblk + i)),
            pl.BlockSpec((None, None, None, ncp, HEAD_DIM), lambda g, b, i: (0, g, b, 0, 0)),
            pl.BlockSpec((None, None, None, HEAD_DIM, ncp), lambda g, b, i: (1, g, b, 0, 0)),
            pl.BlockSpec((None, HEADS_PER_GROUP, nq + 2, BLOCK_Q), lambda g, b, i: (g, 0, 0, 0)),
            pl.BlockSpec((nsel, ncp), lambda g, b, i: (0, 0)),
        ],
        out_specs=[
            pl.BlockSpec((None, None, SUBTILES, HEAD_DIM, GROUP_LANES), lambda g, b, i: (b, g, i, 0, 0)),
            pl.BlockSpec((None, None, nsel, width), lambda g, b, i: (b, g, 0, i)),
        ],
        out_shape=[jax.ShapeDtypeStruct((batch, GROUPS, nq, HEAD_DIM, GROUP_LANES), BF16),
                   jax.ShapeDtypeStruct((batch, GROUPS, nsel, seq), F32)],
        scratch_shapes=[pltpu.VMEM((2 * nq * 8, GROUP_LANES), F32), pltpu.VMEM((BLOCK_Q, GROUP_LANES), F32),
                        pltpu.VMEM((2, 2 * nq * 8, GROUP_LANES), BF16)],
        compiler_params=_params(3),
        name="nsa_cmp_select",
    )(h_t, cn, ct, vb, overlap_t)


N_WIN_PAIRS = -(-(NSA_WINDOW - 1) // (SUBTILES * BLOCK_Q)) + 1
N_WIN_SLOTS = SUBTILES * N_WIN_PAIRS + 1
SEL_ROWS = 16
SEL_STEPS = 2


def _selwin_kernel(q_ref, ksel_ref, kwin_ref, vsel_ref, vwin_ref, sel_ref, oc_ref, gate_ref, vbs_ref, vbw_ref,
                   o_ref, tsel_ref, twin_ref, accs_ref, accw_ref, *, nq):
    b, blk = pl.program_id(1), pl.program_id(2)

    @pl.when((b == 0) & (blk == 0))
    def _():
        _build_skew_table(vbs_ref, tsel_ref, nq + 1)
        _build_skew_table(vbw_ref, twin_ref, N_WIN_SLOTS)

    q_t = _gather_heads_to_lanes(q_ref)
    lanes = q_t.shape[1]
    eye = _identity_tile()
    pair = SUBTILES * BLOCK_Q
    m0 = jnp.full((1, lanes), NEG_INF, F32)
    accs_ref[...] = jnp.zeros_like(accs_ref)
    accw_ref[...] = jnp.zeros_like(accw_ref)
    blocks_per_tile = BLOCK_Q // SEL_BLOCK
    slabs_per_tile = GROUP_LANES // SLAB

    def attend(m, steps, k_ref, tbl_ref, v_ref, acc_ref, q_ext_fn, reach=None):
        tiles = [(n, jnp.clip(blk - n, 0, blk), j) for n in steps for j in range(SUBTILES)]
        lhs = [jnp.concatenate([eye, k_ref[pl.ds(pl.multiple_of(p * pair + j * BLOCK_Q, BLOCK_Q), BLOCK_Q), :]],
                               axis=1) for _, p, j in tiles]
        q_exts = [q_ext_fn(p, j) for _, p, j in tiles]

        def rhs(t, slab):
            n, _, j = tiles[t]
            u, part = slab // slabs_per_tile, slab % slabs_per_tile
            slot = jnp.where(n <= blk, 2 * n + u - j + 1, 0)
            bias = tbl_ref[slot, :, part * SLAB:(part + 1) * SLAB]
            return jnp.concatenate([bias, q_exts[t][:, slab * SLAB:(slab + 1) * SLAB]], axis=0)

        def values(t, slab):
            _, p, j = tiles[t]
            return v_ref[p, :, j * BLOCK_Q:(j + 1) * BLOCK_Q]

        def masked(t, slab):
            n, _, j = tiles[t]
            distance = 2 * n + slab // slabs_per_tile - j
            return distance < 0 or distance >= reach

        return _slabwise_softmax_step(m, acc_ref, lhs, rhs, values, masked if reach is not None else None)

    attend(m0, list(range(N_WIN_PAIRS)), kwin_ref, twin_ref, vwin_ref, accw_ref, lambda p, j: q_t,
           reach=-(-(NSA_WINDOW + BLOCK_Q - 1) // BLOCK_Q))

    def sel_q_ext(p, j):
        per_pair = SUBTILES * blocks_per_tile
        group = sel_ref[pl.ds(pl.multiple_of((p // 2) * 2 * per_pair, 8), 2 * per_pair), :]
        lo, hi = (group[half * per_pair + j * blocks_per_tile:half * per_pair + (j + 1) * blocks_per_tile]
                  for half in range(2))
        neg = jnp.where(jnp.where(p % 2 == 1, hi, lo) > 0.5, 0.0, NEG_INF)
        rows = jnp.concatenate([neg[:, u * BLOCK_Q:(u + 1) * BLOCK_Q]
                                for u in range(SUBTILES) for _ in range(HEADS_PER_GROUP)], axis=1)
        rows = jnp.concatenate([rows, jnp.zeros((SEL_ROWS - blocks_per_tile, lanes), F32)], axis=0)
        return jnp.concatenate([q_t, rows.astype(BF16)], axis=0)

    def sel_body(k, m):
        return attend(m, [SEL_STEPS * k + r for r in range(SEL_STEPS)], ksel_ref, tsel_ref, vsel_ref, accs_ref,
                      sel_q_ext)

    lax.fori_loop(0, (blk + SEL_STEPS) // SEL_STEPS, sel_body, m0)

    def gate_row(branch):
        g = jnp.concatenate([gate_ref[branch, h:h + 1, u * BLOCK_Q:(u + 1) * BLOCK_Q]
                             for u in range(SUBTILES) for h in range(HEADS_PER_GROUP)], axis=1)
        return jax.nn.sigmoid(g.astype(F32))

    o_c = jnp.concatenate([oc_ref[u] for u in range(SUBTILES)], axis=1).astype(F32)
    out = (gate_row(0) * o_c + gate_row(1) * _normalized(accs_ref[...])
           + gate_row(2) * _normalized(accw_ref[...]))
    for u in range(SUBTILES):
        o_ref[u * BLOCK_Q:(u + 1) * BLOCK_Q, :] = _heads_to_columns(
            out[:, u * GROUP_LANES:(u + 1) * GROUP_LANES]).astype(o_ref.dtype)


def _selwin_attention(h_t, ksel_aug, kk, vsel_aug, vwin_aug, sel, oc_t, gates_t, vb_sel, vb_win, batch, seq):
    nq = seq // BLOCK_Q
    nblk = nq // SUBTILES
    tokens = batch * seq
    q_blk0 = SWA_HEADS * HEAD_DIM // GROUP_COLS
    pair = SUBTILES * BLOCK_Q
    lanes = SUBTILES * GROUP_LANES
    return pl.pallas_call(
        functools.partial(_selwin_kernel, nq=nq),
        grid=(GROUPS, batch, nblk),
        in_specs=[
            pl.BlockSpec((GROUP_COLS, pair), lambda g, b, i: (q_blk0 + g, b * nblk + i)),
            pl.BlockSpec((None, seq, HEAD_DIM + SEL_ROWS), lambda g, b, i: (g, b, 0)),
            pl.BlockSpec((None, None, seq, HEAD_DIM), lambda g, b, i: (3, g, b, 0)),
            pl.BlockSpec((None, None, nblk, V_AUG_ROWS, pair), lambda g, b, i: (g, b, 0, 0, 0)),
            pl.BlockSpec((None, None, nblk, V_AUG_ROWS, pair), lambda g, b, i: (g, b, 0, 0, 0)),
            pl.BlockSpec((None, None, seq // SEL_BLOCK, pair), lambda g, b, i: (b, g, 0, i)),
            pl.BlockSpec((None, None, SUBTILES, HEAD_DIM, GROUP_LANES), lambda g, b, i: (b, g, i, 0, 0)),
            pl.BlockSpec((None, 3, HEADS_PER_GROUP, pair), lambda g, b, i: (g, 0, 0, b * nblk + i)),
            pl.BlockSpec((None, HEADS_PER_GROUP, nq + 2, BLOCK_Q), lambda g, b, i: (g, 0, 0, 0)),
            pl.BlockSpec((None, HEADS_PER_GROUP, N_WIN_SLOTS + 1, BLOCK_Q), lambda g, b, i: (g, 0, 0, 0)),
        ],
        out_specs=pl.BlockSpec((pair, GROUP_COLS), lambda g, b, i: (b * nblk + i, g)),
        out_shape=jax.ShapeDtypeStruct((tokens, NSA_HEADS * HEAD_DIM), BF16),
        scratch_shapes=[pltpu.VMEM((nq + 1, BLOCK_Q, GROUP_LANES), BF16),
                        pltpu.VMEM((N_WIN_SLOTS, BLOCK_Q, GROUP_LANES), BF16),
                        pltpu.VMEM((V_AUG_ROWS, lanes), F32),
                        pltpu.VMEM((V_AUG_ROWS, lanes), F32)],
        compiler_params=_params(3),
        name="nsa_sel_win",
    )(h_t, ksel_aug, kk, vsel_aug, vwin_aug, sel, oc_t, gates_t, vb_sel, vb_win)


ROW_SPLITS = 2


def _layer_norm(y, g_ref, b_ref):
    mu = jnp.mean(y, axis=-1, keepdims=True)
    yc = y - mu
    var = jnp.mean(yc * yc, axis=-1, keepdims=True)
    return yc * lax.rsqrt(var + LN_EPS) * g_ref[...] + b_ref[...]


def _merge_kernel(x_ref, ya_ref, yb_ref, wga_ref, wgb_ref, wa_ref, wb_ref, o_ref):
    x = x_ref[...]
    ga = jax.nn.sigmoid(jnp.dot(x, wga_ref[...], preferred_element_type=F32))
    gb = jax.nn.sigmoid(jnp.dot(x, wgb_ref[...], preferred_element_type=F32))
    a = jnp.dot(ya_ref[...], wa_ref[...], preferred_element_type=F32)
    bb = jnp.dot(yb_ref[...], wb_ref[...], preferred_element_type=F32)
    o_ref[...] = (ga * a + gb * bb).astype(o_ref.dtype)


def _merge(x_bf, y_a, y_b, w_ga, w_gb, w_a, w_b, *, tm, tn):
    tokens, d = x_bf.shape
    ya_cols, yb_cols = y_a.shape[1], y_b.shape[1]
    return pl.pallas_call(
        _merge_kernel,
        grid=(tokens // tm, d // tn),
        in_specs=[
            pl.BlockSpec((tm, d), lambda i, j: (i, 0)),
            pl.BlockSpec((tm, ya_cols), lambda i, j: (i, 0)),
            pl.BlockSpec((tm, yb_cols), lambda i, j: (i, 0)),
            pl.BlockSpec((d, tn), lambda i, j: (0, j)),
            pl.BlockSpec((d, tn), lambda i, j: (0, j)),
            pl.BlockSpec((ya_cols, tn), lambda i, j: (0, j)),
            pl.BlockSpec((yb_cols, tn), lambda i, j: (0, j)),
        ],
        out_specs=pl.BlockSpec((tm, tn), lambda i, j: (i, j)),
        out_shape=jax.ShapeDtypeStruct((tokens, d), BF16),
        compiler_params=_params(2),
        name="branch_merge",
    )(x_bf, y_a, y_b, w_ga, w_gb, w_a, w_b)


def _mix_ln_kernel(m_ref, x_ref, w_ref, g_ref, b_ref, o_ref):
    rows = m_ref.shape[0] // ROW_SPLITS
    for r in range(ROW_SPLITS):
        sl = slice(r * rows, (r + 1) * rows)
        mix = jnp.dot(m_ref[sl, :], w_ref[...], preferred_element_type=F32)
        o_ref[sl, :] = _layer_norm(DN_ALPHA * x_ref[sl, :] + mix, g_ref, b_ref)


def _mix_ln(merged, x, w_mix, ln_g, ln_b, *, tm):
    tokens, d = x.shape
    return pl.pallas_call(
        _mix_ln_kernel,
        grid=(tokens // tm,),
        in_specs=[
            pl.BlockSpec((tm, d), lambda i: (i, 0)),
            pl.BlockSpec((tm, d), lambda i: (i, 0)),
            pl.BlockSpec((d, d), lambda i: (0, 0)),
            pl.BlockSpec((1, d), lambda i: (0, 0)),
            pl.BlockSpec((1, d), lambda i: (0, 0)),
        ],
        out_specs=pl.BlockSpec((tm, d), lambda i: (i, 0)),
        out_shape=jax.ShapeDtypeStruct((tokens, d), F32),
        compiler_params=_params(1),
        name="mix_out_ln1",
    )(merged, x, w_mix, ln_g, ln_b)


def _xa_kernel(h_ref, wq_ref, k_ref, v_ref, wo_ref, g_ref, b_ref, o_ref):
    h = h_ref[...]
    q = jnp.dot(h.astype(BF16), wq_ref[...], preferred_element_type=F32) * (XA_HEAD_DIM ** -0.5)
    q = q.astype(BF16)
    outs = []
    for hd in range(XA_HEADS):
        cols = slice(hd * XA_HEAD_DIM, (hd + 1) * XA_HEAD_DIM)
        s = lax.dot_general(q[:, cols], k_ref[:, cols], (((1,), (1,)), ((), ())), preferred_element_type=F32)
        p = jnp.exp(s - jnp.max(s, axis=-1, keepdims=True))
        l = jnp.sum(p, axis=-1, keepdims=True)
        o = jnp.dot(p.astype(BF16), v_ref[:, cols], preferred_element_type=F32)
        outs.append(o * (1.0 / l))
    o = jnp.concatenate(outs, axis=1).astype(BF16)
    xa = jnp.dot(o, wo_ref[...], preferred_element_type=F32)
    o_ref[...] = _layer_norm(DN_ALPHA * h + xa, g_ref, b_ref)


def _cross_attention_ln(h, kv_mem, w_q, w_o, ln_g, ln_b, batch, seq, *, tm):
    tokens, d = h.shape
    mem_len = kv_mem.shape[0] // batch
    xa_dim = XA_HEADS * XA_HEAD_DIM
    nt = seq // tm
    return pl.pallas_call(
        _xa_kernel,
        grid=(batch, nt),
        in_specs=[
            pl.BlockSpec((tm, d), lambda b, i: (b * nt + i, 0)),
            pl.BlockSpec((d, xa_dim), lambda b, i: (0, 0)),
            pl.BlockSpec((mem_len, xa_dim), lambda b, i: (b, 0)),
            pl.BlockSpec((mem_len, xa_dim), lambda b, i: (b, 1)),
            pl.BlockSpec((xa_dim, d), lambda b, i: (0, 0)),
            pl.BlockSpec((1, d), lambda b, i: (0, 0)),
            pl.BlockSpec((1, d), lambda b, i: (0, 0)),
        ],
        out_specs=pl.BlockSpec((tm, d), lambda b, i: (b * nt + i, 0)),
        out_shape=jax.ShapeDtypeStruct((tokens, d), F32),
        compiler_params=_params(2),
        name="cross_attention_ln2",
    )(h, w_q, kv_mem, kv_mem, w_o, ln_g, ln_b)


def _mlp_kernel(h_ref, w1_ref, w2_ref, g_ref, b_ref, o_ref, hb_ref):
    j = pl.program_id(1)

    @pl.when(j == 0)
    def _():
        hb_ref[...] = h_ref[...].astype(BF16)
        o_ref[...] = jnp.zeros_like(o_ref)

    rows = hb_ref.shape[0] // ROW_SPLITS
    subs = [slice(r * rows, (r + 1) * rows) for r in range(ROW_SPLITS)]
    ups = [jnp.dot(hb_ref[sl, :], w1_ref[...], preferred_element_type=F32) for sl in subs]
    for sl, u in zip(subs, ups):
        u = jnp.square(jnp.maximum(u, 0.0)).astype(BF16)
        o_ref[sl, :] += jnp.dot(u, w2_ref[...], preferred_element_type=F32)

    @pl.when(j == pl.num_programs(1) - 1)
    def _():
        o_ref[...] = _layer_norm(DN_ALPHA * h_ref[...] + o_ref[...], g_ref, b_ref)


def _mlp_ln(h, w1, w2, ln_g, ln_b, *, tm, tf):
    tokens, d = h.shape
    d_ff = w1.shape[1]
    return pl.pallas_call(
        _mlp_kernel,
        grid=(tokens // tm, d_ff // tf),
        in_specs=[
            pl.BlockSpec((tm, d), lambda i, j: (i, 0)),
            pl.BlockSpec((d, tf), lambda i, j: (0, j)),
            pl.BlockSpec((tf, d), lambda i, j: (j, 0)),
            pl.BlockSpec((1, d), lambda i, j: (0, 0)),
            pl.BlockSpec((1, d), lambda i, j: (0, 0)),
        ],
        out_specs=pl.BlockSpec((tm, d), lambda i, j: (i, 0)),
        out_shape=jax.ShapeDtypeStruct((tokens, d), F32),
        scratch_shapes=[pltpu.VMEM((tm, d), BF16)],
        compiler_params=_params(2),
        name="mlp_ln3",
    )(h, w1, w2, ln_g, ln_b)


def _overlap_t(seq):
    ncp, nsel = seq // CMP_STRIDE, seq // SEL_BLOCK
    c_start = np.arange(ncp)[None, :] * CMP_STRIDE
    s_start = np.arange(nsel)[:, None] * SEL_BLOCK
    ov = (c_start < s_start + SEL_BLOCK) & (c_start + CMP_BLOCK > s_start) & (np.arange(ncp)[None, :] < ncp - 1)
    return jnp.asarray(ov, BF16)


def kernel(x, mem, w_in, attn_sinks, rel_bias_table, cmp_pos_k, cmp_w1_k, cmp_w2_k, cmp_pos_v, cmp_w1_v, cmp_w2_v,
           w_branch_swa, w_branch_nsa, w_mix_out, ln1_g, ln1_b, xa_w_q, xa_w_kv, xa_w_o, ln2_g, ln2_b,
           mlp_w1, mlp_w2, ln3_g, ln3_b):
    batch, seq, d = x.shape
    tokens = batch * seq
    nq = seq // BLOCK_Q
    assert w_in.shape[0] == 1, "one layer"
    assert seq % (8 * BLOCK_Q) == 0 and d % 512 == 0

    w = w_in[0]
    sizes = (SWA_HEADS * HEAD_DIM, 128, 128, NSA_HEADS * HEAD_DIM, 128, 128, 128, 128, 128, 128, 3 * NSA_HEADS, d, d)
    offs = np.concatenate([[0], np.cumsum(sizes)])
    (w_qa, w_ka, w_va, w_qb, w_kc, w_vc, w_ks, w_vs, w_kw, w_vw, w_gn, w_ga, w_gb) = [
        w[:, offs[n]:offs[n + 1]] for n in range(len(sizes))]
    w_rows = jnp.concatenate([w_ka, w_kc, w_ks, w_kw, w_vc], axis=1).astype(BF16)
    w_gn_pad = jnp.pad(w_gn, ((0, 0), (0, 128 - w_gn.shape[1])))
    w_cols_t = jnp.concatenate([w_qa, w_qb, w_va, w_vs, w_vw, w_gn_pad], axis=1).T.astype(BF16)

    x2 = x.reshape(tokens, d)
    x_bf = x2.astype(BF16)

    h_rows = _matmul(x_bf, w_rows, nt=False, tm=min(2048, tokens), tn=w_rows.shape[1])
    h_t = _matmul(w_cols_t, x_bf, nt=True, tm=1280, tn=min(1024, tokens))

    kk = h_rows[:, :512].reshape(tokens, 4, GROUPS, HEAD_DIM).transpose(1, 2, 0, 3)
    v_cmp = h_rows[:, 512:].reshape(tokens, GROUPS, HEAD_DIM).transpose(1, 0, 2)

    def value_tiles(row0, width):
        v = h_t[row0:row0 + 128].reshape(GROUPS, HEAD_DIM, batch, seq // width, width).transpose(0, 2, 3, 1, 4)
        ones = jnp.ones(v.shape[:3] + (V_AUG_ROWS - HEAD_DIM, width), BF16)
        return jnp.concatenate([v, ones], axis=3)

    vswa_aug = value_tiles(2048, BLOCK_Q)
    vsel_aug, vwin_aug = value_tiles(2176, 2 * BLOCK_Q), value_tiles(2304, 2 * BLOCK_Q)
    gates_t = h_t[2432:2432 + 3 * NSA_HEADS].reshape(GROUPS, HEADS_PER_GROUP, 3, tokens).transpose(0, 2, 1, 3)
    block_of_key = (jnp.arange(tokens) % BLOCK_Q) // SEL_BLOCK
    onehot = (block_of_key[:, None] == jnp.arange(SEL_ROWS)[None, :]).astype(BF16)
    ksel_aug = jnp.concatenate([kk[2], jnp.broadcast_to(onehot, (GROUPS, tokens, SEL_ROWS))], axis=2)

    by_dist = rel_bias_table[_rel_bucket(jnp.arange(seq))].astype(F32).T
    vb_swa = _bias_by_distance(by_dist[:SWA_HEADS], 4, window=SWA_WINDOW)
    vb_sel = _bias_by_distance(by_dist[SWA_HEADS:], nq + 2)
    vb_win = _bias_by_distance(by_dist[SWA_HEADS:], N_WIN_SLOTS + 1, window=NSA_WINDOW)
    vb_cmp = vb_sel
    sink_rows = jnp.repeat(attn_sinks[0].astype(F32), BLOCK_Q).reshape(GROUPS, 1, GROUP_LANES)

    y_a = _swa_attention(h_t, kk, vswa_aug, sink_rows, vb_swa, batch, seq)

    ncp = seq // CMP_STRIDE
    chunk_w = CMP_STRIDE * HEAD_DIM
    chunks = jnp.stack([kk[1], v_cmp]).reshape(2, GROUPS, batch, ncp, chunk_w)
    pos = jnp.stack([cmp_pos_k[0], cmp_pos_v[0]]).astype(F32).reshape(2, 2, 1, chunk_w)
    w1 = jnp.stack([cmp_w1_k[0], cmp_w1_v[0]]).astype(BF16).reshape(2, 2, chunk_w, -1)
    w2 = jnp.stack([cmp_w2_k[0], cmp_w2_v[0]]).astype(BF16)
    cn, ct = _compress(chunks, pos, w1, w2, w2.transpose(0, 2, 1))
    oc_t, sel = _cmp_attention(h_t, cn, ct, vb_cmp, _overlap_t(seq), batch, seq)
    y_b = _selwin_attention(h_t, ksel_aug, kk, vsel_aug, vwin_aug, sel, oc_t, gates_t, vb_sel, vb_win, batch, seq)

    merged = _merge(x_bf, y_a, y_b, w_ga.astype(BF16), w_gb.astype(BF16),
                    w_branch_swa[0].astype(BF16), w_branch_nsa[0].astype(BF16), tm=min(1024, tokens), tn=512)
    h1 = _mix_ln(merged, x2, w_mix_out[0].astype(BF16), ln1_g, ln1_b, tm=512)

    mem_bf = mem.reshape(-1, d).astype(BF16)
    kv_mem = _matmul(mem_bf, xa_w_kv[0].astype(BF16), nt=False, tm=mem_bf.shape[0], tn=512)
    h2 = _cross_attention_ln(h1, kv_mem, xa_w_q[0].astype(BF16), xa_w_o[0].astype(BF16), ln2_g, ln2_b,
                             batch, seq, tm=512)

    h3 = _mlp_ln(h2, mlp_w1[0].astype(BF16), mlp_w2[0].astype(BF16), ln3_g, ln3_b, tm=1024, tf=512)
    return h3.reshape(batch, seq, d)
```

```python
import functools
import math

import numpy as np
import jax
import jax.numpy as jnp
from jax import lax
from jax.experimental import pallas as pl
from jax.experimental.pallas import tpu as pltpu

F32 = jnp.float32
BF16 = jnp.bfloat16

HEAD_DIM = 64
BLOCK_Q = 128
SWA_HEADS = 16
SWA_KV_HEADS = 2
SWA_WINDOW = 128
NSA_HEADS = 16
NSA_KV_HEADS = 2
CMP_BLOCK = 32
CMP_STRIDE = 16
SEL_BLOCK = 64
SEL_TOPK = 16
SEL_LOCAL = 2
NSA_WINDOW = 512
REL_BUCKETS = 32
REL_MAX_DIST = 4096
XA_HEADS = 4
XA_HEAD_DIM = 128
DEPTH = 1
DN_ALPHA = (2.0 * DEPTH) ** 0.25
LN_EPS = 1e-5
NEG_INF = -1e30
FORCE_SCORE = 1e4

GROUPS = 2
HEADS_PER_GROUP = 8
GROUP_LANES = HEADS_PER_GROUP * BLOCK_Q
GROUP_COLS = HEADS_PER_GROUP * HEAD_DIM
SUBTILES = 2

V7X_VMEM_LIMIT_BYTES = 56 * 1024 * 1024


def _params(n_axes):
    return pltpu.CompilerParams(dimension_semantics=("arbitrary",) * n_axes,
                                vmem_limit_bytes=V7X_VMEM_LIMIT_BYTES)


def _mm_kernel(a_ref, b_ref, o_ref, *, nt):
    if nt:
        out = lax.dot_general(a_ref[...], b_ref[...], (((1,), (1,)), ((), ())), preferred_element_type=F32)
    else:
        out = jnp.dot(a_ref[...], b_ref[...], preferred_element_type=F32)
    o_ref[...] = out.astype(o_ref.dtype)


def _matmul(a, b, *, nt, tm, tn, out_dtype=BF16):
    m, k = a.shape
    n = b.shape[0] if nt else b.shape[1]
    b_spec = pl.BlockSpec((tn, k), lambda i, j: (j, 0)) if nt else pl.BlockSpec((k, tn), lambda i, j: (0, j))
    return pl.pallas_call(
        functools.partial(_mm_kernel, nt=nt),
        grid=(m // tm, n // tn),
        in_specs=[pl.BlockSpec((tm, k), lambda i, j: (i, 0)), b_spec],
        out_specs=pl.BlockSpec((tm, tn), lambda i, j: (i, j)),
        out_shape=jax.ShapeDtypeStruct((m, n), out_dtype),
        compiler_params=_params(2),
        name="proj_nt" if nt else "proj_nn",
    )(a, b)


def _rel_bucket(dist):
    exact = REL_BUCKETS // 2
    d = jnp.maximum(dist, 0)
    log_ratio = jnp.log(jnp.maximum(d, 1).astype(F32) / exact) / math.log(REL_MAX_DIST / exact)
    large = jnp.minimum(exact + (log_ratio * (REL_BUCKETS - exact)).astype(jnp.int32), REL_BUCKETS - 1)
    return jnp.where(d < exact, d, large)


def _bias_by_distance(by_dist, rows, *, window=None):
    heads, seq = by_dist.shape
    if window is not None:
        by_dist = jnp.where(jnp.arange(seq) < window, by_dist, NEG_INF)
    padded = jnp.pad(by_dist, ((0, 0), (2 * BLOCK_Q, 0)), constant_values=NEG_INF)[:, :rows * BLOCK_Q]
    return padded.reshape(heads // HEADS_PER_GROUP, HEADS_PER_GROUP, rows, BLOCK_Q)


def _build_skew_table(vb_ref, tbl_ref, n_tiles):
    def body(d, carry):
        for h in range(HEADS_PER_GROUP):
            lo = jnp.broadcast_to(vb_ref[h, pl.ds(d, 1), :], (BLOCK_Q, BLOCK_Q))
            hi = jnp.broadcast_to(vb_ref[h, pl.ds(d + 1, 1), :], (BLOCK_Q, BLOCK_Q))
            y = pltpu.roll(jnp.concatenate([lo, hi], axis=1), 0, 1, stride=1, stride_axis=0)
            tbl_ref[d, :, h * BLOCK_Q:(h + 1) * BLOCK_Q] = y[:, BLOCK_Q:].astype(tbl_ref.dtype)
        return carry

    lax.fori_loop(0, n_tiles, body, 0)


def _build_cmp_bias_table(vb_ref, tbl_ref, skew_ref, nq):
    per_tile = BLOCK_Q // CMP_STRIDE
    assert per_tile == 8 and CMP_BLOCK - 1 + CMP_STRIDE * (per_tile - 1) - BLOCK_Q == CMP_STRIDE - 1
    tbl_ref[pl.ds(nq * 8, nq * 8), :] = jnp.full((nq * 8, GROUP_LANES), NEG_INF, F32)

    def body(d, last_row):
        for h in range(HEADS_PER_GROUP):
            lo = jnp.broadcast_to(vb_ref[h, pl.ds(d + 1, 1), :], (BLOCK_Q, BLOCK_Q))
            hi = jnp.broadcast_to(vb_ref[h, pl.ds(d + 2, 1), :], (BLOCK_Q, BLOCK_Q))
            y = pltpu.roll(jnp.concatenate([lo, hi], axis=1), 0, 1, stride=1, stride_axis=0)
            skew_ref[:, h * BLOCK_Q:(h + 1) * BLOCK_Q] = y[:, BLOCK_Q:]
        offsets = [CMP_BLOCK - 1 + CMP_STRIDE * c for c in range(per_tile - 1)]
        rows = [skew_ref[k:k + 1, :] for k in offsets] + [last_row]
        tbl_ref[pl.ds(pl.multiple_of((nq - 1 - d) * 8, 8), 8), :] = jnp.concatenate(rows, axis=0)
        return skew_ref[CMP_STRIDE - 1:CMP_STRIDE, :]

    lax.fori_loop(0, nq, body, jnp.full((1, GROUP_LANES), NEG_INF, F32))


def _gather_heads_to_lanes(q_ref):
    q = jnp.concatenate([q_ref[h * HEAD_DIM:(h + 1) * HEAD_DIM, u * BLOCK_Q:(u + 1) * BLOCK_Q]
                         for u in range(q_ref.shape[1] // BLOCK_Q) for h in range(HEADS_PER_GROUP)], axis=1)
    return q * jnp.asarray(HEAD_DIM ** -0.5, q.dtype)


def _heads_to_columns(o_t):
    stacked = jnp.concatenate([o_t[:, h * BLOCK_Q:(h + 1) * BLOCK_Q] for h in range(HEADS_PER_GROUP)], axis=0)
    return stacked.T


def _identity_tile():
    r = lax.broadcasted_iota(jnp.int32, (BLOCK_Q, BLOCK_Q), 0)
    c = lax.broadcasted_iota(jnp.int32, (BLOCK_Q, BLOCK_Q), 1)
    return jnp.where(r == c, 1.0, 0.0).astype(BF16)


def _biased_scores(eye, k_t, bias_tile, q_t):
    lhs = jnp.concatenate([eye, k_t], axis=1)
    rhs = jnp.concatenate([bias_tile, q_t], axis=0)
    return jnp.dot(lhs, rhs, preferred_element_type=F32)


def _softmax_probs(m, s_tiles):
    ms, alphas, ps = [], [], [[] for _ in s_tiles]
    for h in range(m.shape[1] // BLOCK_Q):
        lanes = slice(h * BLOCK_Q, (h + 1) * BLOCK_Q)
        cols = [s[:, lanes] for s in s_tiles]
        m_new = m[:, lanes]
        for c in cols:
            m_new = jnp.maximum(m_new, jnp.max(c, axis=0, keepdims=True))
        ms.append(m_new)
        alphas.append(jnp.exp(m[:, lanes] - m_new))
        for j, c in enumerate(cols):
            ps[j].append(jnp.exp((c - m_new).astype(BF16)))
    p = jnp.concatenate([jnp.concatenate(pj, axis=1) for pj in ps], axis=0)
    return jnp.concatenate(ms, axis=1), jnp.concatenate(alphas, axis=1), p


def _softmax_update(carry, s_tiles, v_aug):
    m, acc = carry
    m, alpha, p = _softmax_probs(m, s_tiles)
    return m, alpha * acc + jnp.dot(v_aug, p, preferred_element_type=F32)


def _pipelined_attention(n_steps, scores_fn, values_fn, init, s_scr, p_scr, acc_scr, *, fill):
    m0, acc0 = init

    def put_scores(n):
        for j, s in enumerate(scores_fn(n)):
            s_scr[j] = s

    def fold(alpha, n, p):
        acc_scr[...] = alpha * acc_scr[...] + jnp.dot(values_fn(n), p, preferred_element_type=F32)

    if fill:
        put_scores(0)
        p_scr[...] = jnp.zeros_like(p_scr)
        acc_scr[...] = acc0
        return None

    def body(k, carry):
        m, alpha_a, alpha_b = carry
        a = 2 * k
        s_b = scores_fn(a + 1)
        fold(alpha_a, a - 2, p_scr[0])
        fold(alpha_b, a - 1, p_scr[1])
        m, alpha_a, p_a = _softmax_probs(m, [s_scr.at[j] for j in range(s_scr.shape[0])])
        put_scores(a + 2)
        m, alpha_b, p_b = _softmax_probs(m, s_b)
        p_scr[0] = p_a
        p_scr[1] = p_b
        return m, alpha_a, alpha_b

    trips = (n_steps + 1) // 2
    ones = jnp.ones_like(m0)
    m, alpha_a, alpha_b = lax.fori_loop(0, trips, body, (m0, ones, ones))
    fold(alpha_a, 2 * trips - 2, p_scr[0])
    fold(alpha_b, 2 * trips - 1, p_scr[1])
    return m, acc_scr[...]


SLAB = 2 * BLOCK_Q
LOOKAHEAD = 12


def _slabwise_softmax_step(m, acc_ref, lhs_tiles, rhs_fn, v_fn, masked_fn=None):
    per_slab = [[j for j in range(len(lhs_tiles)) if masked_fn is None or not masked_fn(j, sl)]
                for sl in range(m.shape[1] // SLAB)]
    items = [(sl, j) for sl, tiles in enumerate(per_slab) for j in tiles]

    def scores(k):
        sl, j = items[k]
        lhs = lhs_tiles[j](sl) if callable(lhs_tiles[j]) else lhs_tiles[j]
        return jnp.dot(lhs, rhs_fn(j, sl), preferred_element_type=F32)

    pending = [scores(k) for k in range(min(LOOKAHEAD, len(items)))]
    issued = len(pending)
    new_m = []
    for sl, tiles in enumerate(per_slab):
        lanes = slice(sl * SLAB, (sl + 1) * SLAB)
        m_s, acc_s = m[:, lanes], acc_ref[:, lanes]
        for c in range(0, len(tiles), 2):
            chunk = tiles[c:c + 2]
            m_old, ps = m_s, []
            for j in chunk:
                s = pending.pop(0)
                if issued < len(items):
                    pending.append(scores(issued))
                    issued += 1
                m_next = jnp.maximum(m_s, jnp.max(s, axis=0, keepdims=True))
                ps = [q * jnp.exp(m_s - m_next).astype(BF16) for q in ps]
                ps.append(jnp.exp((s - m_next).astype(BF16)))
                m_s = m_next
            v = jnp.concatenate([v_fn(j, sl) for j in chunk], axis=1)
            acc_s = jnp.exp(m_old - m_s) * acc_s + jnp.dot(v, jnp.concatenate(ps, axis=0),
                                                           preferred_element_type=F32)
        acc_ref[:, lanes] = acc_s
        new_m.append(m_s)
    return jnp.concatenate(new_m, axis=1)


V_AUG_ROWS = HEAD_DIM + 16


def _normalized(acc):
    return acc[:HEAD_DIM] * (1.0 / acc[HEAD_DIM:HEAD_DIM + 1])


def _swa_kernel(q_ref, kprev_ref, kcur_ref, vprev_ref, vcur_ref, sink_ref, vb_ref, o_ref, tbl_ref, acc_ref):
    b, i = pl.program_id(1), pl.program_id(2)

    @pl.when((b == 0) & (i == 0))
    def _():
        _build_skew_table(vb_ref, tbl_ref, 3)

    eye = _identity_tile()
    q_t = _gather_heads_to_lanes(q_ref)
    lanes = q_t.shape[1]
    slabs_per_tile = GROUP_LANES // SLAB
    m0 = jnp.concatenate([sink_ref[...]] * SWA_TILES, axis=1)
    acc_ref[...] = jnp.where(lax.broadcasted_iota(jnp.int32, (V_AUG_ROWS, lanes), 0) < HEAD_DIM, 0.0, 1.0)

    def key_tile(j):
        def at(slab):
            u = slab // slabs_per_tile - j
            k_t = kprev_ref[...] if u < 0 else kcur_ref[u * BLOCK_Q:(u + 1) * BLOCK_Q, :]
            return jnp.concatenate([eye, k_t], axis=1)
        return at

    def rhs(j, slab):
        u, part = slab // slabs_per_tile, slab % slabs_per_tile
        slot = 1 + j if (u > 0 or j == 0) else jnp.where(i > 0, 2, 0)
        return jnp.concatenate([tbl_ref[slot, :, part * SLAB:(part + 1) * SLAB],
                                q_t[:, slab * SLAB:(slab + 1) * SLAB]], axis=0)

    def values(j, slab):
        u = slab // slabs_per_tile - j
        return vprev_ref[...] if u < 0 else vcur_ref[u]

    _slabwise_softmax_step(m0, acc_ref, [key_tile(0), key_tile(1)], rhs, values)
    out = _normalized(acc_ref[...])
    for u in range(SWA_TILES):
        o_ref[u * BLOCK_Q:(u + 1) * BLOCK_Q, :] = _heads_to_columns(
            out[:, u * GROUP_LANES:(u + 1) * GROUP_LANES]).astype(o_ref.dtype)


SWA_TILES = 4


def _swa_attention(h_t, kk, v_aug, sink_rows, vb, batch, seq):
    nq = seq // BLOCK_Q
    nblk = nq // SWA_TILES
    width = SWA_TILES * BLOCK_Q
    tokens = batch * seq

    def prev_tile(i):
        return jnp.maximum(i * SWA_TILES - 1, 0)

    return pl.pallas_call(
        _swa_kernel,
        grid=(GROUPS, batch, nblk),
        in_specs=[
            pl.BlockSpec((GROUP_COLS, width), lambda g, b, i: (g, b * nblk + i)),
            pl.BlockSpec((None, None, BLOCK_Q, HEAD_DIM), lambda g, b, i: (0, g, b * nq + prev_tile(i), 0)),
            pl.BlockSpec((None, None, width, HEAD_DIM), lambda g, b, i: (0, g, b * nblk + i, 0)),
            pl.BlockSpec((None, None, None, V_AUG_ROWS, BLOCK_Q), lambda g, b, i: (g, b, prev_tile(i), 0, 0)),
            pl.BlockSpec((None, None, SWA_TILES, V_AUG_ROWS, BLOCK_Q), lambda g, b, i: (g, b, i, 0, 0)),
            pl.BlockSpec((None, 1, GROUP_LANES), lambda g, b, i: (g, 0, 0)),
            pl.BlockSpec((None, HEADS_PER_GROUP, 4, BLOCK_Q), lambda g, b, i: (g, 0, 0, 0)),
        ],
        out_specs=pl.BlockSpec((width, GROUP_COLS), lambda g, b, i: (b * nblk + i, g)),
        out_shape=jax.ShapeDtypeStruct((tokens, SWA_HEADS * HEAD_DIM), BF16),
        scratch_shapes=[pltpu.VMEM((3, BLOCK_Q, GROUP_LANES), BF16),
                        pltpu.VMEM((V_AUG_ROWS, SWA_TILES * GROUP_LANES), F32)],
        compiler_params=_params(3),
        name="swa_attention",
    )(h_t, kk, kk, v_aug, v_aug, sink_rows, vb)


def _compress_kernel(c_ref, pos_ref, w1_ref, w2_ref, w2t_ref, cn_ref, ct_ref):
    c = c_ref[...].astype(F32)
    top = (c + pos_ref[0]).astype(BF16)
    bot = (c + pos_ref[1]).astype(BF16)
    a = jnp.dot(top, w1_ref[0], preferred_element_type=F32)
    bm = jnp.dot(bot, w1_ref[1], preferred_element_type=F32)
    n = a.shape[0]
    pre = a + pltpu.roll(bm, n - 1, 0)
    hid = jax.nn.gelu(pre).astype(BF16)
    cn_ref[...] = jnp.dot(hid, w2_ref[...], preferred_element_type=F32).astype(cn_ref.dtype)
    ct_ref[...] = lax.dot_general(w2t_ref[...], hid, (((1,), (1,)), ((), ())),
                                  preferred_element_type=F32).astype(ct_ref.dtype)


def _compress(chunks, pos, w1, w2, w2t):
    _, g, b, ncp, width = chunks.shape
    hidden = w1.shape[-1]
    return pl.pallas_call(
        _compress_kernel,
        grid=(2, g, b),
        in_specs=[
            pl.BlockSpec((None, None, None, ncp, width), lambda t, g, b: (t, g, b, 0, 0)),
            pl.BlockSpec((None, 2, 1, width), lambda t, g, b: (t, 0, 0, 0)),
            pl.BlockSpec((None, 2, width, hidden), lambda t, g, b: (t, 0, 0, 0)),
            pl.BlockSpec((None, hidden, HEAD_DIM), lambda t, g, b: (t, 0, 0)),
            pl.BlockSpec((None, HEAD_DIM, hidden), lambda t, g, b: (t, 0, 0)),
        ],
        out_specs=[
            pl.BlockSpec((None, None, None, ncp, HEAD_DIM), lambda t, g, b: (t, g, b, 0, 0)),
            pl.BlockSpec((None, None, None, HEAD_DIM, ncp), lambda t, g, b: (t, g, b, 0, 0)),
        ],
        out_shape=[jax.ShapeDtypeStruct((2, g, b, ncp, HEAD_DIM), BF16),
                   jax.ShapeDtypeStruct((2, g, b, HEAD_DIM, ncp), BF16)],
        compiler_params=_params(3),
        name="nsa_compress",
    )(chunks, pos, w1, w2, w2t)


def _cmp_kernel(q_ref, kc_ref, vct_ref, vb_ref, ov_ref, oc_ref, sel_ref, bias_ref, skew_ref, bias16_ref, *, nq):
    b, i = pl.program_id(1), pl.program_id(2)

    @pl.when((b == 0) & (i == 0))
    def _():
        _build_cmp_bias_table(vb_ref, bias_ref, skew_ref, nq)
        rows = bias_ref.shape[0]
        bias16_ref[0] = bias_ref[...].astype(BF16)
        bias16_ref[1, 0:rows - 16, :] = bias_ref[8:rows - 8, :].astype(BF16)

    ncp = kc_ref.shape[0]
    nsel = sel_ref.shape[0]
    n_tiles = ncp // BLOCK_Q
    q_t = _gather_heads_to_lanes(q_ref)
    width = SUBTILES * BLOCK_Q
    eye = _identity_tile()

    slabs_per_tile = GROUP_LANES // SLAB
    lhs = [jnp.concatenate([eye, kc_ref[t * BLOCK_Q:(t + 1) * BLOCK_Q, :]], axis=1) for t in range(n_tiles)]
    items = [(sl, t) for sl in range(SUBTILES * slabs_per_tile) for t in range(n_tiles)]

    def scores(k):
        sl, t = items[k]
        u, part = sl // slabs_per_tile, sl % slabs_per_tile
        shifted = (nq - 1 - u) % 2
        row0 = pl.multiple_of((nq - 1 - (SUBTILES * i + u)) * 8 + t * BLOCK_Q - 8 * shifted, 16)
        bias = bias16_ref[shifted, pl.ds(row0, BLOCK_Q), part * SLAB:(part + 1) * SLAB]
        return jnp.dot(lhs[t], jnp.concatenate([bias, q_t[:, sl * SLAB:(sl + 1) * SLAB]], axis=0),
                       preferred_element_type=F32)

    q_pos = i * width + lax.broadcasted_iota(jnp.int32, (1, width), 1)
    sees_any = q_pos >= CMP_BLOCK - 1
    psum = [[jnp.zeros((BLOCK_Q, BLOCK_Q), F32) for _ in range(n_tiles)] for _ in range(SUBTILES)]
    pending = [scores(k) for k in range(min(LOOKAHEAD, len(items)))]
    for sl in range(SUBTILES * slabs_per_tile):
        u, part = sl // slabs_per_tile, sl % slabs_per_tile
        s = [pending.pop(0) for _ in range(n_tiles)]
        for k in range(sl * n_tiles + LOOKAHEAD, min((sl + 1) * n_tiles + LOOKAHEAD, len(items))):
            pending.append(scores(k))
        m = functools.reduce(jnp.maximum, [jnp.max(c, axis=0, keepdims=True) for c in s])
        e = [jnp.exp(c - m) for c in s]
        l = functools.reduce(jnp.add, [jnp.sum(c, axis=0, keepdims=True) for c in e])
        seen = sees_any[:, u * BLOCK_Q:(u + 1) * BLOCK_Q]
        inv = jnp.where(jnp.concatenate([seen] * (SLAB // BLOCK_Q), axis=1), 1.0 / l, 0.0)
        o_slab = jnp.zeros((HEAD_DIM, SLAB), F32)
        for t in range(n_tiles):
            p = e[t] * inv
            psum[u][t] = psum[u][t] + functools.reduce(
                jnp.add, [p[:, r * BLOCK_Q:(r + 1) * BLOCK_Q] for r in range(SLAB // BLOCK_Q)])
            o_slab = o_slab + jnp.dot(vct_ref[:, t * BLOCK_Q:(t + 1) * BLOCK_Q], p.astype(BF16),
                                      preferred_element_type=F32)
        oc_ref[u, :, part * SLAB:(part + 1) * SLAB] = o_slab.astype(oc_ref.dtype)

    psum = jnp.concatenate([jnp.concatenate(pu, axis=0) for pu in psum], axis=1)
    hi = psum.astype(BF16)
    lo = (psum - hi.astype(F32)).astype(BF16)
    ov = ov_ref[...]
    score = jnp.dot(ov, hi, preferred_element_type=F32) + jnp.dot(ov, lo, preferred_element_type=F32)

    j_io = lax.broadcasted_iota(jnp.int32, (nsel, width), 0)
    qpos = i * width + lax.broadcasted_iota(jnp.int32, (nsel, width), 1)
    causal = j_io * SEL_BLOCK <= qpos
    back = qpos // SEL_BLOCK - j_io
    forced = (j_io == 0) | ((back >= 0) & (back < SEL_LOCAL))
    score = jnp.where(causal, jnp.where(forced, FORCE_SCORE, score), -1.0)
    slab_rows = lax.broadcasted_iota(jnp.int32, (8, width), 0)
    slabs = [score[8 * g:8 * (g + 1), :] for g in range(nsel // 8)]
    ranks = [jnp.zeros((8, width), F32) for _ in slabs]
    for r in range(nsel):
        row = jnp.broadcast_to(score[r:r + 1, :], (8, width))
        for g, slab in enumerate(slabs):
            if g > r // 8:
                ahead = row >= slab
            elif g < r // 8:
                ahead = row > slab
            else:
                ranks[g] = ranks[g] + jnp.where(slab_rows > r % 8, jnp.where(row >= slab, 1.0, 0.0),
                                                jnp.where(row > slab, 1.0, 0.0))
                continue
            ranks[g] = ranks[g] + jnp.where(ahead, 1.0, 0.0)
    rank = jnp.concatenate(ranks, axis=0)
    sel_ref[...] = jnp.where((rank < min(SEL_TOPK, nsel)) & causal, 1.0, 0.0).astype(sel_ref.dtype)


def _cmp_attention(h_t, cn, ct, vb, overlap_t, batch, seq):
    nq = seq // BLOCK_Q
    ncp = seq // CMP_STRIDE
    nsel = seq // SEL_BLOCK
    q_blk0 = SWA_HEADS * HEAD_DIM // GROUP_COLS
    nblk = nq // SUBTILES
    width = SUBTILES * BLOCK_Q
    return pl.pallas_call(
        functools.partial(_cmp_kernel, nq=nq),
        grid=(GROUPS, batch, nblk),
        in_specs=[
            pl.BlockSpec((GROUP_COLS, width), lambda g, b, i: (q_blk0 + g, b * nblk + i)),
            pl.BlockSpec((None, None, None, ncp, HEAD_DIM), lambda g, b, i: (0, g, b, 0, 0)),
            pl.BlockSpec((None, None, None, HEAD_DIM, ncp), lambda g, b, i: (1, g, b, 0, 0)),
            pl.BlockSpec((None, HEADS_PER_GROUP, nq + 2, BLOCK_Q), lambda g, b, i: (g, 0, 0, 0)),
            pl.BlockSpec((nsel, ncp), lambda g, b, i: (0, 0)),
        ],
        out_specs=[
            pl.BlockSpec((None, None, SUBTILES, HEAD_DIM, GROUP_LANES), lambda g, b, i: (b, g, i, 0, 0)),
            pl.BlockSpec((None, None, nsel, width), lambda g, b, i: (b, g, 0, i)),
        ],
        out_shape=[jax.ShapeDtypeStruct((batch, GROUPS, nq, HEAD_DIM, GROUP_LANES), BF16),
                   jax.ShapeDtypeStruct((batch, GROUPS, nsel, seq), F32)],
        scratch_shapes=[pltpu.VMEM((2 * nq * 8, GROUP_LANES), F32), pltpu.VMEM((BLOCK_Q, GROUP_LANES), F32),
                        pltpu.VMEM((2, 2 * nq * 8, GROUP_LANES), BF16)],
        compiler_params=_params(3),
        name="nsa_cmp_select",
    )(h_t, cn, ct, vb, overlap_t)


N_WIN_PAIRS = -(-(NSA_WINDOW - 1) // (SUBTILES * BLOCK_Q)) + 1
N_WIN_SLOTS = SUBTILES * N_WIN_PAIRS + 1
SEL_ROWS = 16
SEL_STEPS = 2


def _selwin_kernel(q_ref, ksel_ref, kwin_ref, vsel_ref, vwin_ref, sel_ref, oc_ref, gate_ref, vbs_ref, vbw_ref,
                   o_ref, tsel_ref, twin_ref, accs_ref, accw_ref, *, nq):
    b, blk = pl.program_id(1), pl.program_id(2)

    @pl.when((b == 0) & (blk == 0))
    def _():
        _build_skew_table(vbs_ref, tsel_ref, nq + 1)
        _build_skew_table(vbw_ref, twin_ref, N_WIN_SLOTS)

    q_t = _gather_heads_to_lanes(q_ref)
    lanes = q_t.shape[1]
    eye = _identity_tile()
    pair = SUBTILES * BLOCK_Q
    m0 = jnp.full((1, lanes), NEG_INF, F32)
    accs_ref[...] = jnp.zeros_like(accs_ref)
    accw_ref[...] = jnp.zeros_like(accw_ref)
    blocks_per_tile = BLOCK_Q // SEL_BLOCK
    slabs_per_tile = GROUP_LANES // SLAB

    def attend(m, steps, k_ref, tbl_ref, v_ref, acc_ref, q_ext_fn, reach=None):
        tiles = [(n, jnp.clip(blk - n, 0, blk), j) for n in steps for j in range(SUBTILES)]
        lhs = [jnp.concatenate([eye, k_ref[pl.ds(pl.multiple_of(p * pair + j * BLOCK_Q, BLOCK_Q), BLOCK_Q), :]],
                               axis=1) for _, p, j in tiles]
        q_exts = [q_ext_fn(p, j) for _, p, j in tiles]

        def rhs(t, slab):
            n, _, j = tiles[t]
            u, part = slab // slabs_per_tile, slab % slabs_per_tile
            slot = jnp.where(n <= blk, 2 * n + u - j + 1, 0)
            bias = tbl_ref[slot, :, part * SLAB:(part + 1) * SLAB]
            return jnp.concatenate([bias, q_exts[t][:, slab * SLAB:(slab + 1) * SLAB]], axis=0)

        def values(t, slab):
            _, p, j = tiles[t]
            return v_ref[p, :, j * BLOCK_Q:(j + 1) * BLOCK_Q]

        def masked(t, slab):
            n, _, j = tiles[t]
            distance = 2 * n + slab // slabs_per_tile - j
            return distance < 0 or distance >= reach

        return _slabwise_softmax_step(m, acc_ref, lhs, rhs, values, masked if reach is not None else None)

    attend(m0, list(range(N_WIN_PAIRS)), kwin_ref, twin_ref, vwin_ref, accw_ref, lambda p, j: q_t,
           reach=-(-(NSA_WINDOW + BLOCK_Q - 1) // BLOCK_Q))

    def sel_q_ext(p, j):
        per_pair = SUBTILES * blocks_per_tile
        group = sel_ref[pl.ds(pl.multiple_of((p // 2) * 2 * per_pair, 8), 2 * per_pair), :]
        lo, hi = (group[half * per_pair + j * blocks_per_tile:half * per_pair + (j + 1) * blocks_per_tile]
                  for half in range(2))
        neg = jnp.where(jnp.where(p % 2 == 1, hi, lo) > 0.5, 0.0, NEG_INF)
        rows = jnp.concatenate([neg[:, u * BLOCK_Q:(u + 1) * BLOCK_Q]
                                for u in range(SUBTILES) for _ in range(HEADS_PER_GROUP)], axis=1)
        rows = jnp.concatenate([rows, jnp.zeros((SEL_ROWS - blocks_per_tile, lanes), F32)], axis=0)
        return jnp.concatenate([q_t, rows.astype(BF16)], axis=0)

    def sel_body(k, m):
        return attend(m, [SEL_STEPS * k + r for r in range(SEL_STEPS)], ksel_ref, tsel_ref, vsel_ref, accs_ref,
                      sel_q_ext)

    lax.fori_loop(0, (blk + SEL_STEPS) // SEL_STEPS, sel_body, m0)

    def gate_row(branch):
        g = jnp.concatenate([gate_ref[branch, h:h + 1, u * BLOCK_Q:(u + 1) * BLOCK_Q]
                             for u in range(SUBTILES) for h in range(HEADS_PER_GROUP)], axis=1)
        return jax.nn.sigmoid(g.astype(F32))

    o_c = jnp.concatenate([oc_ref[u] for u in range(SUBTILES)], axis=1).astype(F32)
    out = (gate_row(0) * o_c + gate_row(1) * _normalized(accs_ref[...])
           + gate_row(2) * _normalized(accw_ref[...]))
    for u in range(SUBTILES):
        o_ref[u * BLOCK_Q:(u + 1) * BLOCK_Q, :] = _heads_to_columns(
            out[:, u * GROUP_LANES:(u + 1) * GROUP_LANES]).astype(o_ref.dtype)


def _selwin_attention(h_t, ksel_aug, kk, vsel_aug, vwin_aug, sel, oc_t, gates_t, vb_sel, vb_win, batch, seq):
    nq = seq // BLOCK_Q
    nblk = nq // SUBTILES
    tokens = batch * seq
    q_blk0 = SWA_HEADS * HEAD_DIM // GROUP_COLS
    pair = SUBTILES * BLOCK_Q
    lanes = SUBTILES * GROUP_LANES
    return pl.pallas_call(
        functools.partial(_selwin_kernel, nq=nq),
        grid=(GROUPS, batch, nblk),
        in_specs=[
            pl.BlockSpec((GROUP_COLS, pair), lambda g, b, i: (q_blk0 + g, b * nblk + i)),
            pl.BlockSpec((None, seq, HEAD_DIM + SEL_ROWS), lambda g, b, i: (g, b, 0)),
            pl.BlockSpec((None, None, seq, HEAD_DIM), lambda g, b, i: (3, g, b, 0)),
            pl.BlockSpec((None, None, nblk, V_AUG_ROWS, pair), lambda g, b, i: (g, b, 0, 0, 0)),
            pl.BlockSpec((None, None, nblk, V_AUG_ROWS, pair), lambda g, b, i: (g, b, 0, 0, 0)),
            pl.BlockSpec((None, None, seq // SEL_BLOCK, pair), lambda g, b, i: (b, g, 0, i)),
            pl.BlockSpec((None, None, SUBTILES, HEAD_DIM, GROUP_LANES), lambda g, b, i: (b, g, i, 0, 0)),
            pl.BlockSpec((None, 3, HEADS_PER_GROUP, pair), lambda g, b, i: (g, 0, 0, b * nblk + i)),
            pl.BlockSpec((None, HEADS_PER_GROUP, nq + 2, BLOCK_Q), lambda g, b, i: (g, 0, 0, 0)),
            pl.BlockSpec((None, HEADS_PER_GROUP, N_WIN_SLOTS + 1, BLOCK_Q), lambda g, b, i: (g, 0, 0, 0)),
        ],
        out_specs=pl.BlockSpec((pair, GROUP_COLS), lambda g, b, i: (b * nblk + i, g)),
        out_shape=jax.ShapeDtypeStruct((tokens, NSA_HEADS * HEAD_DIM), BF16),
        scratch_shapes=[pltpu.VMEM((nq + 1, BLOCK_Q, GROUP_LANES), BF16),
                        pltpu.VMEM((N_WIN_SLOTS, BLOCK_Q, GROUP_LANES), BF16),
                        pltpu.VMEM((V_AUG_ROWS, lanes), F32),
                        pltpu.VMEM((V_AUG_ROWS, lanes), F32)],
        compiler_params=_params(3),
        name="nsa_sel_win",
    )(h_t, ksel_aug, kk, vsel_aug, vwin_aug, sel, oc_t, gates_t, vb_sel, vb_win)


ROW_SPLITS = 2


def _layer_norm(y, g_ref, b_ref):
    mu = jnp.mean(y, axis=-1, keepdims=True)
    yc = y - mu
    var = jnp.mean(yc * yc, axis=-1, keepdims=True)
    return yc * lax.rsqrt(var + LN_EPS) * g_ref[...] + b_ref[...]


def _merge_kernel(x_ref, ya_ref, yb_ref, wga_ref, wgb_ref, wa_ref, wb_ref, o_ref):
    x = x_ref[...]
    ga = jax.nn.sigmoid(jnp.dot(x, wga_ref[...], preferred_element_type=F32))
    gb = jax.nn.sigmoid(jnp.dot(x, wgb_ref[...], preferred_element_type=F32))
    a = jnp.dot(ya_ref[...], wa_ref[...], preferred_element_type=F32)
    bb = jnp.dot(yb_ref[...], wb_ref[...], preferred_element_type=F32)
    o_ref[...] = (ga * a + gb * bb).astype(o_ref.dtype)


def _merge(x_bf, y_a, y_b, w_ga, w_gb, w_a, w_b, *, tm, tn):
    tokens, d = x_bf.shape
    ya_cols, yb_cols = y_a.shape[1], y_b.shape[1]
    return pl.pallas_call(
        _merge_kernel,
        grid=(tokens // tm, d // tn),
        in_specs=[
            pl.BlockSpec((tm, d), lambda i, j: (i, 0)),
            pl.BlockSpec((tm, ya_cols), lambda i, j: (i, 0)),
            pl.BlockSpec((tm, yb_cols), lambda i, j: (i, 0)),
            pl.BlockSpec((d, tn), lambda i, j: (0, j)),
            pl.BlockSpec((d, tn), lambda i, j: (0, j)),
            pl.BlockSpec((ya_cols, tn), lambda i, j: (0, j)),
            pl.BlockSpec((yb_cols, tn), lambda i, j: (0, j)),
        ],
        out_specs=pl.BlockSpec((tm, tn), lambda i, j: (i, j)),
        out_shape=jax.ShapeDtypeStruct((tokens, d), BF16),
        compiler_params=_params(2),
        name="branch_merge",
    )(x_bf, y_a, y_b, w_ga, w_gb, w_a, w_b)


def _mix_ln_kernel(m_ref, x_ref, w_ref, g_ref, b_ref, o_ref):
    rows = m_ref.shape[0] // ROW_SPLITS
    for r in range(ROW_SPLITS):
        sl = slice(r * rows, (r + 1) * rows)
        mix = jnp.dot(m_ref[sl, :], w_ref[...], preferred_element_type=F32)
        o_ref[sl, :] = _layer_norm(DN_ALPHA * x_ref[sl, :] + mix, g_ref, b_ref)


def _mix_ln(merged, x, w_mix, ln_g, ln_b, *, tm):
    tokens, d = x.shape
    return pl.pallas_call(
        _mix_ln_kernel,
        grid=(tokens // tm,),
        in_specs=[
            pl.BlockSpec((tm, d), lambda i: (i, 0)),
            pl.BlockSpec((tm, d), lambda i: (i, 0)),
            pl.BlockSpec((d, d), lambda i: (0, 0)),
            pl.BlockSpec((1, d), lambda i: (0, 0)),
            pl.BlockSpec((1, d), lambda i: (0, 0)),
        ],
        out_specs=pl.BlockSpec((tm, d), lambda i: (i, 0)),
        out_shape=jax.ShapeDtypeStruct((tokens, d), F32),
        compiler_params=_params(1),
        name="mix_out_ln1",
    )(merged, x, w_mix, ln_g, ln_b)


def _xa_kernel(h_ref, wq_ref, k_ref, v_ref, wo_ref, g_ref, b_ref, o_ref):
    h = h_ref[...]
    q = jnp.dot(h.astype(BF16), wq_ref[...], preferred_element_type=F32) * (XA_HEAD_DIM ** -0.5)
    q = q.astype(BF16)
    outs = []
    for hd in range(XA_HEADS):
        cols = slice(hd * XA_HEAD_DIM, (hd + 1) * XA_HEAD_DIM)
        s = lax.dot_general(q[:, cols], k_ref[:, cols], (((1,), (1,)), ((), ())), preferred_element_type=F32)
        p = jnp.exp(s - jnp.max(s, axis=-1, keepdims=True))
        l = jnp.sum(p, axis=-1, keepdims=True)
        o = jnp.dot(p.astype(BF16), v_ref[:, cols], preferred_element_type=F32)
        outs.append(o * (1.0 / l))
    o = jnp.concatenate(outs, axis=1).astype(BF16)
    xa = jnp.dot(o, wo_ref[...], preferred_element_type=F32)
    o_ref[...] = _layer_norm(DN_ALPHA * h + xa, g_ref, b_ref)


def _cross_attention_ln(h, kv_mem, w_q, w_o, ln_g, ln_b, batch, seq, *, tm):
    tokens, d = h.shape
    mem_len = kv_mem.shape[0] // batch
    xa_dim = XA_HEADS * XA_HEAD_DIM
    nt = seq // tm
    return pl.pallas_call(
        _xa_kernel,
        grid=(batch, nt),
        in_specs=[
            pl.BlockSpec((tm, d), lambda b, i: (b * nt + i, 0)),
            pl.BlockSpec((d, xa_dim), lambda b, i: (0, 0)),
            pl.BlockSpec((mem_len, xa_dim), lambda b, i: (b, 0)),
            pl.BlockSpec((mem_len, xa_dim), lambda b, i: (b, 1)),
            pl.BlockSpec((xa_dim, d), lambda b, i: (0, 0)),
            pl.BlockSpec((1, d), lambda b, i: (0, 0)),
            pl.BlockSpec((1, d), lambda b, i: (0, 0)),
        ],
        out_specs=pl.BlockSpec((tm, d), lambda b, i: (b * nt + i, 0)),
        out_shape=jax.ShapeDtypeStruct((tokens, d), F32),
        compiler_params=_params(2),
        name="cross_attention_ln2",
    )(h, w_q, kv_mem, kv_mem, w_o, ln_g, ln_b)


def _mlp_kernel(h_ref, w1_ref, w2_ref, g_ref, b_ref, o_ref, hb_ref):
    j = pl.program_id(1)

    @pl.when(j == 0)
    def _():
        hb_ref[...] = h_ref[...].astype(BF16)
        o_ref[...] = jnp.zeros_like(o_ref)

    rows = hb_ref.shape[0] // ROW_SPLITS
    subs = [slice(r * rows, (r + 1) * rows) for r in range(ROW_SPLITS)]
    ups = [jnp.dot(hb_ref[sl, :], w1_ref[...], preferred_element_type=F32) for sl in subs]
    for sl, u in zip(subs, ups):
        u = jnp.square(jnp.maximum(u, 0.0)).astype(BF16)
        o_ref[sl, :] += jnp.dot(u, w2_ref[...], preferred_element_type=F32)

    @pl.when(j == pl.num_programs(1) - 1)
    def _():
        o_ref[...] = _layer_norm(DN_ALPHA * h_ref[...] + o_ref[...], g_ref, b_ref)


def _mlp_ln(h, w1, w2, ln_g, ln_b, *, tm, tf):
    tokens, d = h.shape
    d_ff = w1.shape[1]
    return pl.pallas_call(
        _mlp_kernel,
        grid=(tokens // tm, d_ff // tf),
        in_specs=[
            pl.BlockSpec((tm, d), lambda i, j: (i, 0)),
            pl.BlockSpec((d, tf), lambda i, j: (0, j)),
            pl.BlockSpec((tf, d), lambda i, j: (j, 0)),
            pl.BlockSpec((1, d), lambda i, j: (0, 0)),
            pl.BlockSpec((1, d), lambda i, j: (0, 0)),
        ],
        out_specs=pl.BlockSpec((tm, d), lambda i, j: (i, 0)),
        out_shape=jax.ShapeDtypeStruct((tokens, d), F32),
        scratch_shapes=[pltpu.VMEM((tm, d), BF16)],
        compiler_params=_params(2),
        name="mlp_ln3",
    )(h, w1, w2, ln_g, ln_b)


def _overlap_t(seq):
    ncp, nsel = seq // CMP_STRIDE, seq // SEL_BLOCK
    c_start = np.arange(ncp)[None, :] * CMP_STRIDE
    s_start = np.arange(nsel)[:, None] * SEL_BLOCK
    ov = (c_start < s_start + SEL_BLOCK) & (c_start + CMP_BLOCK > s_start) & (np.arange(ncp)[None, :] < ncp - 1)
    return jnp.asarray(ov, BF16)


def kernel(x, mem, w_in, attn_sinks, rel_bias_table, cmp_pos_k, cmp_w1_k, cmp_w2_k, cmp_pos_v, cmp_w1_v, cmp_w2_v,
           w_branch_swa, w_branch_nsa, w_mix_out, ln1_g, ln1_b, xa_w_q, xa_w_kv, xa_w_o, ln2_g, ln2_b,
           mlp_w1, mlp_w2, ln3_g, ln3_b):
    batch, seq, d = x.shape
    tokens = batch * seq
    nq = seq // BLOCK_Q
    assert w_in.shape[0] == 1, "one layer"
    assert seq % (8 * BLOCK_Q) == 0 and d % 512 == 0

    w = w_in[0]
    sizes = (SWA_HEADS * HEAD_DIM, 128, 128, NSA_HEADS * HEAD_DIM, 128, 128, 128, 128, 128, 128, 3 * NSA_HEADS, d, d)
    offs = np.concatenate([[0], np.cumsum(sizes)])
    (w_qa, w_ka, w_va, w_qb, w_kc, w_vc, w_ks, w_vs, w_kw, w_vw, w_gn, w_ga, w_gb) = [
        w[:, offs[n]:offs[n + 1]] for n in range(len(sizes))]
    w_rows = jnp.concatenate([w_ka, w_kc, w_ks, w_kw, w_vc], axis=1).astype(BF16)
    w_gn_pad = jnp.pad(w_gn, ((0, 0), (0, 128 - w_gn.shape[1])))
    w_cols_t = jnp.concatenate([w_qa, w_qb, w_va, w_vs, w_vw, w_gn_pad], axis=1).T.astype(BF16)

    x2 = x.reshape(tokens, d)
    x_bf = x2.astype(BF16)

    h_rows = _matmul(x_bf, w_rows, nt=False, tm=min(2048, tokens), tn=w_rows.shape[1])
    h_t = _matmul(w_cols_t, x_bf, nt=True, tm=1280, tn=min(1024, tokens))

    kk = h_rows[:, :512].reshape(tokens, 4, GROUPS, HEAD_DIM).transpose(1, 2, 0, 3)
    v_cmp = h_rows[:, 512:].reshape(tokens, GROUPS, HEAD_DIM).transpose(1, 0, 2)

    def value_tiles(row0, width):
        v = h_t[row0:row0 + 128].reshape(GROUPS, HEAD_DIM, batch, seq // width, width).transpose(0, 2, 3, 1, 4)
        ones = jnp.ones(v.shape[:3] + (V_AUG_ROWS - HEAD_DIM, width), BF16)
        return jnp.concatenate([v, ones], axis=3)

    vswa_aug = value_tiles(2048, BLOCK_Q)
    vsel_aug, vwin_aug = value_tiles(2176, 2 * BLOCK_Q), value_tiles(2304, 2 * BLOCK_Q)
    gates_t = h_t[2432:2432 + 3 * NSA_HEADS].reshape(GROUPS, HEADS_PER_GROUP, 3, tokens).transpose(0, 2, 1, 3)
    block_of_key = (jnp.arange(tokens) % BLOCK_Q) // SEL_BLOCK
    onehot = (block_of_key[:, None] == jnp.arange(SEL_ROWS)[None, :]).astype(BF16)
    ksel_aug = jnp.concatenate([kk[2], jnp.broadcast_to(onehot, (GROUPS, tokens, SEL_ROWS))], axis=2)

    by_dist = rel_bias_table[_rel_bucket(jnp.arange(seq))].astype(F32).T
    vb_swa = _bias_by_distance(by_dist[:SWA_HEADS], 4, window=SWA_WINDOW)
    vb_sel = _bias_by_distance(by_dist[SWA_HEADS:], nq + 2)
    vb_win = _bias_by_distance(by_dist[SWA_HEADS:], N_WIN_SLOTS + 1, window=NSA_WINDOW)
    vb_cmp = vb_sel
    sink_rows = jnp.repeat(attn_sinks[0].astype(F32), BLOCK_Q).reshape(GROUPS, 1, GROUP_LANES)

    y_a = _swa_attention(h_t, kk, vswa_aug, sink_rows, vb_swa, batch, seq)

    ncp = seq // CMP_STRIDE
    chunk_w = CMP_STRIDE * HEAD_DIM
    chunks = jnp.stack([kk[1], v_cmp]).reshape(2, GROUPS, batch, ncp, chunk_w)
    pos = jnp.stack([cmp_pos_k[0], cmp_pos_v[0]]).astype(F32).reshape(2, 2, 1, chunk_w)
    w1 = jnp.stack([cmp_w1_k[0], cmp_w1_v[0]]).astype(BF16).reshape(2, 2, chunk_w, -1)
    w2 = jnp.stack([cmp_w2_k[0], cmp_w2_v[0]]).astype(BF16)
    cn, ct = _compress(chunks, pos, w1, w2, w2.transpose(0, 2, 1))
    oc_t, sel = _cmp_attention(h_t, cn, ct, vb_cmp, _overlap_t(seq), batch, seq)
    y_b = _selwin_attention(h_t, ksel_aug, kk, vsel_aug, vwin_aug, sel, oc_t, gates_t, vb_sel, vb_win, batch, seq)

    merged = _merge(x_bf, y_a, y_b, w_ga.astype(BF16), w_gb.astype(BF16),
                    w_branch_swa[0].astype(BF16), w_branch_nsa[0].astype(BF16), tm=min(1024, tokens), tn=512)
    h1 = _mix_ln(merged, x2, w_mix_out[0].astype(BF16), ln1_g, ln1_b, tm=512)

    mem_bf = mem.reshape(-1, d).astype(BF16)
    kv_mem = _matmul(mem_bf, xa_w_kv[0].astype(BF16), nt=False, tm=mem_bf.shape[0], tn=512)
    h2 = _cross_attention_ln(h1, kv_mem, xa_w_q[0].astype(BF16), xa_w_o[0].astype(BF16), ln2_g, ln2_b,
                             batch, seq, tm=512)

    h3 = _mlp_ln(h2, mlp_w1[0].astype(BF16), mlp_w2[0].astype(BF16), ln3_g, ln3_b, tm=1024, tf=512)
    return h3.reshape(batch, seq, d)
```

```python
import functools
import math

import numpy as np
import jax
import jax.numpy as jnp
from jax import lax
from jax.experimental import pallas as pl
from jax.experimental.pallas import tpu as pltpu

F32 = jnp.float32
BF16 = jnp.bfloat16

HEAD_DIM = 64
BLOCK_Q = 128
SWA_HEADS = 16
SWA_KV_HEADS = 2
SWA_WINDOW = 128
NSA_HEADS = 16
NSA_KV_HEADS = 2
CMP_BLOCK = 32
CMP_STRIDE = 16
SEL_BLOCK = 64
SEL_TOPK = 16
SEL_LOCAL = 2
NSA_WINDOW = 512
REL_BUCKETS = 32
REL_MAX_DIST = 4096
XA_HEADS = 4
XA_HEAD_DIM = 128
DEPTH = 1
DN_ALPHA = (2.0 * DEPTH) ** 0.25
LN_EPS = 1e-5
NEG_INF = -1e30
FORCE_SCORE = 1e4

GROUPS = 2
HEADS_PER_GROUP = 8
GROUP_LANES = HEADS_PER_GROUP * BLOCK_Q
GROUP_COLS = HEADS_PER_GROUP * HEAD_DIM
SUBTILES = 2

V7X_VMEM_LIMIT_BYTES = 56 * 1024 * 1024


def _params(n_axes):
    return pltpu.CompilerParams(dimension_semantics=("arbitrary",) * n_axes,
                                vmem_limit_bytes=V7X_VMEM_LIMIT_BYTES)


def _mm_kernel(a_ref, b_ref, o_ref, *, nt):
    if nt:
        out = lax.dot_general(a_ref[...], b_ref[...], (((1,), (1,)), ((), ())), preferred_element_type=F32)
    else:
        out = jnp.dot(a_ref[...], b_ref[...], preferred_element_type=F32)
    o_ref[...] = out.astype(o_ref.dtype)


def _matmul(a, b, *, nt, tm, tn, out_dtype=BF16):
    m, k = a.shape
    n = b.shape[0] if nt else b.shape[1]
    b_spec = pl.BlockSpec((tn, k), lambda i, j: (j, 0)) if nt else pl.BlockSpec((k, tn), lambda i, j: (0, j))
    return pl.pallas_call(
        functools.partial(_mm_kernel, nt=nt),
        grid=(m // tm, n // tn),
        in_specs=[pl.BlockSpec((tm, k), lambda i, j: (i, 0)), b_spec],
        out_specs=pl.BlockSpec((tm, tn), lambda i, j: (i, j)),
        out_shape=jax.ShapeDtypeStruct((m, n), out_dtype),
        compiler_params=_params(2),
        name="proj_nt" if nt else "proj_nn",
    )(a, b)


def _in_proj_kernel(x_ref, wt_ref, wr_ref, ht_ref, xb_ref, kk_ref):
    xb = x_ref[...].astype(BF16)
    xb_ref[...] = xb
    ht_ref[...] = lax.dot_general(wt_ref[...], xb, (((1,), (1,)), ((), ())),
                                  preferred_element_type=F32).astype(ht_ref.dtype)
    rows = jnp.dot(xb, wr_ref[...], preferred_element_type=F32)
    for c in range(kk_ref.shape[0]):
        for g in range(GROUPS):
            col = (c * GROUPS + g) * HEAD_DIM
            kk_ref[c, g] = rows[:, col:col + HEAD_DIM].astype(kk_ref.dtype)


def _input_projections(x, w_cols_t, w_rows, *, tm):
    tokens, d = x.shape
    cols = w_cols_t.shape[0]
    n_rows = w_rows.shape[1] // (GROUPS * HEAD_DIM)
    return pl.pallas_call(
        _in_proj_kernel,
        grid=(tokens // tm,),
        in_specs=[
            pl.BlockSpec((tm, d), lambda i: (i, 0)),
            pl.BlockSpec((cols, d), lambda i: (0, 0)),
            pl.BlockSpec((d, w_rows.shape[1]), lambda i: (0, 0)),
        ],
        out_specs=[
            pl.BlockSpec((cols, tm), lambda i: (0, i)),
            pl.BlockSpec((tm, d), lambda i: (i, 0)),
            pl.BlockSpec((n_rows, GROUPS, tm, HEAD_DIM), lambda i: (0, 0, i, 0)),
        ],
        out_shape=[jax.ShapeDtypeStruct((cols, tokens), BF16),
                   jax.ShapeDtypeStruct((tokens, d), BF16),
                   jax.ShapeDtypeStruct((n_rows, GROUPS, tokens, HEAD_DIM), BF16)],
        compiler_params=_params(1),
        name="input_projections",
    )(x, w_cols_t, w_rows)


def _rel_bucket(dist):
    exact = REL_BUCKETS // 2
    d = jnp.maximum(dist, 0)
    log_ratio = jnp.log(jnp.maximum(d, 1).astype(F32) / exact) / math.log(REL_MAX_DIST / exact)
    large = jnp.minimum(exact + (log_ratio * (REL_BUCKETS - exact)).astype(jnp.int32), REL_BUCKETS - 1)
    return jnp.where(d < exact, d, large)


def _bias_by_distance(by_dist, rows, *, window=None):
    heads, seq = by_dist.shape
    if window is not None:
        by_dist = jnp.where(jnp.arange(seq) < window, by_dist, NEG_INF)
    padded = jnp.pad(by_dist, ((0, 0), (2 * BLOCK_Q, 0)), constant_values=NEG_INF)[:, :rows * BLOCK_Q]
    return padded.reshape(heads // HEADS_PER_GROUP, HEADS_PER_GROUP, rows, BLOCK_Q)


def _build_skew_table(vb_ref, tbl_ref, n_tiles):
    def body(d, carry):
        for h in range(HEADS_PER_GROUP):
            lo = jnp.broadcast_to(vb_ref[h, pl.ds(d, 1), :], (BLOCK_Q, BLOCK_Q))
            hi = jnp.broadcast_to(vb_ref[h, pl.ds(d + 1, 1), :], (BLOCK_Q, BLOCK_Q))
            y = pltpu.roll(jnp.concatenate([lo, hi], axis=1), 0, 1, stride=1, stride_axis=0)
            tbl_ref[d, :, h * BLOCK_Q:(h + 1) * BLOCK_Q] = y[:, BLOCK_Q:].astype(tbl_ref.dtype)
        return carry

    lax.fori_loop(0, n_tiles, body, 0)


def _build_cmp_bias_table(vb_ref, tbl_ref, skew_ref, nq):
    per_tile = BLOCK_Q // CMP_STRIDE
    assert per_tile == 8 and CMP_BLOCK - 1 + CMP_STRIDE * (per_tile - 1) - BLOCK_Q == CMP_STRIDE - 1
    tbl_ref[pl.ds(nq * 8, nq * 8), :] = jnp.full((nq * 8, GROUP_LANES), NEG_INF, F32)

    def body(d, last_row):
        for h in range(HEADS_PER_GROUP):
            lo = jnp.broadcast_to(vb_ref[h, pl.ds(d + 1, 1), :], (BLOCK_Q, BLOCK_Q))
            hi = jnp.broadcast_to(vb_ref[h, pl.ds(d + 2, 1), :], (BLOCK_Q, BLOCK_Q))
            y = pltpu.roll(jnp.concatenate([lo, hi], axis=1), 0, 1, stride=1, stride_axis=0)
            skew_ref[:, h * BLOCK_Q:(h + 1) * BLOCK_Q] = y[:, BLOCK_Q:]
        offsets = [CMP_BLOCK - 1 + CMP_STRIDE * c for c in range(per_tile - 1)]
        rows = [skew_ref[k:k + 1, :] for k in offsets] + [last_row]
        tbl_ref[pl.ds(pl.multiple_of((nq - 1 - d) * 8, 8), 8), :] = jnp.concatenate(rows, axis=0)
        return skew_ref[CMP_STRIDE - 1:CMP_STRIDE, :]

    lax.fori_loop(0, nq, body, jnp.full((1, GROUP_LANES), NEG_INF, F32))


def _gather_heads_to_lanes(q_ref):
    q = jnp.concatenate([q_ref[h * HEAD_DIM:(h + 1) * HEAD_DIM, u * BLOCK_Q:(u + 1) * BLOCK_Q]
                         for u in range(q_ref.shape[1] // BLOCK_Q) for h in range(HEADS_PER_GROUP)], axis=1)
    return q * jnp.asarray(HEAD_DIM ** -0.5, q.dtype)


def _heads_to_columns(o_t):
    stacked = jnp.concatenate([o_t[:, h * BLOCK_Q:(h + 1) * BLOCK_Q] for h in range(HEADS_PER_GROUP)], axis=0)
    return stacked.T


def _identity_tile():
    r = lax.broadcasted_iota(jnp.int32, (BLOCK_Q, BLOCK_Q), 0)
    c = lax.broadcasted_iota(jnp.int32, (BLOCK_Q, BLOCK_Q), 1)
    return jnp.where(r == c, 1.0, 0.0).astype(BF16)


def _biased_scores(eye, k_t, bias_tile, q_t):
    lhs = jnp.concatenate([eye, k_t], axis=1)
    rhs = jnp.concatenate([bias_tile, q_t], axis=0)
    return jnp.dot(lhs, rhs, preferred_element_type=F32)


def _softmax_probs(m, s_tiles):
    ms, alphas, ps = [], [], [[] for _ in s_tiles]
    for h in range(m.shape[1] // BLOCK_Q):
        lanes = slice(h * BLOCK_Q, (h + 1) * BLOCK_Q)
        cols = [s[:, lanes] for s in s_tiles]
        m_new = m[:, lanes]
        for c in cols:
            m_new = jnp.maximum(m_new, jnp.max(c, axis=0, keepdims=True))
        ms.append(m_new)
        alphas.append(jnp.exp(m[:, lanes] - m_new))
        for j, c in enumerate(cols):
            ps[j].append(jnp.exp((c - m_new).astype(BF16)))
    p = jnp.concatenate([jnp.concatenate(pj, axis=1) for pj in ps], axis=0)
    return jnp.concatenate(ms, axis=1), jnp.concatenate(alphas, axis=1), p


def _softmax_update(carry, s_tiles, v_aug):
    m, acc = carry
    m, alpha, p = _softmax_probs(m, s_tiles)
    return m, alpha * acc + jnp.dot(v_aug, p, preferred_element_type=F32)


def _pipelined_attention(n_steps, scores_fn, values_fn, init, s_scr, p_scr, acc_scr, *, fill):
    m0, acc0 = init

    def put_scores(n):
        for j, s in enumerate(scores_fn(n)):
            s_scr[j] = s

    def fold(alpha, n, p):
        acc_scr[...] = alpha * acc_scr[...] + jnp.dot(values_fn(n), p, preferred_element_type=F32)

    if fill:
        put_scores(0)
        p_scr[...] = jnp.zeros_like(p_scr)
        acc_scr[...] = acc0
        return None

    def body(k, carry):
        m, alpha_a, alpha_b = carry
        a = 2 * k
        s_b = scores_fn(a + 1)
        fold(alpha_a, a - 2, p_scr[0])
        fold(alpha_b, a - 1, p_scr[1])
        m, alpha_a, p_a = _softmax_probs(m, [s_scr.at[j] for j in range(s_scr.shape[0])])
        put_scores(a + 2)
        m, alpha_b, p_b = _softmax_probs(m, s_b)
        p_scr[0] = p_a
        p_scr[1] = p_b
        return m, alpha_a, alpha_b

    trips = (n_steps + 1) // 2
    ones = jnp.ones_like(m0)
    m, alpha_a, alpha_b = lax.fori_loop(0, trips, body, (m0, ones, ones))
    fold(alpha_a, 2 * trips - 2, p_scr[0])
    fold(alpha_b, 2 * trips - 1, p_scr[1])
    return m, acc_scr[...]


SLAB = 2 * BLOCK_Q
LOOKAHEAD = 12


def _slabwise_softmax_step(m, acc_ref, lhs_tiles, rhs_fn, v_fn, masked_fn=None):
    per_slab = [[j for j in range(len(lhs_tiles)) if masked_fn is None or not masked_fn(j, sl)]
                for sl in range(m.shape[1] // SLAB)]
    items = [(sl, j) for sl, tiles in enumerate(per_slab) for j in tiles]

    def scores(k):
        sl, j = items[k]
        lhs = lhs_tiles[j](sl) if callable(lhs_tiles[j]) else lhs_tiles[j]
        return jnp.dot(lhs, rhs_fn(j, sl), preferred_element_type=F32)

    pending = [scores(k) for k in range(min(LOOKAHEAD, len(items)))]
    issued = len(pending)
    new_m = []
    for sl, tiles in enumerate(per_slab):
        lanes = slice(sl * SLAB, (sl + 1) * SLAB)
        m_s, acc_s = m[:, lanes], acc_ref[:, lanes]
        for c in range(0, len(tiles), 2):
            chunk = tiles[c:c + 2]
            m_old, ps = m_s, []
            for j in chunk:
                s = pending.pop(0)
                if issued < len(items):
                    pending.append(scores(issued))
                    issued += 1
                m_next = jnp.maximum(m_s, jnp.max(s, axis=0, keepdims=True))
                ps = [q * jnp.exp(m_s - m_next).astype(BF16) for q in ps]
                ps.append(jnp.exp((s - m_next).astype(BF16)))
                m_s = m_next
            v = jnp.concatenate([v_fn(j, sl) for j in chunk], axis=1)
            acc_s = jnp.exp(m_old - m_s) * acc_s + jnp.dot(v, jnp.concatenate(ps, axis=0),
                                                           preferred_element_type=F32)
        acc_ref[:, lanes] = acc_s
        new_m.append(m_s)
    return jnp.concatenate(new_m, axis=1)


V_AUG_ROWS = HEAD_DIM + 16


def _normalized(acc):
    return acc[:HEAD_DIM] * (1.0 / acc[HEAD_DIM:HEAD_DIM + 1])


def _swa_kernel(q_ref, kprev_ref, kcur_ref, vprev_ref, vcur_ref, sink_ref, vb_ref, o_ref, tbl_ref, acc_ref):
    b, i = pl.program_id(1), pl.program_id(2)

    @pl.when((b == 0) & (i == 0))
    def _():
        _build_skew_table(vb_ref, tbl_ref, 3)

    eye = _identity_tile()
    q_t = _gather_heads_to_lanes(q_ref)
    lanes = q_t.shape[1]
    slabs_per_tile = GROUP_LANES // SLAB
    m0 = jnp.concatenate([sink_ref[...]] * SWA_TILES, axis=1)
    acc_ref[...] = jnp.where(lax.broadcasted_iota(jnp.int32, (V_AUG_ROWS, lanes), 0) < HEAD_DIM, 0.0, 1.0)

    def key_tile(j):
        def at(slab):
            u = slab // slabs_per_tile - j
            k_t = kprev_ref[...] if u < 0 else kcur_ref[u * BLOCK_Q:(u + 1) * BLOCK_Q, :]
            return jnp.concatenate([eye, k_t], axis=1)
        return at

    def rhs(j, slab):
        u, part = slab // slabs_per_tile, slab % slabs_per_tile
        slot = 1 + j if (u > 0 or j == 0) else jnp.where(i > 0, 2, 0)
        return jnp.concatenate([tbl_ref[slot, :, part * SLAB:(part + 1) * SLAB],
                                q_t[:, slab * SLAB:(slab + 1) * SLAB]], axis=0)

    def values(j, slab):
        u = slab // slabs_per_tile - j
        return vprev_ref[...] if u < 0 else vcur_ref[u]

    _slabwise_softmax_step(m0, acc_ref, [key_tile(0), key_tile(1)], rhs, values)
    out = _normalized(acc_ref[...])
    for u in range(SWA_TILES):
        o_ref[u * BLOCK_Q:(u + 1) * BLOCK_Q, :] = _heads_to_columns(
            out[:, u * GROUP_LANES:(u + 1) * GROUP_LANES]).astype(o_ref.dtype)


SWA_TILES = 4


def _swa_attention(h_t, kk, v_aug, sink_rows, vb, batch, seq):
    nq = seq // BLOCK_Q
    nblk = nq // SWA_TILES
    width = SWA_TILES * BLOCK_Q
    tokens = batch * seq

    def prev_tile(i):
        return jnp.maximum(i * SWA_TILES - 1, 0)

    return pl.pallas_call(
        _swa_kernel,
        grid=(GROUPS, batch, nblk),
        in_specs=[
            pl.BlockSpec((GROUP_COLS, width), lambda g, b, i: (g, b * nblk + i)),
            pl.BlockSpec((None, None, BLOCK_Q, HEAD_DIM), lambda g, b, i: (0, g, b * nq + prev_tile(i), 0)),
            pl.BlockSpec((None, None, width, HEAD_DIM), lambda g, b, i: (0, g, b * nblk + i, 0)),
            pl.BlockSpec((None, None, None, V_AUG_ROWS, BLOCK_Q), lambda g, b, i: (g, b, prev_tile(i), 0, 0)),
            pl.BlockSpec((None, None, SWA_TILES, V_AUG_ROWS, BLOCK_Q), lambda g, b, i: (g, b, i, 0, 0)),
            pl.BlockSpec((None, 1, GROUP_LANES), lambda g, b, i: (g, 0, 0)),
            pl.BlockSpec((None, HEADS_PER_GROUP, 4, BLOCK_Q), lambda g, b, i: (g, 0, 0, 0)),
        ],
        out_specs=pl.BlockSpec((width, GROUP_COLS), lambda g, b, i: (b * nblk + i, g)),
        out_shape=jax.ShapeDtypeStruct((tokens, SWA_HEADS * HEAD_DIM), BF16),
        scratch_shapes=[pltpu.VMEM((3, BLOCK_Q, GROUP_LANES), BF16),
                        pltpu.VMEM((V_AUG_ROWS, SWA_TILES * GROUP_LANES), F32)],
        compiler_params=_params(3),
        name="swa_attention",
    )(h_t, kk, kk, v_aug, v_aug, sink_rows, vb)


def _compress_kernel(c_ref, pos_ref, w1_ref, w2_ref, w2t_ref, cn_ref, ct_ref):
    c = c_ref[...].astype(F32)
    top = (c + pos_ref[0]).astype(BF16)
    bot = (c + pos_ref[1]).astype(BF16)
    a = jnp.dot(top, w1_ref[0], preferred_element_type=F32)
    bm = jnp.dot(bot, w1_ref[1], preferred_element_type=F32)
    n = a.shape[0]
    pre = a + pltpu.roll(bm, n - 1, 0)
    hid = jax.nn.gelu(pre).astype(BF16)
    cn_ref[...] = jnp.dot(hid, w2_ref[...], preferred_element_type=F32).astype(cn_ref.dtype)
    ct_ref[...] = lax.dot_general(w2t_ref[...], hid, (((1,), (1,)), ((), ())),
                                  preferred_element_type=F32).astype(ct_ref.dtype)


def _compress(chunks, which, pos, w1, w2, w2t):
    _, g, b, ncp, width = chunks.shape
    hidden = w1.shape[-1]
    first, step = which[0], which[1] - which[0]
    return pl.pallas_call(
        _compress_kernel,
        grid=(2, g, b),
        in_specs=[
            pl.BlockSpec((None, None, None, ncp, width), lambda t, g, b: (first + step * t, g, b, 0, 0)),
            pl.BlockSpec((None, 2, 1, width), lambda t, g, b: (t, 0, 0, 0)),
            pl.BlockSpec((None, 2, width, hidden), lambda t, g, b: (t, 0, 0, 0)),
            pl.BlockSpec((None, hidden, HEAD_DIM), lambda t, g, b: (t, 0, 0)),
            pl.BlockSpec((None, HEAD_DIM, hidden), lambda t, g, b: (t, 0, 0)),
        ],
        out_specs=[
            pl.BlockSpec((None, None, None, ncp, HEAD_DIM), lambda t, g, b: (t, g, b, 0, 0)),
            pl.BlockSpec((None, None, None, HEAD_DIM, ncp), lambda t, g, b: (t, g, b, 0, 0)),
        ],
        out_shape=[jax.ShapeDtypeStruct((2, g, b, ncp, HEAD_DIM), BF16),
                   jax.ShapeDtypeStruct((2, g, b, HEAD_DIM, ncp), BF16)],
        compiler_params=_params(3),
        name="nsa_compress",
    )(chunks, pos, w1, w2, w2t)


def _cmp_kernel(q_ref, kc_ref, vct_ref, vb_ref, ov_ref, oc_ref, sel_ref, bias_ref, skew_ref, bias16_ref, *, nq):
    b, i = pl.program_id(1), pl.program_id(2)

    @pl.when((b == 0) & (i == 0))
    def _():
        _build_cmp_bias_table(vb_ref, bias_ref, skew_ref, nq)
        rows = bias_ref.shape[0]
        bias16_ref[0] = bias_ref[...].astype(BF16)
        bias16_ref[1, 0:rows - 16, :] = bias_ref[8:rows - 8, :].astype(BF16)

    ncp = kc_ref.shape[0]
    nsel = sel_ref.shape[0]
    n_tiles = ncp // BLOCK_Q
    q_t = _gather_heads_to_lanes(q_ref)
    width = SUBTILES * BLOCK_Q
    eye = _identity_tile()

    slabs_per_tile = GROUP_LANES // SLAB
    lhs = [jnp.concatenate([eye, kc_ref[t * BLOCK_Q:(t + 1) * BLOCK_Q, :]], axis=1) for t in range(n_tiles)]
    items = [(sl, t) for sl in range(SUBTILES * slabs_per_tile) for t in range(n_tiles)]

    def scores(k):
        sl, t = items[k]
        u, part = sl // slabs_per_tile, sl % slabs_per_tile
        shifted = (nq - 1 - u) % 2
        row0 = pl.multiple_of((nq - 1 - (SUBTILES * i + u)) * 8 + t * BLOCK_Q - 8 * shifted, 16)
        bias = bias16_ref[shifted, pl.ds(row0, BLOCK_Q), part * SLAB:(part + 1) * SLAB]
        return jnp.dot(lhs[t], jnp.concatenate([bias, q_t[:, sl * SLAB:(sl + 1) * SLAB]], axis=0),
                       preferred_element_type=F32)

    q_pos = i * width + lax.broadcasted_iota(jnp.int32, (1, width), 1)
    sees_any = q_pos >= CMP_BLOCK - 1
    psum = [[jnp.zeros((BLOCK_Q, BLOCK_Q), F32) for _ in range(n_tiles)] for _ in range(SUBTILES)]
    pending = [scores(k) for k in range(min(LOOKAHEAD, len(items)))]
    for sl in range(SUBTILES * slabs_per_tile):
        u, part = sl // slabs_per_tile, sl % slabs_per_tile
        s = [pending.pop(0) for _ in range(n_tiles)]
        for k in range(sl * n_tiles + LOOKAHEAD, min((sl + 1) * n_tiles + LOOKAHEAD, len(items))):
            pending.append(scores(k))
        m = functools.reduce(jnp.maximum, [jnp.max(c, axis=0, keepdims=True) for c in s])
        e = [jnp.exp(c - m) for c in s]
        l = functools.reduce(jnp.add, [jnp.sum(c, axis=0, keepdims=True) for c in e])
        seen = sees_any[:, u * BLOCK_Q:(u + 1) * BLOCK_Q]
        inv = jnp.where(jnp.concatenate([seen] * (SLAB // BLOCK_Q), axis=1), 1.0 / l, 0.0)
        o_slab = jnp.zeros((HEAD_DIM, SLAB), F32)
        for t in range(n_tiles):
            p = e[t] * inv
            psum[u][t] = psum[u][t] + functools.reduce(
                jnp.add, [p[:, r * BLOCK_Q:(r + 1) * BLOCK_Q] for r in range(SLAB // BLOCK_Q)])
            o_slab = o_slab + jnp.dot(vct_ref[:, t * BLOCK_Q:(t + 1) * BLOCK_Q], p.astype(BF16),
                                      preferred_element_type=F32)
        oc_ref[u, :, part * SLAB:(part + 1) * SLAB] = o_slab.astype(oc_ref.dtype)

    psum = jnp.concatenate([jnp.concatenate(pu, axis=0) for pu in psum], axis=1)
    hi = psum.astype(BF16)
    lo = (psum - hi.astype(F32)).astype(BF16)
    ov = ov_ref[...]
    score = jnp.dot(ov, hi, preferred_element_type=F32) + jnp.dot(ov, lo, preferred_element_type=F32)

    j_io = lax.broadcasted_iota(jnp.int32, (nsel, width), 0)
    qpos = i * width + lax.broadcasted_iota(jnp.int32, (nsel, width), 1)
    causal = j_io * SEL_BLOCK <= qpos
    back = qpos // SEL_BLOCK - j_io
    forced = (j_io == 0) | ((back >= 0) & (back < SEL_LOCAL))
    score = jnp.where(causal, jnp.where(forced, FORCE_SCORE, score), -1.0)
    slab_rows = lax.broadcasted_iota(jnp.int32, (8, width), 0)
    slabs = [score[8 * g:8 * (g + 1), :] for g in range(nsel // 8)]
    ranks = [jnp.zeros((8, width), F32) for _ in slabs]
    for r in range(nsel):
        row = jnp.broadcast_to(score[r:r + 1, :], (8, width))
        for g, slab in enumerate(slabs):
            if g > r // 8:
                ahead = row >= slab
            elif g < r // 8:
                ahead = row > slab
            else:
                ranks[g] = ranks[g] + jnp.where(slab_rows > r % 8, jnp.where(row >= slab, 1.0, 0.0),
                                                jnp.where(row > slab, 1.0, 0.0))
                continue
            ranks[g] = ranks[g] + jnp.where(ahead, 1.0, 0.0)
    rank = jnp.concatenate(ranks, axis=0)
    sel_ref[...] = jnp.where((rank < min(SEL_TOPK, nsel)) & causal, 1.0, 0.0).astype(sel_ref.dtype)


def _cmp_attention(h_t, cn, ct, vb, overlap_t, batch, seq):
    nq = seq // BLOCK_Q
    ncp = seq // CMP_STRIDE
    nsel = seq // SEL_BLOCK
    q_blk0 = SWA_HEADS * HEAD_DIM // GROUP_COLS
    nblk = nq // SUBTILES
    width = SUBTILES * BLOCK_Q
    return pl.pallas_call(
        functools.partial(_cmp_kernel, nq=nq),
        grid=(GROUPS, batch, nblk),
        in_specs=[
            pl.BlockSpec((GROUP_COLS, width), lambda g, b, i: (q_blk0 + g, b * nblk + i)),
            pl.BlockSpec((None, None, None, ncp, HEAD_DIM), lambda g, b, i: (0, g, b, 0, 0)),
            pl.BlockSpec((None, None, None, HEAD_DIM, ncp), lambda g, b, i: (1, g, b, 0, 0)),
            pl.BlockSpec((None, HEADS_PER_GROUP, nq + 2, BLOCK_Q), lambda g, b, i: (g, 0, 0, 0)),
            pl.BlockSpec((nsel, ncp), lambda g, b, i: (0, 0)),
        ],
        out_specs=[
            pl.BlockSpec((None, None, SUBTILES, HEAD_DIM, GROUP_LANES), lambda g, b, i: (b, g, i, 0, 0)),
            pl.BlockSpec((None, None, nsel, width), lambda g, b, i: (b, g, 0, i)),
        ],
        out_shape=[jax.ShapeDtypeStruct((batch, GROUPS, nq, HEAD_DIM, GROUP_LANES), BF16),
                   jax.ShapeDtypeStruct((batch, GROUPS, nsel, seq), F32)],
        scratch_shapes=[pltpu.VMEM((2 * nq * 8, GROUP_LANES), F32), pltpu.VMEM((BLOCK_Q, GROUP_LANES), F32),
                        pltpu.VMEM((2, 2 * nq * 8, GROUP_LANES), BF16)],
        compiler_params=_params(3),
        name="nsa_cmp_select",
    )(h_t, cn, ct, vb, overlap_t)


N_WIN_PAIRS = -(-(NSA_WINDOW - 1) // (SUBTILES * BLOCK_Q)) + 1
N_WIN_SLOTS = SUBTILES * N_WIN_PAIRS + 1
SEL_ROWS = 16
SEL_STEPS = 2


def _selwin_kernel(q_ref, ksel_ref, kwin_ref, vsel_ref, vwin_ref, sel_ref, oc_ref, gate_ref, vbs_ref, vbw_ref,
                   o_ref, tsel_ref, twin_ref, accs_ref, accw_ref, *, nq):
    b, blk = pl.program_id(1), pl.program_id(2)

    @pl.when((b == 0) & (blk == 0))
    def _():
        _build_skew_table(vbs_ref, tsel_ref, nq + 1)
        _build_skew_table(vbw_ref, twin_ref, N_WIN_SLOTS)

    q_t = _gather_heads_to_lanes(q_ref)
    lanes = q_t.shape[1]
    eye = _identity_tile()
    pair = SUBTILES * BLOCK_Q
    m0 = jnp.full((1, lanes), NEG_INF, F32)
    accs_ref[...] = jnp.zeros_like(accs_ref)
    accw_ref[...] = jnp.zeros_like(accw_ref)
    blocks_per_tile = BLOCK_Q // SEL_BLOCK
    slabs_per_tile = GROUP_LANES // SLAB

    def attend(m, steps, k_ref, tbl_ref, v_ref, acc_ref, q_ext_fn, reach=None):
        tiles = [(n, jnp.clip(blk - n, 0, blk), j) for n in steps for j in range(SUBTILES)]
        lhs = [jnp.concatenate([eye, k_ref[pl.ds(pl.multiple_of(p * pair + j * BLOCK_Q, BLOCK_Q), BLOCK_Q), :]],
                               axis=1) for _, p, j in tiles]
        q_exts = [q_ext_fn(p, j) for _, p, j in tiles]

        def rhs(t, slab):
            n, _, j = tiles[t]
            u, part = slab // slabs_per_tile, slab % slabs_per_tile
            slot = jnp.where(n <= blk, 2 * n + u - j + 1, 0)
            bias = tbl_ref[slot, :, part * SLAB:(part + 1) * SLAB]
            return jnp.concatenate([bias, q_exts[t][:, slab * SLAB:(slab + 1) * SLAB]], axis=0)

        def values(t, slab):
            _, p, j = tiles[t]
            return v_ref[p, :, j * BLOCK_Q:(j + 1) * BLOCK_Q]

        def masked(t, slab):
            n, _, j = tiles[t]
            distance = 2 * n + slab // slabs_per_tile - j
            return distance < 0 or distance >= reach

        return _slabwise_softmax_step(m, acc_ref, lhs, rhs, values, masked if reach is not None else None)

    attend(m0, list(range(N_WIN_PAIRS)), kwin_ref, twin_ref, vwin_ref, accw_ref, lambda p, j: q_t,
           reach=-(-(NSA_WINDOW + BLOCK_Q - 1) // BLOCK_Q))

    def sel_q_ext(p, j):
        per_pair = SUBTILES * blocks_per_tile
        group = sel_ref[pl.ds(pl.multiple_of((p // 2) * 2 * per_pair, 8), 2 * per_pair), :]
        lo, hi = (group[half * per_pair + j * blocks_per_tile:half * per_pair + (j + 1) * blocks_per_tile]
                  for half in range(2))
        neg = jnp.where(jnp.where(p % 2 == 1, hi, lo) > 0.5, 0.0, NEG_INF)
        rows = jnp.concatenate([neg[:, u * BLOCK_Q:(u + 1) * BLOCK_Q]
                                for u in range(SUBTILES) for _ in range(HEADS_PER_GROUP)], axis=1)
        rows = jnp.concatenate([rows, jnp.zeros((SEL_ROWS - blocks_per_tile, lanes), F32)], axis=0)
        return jnp.concatenate([q_t, rows.astype(BF16)], axis=0)

    def sel_body(k, m):
        return attend(m, [SEL_STEPS * k + r for r in range(SEL_STEPS)], ksel_ref, tsel_ref, vsel_ref, accs_ref,
                      sel_q_ext)

    lax.fori_loop(0, (blk + SEL_STEPS) // SEL_STEPS, sel_body, m0)

    def gate_row(branch):
        g = jnp.concatenate([gate_ref[branch, h:h + 1, u * BLOCK_Q:(u + 1) * BLOCK_Q]
                             for u in range(SUBTILES) for h in range(HEADS_PER_GROUP)], axis=1)
        return jax.nn.sigmoid(g.astype(F32))

    o_c = jnp.concatenate([oc_ref[u] for u in range(SUBTILES)], axis=1).astype(F32)
    out = (gate_row(0) * o_c + gate_row(1) * _normalized(accs_ref[...])
           + gate_row(2) * _normalized(accw_ref[...]))
    for u in range(SUBTILES):
        o_ref[u * BLOCK_Q:(u + 1) * BLOCK_Q, :] = _heads_to_columns(
            out[:, u * GROUP_LANES:(u + 1) * GROUP_LANES]).astype(o_ref.dtype)


def _selwin_attention(h_t, ksel_aug, kk, vsel_aug, vwin_aug, sel, oc_t, gates_t, vb_sel, vb_win, batch, seq):
    nq = seq // BLOCK_Q
    nblk = nq // SUBTILES
    tokens = batch * seq
    q_blk0 = SWA_HEADS * HEAD_DIM // GROUP_COLS
    pair = SUBTILES * BLOCK_Q
    lanes = SUBTILES * GROUP_LANES
    return pl.pallas_call(
        functools.partial(_selwin_kernel, nq=nq),
        grid=(GROUPS, batch, nblk),
        in_specs=[
            pl.BlockSpec((GROUP_COLS, pair), lambda g, b, i: (q_blk0 + g, b * nblk + i)),
            pl.BlockSpec((None, seq, HEAD_DIM + SEL_ROWS), lambda g, b, i: (g, b, 0)),
            pl.BlockSpec((None, None, seq, HEAD_DIM), lambda g, b, i: (3, g, b, 0)),
            pl.BlockSpec((None, None, nblk, V_AUG_ROWS, pair), lambda g, b, i: (g, b, 0, 0, 0)),
            pl.BlockSpec((None, None, nblk, V_AUG_ROWS, pair), lambda g, b, i: (g, b, 0, 0, 0)),
            pl.BlockSpec((None, None, seq // SEL_BLOCK, pair), lambda g, b, i: (b, g, 0, i)),
            pl.BlockSpec((None, None, SUBTILES, HEAD_DIM, GROUP_LANES), lambda g, b, i: (b, g, i, 0, 0)),
            pl.BlockSpec((None, 3, HEADS_PER_GROUP, pair), lambda g, b, i: (g, 0, 0, b * nblk + i)),
            pl.BlockSpec((None, HEADS_PER_GROUP, nq + 2, BLOCK_Q), lambda g, b, i: (g, 0, 0, 0)),
            pl.BlockSpec((None, HEADS_PER_GROUP, N_WIN_SLOTS + 1, BLOCK_Q), lambda g, b, i: (g, 0, 0, 0)),
        ],
        out_specs=pl.BlockSpec((pair, GROUP_COLS), lambda g, b, i: (b * nblk + i, g)),
        out_shape=jax.ShapeDtypeStruct((tokens, NSA_HEADS * HEAD_DIM), BF16),
        scratch_shapes=[pltpu.VMEM((nq + 1, BLOCK_Q, GROUP_LANES), BF16),
                        pltpu.VMEM((N_WIN_SLOTS, BLOCK_Q, GROUP_LANES), BF16),
                        pltpu.VMEM((V_AUG_ROWS, lanes), F32),
                        pltpu.VMEM((V_AUG_ROWS, lanes), F32)],
        compiler_params=_params(3),
        name="nsa_sel_win",
    )(h_t, ksel_aug, kk, vsel_aug, vwin_aug, sel, oc_t, gates_t, vb_sel, vb_win)


ROW_SPLITS = 2


def _layer_norm(y, g_ref, b_ref):
    mu = jnp.mean(y, axis=-1, keepdims=True)
    yc = y - mu
    var = jnp.mean(yc * yc, axis=-1, keepdims=True)
    return yc * lax.rsqrt(var + LN_EPS) * g_ref[...] + b_ref[...]


def _merge_kernel(x_ref, ya_ref, yb_ref, wga_ref, wgb_ref, wa_ref, wb_ref, o_ref):
    x = x_ref[...]
    ga = jax.nn.sigmoid(jnp.dot(x, wga_ref[...], preferred_element_type=F32))
    gb = jax.nn.sigmoid(jnp.dot(x, wgb_ref[...], preferred_element_type=F32))
    a = jnp.dot(ya_ref[...], wa_ref[...], preferred_element_type=F32)
    bb = jnp.dot(yb_ref[...], wb_ref[...], preferred_element_type=F32)
    o_ref[...] = (ga * a + gb * bb).astype(o_ref.dtype)


def _merge(x_bf, y_a, y_b, w_ga, w_gb, w_a, w_b, *, tm, tn):
    tokens, d = x_bf.shape
    ya_cols, yb_cols = y_a.shape[1], y_b.shape[1]
    return pl.pallas_call(
        _merge_kernel,
        grid=(tokens // tm, d // tn),
        in_specs=[
            pl.BlockSpec((tm, d), lambda i, j: (i, 0)),
            pl.BlockSpec((tm, ya_cols), lambda i, j: (i, 0)),
            pl.BlockSpec((tm, yb_cols), lambda i, j: (i, 0)),
            pl.BlockSpec((d, tn), lambda i, j: (0, j)),
            pl.BlockSpec((d, tn), lambda i, j: (0, j)),
            pl.BlockSpec((ya_cols, tn), lambda i, j: (0, j)),
            pl.BlockSpec((yb_cols, tn), lambda i, j: (0, j)),
        ],
        out_specs=pl.BlockSpec((tm, tn), lambda i, j: (i, j)),
        out_shape=jax.ShapeDtypeStruct((tokens, d), BF16),
        compiler_params=_params(2),
        name="branch_merge",
    )(x_bf, y_a, y_b, w_ga, w_gb, w_a, w_b)


def _mix_ln_kernel(m_ref, x_ref, w_ref, g_ref, b_ref, o_ref):
    rows = m_ref.shape[0] // ROW_SPLITS
    for r in range(ROW_SPLITS):
        sl = slice(r * rows, (r + 1) * rows)
        mix = jnp.dot(m_ref[sl, :], w_ref[...], preferred_element_type=F32)
        o_ref[sl, :] = _layer_norm(DN_ALPHA * x_ref[sl, :] + mix, g_ref, b_ref)


def _mix_ln(merged, x, w_mix, ln_g, ln_b, *, tm):
    tokens, d = x.shape
    return pl.pallas_call(
        _mix_ln_kernel,
        grid=(tokens // tm,),
        in_specs=[
            pl.BlockSpec((tm, d), lambda i: (i, 0)),
            pl.BlockSpec((tm, d), lambda i: (i, 0)),
            pl.BlockSpec((d, d), lambda i: (0, 0)),
            pl.BlockSpec((1, d), lambda i: (0, 0)),
            pl.BlockSpec((1, d), lambda i: (0, 0)),
        ],
        out_specs=pl.BlockSpec((tm, d), lambda i: (i, 0)),
        out_shape=jax.ShapeDtypeStruct((tokens, d), F32),
        compiler_params=_params(1),
        name="mix_out_ln1",
    )(merged, x, w_mix, ln_g, ln_b)


def _xa_kernel(h_ref, wq_ref, k_ref, v_ref, wo_ref, g_ref, b_ref, o_ref):
    h = h_ref[...]
    q = jnp.dot(h.astype(BF16), wq_ref[...], preferred_element_type=F32) * (XA_HEAD_DIM ** -0.5)
    q = q.astype(BF16)
    outs = []
    for hd in range(XA_HEADS):
        cols = slice(hd * XA_HEAD_DIM, (hd + 1) * XA_HEAD_DIM)
        s = lax.dot_general(q[:, cols], k_ref[:, cols], (((1,), (1,)), ((), ())), preferred_element_type=F32)
        p = jnp.exp(s - jnp.max(s, axis=-1, keepdims=True))
        l = jnp.sum(p, axis=-1, keepdims=True)
        o = jnp.dot(p.astype(BF16), v_ref[:, cols], preferred_element_type=F32)
        outs.append(o * (1.0 / l))
    o = jnp.concatenate(outs, axis=1).astype(BF16)
    xa = jnp.dot(o, wo_ref[...], preferred_element_type=F32)
    o_ref[...] = _layer_norm(DN_ALPHA * h + xa, g_ref, b_ref)


def _cross_attention_ln(h, kv_mem, w_q, w_o, ln_g, ln_b, batch, seq, *, tm):
    tokens, d = h.shape
    mem_len = kv_mem.shape[0] // batch
    xa_dim = XA_HEADS * XA_HEAD_DIM
    nt = seq // tm
    return pl.pallas_call(
        _xa_kernel,
        grid=(batch, nt),
        in_specs=[
            pl.BlockSpec((tm, d), lambda b, i: (b * nt + i, 0)),
            pl.BlockSpec((d, xa_dim), lambda b, i: (0, 0)),
            pl.BlockSpec((mem_len, xa_dim), lambda b, i: (b, 0)),
            pl.BlockSpec((mem_len, xa_dim), lambda b, i: (b, 1)),
            pl.BlockSpec((xa_dim, d), lambda b, i: (0, 0)),
            pl.BlockSpec((1, d), lambda b, i: (0, 0)),
            pl.BlockSpec((1, d), lambda b, i: (0, 0)),
        ],
        out_specs=pl.BlockSpec((tm, d), lambda b, i: (b * nt + i, 0)),
        out_shape=jax.ShapeDtypeStruct((tokens, d), F32),
        compiler_params=_params(2),
        name="cross_attention_ln2",
    )(h, w_q, kv_mem, kv_mem, w_o, ln_g, ln_b)


def _mlp_kernel(h_ref, w1_ref, w2_ref, g_ref, b_ref, o_ref, hb_ref):
    j = pl.program_id(1)

    @pl.when(j == 0)
    def _():
        hb_ref[...] = h_ref[...].astype(BF16)
        o_ref[...] = jnp.zeros_like(o_ref)

    rows = hb_ref.shape[0] // ROW_SPLITS
    subs = [slice(r * rows, (r + 1) * rows) for r in range(ROW_SPLITS)]
    ups = [jnp.dot(hb_ref[sl, :], w1_ref[...], preferred_element_type=F32) for sl in subs]
    for sl, u in zip(subs, ups):
        u = jnp.square(jnp.maximum(u, 0.0)).astype(BF16)
        o_ref[sl, :] += jnp.dot(u, w2_ref[...], preferred_element_type=F32)

    @pl.when(j == pl.num_programs(1) - 1)
    def _():
        o_ref[...] = _layer_norm(DN_ALPHA * h_ref[...] + o_ref[...], g_ref, b_ref)


def _mlp_ln(h, w1, w2, ln_g, ln_b, *, tm, tf):
    tokens, d = h.shape
    d_ff = w1.shape[1]
    return pl.pallas_call(
        _mlp_kernel,
        grid=(tokens // tm, d_ff // tf),
        in_specs=[
            pl.BlockSpec((tm, d), lambda i, j: (i, 0)),
            pl.BlockSpec((d, tf), lambda i, j: (0, j)),
            pl.BlockSpec((tf, d), lambda i, j: (j, 0)),
            pl.BlockSpec((1, d), lambda i, j: (0, 0)),
            pl.BlockSpec((1, d), lambda i, j: (0, 0)),
        ],
        out_specs=pl.BlockSpec((tm, d), lambda i, j: (i, 0)),
        out_shape=jax.ShapeDtypeStruct((tokens, d), F32),
        scratch_shapes=[pltpu.VMEM((tm, d), BF16)],
        compiler_params=_params(2),
        name="mlp_ln3",
    )(h, w1, w2, ln_g, ln_b)


def _overlap_t(seq):
    ncp, nsel = seq // CMP_STRIDE, seq // SEL_BLOCK
    c_start = np.arange(ncp)[None, :] * CMP_STRIDE
    s_start = np.arange(nsel)[:, None] * SEL_BLOCK
    ov = (c_start < s_start + SEL_BLOCK) & (c_start + CMP_BLOCK > s_start) & (np.arange(ncp)[None, :] < ncp - 1)
    return jnp.asarray(ov, BF16)


def kernel(x, mem, w_in, attn_sinks, rel_bias_table, cmp_pos_k, cmp_w1_k, cmp_w2_k, cmp_pos_v, cmp_w1_v, cmp_w2_v,
           w_branch_swa, w_branch_nsa, w_mix_out, ln1_g, ln1_b, xa_w_q, xa_w_kv, xa_w_o, ln2_g, ln2_b,
           mlp_w1, mlp_w2, ln3_g, ln3_b):
    batch, seq, d = x.shape
    tokens = batch * seq
    nq = seq // BLOCK_Q
    assert w_in.shape[0] == 1, "one layer"
    assert seq % (8 * BLOCK_Q) == 0 and d % 512 == 0

    w = w_in[0]
    sizes = (SWA_HEADS * HEAD_DIM, 128, 128, NSA_HEADS * HEAD_DIM, 128, 128, 128, 128, 128, 128, 3 * NSA_HEADS, d, d)
    offs = np.concatenate([[0], np.cumsum(sizes)])
    (w_qa, w_ka, w_va, w_qb, w_kc, w_vc, w_ks, w_vs, w_kw, w_vw, w_gn, w_ga, w_gb) = [
        w[:, offs[n]:offs[n + 1]] for n in range(len(sizes))]
    w_rows = jnp.concatenate([w_ka, w_kc, w_ks, w_kw, w_vc], axis=1).astype(BF16)
    w_gn_pad = jnp.pad(w_gn, ((0, 0), (0, 128 - w_gn.shape[1])))
    w_cols_t = jnp.concatenate([w_qa, w_qb, w_va, w_vs, w_vw, w_gn_pad], axis=1).T.astype(BF16)

    x2 = x.reshape(tokens, d)

    h_t, x_bf, kk = _input_projections(x2, w_cols_t, w_rows, tm=512)

    def value_tiles(row0, width):
        v = h_t[row0:row0 + 128].reshape(GROUPS, HEAD_DIM, batch, seq // width, width).transpose(0, 2, 3, 1, 4)
        ones = jnp.ones(v.shape[:3] + (V_AUG_ROWS - HEAD_DIM, width), BF16)
        return jnp.concatenate([v, ones], axis=3)

    vswa_aug = value_tiles(2048, BLOCK_Q)
    vsel_aug, vwin_aug = value_tiles(2176, 2 * BLOCK_Q), value_tiles(2304, 2 * BLOCK_Q)
    gates_t = h_t[2432:2432 + 3 * NSA_HEADS].reshape(GROUPS, HEADS_PER_GROUP, 3, tokens).transpose(0, 2, 1, 3)
    block_of_key = (jnp.arange(tokens) % BLOCK_Q) // SEL_BLOCK
    onehot = (block_of_key[:, None] == jnp.arange(SEL_ROWS)[None, :]).astype(BF16)
    ksel_aug = jnp.concatenate([kk[2], jnp.broadcast_to(onehot, (GROUPS, tokens, SEL_ROWS))], axis=2)

    by_dist = rel_bias_table[_rel_bucket(jnp.arange(seq))].astype(F32).T
    vb_swa = _bias_by_distance(by_dist[:SWA_HEADS], 4, window=SWA_WINDOW)
    vb_sel = _bias_by_distance(by_dist[SWA_HEADS:], nq + 2)
    vb_win = _bias_by_distance(by_dist[SWA_HEADS:], N_WIN_SLOTS + 1, window=NSA_WINDOW)
    vb_cmp = vb_sel
    sink_rows = jnp.repeat(attn_sinks[0].astype(F32), BLOCK_Q).reshape(GROUPS, 1, GROUP_LANES)

    y_a = _swa_attention(h_t, kk, vswa_aug, sink_rows, vb_swa, batch, seq)

    ncp = seq // CMP_STRIDE
    chunk_w = CMP_STRIDE * HEAD_DIM
    chunks = kk.reshape(kk.shape[0], GROUPS, batch, ncp, chunk_w)
    pos = jnp.stack([cmp_pos_k[0], cmp_pos_v[0]]).astype(F32).reshape(2, 2, 1, chunk_w)
    w1 = jnp.stack([cmp_w1_k[0], cmp_w1_v[0]]).astype(BF16).reshape(2, 2, chunk_w, -1)
    w2 = jnp.stack([cmp_w2_k[0], cmp_w2_v[0]]).astype(BF16)
    cn, ct = _compress(chunks, (1, 4), pos, w1, w2, w2.transpose(0, 2, 1))
    oc_t, sel = _cmp_attention(h_t, cn, ct, vb_cmp, _overlap_t(seq), batch, seq)
    y_b = _selwin_attention(h_t, ksel_aug, kk, vsel_aug, vwin_aug, sel, oc_t, gates_t, vb_sel, vb_win, batch, seq)

    merged = _merge(x_bf, y_a, y_b, w_ga.astype(BF16), w_gb.astype(BF16),
                    w_branch_swa[0].astype(BF16), w_branch_nsa[0].astype(BF16), tm=min(1024, tokens), tn=512)
    h1 = _mix_ln(merged, x2, w_mix_out[0].astype(BF16), ln1_g, ln1_b, tm=512)

    mem_bf = mem.reshape(-1, d).astype(BF16)
    kv_mem = _matmul(mem_bf, xa_w_kv[0].astype(BF16), nt=False, tm=mem_bf.shape[0], tn=512)
    h2 = _cross_attention_ln(h1, kv_mem, xa_w_q[0].astype(BF16), xa_w_o[0].astype(BF16), ln2_g, ln2_b,
                             batch, seq, tm=512)

    h3 = _mlp_ln(h2, mlp_w1[0].astype(BF16), mlp_w2[0].astype(BF16), ln3_g, ln3_b, tm=1024, tf=512)
    return h3.reshape(batch, seq, d)
```

```python
import functools
import math

import numpy as np
import jax
import jax.numpy as jnp
from jax import lax
from jax.experimental import pallas as pl
from jax.experimental.pallas import tpu as pltpu

F32 = jnp.float32
BF16 = jnp.bfloat16

HEAD_DIM = 64
BLOCK_Q = 128
SWA_HEADS = 16
SWA_KV_HEADS = 2
SWA_WINDOW = 128
NSA_HEADS = 16
NSA_KV_HEADS = 2
CMP_BLOCK = 32
CMP_STRIDE = 16
SEL_BLOCK = 64
SEL_TOPK = 16
SEL_LOCAL = 2
NSA_WINDOW = 512
REL_BUCKETS = 32
REL_MAX_DIST = 4096
XA_HEADS = 4
XA_HEAD_DIM = 128
DEPTH = 1
DN_ALPHA = (2.0 * DEPTH) ** 0.25
LN_EPS = 1e-5
NEG_INF = -1e30
FORCE_SCORE = 1e4

GROUPS = 2
HEADS_PER_GROUP = 8
GROUP_LANES = HEADS_PER_GROUP * BLOCK_Q
GROUP_COLS = HEADS_PER_GROUP * HEAD_DIM
SUBTILES = 2

V7X_VMEM_LIMIT_BYTES = 56 * 1024 * 1024


def _params(n_axes):
    return pltpu.CompilerParams(dimension_semantics=("arbitrary",) * n_axes,
                                vmem_limit_bytes=V7X_VMEM_LIMIT_BYTES)


def _mm_kernel(a_ref, b_ref, o_ref, *, nt):
    if nt:
        out = lax.dot_general(a_ref[...], b_ref[...], (((1,), (1,)), ((), ())), preferred_element_type=F32)
    else:
        out = jnp.dot(a_ref[...], b_ref[...], preferred_element_type=F32)
    o_ref[...] = out.astype(o_ref.dtype)


def _matmul(a, b, *, nt, tm, tn, out_dtype=BF16):
    m, k = a.shape
    n = b.shape[0] if nt else b.shape[1]
    b_spec = pl.BlockSpec((tn, k), lambda i, j: (j, 0)) if nt else pl.BlockSpec((k, tn), lambda i, j: (0, j))
    return pl.pallas_call(
        functools.partial(_mm_kernel, nt=nt),
        grid=(m // tm, n // tn),
        in_specs=[pl.BlockSpec((tm, k), lambda i, j: (i, 0)), b_spec],
        out_specs=pl.BlockSpec((tm, tn), lambda i, j: (i, j)),
        out_shape=jax.ShapeDtypeStruct((m, n), out_dtype),
        compiler_params=_params(2),
        name="proj_nt" if nt else "proj_nn",
    )(a, b)


def _in_proj_kernel(x_ref, wt_ref, wr_ref, ht_ref, xb_ref, kk_ref):
    xb = x_ref[...].astype(BF16)
    xb_ref[...] = xb
    ht_ref[...] = lax.dot_general(wt_ref[...], xb, (((1,), (1,)), ((), ())),
                                  preferred_element_type=F32).astype(ht_ref.dtype)
    rows = jnp.dot(xb, wr_ref[...], preferred_element_type=F32)
    for c in range(kk_ref.shape[0]):
        for g in range(GROUPS):
            col = (c * GROUPS + g) * HEAD_DIM
            kk_ref[c, g] = rows[:, col:col + HEAD_DIM].astype(kk_ref.dtype)


def _input_projections(x, w_cols_t, w_rows, *, tm):
    tokens, d = x.shape
    cols = w_cols_t.shape[0]
    n_rows = w_rows.shape[1] // (GROUPS * HEAD_DIM)
    return pl.pallas_call(
        _in_proj_kernel,
        grid=(tokens // tm,),
        in_specs=[
            pl.BlockSpec((tm, d), lambda i: (i, 0)),
            pl.BlockSpec((cols, d), lambda i: (0, 0)),
            pl.BlockSpec((d, w_rows.shape[1]), lambda i: (0, 0)),
        ],
        out_specs=[
            pl.BlockSpec((cols, tm), lambda i: (0, i)),
            pl.BlockSpec((tm, d), lambda i: (i, 0)),
            pl.BlockSpec((n_rows, GROUPS, tm, HEAD_DIM), lambda i: (0, 0, i, 0)),
        ],
        out_shape=[jax.ShapeDtypeStruct((cols, tokens), BF16),
                   jax.ShapeDtypeStruct((tokens, d), BF16),
                   jax.ShapeDtypeStruct((n_rows, GROUPS, tokens, HEAD_DIM), BF16)],
        compiler_params=_params(1),
        name="input_projections",
    )(x, w_cols_t, w_rows)


def _rel_bucket(dist):
    exact = REL_BUCKETS // 2
    d = jnp.maximum(dist, 0)
    log_ratio = jnp.log(jnp.maximum(d, 1).astype(F32) / exact) / math.log(REL_MAX_DIST / exact)
    large = jnp.minimum(exact + (log_ratio * (REL_BUCKETS - exact)).astype(jnp.int32), REL_BUCKETS - 1)
    return jnp.where(d < exact, d, large)


def _bias_by_distance(by_dist, rows, *, window=None):
    heads, seq = by_dist.shape
    if window is not None:
        by_dist = jnp.where(jnp.arange(seq) < window, by_dist, NEG_INF)
    padded = jnp.pad(by_dist, ((0, 0), (2 * BLOCK_Q, 0)), constant_values=NEG_INF)[:, :rows * BLOCK_Q]
    return padded.reshape(heads // HEADS_PER_GROUP, HEADS_PER_GROUP, rows, BLOCK_Q)


def _build_skew_table(vb_ref, tbl_ref, n_tiles):
    def body(d, carry):
        for h in range(HEADS_PER_GROUP):
            lo = jnp.broadcast_to(vb_ref[h, pl.ds(d, 1), :], (BLOCK_Q, BLOCK_Q))
            hi = jnp.broadcast_to(vb_ref[h, pl.ds(d + 1, 1), :], (BLOCK_Q, BLOCK_Q))
            y = pltpu.roll(jnp.concatenate([lo, hi], axis=1), 0, 1, stride=1, stride_axis=0)
            tbl_ref[d, :, h * BLOCK_Q:(h + 1) * BLOCK_Q] = y[:, BLOCK_Q:].astype(tbl_ref.dtype)
        return carry

    lax.fori_loop(0, n_tiles, body, 0)


def _build_cmp_bias_table(vb_ref, tbl_ref, skew_ref, nq):
    per_tile = BLOCK_Q // CMP_STRIDE
    assert per_tile == 8 and CMP_BLOCK - 1 + CMP_STRIDE * (per_tile - 1) - BLOCK_Q == CMP_STRIDE - 1
    tbl_ref[pl.ds(nq * 8, nq * 8), :] = jnp.full((nq * 8, GROUP_LANES), NEG_INF, F32)

    def body(d, last_row):
        for h in range(HEADS_PER_GROUP):
            lo = jnp.broadcast_to(vb_ref[h, pl.ds(d + 1, 1), :], (BLOCK_Q, BLOCK_Q))
            hi = jnp.broadcast_to(vb_ref[h, pl.ds(d + 2, 1), :], (BLOCK_Q, BLOCK_Q))
            y = pltpu.roll(jnp.concatenate([lo, hi], axis=1), 0, 1, stride=1, stride_axis=0)
            skew_ref[:, h * BLOCK_Q:(h + 1) * BLOCK_Q] = y[:, BLOCK_Q:]
        offsets = [CMP_BLOCK - 1 + CMP_STRIDE * c for c in range(per_tile - 1)]
        rows = [skew_ref[k:k + 1, :] for k in offsets] + [last_row]
        tbl_ref[pl.ds(pl.multiple_of((nq - 1 - d) * 8, 8), 8), :] = jnp.concatenate(rows, axis=0)
        return skew_ref[CMP_STRIDE - 1:CMP_STRIDE, :]

    lax.fori_loop(0, nq, body, jnp.full((1, GROUP_LANES), NEG_INF, F32))


def _gather_heads_to_lanes(q_ref):
    q = jnp.concatenate([q_ref[h * HEAD_DIM:(h + 1) * HEAD_DIM, u * BLOCK_Q:(u + 1) * BLOCK_Q]
                         for u in range(q_ref.shape[1] // BLOCK_Q) for h in range(HEADS_PER_GROUP)], axis=1)
    return q * jnp.asarray(HEAD_DIM ** -0.5, q.dtype)


def _heads_to_columns(o_t):
    stacked = jnp.concatenate([o_t[:, h * BLOCK_Q:(h + 1) * BLOCK_Q] for h in range(HEADS_PER_GROUP)], axis=0)
    return stacked.T


def _identity_tile():
    r = lax.broadcasted_iota(jnp.int32, (BLOCK_Q, BLOCK_Q), 0)
    c = lax.broadcasted_iota(jnp.int32, (BLOCK_Q, BLOCK_Q), 1)
    return jnp.where(r == c, 1.0, 0.0).astype(BF16)


def _biased_scores(eye, k_t, bias_tile, q_t):
    lhs = jnp.concatenate([eye, k_t], axis=1)
    rhs = jnp.concatenate([bias_tile, q_t], axis=0)
    return jnp.dot(lhs, rhs, preferred_element_type=F32)


def _softmax_probs(m, s_tiles):
    ms, alphas, ps = [], [], [[] for _ in s_tiles]
    for h in range(m.shape[1] // BLOCK_Q):
        lanes = slice(h * BLOCK_Q, (h + 1) * BLOCK_Q)
        cols = [s[:, lanes] for s in s_tiles]
        m_new = m[:, lanes]
        for c in cols:
            m_new = jnp.maximum(m_new, jnp.max(c, axis=0, keepdims=True))
        ms.append(m_new)
        alphas.append(jnp.exp(m[:, lanes] - m_new))
        for j, c in enumerate(cols):
            ps[j].append(jnp.exp((c - m_new).astype(BF16)))
    p = jnp.concatenate([jnp.concatenate(pj, axis=1) for pj in ps], axis=0)
    return jnp.concatenate(ms, axis=1), jnp.concatenate(alphas, axis=1), p


def _softmax_update(carry, s_tiles, v_aug):
    m, acc = carry
    m, alpha, p = _softmax_probs(m, s_tiles)
    return m, alpha * acc + jnp.dot(v_aug, p, preferred_element_type=F32)


def _pipelined_attention(n_steps, scores_fn, values_fn, init, s_scr, p_scr, acc_scr, *, fill):
    m0, acc0 = init

    def put_scores(n):
        for j, s in enumerate(scores_fn(n)):
            s_scr[j] = s

    def fold(alpha, n, p):
        acc_scr[...] = alpha * acc_scr[...] + jnp.dot(values_fn(n), p, preferred_element_type=F32)

    if fill:
        put_scores(0)
        p_scr[...] = jnp.zeros_like(p_scr)
        acc_scr[...] = acc0
        return None

    def body(k, carry):
        m, alpha_a, alpha_b = carry
        a = 2 * k
        s_b = scores_fn(a + 1)
        fold(alpha_a, a - 2, p_scr[0])
        fold(alpha_b, a - 1, p_scr[1])
        m, alpha_a, p_a = _softmax_probs(m, [s_scr.at[j] for j in range(s_scr.shape[0])])
        put_scores(a + 2)
        m, alpha_b, p_b = _softmax_probs(m, s_b)
        p_scr[0] = p_a
        p_scr[1] = p_b
        return m, alpha_a, alpha_b

    trips = (n_steps + 1) // 2
    ones = jnp.ones_like(m0)
    m, alpha_a, alpha_b = lax.fori_loop(0, trips, body, (m0, ones, ones))
    fold(alpha_a, 2 * trips - 2, p_scr[0])
    fold(alpha_b, 2 * trips - 1, p_scr[1])
    return m, acc_scr[...]


SLAB = 2 * BLOCK_Q
LOOKAHEAD = 12


def _slabwise_softmax_step(m, acc_ref, lhs_tiles, rhs_fn, v_fn, masked_fn=None):
    per_slab = [[j for j in range(len(lhs_tiles)) if masked_fn is None or not masked_fn(j, sl)]
                for sl in range(m.shape[1] // SLAB)]
    items = [(sl, j) for sl, tiles in enumerate(per_slab) for j in tiles]

    def scores(k):
        sl, j = items[k]
        lhs = lhs_tiles[j](sl) if callable(lhs_tiles[j]) else lhs_tiles[j]
        return jnp.dot(lhs, rhs_fn(j, sl), preferred_element_type=F32)

    pending = [scores(k) for k in range(min(LOOKAHEAD, len(items)))]
    issued = len(pending)
    new_m = []
    for sl, tiles in enumerate(per_slab):
        lanes = slice(sl * SLAB, (sl + 1) * SLAB)
        m_s, acc_s = m[:, lanes], acc_ref[:, lanes]
        for c in range(0, len(tiles), 2):
            chunk = tiles[c:c + 2]
            m_old, ps = m_s, []
            for j in chunk:
                s = pending.pop(0)
                if issued < len(items):
                    pending.append(scores(issued))
                    issued += 1
                m_next = jnp.maximum(m_s, jnp.max(s, axis=0, keepdims=True))
                ps = [q * jnp.exp(m_s - m_next).astype(BF16) for q in ps]
                ps.append(jnp.exp((s - m_next).astype(BF16)))
                m_s = m_next
            v = jnp.concatenate([v_fn(j, sl) for j in chunk], axis=1)
            acc_s = jnp.exp(m_old - m_s) * acc_s + jnp.dot(v, jnp.concatenate(ps, axis=0),
                                                           preferred_element_type=F32)
        acc_ref[:, lanes] = acc_s
        new_m.append(m_s)
    return jnp.concatenate(new_m, axis=1)


V_AUG_ROWS = HEAD_DIM + 16


def _normalized(acc):
    return acc[:HEAD_DIM] * (1.0 / acc[HEAD_DIM:HEAD_DIM + 1])


def _swa_kernel(q_ref, kprev_ref, kcur_ref, vprev_ref, vcur_ref, sink_ref, vb_ref, o_ref, tbl_ref, acc_ref):
    b, i = pl.program_id(1), pl.program_id(2)

    @pl.when((b == 0) & (i == 0))
    def _():
        _build_skew_table(vb_ref, tbl_ref, 3)

    eye = _identity_tile()
    q_t = _gather_heads_to_lanes(q_ref)
    lanes = q_t.shape[1]
    slabs_per_tile = GROUP_LANES // SLAB
    m0 = jnp.concatenate([sink_ref[...]] * SWA_TILES, axis=1)
    acc_ref[...] = jnp.where(lax.broadcasted_iota(jnp.int32, (V_AUG_ROWS, lanes), 0) < HEAD_DIM, 0.0, 1.0)

    def key_tile(j):
        def at(slab):
            u = slab // slabs_per_tile - j
            k_t = kprev_ref[...] if u < 0 else kcur_ref[u * BLOCK_Q:(u + 1) * BLOCK_Q, :]
            return jnp.concatenate([eye, k_t], axis=1)
        return at

    def rhs(j, slab):
        u, part = slab // slabs_per_tile, slab % slabs_per_tile
        slot = 1 + j if (u > 0 or j == 0) else jnp.where(i > 0, 2, 0)
        return jnp.concatenate([tbl_ref[slot, :, part * SLAB:(part + 1) * SLAB],
                                q_t[:, slab * SLAB:(slab + 1) * SLAB]], axis=0)

    def values(j, slab):
        u = slab // slabs_per_tile - j
        return vprev_ref[...] if u < 0 else vcur_ref[u]

    _slabwise_softmax_step(m0, acc_ref, [key_tile(0), key_tile(1)], rhs, values)
    out = _normalized(acc_ref[...])
    for u in range(SWA_TILES):
        o_ref[u * BLOCK_Q:(u + 1) * BLOCK_Q, :] = _heads_to_columns(
            out[:, u * GROUP_LANES:(u + 1) * GROUP_LANES]).astype(o_ref.dtype)


SWA_TILES = 4


def _swa_attention(h_t, kk, v_aug, sink_rows, vb, batch, seq):
    nq = seq // BLOCK_Q
    nblk = nq // SWA_TILES
    width = SWA_TILES * BLOCK_Q
    tokens = batch * seq

    def prev_tile(i):
        return jnp.maximum(i * SWA_TILES - 1, 0)

    return pl.pallas_call(
        _swa_kernel,
        grid=(GROUPS, batch, nblk),
        in_specs=[
            pl.BlockSpec((GROUP_COLS, width), lambda g, b, i: (g, b * nblk + i)),
            pl.BlockSpec((None, None, BLOCK_Q, HEAD_DIM), lambda g, b, i: (0, g, b * nq + prev_tile(i), 0)),
            pl.BlockSpec((None, None, width, HEAD_DIM), lambda g, b, i: (0, g, b * nblk + i, 0)),
            pl.BlockSpec((None, None, None, V_AUG_ROWS, BLOCK_Q), lambda g, b, i: (g, b, prev_tile(i), 0, 0)),
            pl.BlockSpec((None, None, SWA_TILES, V_AUG_ROWS, BLOCK_Q), lambda g, b, i: (g, b, i, 0, 0)),
            pl.BlockSpec((None, 1, GROUP_LANES), lambda g, b, i: (g, 0, 0)),
            pl.BlockSpec((None, HEADS_PER_GROUP, 4, BLOCK_Q), lambda g, b, i: (g, 0, 0, 0)),
        ],
        out_specs=pl.BlockSpec((width, GROUP_COLS), lambda g, b, i: (b * nblk + i, g)),
        out_shape=jax.ShapeDtypeStruct((tokens, SWA_HEADS * HEAD_DIM), BF16),
        scratch_shapes=[pltpu.VMEM((3, BLOCK_Q, GROUP_LANES), BF16),
                        pltpu.VMEM((V_AUG_ROWS, SWA_TILES * GROUP_LANES), F32)],
        compiler_params=_params(3),
        name="swa_attention",
    )(h_t, kk, kk, v_aug, v_aug, sink_rows, vb)


def _compress_kernel(c_ref, pos_ref, w1_ref, w2_ref, w2t_ref, cn_ref, ct_ref):
    c = c_ref[...].astype(F32)
    top = (c + pos_ref[0]).astype(BF16)
    bot = (c + pos_ref[1]).astype(BF16)
    a = jnp.dot(top, w1_ref[0], preferred_element_type=F32)
    bm = jnp.dot(bot, w1_ref[1], preferred_element_type=F32)
    n = a.shape[0]
    pre = a + pltpu.roll(bm, n - 1, 0)
    hid = jax.nn.gelu(pre).astype(BF16)
    cn_ref[...] = jnp.dot(hid, w2_ref[...], preferred_element_type=F32).astype(cn_ref.dtype)
    ct_ref[...] = lax.dot_general(w2t_ref[...], hid, (((1,), (1,)), ((), ())),
                                  preferred_element_type=F32).astype(ct_ref.dtype)


def _compress(chunks, pos, w1, w2, w2t):
    _, g, b, ncp, width = chunks.shape
    hidden = w1.shape[-1]
    return pl.pallas_call(
        _compress_kernel,
        grid=(2, g, b),
        in_specs=[
            pl.BlockSpec((None, None, None, ncp, width), lambda t, g, b: (t, g, b, 0, 0)),
            pl.BlockSpec((None, 2, 1, width), lambda t, g, b: (t, 0, 0, 0)),
            pl.BlockSpec((None, 2, width, hidden), lambda t, g, b: (t, 0, 0, 0)),
            pl.BlockSpec((None, hidden, HEAD_DIM), lambda t, g, b: (t, 0, 0)),
            pl.BlockSpec((None, HEAD_DIM, hidden), lambda t, g, b: (t, 0, 0)),
        ],
        out_specs=[
            pl.BlockSpec((None, None, None, ncp, HEAD_DIM), lambda t, g, b: (t, g, b, 0, 0)),
            pl.BlockSpec((None, None, None, HEAD_DIM, ncp), lambda t, g, b: (t, g, b, 0, 0)),
        ],
        out_shape=[jax.ShapeDtypeStruct((2, g, b, ncp, HEAD_DIM), BF16),
                   jax.ShapeDtypeStruct((2, g, b, HEAD_DIM, ncp), BF16)],
        compiler_params=_params(3),
        name="nsa_compress",
    )(chunks, pos, w1, w2, w2t)


def _cmp_kernel(q_ref, kc_ref, vct_ref, vb_ref, ov_ref, oc_ref, sel_ref, bias_ref, skew_ref, bias16_ref, *, nq):
    b, i = pl.program_id(1), pl.program_id(2)

    @pl.when((b == 0) & (i == 0))
    def _():
        _build_cmp_bias_table(vb_ref, bias_ref, skew_ref, nq)
        rows = bias_ref.shape[0]
        bias16_ref[0] = bias_ref[...].astype(BF16)
        bias16_ref[1, 0:rows - 16, :] = bias_ref[8:rows - 8, :].astype(BF16)

    ncp = kc_ref.shape[0]
    nsel = sel_ref.shape[0]
    n_tiles = ncp // BLOCK_Q
    q_t = _gather_heads_to_lanes(q_ref)
    width = SUBTILES * BLOCK_Q
    eye = _identity_tile()

    slabs_per_tile = GROUP_LANES // SLAB
    lhs = [jnp.concatenate([eye, kc_ref[t * BLOCK_Q:(t + 1) * BLOCK_Q, :]], axis=1) for t in range(n_tiles)]
    items = [(sl, t) for sl in range(SUBTILES * slabs_per_tile) for t in range(n_tiles)]

    def scores(k):
        sl, t = items[k]
        u, part = sl // slabs_per_tile, sl % slabs_per_tile
        shifted = (nq - 1 - u) % 2
        row0 = pl.multiple_of((nq - 1 - (SUBTILES * i + u)) * 8 + t * BLOCK_Q - 8 * shifted, 16)
        bias = bias16_ref[shifted, pl.ds(row0, BLOCK_Q), part * SLAB:(part + 1) * SLAB]
        return jnp.dot(lhs[t], jnp.concatenate([bias, q_t[:, sl * SLAB:(sl + 1) * SLAB]], axis=0),
                       preferred_element_type=F32)

    q_pos = i * width + lax.broadcasted_iota(jnp.int32, (1, width), 1)
    sees_any = q_pos >= CMP_BLOCK - 1
    psum = [[jnp.zeros((BLOCK_Q, BLOCK_Q), F32) for _ in range(n_tiles)] for _ in range(SUBTILES)]
    pending = [scores(k) for k in range(min(LOOKAHEAD, len(items)))]
    for sl in range(SUBTILES * slabs_per_tile):
        u, part = sl // slabs_per_tile, sl % slabs_per_tile
        s = [pending.pop(0) for _ in range(n_tiles)]
        for k in range(sl * n_tiles + LOOKAHEAD, min((sl + 1) * n_tiles + LOOKAHEAD, len(items))):
            pending.append(scores(k))
        m = functools.reduce(jnp.maximum, [jnp.max(c, axis=0, keepdims=True) for c in s])
        e = [jnp.exp(c - m) for c in s]
        l = functools.reduce(jnp.add, [jnp.sum(c, axis=0, keepdims=True) for c in e])
        seen = sees_any[:, u * BLOCK_Q:(u + 1) * BLOCK_Q]
        inv = jnp.where(jnp.concatenate([seen] * (SLAB // BLOCK_Q), axis=1), 1.0 / l, 0.0)
        o_slab = jnp.zeros((HEAD_DIM, SLAB), F32)
        for t in range(n_tiles):
            p = e[t] * inv
            psum[u][t] = psum[u][t] + functools.reduce(
                jnp.add, [p[:, r * BLOCK_Q:(r + 1) * BLOCK_Q] for r in range(SLAB // BLOCK_Q)])
            o_slab = o_slab + jnp.dot(vct_ref[:, t * BLOCK_Q:(t + 1) * BLOCK_Q], p.astype(BF16),
                                      preferred_element_type=F32)
        oc_ref[u, :, part * SLAB:(part + 1) * SLAB] = o_slab.astype(oc_ref.dtype)

    psum = jnp.concatenate([jnp.concatenate(pu, axis=0) for pu in psum], axis=1)
    hi = psum.astype(BF16)
    lo = (psum - hi.astype(F32)).astype(BF16)
    ov = ov_ref[...]
    score = jnp.dot(ov, hi, preferred_element_type=F32) + jnp.dot(ov, lo, preferred_element_type=F32)

    j_io = lax.broadcasted_iota(jnp.int32, (nsel, width), 0)
    qpos = i * width + lax.broadcasted_iota(jnp.int32, (nsel, width), 1)
    causal = j_io * SEL_BLOCK <= qpos
    back = qpos // SEL_BLOCK - j_io
    forced = (j_io == 0) | ((back >= 0) & (back < SEL_LOCAL))
    score = jnp.where(causal, jnp.where(forced, FORCE_SCORE, score), -1.0)
    slab_rows = lax.broadcasted_iota(jnp.int32, (8, width), 0)
    slabs = [score[8 * g:8 * (g + 1), :] for g in range(nsel // 8)]
    ranks = [jnp.zeros((8, width), F32) for _ in slabs]
    for r in range(nsel):
        row = jnp.broadcast_to(score[r:r + 1, :], (8, width))
        for g, slab in enumerate(slabs):
            if g > r // 8:
                ahead = row >= slab
            elif g < r // 8:
                ahead = row > slab
            else:
                ranks[g] = ranks[g] + jnp.where(slab_rows > r % 8, jnp.where(row >= slab, 1.0, 0.0),
                                                jnp.where(row > slab, 1.0, 0.0))
                continue
            ranks[g] = ranks[g] + jnp.where(ahead, 1.0, 0.0)
    rank = jnp.concatenate(ranks, axis=0)
    sel_ref[...] = jnp.where((rank < min(SEL_TOPK, nsel)) & causal, 1.0, 0.0).astype(sel_ref.dtype)


def _cmp_attention(h_t, cn, ct, vb, overlap_t, batch, seq):
    nq = seq // BLOCK_Q
    ncp = seq // CMP_STRIDE
    nsel = seq // SEL_BLOCK
    q_blk0 = SWA_HEADS * HEAD_DIM // GROUP_COLS
    nblk = nq // SUBTILES
    width = SUBTILES * BLOCK_Q
    return pl.pallas_call(
        functools.partial(_cmp_kernel, nq=nq),
        grid=(GROUPS, batch, nblk),
        in_specs=[
            pl.BlockSpec((GROUP_COLS, width), lambda g, b, i: (q_blk0 + g, b * nblk + i)),
            pl.BlockSpec((None, None, None, ncp, HEAD_DIM), lambda g, b, i: (0, g, b, 0, 0)),
            pl.BlockSpec((None, None, None, HEAD_DIM, ncp), lambda g, b, i: (1, g, b, 0, 0)),
            pl.BlockSpec((None, HEADS_PER_GROUP, nq + 2, BLOCK_Q), lambda g, b, i: (g, 0, 0, 0)),
            pl.BlockSpec((nsel, ncp), lambda g, b, i: (0, 0)),
        ],
        out_specs=[
            pl.BlockSpec((None, None, SUBTILES, HEAD_DIM, GROUP_LANES), lambda g, b, i: (b, g, i, 0, 0)),
            pl.BlockSpec((None, None, nsel, width), lambda g, b, i: (b, g, 0, i)),
        ],
        out_shape=[jax.ShapeDtypeStruct((batch, GROUPS, nq, HEAD_DIM, GROUP_LANES), BF16),
                   jax.ShapeDtypeStruct((batch, GROUPS, nsel, seq), F32)],
        scratch_shapes=[pltpu.VMEM((2 * nq * 8, GROUP_LANES), F32), pltpu.VMEM((BLOCK_Q, GROUP_LANES), F32),
                        pltpu.VMEM((2, 2 * nq * 8, GROUP_LANES), BF16)],
        compiler_params=_params(3),
        name="nsa_cmp_select",
    )(h_t, cn, ct, vb, overlap_t)


N_WIN_PAIRS = -(-(NSA_WINDOW - 1) // (SUBTILES * BLOCK_Q)) + 1
N_WIN_SLOTS = SUBTILES * N_WIN_PAIRS + 1
SEL_ROWS = 16
SEL_STEPS = 2


def _selwin_kernel(q_ref, ksel_ref, kwin_ref, vsel_ref, vwin_ref, sel_ref, oc_ref, gate_ref, vbs_ref, vbw_ref,
                   o_ref, tsel_ref, twin_ref, accs_ref, accw_ref, *, nq):
    b, blk = pl.program_id(1), pl.program_id(2)

    @pl.when((b == 0) & (blk == 0))
    def _():
        _build_skew_table(vbs_ref, tsel_ref, nq + 1)
        _build_skew_table(vbw_ref, twin_ref, N_WIN_SLOTS)

    q_t = _gather_heads_to_lanes(q_ref)
    lanes = q_t.shape[1]
    eye = _identity_tile()
    pair = SUBTILES * BLOCK_Q
    m0 = jnp.full((1, lanes), NEG_INF, F32)
    accs_ref[...] = jnp.zeros_like(accs_ref)
    accw_ref[...] = jnp.zeros_like(accw_ref)
    blocks_per_tile = BLOCK_Q // SEL_BLOCK
    slabs_per_tile = GROUP_LANES // SLAB

    def attend(m, steps, k_ref, tbl_ref, v_ref, acc_ref, q_ext_fn, reach=None):
        tiles = [(n, jnp.clip(blk - n, 0, blk), j) for n in steps for j in range(SUBTILES)]
        lhs = [jnp.concatenate([eye, k_ref[pl.ds(pl.multiple_of(p * pair + j * BLOCK_Q, BLOCK_Q), BLOCK_Q), :]],
                               axis=1) for _, p, j in tiles]
        q_exts = [q_ext_fn(p, j) for _, p, j in tiles]

        def rhs(t, slab):
            n, _, j = tiles[t]
            u, part = slab // slabs_per_tile, slab % slabs_per_tile
            slot = jnp.where(n <= blk, 2 * n + u - j + 1, 0)
            bias = tbl_ref[slot, :, part * SLAB:(part + 1) * SLAB]
            return jnp.concatenate([bias, q_exts[t][:, slab * SLAB:(slab + 1) * SLAB]], axis=0)

        def values(t, slab):
            _, p, j = tiles[t]
            return v_ref[p, :, j * BLOCK_Q:(j + 1) * BLOCK_Q]

        def masked(t, slab):
            n, _, j = tiles[t]
            distance = 2 * n + slab // slabs_per_tile - j
            return distance < 0 or distance >= reach

        return _slabwise_softmax_step(m, acc_ref, lhs, rhs, values, masked if reach is not None else None)

    attend(m0, list(range(N_WIN_PAIRS)), kwin_ref, twin_ref, vwin_ref, accw_ref, lambda p, j: q_t,
           reach=-(-(NSA_WINDOW + BLOCK_Q - 1) // BLOCK_Q))

    def sel_q_ext(p, j):
        per_pair = SUBTILES * blocks_per_tile
        group = sel_ref[pl.ds(pl.multiple_of((p // 2) * 2 * per_pair, 8), 2 * per_pair), :]
        lo, hi = (group[half * per_pair + j * blocks_per_tile:half * per_pair + (j + 1) * blocks_per_tile]
                  for half in range(2))
        neg = jnp.where(jnp.where(p % 2 == 1, hi, lo) > 0.5, 0.0, NEG_INF)
        rows = jnp.concatenate([neg[:, u * BLOCK_Q:(u + 1) * BLOCK_Q]
                                for u in range(SUBTILES) for _ in range(HEADS_PER_GROUP)], axis=1)
        rows = jnp.concatenate([rows, jnp.zeros((SEL_ROWS - blocks_per_tile, lanes), F32)], axis=0)
        return jnp.concatenate([q_t, rows.astype(BF16)], axis=0)

    def sel_body(k, m):
        return attend(m, [SEL_STEPS * k + r for r in range(SEL_STEPS)], ksel_ref, tsel_ref, vsel_ref, accs_ref,
                      sel_q_ext)

    lax.fori_loop(0, (blk + SEL_STEPS) // SEL_STEPS, sel_body, m0)

    def gate_row(branch):
        g = jnp.concatenate([gate_ref[branch, h:h + 1, u * BLOCK_Q:(u + 1) * BLOCK_Q]
                             for u in range(SUBTILES) for h in range(HEADS_PER_GROUP)], axis=1)
        return jax.nn.sigmoid(g.astype(F32))

    o_c = jnp.concatenate([oc_ref[u] for u in range(SUBTILES)], axis=1).astype(F32)
    out = (gate_row(0) * o_c + gate_row(1) * _normalized(accs_ref[...])
           + gate_row(2) * _normalized(accw_ref[...]))
    for u in range(SUBTILES):
        o_ref[u * BLOCK_Q:(u + 1) * BLOCK_Q, :] = _heads_to_columns(
            out[:, u * GROUP_LANES:(u + 1) * GROUP_LANES]).astype(o_ref.dtype)


def _selwin_attention(h_t, ksel_aug, kk, vsel_aug, vwin_aug, sel, oc_t, gates_t, vb_sel, vb_win, batch, seq):
    nq = seq // BLOCK_Q
    nblk = nq // SUBTILES
    tokens = batch * seq
    q_blk0 = SWA_HEADS * HEAD_DIM // GROUP_COLS
    pair = SUBTILES * BLOCK_Q
    lanes = SUBTILES * GROUP_LANES
    return pl.pallas_call(
        functools.partial(_selwin_kernel, nq=nq),
        grid=(GROUPS, batch, nblk),
        in_specs=[
            pl.BlockSpec((GROUP_COLS, pair), lambda g, b, i: (q_blk0 + g, b * nblk + i)),
            pl.BlockSpec((None, seq, HEAD_DIM + SEL_ROWS), lambda g, b, i: (g, b, 0)),
            pl.BlockSpec((None, None, seq, HEAD_DIM), lambda g, b, i: (3, g, b, 0)),
            pl.BlockSpec((None, None, nblk, V_AUG_ROWS, pair), lambda g, b, i: (g, b, 0, 0, 0)),
            pl.BlockSpec((None, None, nblk, V_AUG_ROWS, pair), lambda g, b, i: (g, b, 0, 0, 0)),
            pl.BlockSpec((None, None, seq // SEL_BLOCK, pair), lambda g, b, i: (b, g, 0, i)),
            pl.BlockSpec((None, None, SUBTILES, HEAD_DIM, GROUP_LANES), lambda g, b, i: (b, g, i, 0, 0)),
            pl.BlockSpec((None, 3, HEADS_PER_GROUP, pair), lambda g, b, i: (g, 0, 0, b * nblk + i)),
            pl.BlockSpec((None, HEADS_PER_GROUP, nq + 2, BLOCK_Q), lambda g, b, i: (g, 0, 0, 0)),
            pl.BlockSpec((None, HEADS_PER_GROUP, N_WIN_SLOTS + 1, BLOCK_Q), lambda g, b, i: (g, 0, 0, 0)),
        ],
        out_specs=pl.BlockSpec((pair, GROUP_COLS), lambda g, b, i: (b * nblk + i, g)),
        out_shape=jax.ShapeDtypeStruct((tokens, NSA_HEADS * HEAD_DIM), BF16),
        scratch_shapes=[pltpu.VMEM((nq + 1, BLOCK_Q, GROUP_LANES), BF16),
                        pltpu.VMEM((N_WIN_SLOTS, BLOCK_Q, GROUP_LANES), BF16),
                        pltpu.VMEM((V_AUG_ROWS, lanes), F32),
                        pltpu.VMEM((V_AUG_ROWS, lanes), F32)],
        compiler_params=_params(3),
        name="nsa_sel_win",
    )(h_t, ksel_aug, kk, vsel_aug, vwin_aug, sel, oc_t, gates_t, vb_sel, vb_win)


ROW_SPLITS = 2


def _layer_norm(y, g_ref, b_ref):
    mu = jnp.mean(y, axis=-1, keepdims=True)
    yc = y - mu
    var = jnp.mean(yc * yc, axis=-1, keepdims=True)
    return yc * lax.rsqrt(var + LN_EPS) * g_ref[...] + b_ref[...]


def _merge_kernel(x_ref, ya_ref, yb_ref, wga_ref, wgb_ref, wa_ref, wb_ref, o_ref):
    x = x_ref[...]
    ga = jax.nn.sigmoid(jnp.dot(x, wga_ref[...], preferred_element_type=F32))
    gb = jax.nn.sigmoid(jnp.dot(x, wgb_ref[...], preferred_element_type=F32))
    a = jnp.dot(ya_ref[...], wa_ref[...], preferred_element_type=F32)
    bb = jnp.dot(yb_ref[...], wb_ref[...], preferred_element_type=F32)
    o_ref[...] = (ga * a + gb * bb).astype(o_ref.dtype)


def _merge(x_bf, y_a, y_b, w_ga, w_gb, w_a, w_b, *, tm, tn):
    tokens, d = x_bf.shape
    ya_cols, yb_cols = y_a.shape[1], y_b.shape[1]
    return pl.pallas_call(
        _merge_kernel,
        grid=(tokens // tm, d // tn),
        in_specs=[
            pl.BlockSpec((tm, d), lambda i, j: (i, 0)),
            pl.BlockSpec((tm, ya_cols), lambda i, j: (i, 0)),
            pl.BlockSpec((tm, yb_cols), lambda i, j: (i, 0)),
            pl.BlockSpec((d, tn), lambda i, j: (0, j)),
            pl.BlockSpec((d, tn), lambda i, j: (0, j)),
            pl.BlockSpec((ya_cols, tn), lambda i, j: (0, j)),
            pl.BlockSpec((yb_cols, tn), lambda i, j: (0, j)),
        ],
        out_specs=pl.BlockSpec((tm, tn), lambda i, j: (i, j)),
        out_shape=jax.ShapeDtypeStruct((tokens, d), BF16),
        compiler_params=_params(2),
        name="branch_merge",
    )(x_bf, y_a, y_b, w_ga, w_gb, w_a, w_b)


def _mix_ln_kernel(m_ref, x_ref, w_ref, g_ref, b_ref, o_ref):
    rows = m_ref.shape[0] // ROW_SPLITS
    for r in range(ROW_SPLITS):
        sl = slice(r * rows, (r + 1) * rows)
        mix = jnp.dot(m_ref[sl, :], w_ref[...], preferred_element_type=F32)
        o_ref[sl, :] = _layer_norm(DN_ALPHA * x_ref[sl, :] + mix, g_ref, b_ref)


def _mix_ln(merged, x, w_mix, ln_g, ln_b, *, tm):
    tokens, d = x.shape
    return pl.pallas_call(
        _mix_ln_kernel,
        grid=(tokens // tm,),
        in_specs=[
            pl.BlockSpec((tm, d), lambda i: (i, 0)),
            pl.BlockSpec((tm, d), lambda i: (i, 0)),
            pl.BlockSpec((d, d), lambda i: (0, 0)),
            pl.BlockSpec((1, d), lambda i: (0, 0)),
            pl.BlockSpec((1, d), lambda i: (0, 0)),
        ],
        out_specs=pl.BlockSpec((tm, d), lambda i: (i, 0)),
        out_shape=jax.ShapeDtypeStruct((tokens, d), F32),
        compiler_params=_params(1),
        name="mix_out_ln1",
    )(merged, x, w_mix, ln_g, ln_b)


def _xa_kernel(h_ref, wq_ref, k_ref, v_ref, wo_ref, g_ref, b_ref, o_ref):
    h = h_ref[...]
    q = jnp.dot(h.astype(BF16), wq_ref[...], preferred_element_type=F32) * (XA_HEAD_DIM ** -0.5)
    q = q.astype(BF16)
    outs = []
    for hd in range(XA_HEADS):
        cols = slice(hd * XA_HEAD_DIM, (hd + 1) * XA_HEAD_DIM)
        s = lax.dot_general(q[:, cols], k_ref[:, cols], (((1,), (1,)), ((), ())), preferred_element_type=F32)
        p = jnp.exp(s - jnp.max(s, axis=-1, keepdims=True))
        l = jnp.sum(p, axis=-1, keepdims=True)
        o = jnp.dot(p.astype(BF16), v_ref[:, cols], preferred_element_type=F32)
        outs.append(o * (1.0 / l))
    o = jnp.concatenate(outs, axis=1).astype(BF16)
    xa = jnp.dot(o, wo_ref[...], preferred_element_type=F32)
    o_ref[...] = _layer_norm(DN_ALPHA * h + xa, g_ref, b_ref)


def _cross_attention_ln(h, kv_mem, w_q, w_o, ln_g, ln_b, batch, seq, *, tm):
    tokens, d = h.shape
    mem_len = kv_mem.shape[0] // batch
    xa_dim = XA_HEADS * XA_HEAD_DIM
    nt = seq // tm
    return pl.pallas_call(
        _xa_kernel,
        grid=(batch, nt),
        in_specs=[
            pl.BlockSpec((tm, d), lambda b, i: (b * nt + i, 0)),
            pl.BlockSpec((d, xa_dim), lambda b, i: (0, 0)),
            pl.BlockSpec((mem_len, xa_dim), lambda b, i: (b, 0)),
            pl.BlockSpec((mem_len, xa_dim), lambda b, i: (b, 1)),
            pl.BlockSpec((xa_dim, d), lambda b, i: (0, 0)),
            pl.BlockSpec((1, d), lambda b, i: (0, 0)),
            pl.BlockSpec((1, d), lambda b, i: (0, 0)),
        ],
        out_specs=pl.BlockSpec((tm, d), lambda b, i: (b * nt + i, 0)),
        out_shape=jax.ShapeDtypeStruct((tokens, d), F32),
        compiler_params=_params(2),
        name="cross_attention_ln2",
    )(h, w_q, kv_mem, kv_mem, w_o, ln_g, ln_b)


def _mlp_kernel(h_ref, w1_ref, w2_ref, g_ref, b_ref, o_ref, hb_ref):
    j = pl.program_id(1)

    @pl.when(j == 0)
    def _():
        hb_ref[...] = h_ref[...].astype(BF16)
        o_ref[...] = jnp.zeros_like(o_ref)

    rows = hb_ref.shape[0] // ROW_SPLITS
    subs = [slice(r * rows, (r + 1) * rows) for r in range(ROW_SPLITS)]
    ups = [jnp.dot(hb_ref[sl, :], w1_ref[...], preferred_element_type=F32) for sl in subs]
    for sl, u in zip(subs, ups):
        u = jnp.square(jnp.maximum(u, 0.0)).astype(BF16)
        o_ref[sl, :] += jnp.dot(u, w2_ref[...], preferred_element_type=F32)

    @pl.when(j == pl.num_programs(1) - 1)
    def _():
        o_ref[...] = _layer_norm(DN_ALPHA * h_ref[...] + o_ref[...], g_ref, b_ref)


def _mlp_ln(h, w1, w2, ln_g, ln_b, *, tm, tf):
    tokens, d = h.shape
    d_ff = w1.shape[1]
    return pl.pallas_call(
        _mlp_kernel,
        grid=(tokens // tm, d_ff // tf),
        in_specs=[
            pl.BlockSpec((tm, d), lambda i, j: (i, 0)),
            pl.BlockSpec((d, tf), lambda i, j: (0, j)),
            pl.BlockSpec((tf, d), lambda i, j: (j, 0)),
            pl.BlockSpec((1, d), lambda i, j: (0, 0)),
            pl.BlockSpec((1, d), lambda i, j: (0, 0)),
        ],
        out_specs=pl.BlockSpec((tm, d), lambda i, j: (i, 0)),
        out_shape=jax.ShapeDtypeStruct((tokens, d), F32),
        scratch_shapes=[pltpu.VMEM((tm, d), BF16)],
        compiler_params=_params(2),
        name="mlp_ln3",
    )(h, w1, w2, ln_g, ln_b)


def _overlap_t(seq):
    ncp, nsel = seq // CMP_STRIDE, seq // SEL_BLOCK
    c_start = np.arange(ncp)[None, :] * CMP_STRIDE
    s_start = np.arange(nsel)[:, None] * SEL_BLOCK
    ov = (c_start < s_start + SEL_BLOCK) & (c_start + CMP_BLOCK > s_start) & (np.arange(ncp)[None, :] < ncp - 1)
    return jnp.asarray(ov, BF16)


def kernel(x, mem, w_in, attn_sinks, rel_bias_table, cmp_pos_k, cmp_w1_k, cmp_w2_k, cmp_pos_v, cmp_w1_v, cmp_w2_v,
           w_branch_swa, w_branch_nsa, w_mix_out, ln1_g, ln1_b, xa_w_q, xa_w_kv, xa_w_o, ln2_g, ln2_b,
           mlp_w1, mlp_w2, ln3_g, ln3_b):
    batch, seq, d = x.shape
    tokens = batch * seq
    nq = seq // BLOCK_Q
    assert w_in.shape[0] == 1, "one layer"
    assert seq % (8 * BLOCK_Q) == 0 and d % 512 == 0

    w = w_in[0]
    sizes = (SWA_HEADS * HEAD_DIM, 128, 128, NSA_HEADS * HEAD_DIM, 128, 128, 128, 128, 128, 128, 3 * NSA_HEADS, d, d)
    offs = np.concatenate([[0], np.cumsum(sizes)])
    (w_qa, w_ka, w_va, w_qb, w_kc, w_vc, w_ks, w_vs, w_kw, w_vw, w_gn, w_ga, w_gb) = [
        w[:, offs[n]:offs[n + 1]] for n in range(len(sizes))]
    w_rows = jnp.concatenate([w_ka, w_kc, w_ks, w_kw, w_vc], axis=1).astype(BF16)
    w_gn_pad = jnp.pad(w_gn, ((0, 0), (0, 128 - w_gn.shape[1])))
    w_cols_t = jnp.concatenate([w_qa, w_qb, w_va, w_vs, w_vw, w_gn_pad], axis=1).T.astype(BF16)

    x2 = x.reshape(tokens, d)

    h_t, x_bf, kk = _input_projections(x2, w_cols_t, w_rows, tm=512)

    def value_tiles(row0, width):
        v = h_t[row0:row0 + 128].reshape(GROUPS, HEAD_DIM, batch, seq // width, width).transpose(0, 2, 3, 1, 4)
        ones = jnp.ones(v.shape[:3] + (V_AUG_ROWS - HEAD_DIM, width), BF16)
        return jnp.concatenate([v, ones], axis=3)

    vswa_aug = value_tiles(2048, BLOCK_Q)
    vsel_aug, vwin_aug = value_tiles(2176, 2 * BLOCK_Q), value_tiles(2304, 2 * BLOCK_Q)
    gates_t = h_t[2432:2432 + 3 * NSA_HEADS].reshape(GROUPS, HEADS_PER_GROUP, 3, tokens).transpose(0, 2, 1, 3)
    block_of_key = (jnp.arange(tokens) % BLOCK_Q) // SEL_BLOCK
    onehot = (block_of_key[:, None] == jnp.arange(SEL_ROWS)[None, :]).astype(BF16)
    ksel_aug = jnp.concatenate([kk[2], jnp.broadcast_to(onehot, (GROUPS, tokens, SEL_ROWS))], axis=2)

    by_dist = rel_bias_table[_rel_bucket(jnp.arange(seq))].astype(F32).T
    vb_swa = _bias_by_distance(by_dist[:SWA_HEADS], 4, window=SWA_WINDOW)
    vb_sel = _bias_by_distance(by_dist[SWA_HEADS:], nq + 2)
    vb_win = _bias_by_distance(by_dist[SWA_HEADS:], N_WIN_SLOTS + 1, window=NSA_WINDOW)
    vb_cmp = vb_sel
    sink_rows = jnp.repeat(attn_sinks[0].astype(F32), BLOCK_Q).reshape(GROUPS, 1, GROUP_LANES)

    y_a = _swa_attention(h_t, kk, vswa_aug, sink_rows, vb_swa, batch, seq)

    ncp = seq // CMP_STRIDE
    chunk_w = CMP_STRIDE * HEAD_DIM
    chunks = jnp.stack([kk[1], kk[4]]).reshape(2, GROUPS, batch, ncp, chunk_w)
    pos = jnp.stack([cmp_pos_k[0], cmp_pos_v[0]]).astype(F32).reshape(2, 2, 1, chunk_w)
    w1 = jnp.stack([cmp_w1_k[0], cmp_w1_v[0]]).astype(BF16).reshape(2, 2, chunk_w, -1)
    w2 = jnp.stack([cmp_w2_k[0], cmp_w2_v[0]]).astype(BF16)
    cn, ct = _compress(chunks, pos, w1, w2, w2.transpose(0, 2, 1))
    oc_t, sel = _cmp_attention(h_t, cn, ct, vb_cmp, _overlap_t(seq), batch, seq)
    y_b = _selwin_attention(h_t, ksel_aug, kk, vsel_aug, vwin_aug, sel, oc_t, gates_t, vb_sel, vb_win, batch, seq)

    merged = _merge(x_bf, y_a, y_b, w_ga.astype(BF16), w_gb.astype(BF16),
                    w_branch_swa[0].astype(BF16), w_branch_nsa[0].astype(BF16), tm=min(1024, tokens), tn=512)
    h1 = _mix_ln(merged, x2, w_mix_out[0].astype(BF16), ln1_g, ln1_b, tm=512)

    mem_bf = mem.reshape(-1, d).astype(BF16)
    kv_mem = _matmul(mem_bf, xa_w_kv[0].astype(BF16), nt=False, tm=mem_bf.shape[0], tn=512)
    h2 = _cross_attention_ln(h1, kv_mem, xa_w_q[0].astype(BF16), xa_w_o[0].astype(BF16), ln2_g, ln2_b,
                             batch, seq, tm=512)

    h3 = _mlp_ln(h2, mlp_w1[0].astype(BF16), mlp_w2[0].astype(BF16), ln3_g, ln3_b, tm=1024, tf=512)
    return h3.reshape(batch, seq, d)
```

```python
import functools
import math

import numpy as np
import jax
import jax.numpy as jnp
from jax import lax
from jax.experimental import pallas as pl
from jax.experimental.pallas import tpu as pltpu

F32 = jnp.float32
BF16 = jnp.bfloat16

HEAD_DIM = 64
BLOCK_Q = 128
SWA_HEADS = 16
SWA_KV_HEADS = 2
SWA_WINDOW = 128
NSA_HEADS = 16
NSA_KV_HEADS = 2
CMP_BLOCK = 32
CMP_STRIDE = 16
SEL_BLOCK = 64
SEL_TOPK = 16
SEL_LOCAL = 2
NSA_WINDOW = 512
REL_BUCKETS = 32
REL_MAX_DIST = 4096
XA_HEADS = 4
XA_HEAD_DIM = 128
DEPTH = 1
DN_ALPHA = (2.0 * DEPTH) ** 0.25
LN_EPS = 1e-5
NEG_INF = -1e30
FORCE_SCORE = 1e4

GROUPS = 2
HEADS_PER_GROUP = 8
GROUP_LANES = HEADS_PER_GROUP * BLOCK_Q
GROUP_COLS = HEADS_PER_GROUP * HEAD_DIM
SUBTILES = 2

V7X_VMEM_LIMIT_BYTES = 56 * 1024 * 1024


def _params(n_axes):
    return pltpu.CompilerParams(dimension_semantics=("arbitrary",) * n_axes,
                                vmem_limit_bytes=V7X_VMEM_LIMIT_BYTES)


def _mm_kernel(a_ref, b_ref, o_ref):
    o_ref[...] = jnp.dot(a_ref[...], b_ref[...], preferred_element_type=F32).astype(o_ref.dtype)


def _matmul(a, b, *, tm, tn, out_dtype=BF16):
    m, k = a.shape
    n = b.shape[1]
    return pl.pallas_call(
        _mm_kernel,
        grid=(m // tm, n // tn),
        in_specs=[pl.BlockSpec((tm, k), lambda i, j: (i, 0)), pl.BlockSpec((k, tn), lambda i, j: (0, j))],
        out_specs=pl.BlockSpec((tm, tn), lambda i, j: (i, j)),
        out_shape=jax.ShapeDtypeStruct((m, n), out_dtype),
        compiler_params=_params(2),
        name="mem_kv_projection",
    )(a, b)


def _in_proj_kernel(x_ref, wt_ref, wr_ref, ht_ref, xb_ref, kk_ref):
    xb = x_ref[...].astype(BF16)
    xb_ref[...] = xb
    ht_ref[...] = lax.dot_general(wt_ref[...], xb, (((1,), (1,)), ((), ())),
                                  preferred_element_type=F32).astype(ht_ref.dtype)
    rows = jnp.dot(xb, wr_ref[...], preferred_element_type=F32)
    for c in range(kk_ref.shape[0]):
        for g in range(GROUPS):
            col = (c * GROUPS + g) * HEAD_DIM
            kk_ref[c, g] = rows[:, col:col + HEAD_DIM].astype(kk_ref.dtype)


def _input_projections(x, w_cols_t, w_rows, *, tm):
    tokens, d = x.shape
    cols = w_cols_t.shape[0]
    n_rows = w_rows.shape[1] // (GROUPS * HEAD_DIM)
    return pl.pallas_call(
        _in_proj_kernel,
        grid=(tokens // tm,),
        in_specs=[
            pl.BlockSpec((tm, d), lambda i: (i, 0)),
            pl.BlockSpec((cols, d), lambda i: (0, 0)),
            pl.BlockSpec((d, w_rows.shape[1]), lambda i: (0, 0)),
        ],
        out_specs=[
            pl.BlockSpec((cols, tm), lambda i: (0, i)),
            pl.BlockSpec((tm, d), lambda i: (i, 0)),
            pl.BlockSpec((n_rows, GROUPS, tm, HEAD_DIM), lambda i: (0, 0, i, 0)),
        ],
        out_shape=[jax.ShapeDtypeStruct((cols, tokens), BF16),
                   jax.ShapeDtypeStruct((tokens, d), BF16),
                   jax.ShapeDtypeStruct((n_rows, GROUPS, tokens, HEAD_DIM), BF16)],
        compiler_params=_params(1),
        name="input_projections",
    )(x, w_cols_t, w_rows)


def _rel_bucket(dist):
    exact = REL_BUCKETS // 2
    d = jnp.maximum(dist, 0)
    log_ratio = jnp.log(jnp.maximum(d, 1).astype(F32) / exact) / math.log(REL_MAX_DIST / exact)
    large = jnp.minimum(exact + (log_ratio * (REL_BUCKETS - exact)).astype(jnp.int32), REL_BUCKETS - 1)
    return jnp.where(d < exact, d, large)


def _bias_by_distance(by_dist, rows, *, window=None):
    heads, seq = by_dist.shape
    if window is not None:
        by_dist = jnp.where(jnp.arange(seq) < window, by_dist, NEG_INF)
    padded = jnp.pad(by_dist, ((0, 0), (2 * BLOCK_Q, 0)), constant_values=NEG_INF)[:, :rows * BLOCK_Q]
    return padded.reshape(heads // HEADS_PER_GROUP, HEADS_PER_GROUP, rows, BLOCK_Q)


def _build_skew_table(vb_ref, tbl_ref, n_tiles):
    def body(d, carry):
        for h in range(HEADS_PER_GROUP):
            lo = jnp.broadcast_to(vb_ref[h, pl.ds(d, 1), :], (BLOCK_Q, BLOCK_Q))
            hi = jnp.broadcast_to(vb_ref[h, pl.ds(d + 1, 1), :], (BLOCK_Q, BLOCK_Q))
            y = pltpu.roll(jnp.concatenate([lo, hi], axis=1), 0, 1, stride=1, stride_axis=0)
            tbl_ref[d, :, h * BLOCK_Q:(h + 1) * BLOCK_Q] = y[:, BLOCK_Q:].astype(tbl_ref.dtype)
        return carry

    lax.fori_loop(0, n_tiles, body, 0)


def _build_cmp_bias_table(vb_ref, tbl_ref, skew_ref, nq):
    per_tile = BLOCK_Q // CMP_STRIDE
    assert per_tile == 8 and CMP_BLOCK - 1 + CMP_STRIDE * (per_tile - 1) - BLOCK_Q == CMP_STRIDE - 1
    tbl_ref[pl.ds(nq * 8, nq * 8), :] = jnp.full((nq * 8, GROUP_LANES), NEG_INF, F32)

    def body(d, last_row):
        for h in range(HEADS_PER_GROUP):
            lo = jnp.broadcast_to(vb_ref[h, pl.ds(d + 1, 1), :], (BLOCK_Q, BLOCK_Q))
            hi = jnp.broadcast_to(vb_ref[h, pl.ds(d + 2, 1), :], (BLOCK_Q, BLOCK_Q))
            y = pltpu.roll(jnp.concatenate([lo, hi], axis=1), 0, 1, stride=1, stride_axis=0)
            skew_ref[:, h * BLOCK_Q:(h + 1) * BLOCK_Q] = y[:, BLOCK_Q:]
        offsets = [CMP_BLOCK - 1 + CMP_STRIDE * c for c in range(per_tile - 1)]
        rows = [skew_ref[k:k + 1, :] for k in offsets] + [last_row]
        tbl_ref[pl.ds(pl.multiple_of((nq - 1 - d) * 8, 8), 8), :] = jnp.concatenate(rows, axis=0)
        return skew_ref[CMP_STRIDE - 1:CMP_STRIDE, :]

    lax.fori_loop(0, nq, body, jnp.full((1, GROUP_LANES), NEG_INF, F32))


def _gather_heads_to_lanes(q_ref):
    q = jnp.concatenate([q_ref[h * HEAD_DIM:(h + 1) * HEAD_DIM, u * BLOCK_Q:(u + 1) * BLOCK_Q]
                         for u in range(q_ref.shape[1] // BLOCK_Q) for h in range(HEADS_PER_GROUP)], axis=1)
    return q * jnp.asarray(HEAD_DIM ** -0.5, q.dtype)


def _heads_to_columns(o_t):
    stacked = jnp.concatenate([o_t[:, h * BLOCK_Q:(h + 1) * BLOCK_Q] for h in range(HEADS_PER_GROUP)], axis=0)
    return stacked.T


def _identity_tile():
    r = lax.broadcasted_iota(jnp.int32, (BLOCK_Q, BLOCK_Q), 0)
    c = lax.broadcasted_iota(jnp.int32, (BLOCK_Q, BLOCK_Q), 1)
    return jnp.where(r == c, 1.0, 0.0).astype(BF16)


SLAB = 2 * BLOCK_Q
LOOKAHEAD = 12


def _slabwise_softmax_step(m, acc_ref, lhs_tiles, rhs_fn, v_fn, masked_fn=None):
    per_slab = [[j for j in range(len(lhs_tiles)) if masked_fn is None or not masked_fn(j, sl)]
                for sl in range(m.shape[1] // SLAB)]
    items = [(sl, j) for sl, tiles in enumerate(per_slab) for j in tiles]

    def scores(k):
        sl, j = items[k]
        lhs = lhs_tiles[j](sl) if callable(lhs_tiles[j]) else lhs_tiles[j]
        return jnp.dot(lhs, rhs_fn(j, sl), preferred_element_type=F32)

    pending = [scores(k) for k in range(min(LOOKAHEAD, len(items)))]
    issued = len(pending)
    new_m = []
    for sl, tiles in enumerate(per_slab):
        lanes = slice(sl * SLAB, (sl + 1) * SLAB)
        m_s, acc_s = m[:, lanes], acc_ref[:, lanes]
        for c in range(0, len(tiles), 2):
            chunk = tiles[c:c + 2]
            m_old, ps = m_s, []
            for j in chunk:
                s = pending.pop(0)
                if issued < len(items):
                    pending.append(scores(issued))
                    issued += 1
                m_next = jnp.maximum(m_s, jnp.max(s, axis=0, keepdims=True))
                ps = [q * jnp.exp(m_s - m_next).astype(BF16) for q in ps]
                ps.append(jnp.exp((s - m_next).astype(BF16)))
                m_s = m_next
            v = jnp.concatenate([v_fn(j, sl) for j in chunk], axis=1)
            acc_s = jnp.exp(m_old - m_s) * acc_s + jnp.dot(v, jnp.concatenate(ps, axis=0),
                                                           preferred_element_type=F32)
        acc_ref[:, lanes] = acc_s
        new_m.append(m_s)
    return jnp.concatenate(new_m, axis=1)


V_AUG_ROWS = HEAD_DIM + 16


def _normalized(acc):
    return acc[:HEAD_DIM] * (1.0 / acc[HEAD_DIM:HEAD_DIM + 1])


def _swa_kernel(q_ref, kprev_ref, kcur_ref, vprev_ref, vcur_ref, sink_ref, vb_ref, o_ref, tbl_ref, acc_ref):
    b, i = pl.program_id(1), pl.program_id(2)

    @pl.when((b == 0) & (i == 0))
    def _():
        _build_skew_table(vb_ref, tbl_ref, 3)

    eye = _identity_tile()
    q_t = _gather_heads_to_lanes(q_ref)
    lanes = q_t.shape[1]
    slabs_per_tile = GROUP_LANES // SLAB
    m0 = jnp.concatenate([sink_ref[...]] * SWA_TILES, axis=1)
    acc_ref[...] = jnp.where(lax.broadcasted_iota(jnp.int32, (V_AUG_ROWS, lanes), 0) < HEAD_DIM, 0.0, 1.0)

    def key_tile(j):
        def at(slab):
            u = slab // slabs_per_tile - j
            k_t = kprev_ref[...] if u < 0 else kcur_ref[u * BLOCK_Q:(u + 1) * BLOCK_Q, :]
            return jnp.concatenate([eye, k_t], axis=1)
        return at

    def rhs(j, slab):
        u, part = slab // slabs_per_tile, slab % slabs_per_tile
        slot = 1 + j if (u > 0 or j == 0) else jnp.where(i > 0, 2, 0)
        return jnp.concatenate([tbl_ref[slot, :, part * SLAB:(part + 1) * SLAB],
                                q_t[:, slab * SLAB:(slab + 1) * SLAB]], axis=0)

    def values(j, slab):
        u = slab // slabs_per_tile - j
        return vprev_ref[...] if u < 0 else vcur_ref[u]

    _slabwise_softmax_step(m0, acc_ref, [key_tile(0), key_tile(1)], rhs, values)
    out = _normalized(acc_ref[...])
    for u in range(SWA_TILES):
        o_ref[u * BLOCK_Q:(u + 1) * BLOCK_Q, :] = _heads_to_columns(
            out[:, u * GROUP_LANES:(u + 1) * GROUP_LANES]).astype(o_ref.dtype)


SWA_TILES = 4


def _swa_attention(h_t, kk, v_aug, sink_rows, vb, batch, seq):
    nq = seq // BLOCK_Q
    nblk = nq // SWA_TILES
    width = SWA_TILES * BLOCK_Q
    tokens = batch * seq

    def prev_tile(i):
        return jnp.maximum(i * SWA_TILES - 1, 0)

    return pl.pallas_call(
        _swa_kernel,
        grid=(GROUPS, batch, nblk),
        in_specs=[
            pl.BlockSpec((GROUP_COLS, width), lambda g, b, i: (g, b * nblk + i)),
            pl.BlockSpec((None, None, BLOCK_Q, HEAD_DIM), lambda g, b, i: (0, g, b * nq + prev_tile(i), 0)),
            pl.BlockSpec((None, None, width, HEAD_DIM), lambda g, b, i: (0, g, b * nblk + i, 0)),
            pl.BlockSpec((None, None, None, V_AUG_ROWS, BLOCK_Q), lambda g, b, i: (g, b, prev_tile(i), 0, 0)),
            pl.BlockSpec((None, None, SWA_TILES, V_AUG_ROWS, BLOCK_Q), lambda g, b, i: (g, b, i, 0, 0)),
            pl.BlockSpec((None, 1, GROUP_LANES), lambda g, b, i: (g, 0, 0)),
            pl.BlockSpec((None, HEADS_PER_GROUP, 4, BLOCK_Q), lambda g, b, i: (g, 0, 0, 0)),
        ],
        out_specs=pl.BlockSpec((width, GROUP_COLS), lambda g, b, i: (b * nblk + i, g)),
        out_shape=jax.ShapeDtypeStruct((tokens, SWA_HEADS * HEAD_DIM), BF16),
        scratch_shapes=[pltpu.VMEM((3, BLOCK_Q, GROUP_LANES), BF16),
                        pltpu.VMEM((V_AUG_ROWS, SWA_TILES * GROUP_LANES), F32)],
        compiler_params=_params(3),
        name="swa_attention",
    )(h_t, kk, kk, v_aug, v_aug, sink_rows, vb)


def _compress_kernel(c_ref, pos_ref, w1_ref, w2_ref, w2t_ref, cn_ref, ct_ref):
    c = c_ref[...].astype(F32)
    top = (c + pos_ref[0]).astype(BF16)
    bot = (c + pos_ref[1]).astype(BF16)
    a = jnp.dot(top, w1_ref[0], preferred_element_type=F32)
    bm = jnp.dot(bot, w1_ref[1], preferred_element_type=F32)
    n = a.shape[0]
    pre = a + pltpu.roll(bm, n - 1, 0)
    hid = jax.nn.gelu(pre).astype(BF16)
    cn_ref[...] = jnp.dot(hid, w2_ref[...], preferred_element_type=F32).astype(cn_ref.dtype)
    ct_ref[...] = lax.dot_general(w2t_ref[...], hid, (((1,), (1,)), ((), ())),
                                  preferred_element_type=F32).astype(ct_ref.dtype)


def _compress(chunks, pos, w1, w2, w2t):
    _, g, b, ncp, width = chunks.shape
    hidden = w1.shape[-1]
    return pl.pallas_call(
        _compress_kernel,
        grid=(2, g, b),
        in_specs=[
            pl.BlockSpec((None, None, None, ncp, width), lambda t, g, b: (t, g, b, 0, 0)),
            pl.BlockSpec((None, 2, 1, width), lambda t, g, b: (t, 0, 0, 0)),
            pl.BlockSpec((None, 2, width, hidden), lambda t, g, b: (t, 0, 0, 0)),
            pl.BlockSpec((None, hidden, HEAD_DIM), lambda t, g, b: (t, 0, 0)),
            pl.BlockSpec((None, HEAD_DIM, hidden), lambda t, g, b: (t, 0, 0)),
        ],
        out_specs=[
            pl.BlockSpec((None, None, None, ncp, HEAD_DIM), lambda t, g, b: (t, g, b, 0, 0)),
            pl.BlockSpec((None, None, None, HEAD_DIM, ncp), lambda t, g, b: (t, g, b, 0, 0)),
        ],
        out_shape=[jax.ShapeDtypeStruct((2, g, b, ncp, HEAD_DIM), BF16),
                   jax.ShapeDtypeStruct((2, g, b, HEAD_DIM, ncp), BF16)],
        compiler_params=_params(3),
        name="nsa_compress",
    )(chunks, pos, w1, w2, w2t)


def _cmp_kernel(q_ref, kc_ref, vct_ref, vb_ref, ov_ref, oc_ref, sel_ref, bias_ref, skew_ref, bias16_ref, *, nq):
    b, i = pl.program_id(1), pl.program_id(2)

    @pl.when((b == 0) & (i == 0))
    def _():
        _build_cmp_bias_table(vb_ref, bias_ref, skew_ref, nq)
        rows = bias_ref.shape[0]
        bias16_ref[0] = bias_ref[...].astype(BF16)
        bias16_ref[1, 0:rows - 16, :] = bias_ref[8:rows - 8, :].astype(BF16)

    ncp = kc_ref.shape[0]
    nsel = sel_ref.shape[0]
    n_tiles = ncp // BLOCK_Q
    q_t = _gather_heads_to_lanes(q_ref)
    width = SUBTILES * BLOCK_Q
    eye = _identity_tile()

    slabs_per_tile = GROUP_LANES // SLAB
    lhs = [jnp.concatenate([eye, kc_ref[t * BLOCK_Q:(t + 1) * BLOCK_Q, :]], axis=1) for t in range(n_tiles)]
    items = [(sl, t) for sl in range(SUBTILES * slabs_per_tile) for t in range(n_tiles)]

    def scores(k):
        sl, t = items[k]
        u, part = sl // slabs_per_tile, sl % slabs_per_tile
        shifted = (nq - 1 - u) % 2
        row0 = pl.multiple_of((nq - 1 - (SUBTILES * i + u)) * 8 + t * BLOCK_Q - 8 * shifted, 16)
        bias = bias16_ref[shifted, pl.ds(row0, BLOCK_Q), part * SLAB:(part + 1) * SLAB]
        return jnp.dot(lhs[t], jnp.concatenate([bias, q_t[:, sl * SLAB:(sl + 1) * SLAB]], axis=0),
                       preferred_element_type=F32)

    q_pos = i * width + lax.broadcasted_iota(jnp.int32, (1, width), 1)
    sees_any = q_pos >= CMP_BLOCK - 1
    psum = [[jnp.zeros((BLOCK_Q, BLOCK_Q), F32) for _ in range(n_tiles)] for _ in range(SUBTILES)]
    pending = [scores(k) for k in range(min(LOOKAHEAD, len(items)))]
    for sl in range(SUBTILES * slabs_per_tile):
        u, part = sl // slabs_per_tile, sl % slabs_per_tile
        s = [pending.pop(0) for _ in range(n_tiles)]
        for k in range(sl * n_tiles + LOOKAHEAD, min((sl + 1) * n_tiles + LOOKAHEAD, len(items))):
            pending.append(scores(k))
        m = functools.reduce(jnp.maximum, [jnp.max(c, axis=0, keepdims=True) for c in s])
        e = [jnp.exp(c - m) for c in s]
        l = functools.reduce(jnp.add, [jnp.sum(c, axis=0, keepdims=True) for c in e])
        seen = sees_any[:, u * BLOCK_Q:(u + 1) * BLOCK_Q]
        inv = jnp.where(jnp.concatenate([seen] * (SLAB // BLOCK_Q), axis=1), 1.0 / l, 0.0)
        o_slab = jnp.zeros((HEAD_DIM, SLAB), F32)
        for t in range(n_tiles):
            p = e[t] * inv
            psum[u][t] = psum[u][t] + functools.reduce(
                jnp.add, [p[:, r * BLOCK_Q:(r + 1) * BLOCK_Q] for r in range(SLAB // BLOCK_Q)])
            o_slab = o_slab + jnp.dot(vct_ref[:, t * BLOCK_Q:(t + 1) * BLOCK_Q], p.astype(BF16),
                                      preferred_element_type=F32)
        oc_ref[u, :, part * SLAB:(part + 1) * SLAB] = o_slab.astype(oc_ref.dtype)

    psum = jnp.concatenate([jnp.concatenate(pu, axis=0) for pu in psum], axis=1)
    hi = psum.astype(BF16)
    lo = (psum - hi.astype(F32)).astype(BF16)
    ov = ov_ref[...]
    score = jnp.dot(ov, hi, preferred_element_type=F32) + jnp.dot(ov, lo, preferred_element_type=F32)

    j_io = lax.broadcasted_iota(jnp.int32, (nsel, width), 0)
    qpos = i * width + lax.broadcasted_iota(jnp.int32, (nsel, width), 1)
    causal = j_io * SEL_BLOCK <= qpos
    back = qpos // SEL_BLOCK - j_io
    forced = (j_io == 0) | ((back >= 0) & (back < SEL_LOCAL))
    score = jnp.where(causal, jnp.where(forced, FORCE_SCORE, score), -1.0)
    slab_rows = lax.broadcasted_iota(jnp.int32, (8, width), 0)
    slabs = [score[8 * g:8 * (g + 1), :] for g in range(nsel // 8)]
    ranks = [jnp.zeros((8, width), F32) for _ in slabs]
    for r in range(nsel):
        row = jnp.broadcast_to(score[r:r + 1, :], (8, width))
        for g, slab in enumerate(slabs):
            if g > r // 8:
                ahead = row >= slab
            elif g < r // 8:
                ahead = row > slab
            else:
                ranks[g] = ranks[g] + jnp.where(slab_rows > r % 8, jnp.where(row >= slab, 1.0, 0.0),
                                                jnp.where(row > slab, 1.0, 0.0))
                continue
            ranks[g] = ranks[g] + jnp.where(ahead, 1.0, 0.0)
    rank = jnp.concatenate(ranks, axis=0)
    sel_ref[...] = jnp.where((rank < min(SEL_TOPK, nsel)) & causal, 1.0, 0.0).astype(sel_ref.dtype)


def _cmp_attention(h_t, cn, ct, vb, overlap_t, batch, seq):
    nq = seq // BLOCK_Q
    ncp = seq // CMP_STRIDE
    nsel = seq // SEL_BLOCK
    q_blk0 = SWA_HEADS * HEAD_DIM // GROUP_COLS
    nblk = nq // SUBTILES
    width = SUBTILES * BLOCK_Q
    return pl.pallas_call(
        functools.partial(_cmp_kernel, nq=nq),
        grid=(GROUPS, batch, nblk),
        in_specs=[
            pl.BlockSpec((GROUP_COLS, width), lambda g, b, i: (q_blk0 + g, b * nblk + i)),
            pl.BlockSpec((None, None, None, ncp, HEAD_DIM), lambda g, b, i: (0, g, b, 0, 0)),
            pl.BlockSpec((None, None, None, HEAD_DIM, ncp), lambda g, b, i: (1, g, b, 0, 0)),
            pl.BlockSpec((None, HEADS_PER_GROUP, nq + 2, BLOCK_Q), lambda g, b, i: (g, 0, 0, 0)),
            pl.BlockSpec((nsel, ncp), lambda g, b, i: (0, 0)),
        ],
        out_specs=[
            pl.BlockSpec((None, None, SUBTILES, HEAD_DIM, GROUP_LANES), lambda g, b, i: (b, g, i, 0, 0)),
            pl.BlockSpec((None, None, nsel, width), lambda g, b, i: (b, g, 0, i)),
        ],
        out_shape=[jax.ShapeDtypeStruct((batch, GROUPS, nq, HEAD_DIM, GROUP_LANES), BF16),
                   jax.ShapeDtypeStruct((batch, GROUPS, nsel, seq), F32)],
        scratch_shapes=[pltpu.VMEM((2 * nq * 8, GROUP_LANES), F32), pltpu.VMEM((BLOCK_Q, GROUP_LANES), F32),
                        pltpu.VMEM((2, 2 * nq * 8, GROUP_LANES), BF16)],
        compiler_params=_params(3),
        name="nsa_cmp_select",
    )(h_t, cn, ct, vb, overlap_t)


N_WIN_PAIRS = -(-(NSA_WINDOW - 1) // (SUBTILES * BLOCK_Q)) + 1
N_WIN_SLOTS = SUBTILES * N_WIN_PAIRS + 1
SEL_ROWS = 16
SEL_STEPS = 2


def _selwin_kernel(q_ref, ksel_ref, kwin_ref, vsel_ref, vwin_ref, sel_ref, oc_ref, gate_ref, vbs_ref, vbw_ref,
                   o_ref, tsel_ref, twin_ref, accs_ref, accw_ref, *, nq):
    b, blk = pl.program_id(1), pl.program_id(2)

    @pl.when((b == 0) & (blk == 0))
    def _():
        _build_skew_table(vbs_ref, tsel_ref, nq + 1)
        _build_skew_table(vbw_ref, twin_ref, N_WIN_SLOTS)

    q_t = _gather_heads_to_lanes(q_ref)
    lanes = q_t.shape[1]
    eye = _identity_tile()
    pair = SUBTILES * BLOCK_Q
    m0 = jnp.full((1, lanes), NEG_INF, F32)
    accs_ref[...] = jnp.zeros_like(accs_ref)
    accw_ref[...] = jnp.zeros_like(accw_ref)
    blocks_per_tile = BLOCK_Q // SEL_BLOCK
    slabs_per_tile = GROUP_LANES // SLAB

    def attend(m, steps, k_ref, tbl_ref, v_ref, acc_ref, q_ext_fn, reach=None):
        tiles = [(n, jnp.clip(blk - n, 0, blk), j) for n in steps for j in range(SUBTILES)]
        lhs = [jnp.concatenate([eye, k_ref[pl.ds(pl.multiple_of(p * pair + j * BLOCK_Q, BLOCK_Q), BLOCK_Q), :]],
                               axis=1) for _, p, j in tiles]
        q_exts = [q_ext_fn(p, j) for _, p, j in tiles]

        def rhs(t, slab):
            n, _, j = tiles[t]
            u, part = slab // slabs_per_tile, slab % slabs_per_tile
            slot = jnp.where(n <= blk, 2 * n + u - j + 1, 0)
            bias = tbl_ref[slot, :, part * SLAB:(part + 1) * SLAB]
            return jnp.concatenate([bias, q_exts[t][:, slab * SLAB:(slab + 1) * SLAB]], axis=0)

        def values(t, slab):
            _, p, j = tiles[t]
            return v_ref[p, :, j * BLOCK_Q:(j + 1) * BLOCK_Q]

        def masked(t, slab):
            n, _, j = tiles[t]
            distance = 2 * n + slab // slabs_per_tile - j
            return distance < 0 or distance >= reach

        return _slabwise_softmax_step(m, acc_ref, lhs, rhs, values, masked if reach is not None else None)

    attend(m0, list(range(N_WIN_PAIRS)), kwin_ref, twin_ref, vwin_ref, accw_ref, lambda p, j: q_t,
           reach=-(-(NSA_WINDOW + BLOCK_Q - 1) // BLOCK_Q))

    def sel_q_ext(p, j):
        per_pair = SUBTILES * blocks_per_tile
        group = sel_ref[pl.ds(pl.multiple_of((p // 2) * 2 * per_pair, 8), 2 * per_pair), :]
        lo, hi = (group[half * per_pair + j * blocks_per_tile:half * per_pair + (j + 1) * blocks_per_tile]
                  for half in range(2))
        neg = jnp.where(jnp.where(p % 2 == 1, hi, lo) > 0.5, 0.0, NEG_INF)
        rows = jnp.concatenate([neg[:, u * BLOCK_Q:(u + 1) * BLOCK_Q]
                                for u in range(SUBTILES) for _ in range(HEADS_PER_GROUP)], axis=1)
        rows = jnp.concatenate([rows, jnp.zeros((SEL_ROWS - blocks_per_tile, lanes), F32)], axis=0)
        return jnp.concatenate([q_t, rows.astype(BF16)], axis=0)

    def sel_body(k, m):
        return attend(m, [SEL_STEPS * k + r for r in range(SEL_STEPS)], ksel_ref, tsel_ref, vsel_ref, accs_ref,
                      sel_q_ext)

    lax.fori_loop(0, (blk + SEL_STEPS) // SEL_STEPS, sel_body, m0)

    def gate_row(branch):
        g = jnp.concatenate([gate_ref[branch, h:h + 1, u * BLOCK_Q:(u + 1) * BLOCK_Q]
                             for u in range(SUBTILES) for h in range(HEADS_PER_GROUP)], axis=1)
        return jax.nn.sigmoid(g.astype(F32))

    o_c = jnp.concatenate([oc_ref[u] for u in range(SUBTILES)], axis=1).astype(F32)
    out = (gate_row(0) * o_c + gate_row(1) * _normalized(accs_ref[...])
           + gate_row(2) * _normalized(accw_ref[...]))
    for u in range(SUBTILES):
        o_ref[u * BLOCK_Q:(u + 1) * BLOCK_Q, :] = _heads_to_columns(
            out[:, u * GROUP_LANES:(u + 1) * GROUP_LANES]).astype(o_ref.dtype)


def _selwin_attention(h_t, ksel_aug, kk, vsel_aug, vwin_aug, sel, oc_t, gates_t, vb_sel, vb_win, batch, seq):
    nq = seq // BLOCK_Q
    nblk = nq // SUBTILES
    tokens = batch * seq
    q_blk0 = SWA_HEADS * HEAD_DIM // GROUP_COLS
    pair = SUBTILES * BLOCK_Q
    lanes = SUBTILES * GROUP_LANES
    return pl.pallas_call(
        functools.partial(_selwin_kernel, nq=nq),
        grid=(GROUPS, batch, nblk),
        in_specs=[
            pl.BlockSpec((GROUP_COLS, pair), lambda g, b, i: (q_blk0 + g, b * nblk + i)),
            pl.BlockSpec((None, seq, HEAD_DIM + SEL_ROWS), lambda g, b, i: (g, b, 0)),
            pl.BlockSpec((None, None, seq, HEAD_DIM), lambda g, b, i: (3, g, b, 0)),
            pl.BlockSpec((None, None, nblk, V_AUG_ROWS, pair), lambda g, b, i: (g, b, 0, 0, 0)),
            pl.BlockSpec((None, None, nblk, V_AUG_ROWS, pair), lambda g, b, i: (g, b, 0, 0, 0)),
            pl.BlockSpec((None, None, seq // SEL_BLOCK, pair), lambda g, b, i: (b, g, 0, i)),
            pl.BlockSpec((None, None, SUBTILES, HEAD_DIM, GROUP_LANES), lambda g, b, i: (b, g, i, 0, 0)),
            pl.BlockSpec((None, 3, HEADS_PER_GROUP, pair), lambda g, b, i: (g, 0, 0, b * nblk + i)),
            pl.BlockSpec((None, HEADS_PER_GROUP, nq + 2, BLOCK_Q), lambda g, b, i: (g, 0, 0, 0)),
            pl.BlockSpec((None, HEADS_PER_GROUP, N_WIN_SLOTS + 1, BLOCK_Q), lambda g, b, i: (g, 0, 0, 0)),
        ],
        out_specs=pl.BlockSpec((pair, GROUP_COLS), lambda g, b, i: (b * nblk + i, g)),
        out_shape=jax.ShapeDtypeStruct((tokens, NSA_HEADS * HEAD_DIM), BF16),
        scratch_shapes=[pltpu.VMEM((nq + 1, BLOCK_Q, GROUP_LANES), BF16),
                        pltpu.VMEM((N_WIN_SLOTS, BLOCK_Q, GROUP_LANES), BF16),
                        pltpu.VMEM((V_AUG_ROWS, lanes), F32),
                        pltpu.VMEM((V_AUG_ROWS, lanes), F32)],
        compiler_params=_params(3),
        name="nsa_sel_win",
    )(h_t, ksel_aug, kk, vsel_aug, vwin_aug, sel, oc_t, gates_t, vb_sel, vb_win)


def _layer_norm(y, g_ref, b_ref):
    mu = jnp.mean(y, axis=-1, keepdims=True)
    yc = y - mu
    var = jnp.mean(yc * yc, axis=-1, keepdims=True)
    return yc * lax.rsqrt(var + LN_EPS) * g_ref[...] + b_ref[...]


def _merge_kernel(x_ref, ya_ref, yb_ref, wga_ref, wgb_ref, wa_ref, wb_ref, o_ref):
    x = x_ref[...]
    ga = jax.nn.sigmoid(jnp.dot(x, wga_ref[...], preferred_element_type=F32))
    gb = jax.nn.sigmoid(jnp.dot(x, wgb_ref[...], preferred_element_type=F32))
    a = jnp.dot(ya_ref[...], wa_ref[...], preferred_element_type=F32)
    bb = jnp.dot(yb_ref[...], wb_ref[...], preferred_element_type=F32)
    o_ref[...] = (ga * a + gb * bb).astype(o_ref.dtype)


def _merge(x_bf, y_a, y_b, w_ga, w_gb, w_a, w_b, *, tm, tn):
    tokens, d = x_bf.shape
    ya_cols, yb_cols = y_a.shape[1], y_b.shape[1]
    return pl.pallas_call(
        _merge_kernel,
        grid=(tokens // tm, d // tn),
        in_specs=[
            pl.BlockSpec((tm, d), lambda i, j: (i, 0)),
            pl.BlockSpec((tm, ya_cols), lambda i, j: (i, 0)),
            pl.BlockSpec((tm, yb_cols), lambda i, j: (i, 0)),
            pl.BlockSpec((d, tn), lambda i, j: (0, j)),
            pl.BlockSpec((d, tn), lambda i, j: (0, j)),
            pl.BlockSpec((ya_cols, tn), lambda i, j: (0, j)),
            pl.BlockSpec((yb_cols, tn), lambda i, j: (0, j)),
        ],
        out_specs=pl.BlockSpec((tm, tn), lambda i, j: (i, j)),
        out_shape=jax.ShapeDtypeStruct((tokens, d), BF16),
        compiler_params=_params(2),
        name="branch_merge",
    )(x_bf, y_a, y_b, w_ga, w_gb, w_a, w_b)


def _mix_ln_kernel(m_ref, x_ref, w_ref, g_ref, b_ref, o_ref):
    mix = jnp.dot(m_ref[...], w_ref[...], preferred_element_type=F32)
    o_ref[...] = _layer_norm(DN_ALPHA * x_ref[...] + mix, g_ref, b_ref)


def _mix_ln(merged, x, w_mix, ln_g, ln_b, *, tm):
    tokens, d = x.shape
    return pl.pallas_call(
        _mix_ln_kernel,
        grid=(tokens // tm,),
        in_specs=[
            pl.BlockSpec((tm, d), lambda i: (i, 0)),
            pl.BlockSpec((tm, d), lambda i: (i, 0)),
            pl.BlockSpec((d, d), lambda i: (0, 0)),
            pl.BlockSpec((1, d), lambda i: (0, 0)),
            pl.BlockSpec((1, d), lambda i: (0, 0)),
        ],
        out_specs=pl.BlockSpec((tm, d), lambda i: (i, 0)),
        out_shape=jax.ShapeDtypeStruct((tokens, d), F32),
        compiler_params=_params(1),
        name="mix_out_ln1",
    )(merged, x, w_mix, ln_g, ln_b)


def _xa_kernel(h_ref, wq_ref, k_ref, v_ref, wo_ref, g_ref, b_ref, o_ref):
    h = h_ref[...]
    q = jnp.dot(h.astype(BF16), wq_ref[...], preferred_element_type=F32) * (XA_HEAD_DIM ** -0.5)
    q = q.astype(BF16)
    outs = []
    for hd in range(XA_HEADS):
        cols = slice(hd * XA_HEAD_DIM, (hd + 1) * XA_HEAD_DIM)
        s = lax.dot_general(q[:, cols], k_ref[:, cols], (((1,), (1,)), ((), ())), preferred_element_type=F32)
        p = jnp.exp(s - jnp.max(s, axis=-1, keepdims=True))
        l = jnp.sum(p, axis=-1, keepdims=True)
        o = jnp.dot(p.astype(BF16), v_ref[:, cols], preferred_element_type=F32)
        outs.append(o * (1.0 / l))
    o = jnp.concatenate(outs, axis=1).astype(BF16)
    xa = jnp.dot(o, wo_ref[...], preferred_element_type=F32)
    o_ref[...] = _layer_norm(DN_ALPHA * h + xa, g_ref, b_ref)


def _cross_attention_ln(h, kv_mem, w_q, w_o, ln_g, ln_b, batch, seq, *, tm):
    tokens, d = h.shape
    mem_len = kv_mem.shape[0] // batch
    xa_dim = XA_HEADS * XA_HEAD_DIM
    nt = seq // tm
    return pl.pallas_call(
        _xa_kernel,
        grid=(batch, nt),
        in_specs=[
            pl.BlockSpec((tm, d), lambda b, i: (b * nt + i, 0)),
            pl.BlockSpec((d, xa_dim), lambda b, i: (0, 0)),
            pl.BlockSpec((mem_len, xa_dim), lambda b, i: (b, 0)),
            pl.BlockSpec((mem_len, xa_dim), lambda b, i: (b, 1)),
            pl.BlockSpec((xa_dim, d), lambda b, i: (0, 0)),
            pl.BlockSpec((1, d), lambda b, i: (0, 0)),
            pl.BlockSpec((1, d), lambda b, i: (0, 0)),
        ],
        out_specs=pl.BlockSpec((tm, d), lambda b, i: (b * nt + i, 0)),
        out_shape=jax.ShapeDtypeStruct((tokens, d), F32),
        compiler_params=_params(2),
        name="cross_attention_ln2",
    )(h, w_q, kv_mem, kv_mem, w_o, ln_g, ln_b)


def _mlp_kernel(h_ref, w1_ref, w2_ref, g_ref, b_ref, o_ref, hb_ref):
    j = pl.program_id(1)

    @pl.when(j == 0)
    def _():
        hb_ref[...] = h_ref[...].astype(BF16)
        o_ref[...] = jnp.zeros_like(o_ref)

    u = jnp.dot(hb_ref[...], w1_ref[...], preferred_element_type=F32)
    u = jnp.square(jnp.maximum(u, 0.0)).astype(BF16)
    o_ref[...] += jnp.dot(u, w2_ref[...], preferred_element_type=F32)

    @pl.when(j == pl.num_programs(1) - 1)
    def _():
        o_ref[...] = _layer_norm(DN_ALPHA * h_ref[...] + o_ref[...], g_ref, b_ref)


def _mlp_ln(h, w1, w2, ln_g, ln_b, *, tm, tf):
    tokens, d = h.shape
    d_ff = w1.shape[1]
    return pl.pallas_call(
        _mlp_kernel,
        grid=(tokens // tm, d_ff // tf),
        in_specs=[
            pl.BlockSpec((tm, d), lambda i, j: (i, 0)),
            pl.BlockSpec((d, tf), lambda i, j: (0, j)),
            pl.BlockSpec((tf, d), lambda i, j: (j, 0)),
            pl.BlockSpec((1, d), lambda i, j: (0, 0)),
            pl.BlockSpec((1, d), lambda i, j: (0, 0)),
        ],
        out_specs=pl.BlockSpec((tm, d), lambda i, j: (i, 0)),
        out_shape=jax.ShapeDtypeStruct((tokens, d), F32),
        scratch_shapes=[pltpu.VMEM((tm, d), BF16)],
        compiler_params=_params(2),
        name="mlp_ln3",
    )(h, w1, w2, ln_g, ln_b)


def _overlap_t(seq):
    ncp, nsel = seq // CMP_STRIDE, seq // SEL_BLOCK
    c_start = np.arange(ncp)[None, :] * CMP_STRIDE
    s_start = np.arange(nsel)[:, None] * SEL_BLOCK
    ov = (c_start < s_start + SEL_BLOCK) & (c_start + CMP_BLOCK > s_start) & (np.arange(ncp)[None, :] < ncp - 1)
    return jnp.asarray(ov, BF16)


def kernel(x, mem, w_in, attn_sinks, rel_bias_table, cmp_pos_k, cmp_w1_k, cmp_w2_k, cmp_pos_v, cmp_w1_v, cmp_w2_v,
           w_branch_swa, w_branch_nsa, w_mix_out, ln1_g, ln1_b, xa_w_q, xa_w_kv, xa_w_o, ln2_g, ln2_b,
           mlp_w1, mlp_w2, ln3_g, ln3_b):
    batch, seq, d = x.shape
    tokens = batch * seq
    nq = seq // BLOCK_Q
    assert w_in.shape[0] == 1, "one layer"
    assert seq % (8 * BLOCK_Q) == 0 and d % 512 == 0

    w = w_in[0]
    sizes = (SWA_HEADS * HEAD_DIM, 128, 128, NSA_HEADS * HEAD_DIM, 128, 128, 128, 128, 128, 128, 3 * NSA_HEADS, d, d)
    offs = np.concatenate([[0], np.cumsum(sizes)])
    (w_qa, w_ka, w_va, w_qb, w_kc, w_vc, w_ks, w_vs, w_kw, w_vw, w_gn, w_ga, w_gb) = [
        w[:, offs[n]:offs[n + 1]] for n in range(len(sizes))]
    w_rows = jnp.concatenate([w_ka, w_kc, w_ks, w_kw, w_vc], axis=1).astype(BF16)
    w_gn_pad = jnp.pad(w_gn, ((0, 0), (0, 128 - w_gn.shape[1])))
    w_cols_t = jnp.concatenate([w_qa, w_qb, w_va, w_vs, w_vw, w_gn_pad], axis=1).T.astype(BF16)

    x2 = x.reshape(tokens, d)

    h_t, x_bf, kk = _input_projections(x2, w_cols_t, w_rows, tm=512)

    def value_tiles(row0, width):
        v = h_t[row0:row0 + 128].reshape(GROUPS, HEAD_DIM, batch, seq // width, width).transpose(0, 2, 3, 1, 4)
        ones = jnp.ones(v.shape[:3] + (V_AUG_ROWS - HEAD_DIM, width), BF16)
        return jnp.concatenate([v, ones], axis=3)

    vswa_aug = value_tiles(2048, BLOCK_Q)
    vsel_aug, vwin_aug = value_tiles(2176, 2 * BLOCK_Q), value_tiles(2304, 2 * BLOCK_Q)
    gates_t = h_t[2432:2432 + 3 * NSA_HEADS].reshape(GROUPS, HEADS_PER_GROUP, 3, tokens).transpose(0, 2, 1, 3)
    block_of_key = (jnp.arange(tokens) % BLOCK_Q) // SEL_BLOCK
    onehot = (block_of_key[:, None] == jnp.arange(SEL_ROWS)[None, :]).astype(BF16)
    ksel_aug = jnp.concatenate([kk[2], jnp.broadcast_to(onehot, (GROUPS, tokens, SEL_ROWS))], axis=2)

    by_dist = rel_bias_table[_rel_bucket(jnp.arange(seq))].astype(F32).T
    vb_swa = _bias_by_distance(by_dist[:SWA_HEADS], 4, window=SWA_WINDOW)
    vb_sel = _bias_by_distance(by_dist[SWA_HEADS:], nq + 2)
    vb_win = _bias_by_distance(by_dist[SWA_HEADS:], N_WIN_SLOTS + 1, window=NSA_WINDOW)
    vb_cmp = vb_sel
    sink_rows = jnp.repeat(attn_sinks[0].astype(F32), BLOCK_Q).reshape(GROUPS, 1, GROUP_LANES)

    y_a = _swa_attention(h_t, kk, vswa_aug, sink_rows, vb_swa, batch, seq)

    ncp = seq // CMP_STRIDE
    chunk_w = CMP_STRIDE * HEAD_DIM
    chunks = jnp.stack([kk[1], kk[4]]).reshape(2, GROUPS, batch, ncp, chunk_w)
    pos = jnp.stack([cmp_pos_k[0], cmp_pos_v[0]]).astype(F32).reshape(2, 2, 1, chunk_w)
    w1 = jnp.stack([cmp_w1_k[0], cmp_w1_v[0]]).astype(BF16).reshape(2, 2, chunk_w, -1)
    w2 = jnp.stack([cmp_w2_k[0], cmp_w2_v[0]]).astype(BF16)
    cn, ct = _compress(chunks, pos, w1, w2, w2.transpose(0, 2, 1))
    oc_t, sel = _cmp_attention(h_t, cn, ct, vb_cmp, _overlap_t(seq), batch, seq)
    y_b = _selwin_attention(h_t, ksel_aug, kk, vsel_aug, vwin_aug, sel, oc_t, gates_t, vb_sel, vb_win, batch, seq)

    merged = _merge(x_bf, y_a, y_b, w_ga.astype(BF16), w_gb.astype(BF16),
                    w_branch_swa[0].astype(BF16), w_branch_nsa[0].astype(BF16), tm=min(1024, tokens), tn=512)
    h1 = _mix_ln(merged, x2, w_mix_out[0].astype(BF16), ln1_g, ln1_b, tm=512)

    mem_bf = mem.reshape(-1, d).astype(BF16)
    kv_mem = _matmul(mem_bf, xa_w_kv[0].astype(BF16), tm=mem_bf.shape[0], tn=512)
    h2 = _cross_attention_ln(h1, kv_mem, xa_w_q[0].astype(BF16), xa_w_o[0].astype(BF16), ln2_g, ln2_b,
                             batch, seq, tm=512)

    h3 = _mlp_ln(h2, mlp_w1[0].astype(BF16), mlp_w2[0].astype(BF16), ln3_g, ln3_b, tm=1024, tf=512)
    return h3.reshape(batch, seq, d)
```

```python
import functools
import math

import numpy as np
import jax
import jax.numpy as jnp
from jax import lax
from jax.experimental import pallas as pl
from jax.experimental.pallas import tpu as pltpu

F32 = jnp.float32
BF16 = jnp.bfloat16

HEAD_DIM = 64
BLOCK_Q = 128
SWA_HEADS = 16
SWA_KV_HEADS = 2
SWA_WINDOW = 128
NSA_HEADS = 16
NSA_KV_HEADS = 2
CMP_BLOCK = 32
CMP_STRIDE = 16
SEL_BLOCK = 64
SEL_TOPK = 16
SEL_LOCAL = 2
NSA_WINDOW = 512
REL_BUCKETS = 32
REL_MAX_DIST = 4096
XA_HEADS = 4
XA_HEAD_DIM = 128
DEPTH = 1
DN_ALPHA = (2.0 * DEPTH) ** 0.25
LN_EPS = 1e-5
NEG_INF = -1e30
FORCE_SCORE = 1e4

GROUPS = 2
HEADS_PER_GROUP = 8
GROUP_LANES = HEADS_PER_GROUP * BLOCK_Q
GROUP_COLS = HEADS_PER_GROUP * HEAD_DIM
SUBTILES = 2

V7X_VMEM_LIMIT_BYTES = 56 * 1024 * 1024


def _params(n_axes):
    return pltpu.CompilerParams(dimension_semantics=("arbitrary",) * n_axes,
                                vmem_limit_bytes=V7X_VMEM_LIMIT_BYTES)


def _mm_kernel(a_ref, b_ref, o_ref):
    o_ref[...] = jnp.dot(a_ref[...], b_ref[...], preferred_element_type=F32).astype(o_ref.dtype)


def _matmul(a, b, *, tm, tn, out_dtype=BF16):
    m, k = a.shape
    n = b.shape[1]
    return pl.pallas_call(
        _mm_kernel,
        grid=(m // tm, n // tn),
        in_specs=[pl.BlockSpec((tm, k), lambda i, j: (i, 0)), pl.BlockSpec((k, tn), lambda i, j: (0, j))],
        out_specs=pl.BlockSpec((tm, tn), lambda i, j: (i, j)),
        out_shape=jax.ShapeDtypeStruct((m, n), out_dtype),
        compiler_params=_params(2),
        name="mem_kv_projection",
    )(a, b)


def _in_proj_kernel(x_ref, wt_ref, wr_ref, ht_ref, xb_ref, kk_ref):
    xb = x_ref[...].astype(BF16)
    xb_ref[...] = xb
    ht_ref[...] = lax.dot_general(wt_ref[...], xb, (((1,), (1,)), ((), ())),
                                  preferred_element_type=F32).astype(ht_ref.dtype)
    rows = jnp.dot(xb, wr_ref[...], preferred_element_type=F32)
    for c in range(kk_ref.shape[0]):
        for g in range(GROUPS):
            col = (c * GROUPS + g) * HEAD_DIM
            kk_ref[c, g] = rows[:, col:col + HEAD_DIM].astype(kk_ref.dtype)


def _input_projections(x, w_cols_t, w_rows, *, tm):
    tokens, d = x.shape
    cols = w_cols_t.shape[0]
    n_rows = w_rows.shape[1] // (GROUPS * HEAD_DIM)
    return pl.pallas_call(
        _in_proj_kernel,
        grid=(tokens // tm,),
        in_specs=[
            pl.BlockSpec((tm, d), lambda i: (i, 0)),
            pl.BlockSpec((cols, d), lambda i: (0, 0)),
            pl.BlockSpec((d, w_rows.shape[1]), lambda i: (0, 0)),
        ],
        out_specs=[
            pl.BlockSpec((cols, tm), lambda i: (0, i)),
            pl.BlockSpec((tm, d), lambda i: (i, 0)),
            pl.BlockSpec((n_rows, GROUPS, tm, HEAD_DIM), lambda i: (0, 0, i, 0)),
        ],
        out_shape=[jax.ShapeDtypeStruct((cols, tokens), BF16),
                   jax.ShapeDtypeStruct((tokens, d), BF16),
                   jax.ShapeDtypeStruct((n_rows, GROUPS, tokens, HEAD_DIM), BF16)],
        compiler_params=_params(1),
        name="input_projections",
    )(x, w_cols_t, w_rows)


def _rel_bucket(dist):
    exact = REL_BUCKETS // 2
    d = jnp.maximum(dist, 0)
    log_ratio = jnp.log(jnp.maximum(d, 1).astype(F32) / exact) / math.log(REL_MAX_DIST / exact)
    large = jnp.minimum(exact + (log_ratio * (REL_BUCKETS - exact)).astype(jnp.int32), REL_BUCKETS - 1)
    return jnp.where(d < exact, d, large)


def _bias_by_distance(by_dist, rows, *, window=None):
    heads, seq = by_dist.shape
    if window is not None:
        by_dist = jnp.where(jnp.arange(seq) < window, by_dist, NEG_INF)
    padded = jnp.pad(by_dist, ((0, 0), (2 * BLOCK_Q, 0)), constant_values=NEG_INF)[:, :rows * BLOCK_Q]
    return padded.reshape(heads // HEADS_PER_GROUP, HEADS_PER_GROUP, rows, BLOCK_Q)


def _build_skew_table(vb_ref, tbl_ref, n_tiles):
    def body(d, carry):
        for h in range(HEADS_PER_GROUP):
            lo = jnp.broadcast_to(vb_ref[h, pl.ds(d, 1), :], (BLOCK_Q, BLOCK_Q))
            hi = jnp.broadcast_to(vb_ref[h, pl.ds(d + 1, 1), :], (BLOCK_Q, BLOCK_Q))
            y = pltpu.roll(jnp.concatenate([lo, hi], axis=1), 0, 1, stride=1, stride_axis=0)
            tbl_ref[d, :, h * BLOCK_Q:(h + 1) * BLOCK_Q] = y[:, BLOCK_Q:].astype(tbl_ref.dtype)
        return carry

    lax.fori_loop(0, n_tiles, body, 0)


def _build_cmp_bias_table(vb_ref, tbl_ref, skew_ref, nq):
    per_tile = BLOCK_Q // CMP_STRIDE
    assert per_tile == 8 and CMP_BLOCK - 1 + CMP_STRIDE * (per_tile - 1) - BLOCK_Q == CMP_STRIDE - 1
    tbl_ref[pl.ds(nq * 8, nq * 8), :] = jnp.full((nq * 8, GROUP_LANES), NEG_INF, F32)

    def body(d, last_row):
        for h in range(HEADS_PER_GROUP):
            lo = jnp.broadcast_to(vb_ref[h, pl.ds(d + 1, 1), :], (BLOCK_Q, BLOCK_Q))
            hi = jnp.broadcast_to(vb_ref[h, pl.ds(d + 2, 1), :], (BLOCK_Q, BLOCK_Q))
            y = pltpu.roll(jnp.concatenate([lo, hi], axis=1), 0, 1, stride=1, stride_axis=0)
            skew_ref[:, h * BLOCK_Q:(h + 1) * BLOCK_Q] = y[:, BLOCK_Q:]
        offsets = [CMP_BLOCK - 1 + CMP_STRIDE * c for c in range(per_tile - 1)]
        rows = [skew_ref[k:k + 1, :] for k in offsets] + [last_row]
        tbl_ref[pl.ds(pl.multiple_of((nq - 1 - d) * 8, 8), 8), :] = jnp.concatenate(rows, axis=0)
        return skew_ref[CMP_STRIDE - 1:CMP_STRIDE, :]

    lax.fori_loop(0, nq, body, jnp.full((1, GROUP_LANES), NEG_INF, F32))


def _gather_heads_to_lanes(q_ref):
    q = jnp.concatenate([q_ref[h * HEAD_DIM:(h + 1) * HEAD_DIM, u * BLOCK_Q:(u + 1) * BLOCK_Q]
                         for u in range(q_ref.shape[1] // BLOCK_Q) for h in range(HEADS_PER_GROUP)], axis=1)
    return q * jnp.asarray(HEAD_DIM ** -0.5, q.dtype)


def _heads_to_columns(o_t):
    stacked = jnp.concatenate([o_t[:, h * BLOCK_Q:(h + 1) * BLOCK_Q] for h in range(HEADS_PER_GROUP)], axis=0)
    return stacked.T


def _identity_tile():
    r = lax.broadcasted_iota(jnp.int32, (BLOCK_Q, BLOCK_Q), 0)
    c = lax.broadcasted_iota(jnp.int32, (BLOCK_Q, BLOCK_Q), 1)
    return jnp.where(r == c, 1.0, 0.0).astype(BF16)


SLAB = 2 * BLOCK_Q
LOOKAHEAD = 12


def _slabwise_softmax_step(m, acc_ref, lhs_tiles, rhs_fn, v_fn, masked_fn=None):
    per_slab = [[j for j in range(len(lhs_tiles)) if masked_fn is None or not masked_fn(j, sl)]
                for sl in range(m.shape[1] // SLAB)]
    items = [(sl, j) for sl, tiles in enumerate(per_slab) for j in tiles]

    def scores(k):
        sl, j = items[k]
        lhs = lhs_tiles[j](sl) if callable(lhs_tiles[j]) else lhs_tiles[j]
        return jnp.dot(lhs, rhs_fn(j, sl), preferred_element_type=F32)

    pending = [scores(k) for k in range(min(LOOKAHEAD, len(items)))]
    issued = len(pending)
    new_m = []
    for sl, tiles in enumerate(per_slab):
        lanes = slice(sl * SLAB, (sl + 1) * SLAB)
        m_s, acc_s = m[:, lanes], acc_ref[:, lanes]
        for c in range(0, len(tiles), 2):
            chunk = tiles[c:c + 2]
            m_old, ps = m_s, []
            for j in chunk:
                s = pending.pop(0)
                if issued < len(items):
                    pending.append(scores(issued))
                    issued += 1
                m_next = jnp.maximum(m_s, jnp.max(s, axis=0, keepdims=True))
                ps = [q * jnp.exp(m_s - m_next).astype(BF16) for q in ps]
                ps.append(jnp.exp((s - m_next).astype(BF16)))
                m_s = m_next
            v = jnp.concatenate([v_fn(j, sl) for j in chunk], axis=1)
            acc_s = jnp.exp(m_old - m_s) * acc_s + jnp.dot(v, jnp.concatenate(ps, axis=0),
                                                           preferred_element_type=F32)
        acc_ref[:, lanes] = acc_s
        new_m.append(m_s)
    return jnp.concatenate(new_m, axis=1)


V_AUG_ROWS = HEAD_DIM + 16


def _normalized(acc):
    return acc[:HEAD_DIM] * (1.0 / acc[HEAD_DIM:HEAD_DIM + 1])


def _swa_kernel(q_ref, kprev_ref, kcur_ref, vprev_ref, vcur_ref, sink_ref, vb_ref, o_ref, tbl_ref, acc_ref):
    b, i = pl.program_id(1), pl.program_id(2)

    @pl.when((b == 0) & (i == 0))
    def _():
        _build_skew_table(vb_ref, tbl_ref, 3)

    eye = _identity_tile()
    q_t = _gather_heads_to_lanes(q_ref)
    lanes = q_t.shape[1]
    slabs_per_tile = GROUP_LANES // SLAB
    m0 = jnp.concatenate([sink_ref[...]] * SWA_TILES, axis=1)
    acc_ref[...] = jnp.where(lax.broadcasted_iota(jnp.int32, (V_AUG_ROWS, lanes), 0) < HEAD_DIM, 0.0, 1.0)

    def key_tile(j):
        def at(slab):
            u = slab // slabs_per_tile - j
            k_t = kprev_ref[...] if u < 0 else kcur_ref[u * BLOCK_Q:(u + 1) * BLOCK_Q, :]
            return jnp.concatenate([eye, k_t], axis=1)
        return at

    def rhs(j, slab):
        u, part = slab // slabs_per_tile, slab % slabs_per_tile
        slot = 1 + j if (u > 0 or j == 0) else jnp.where(i > 0, 2, 0)
        return jnp.concatenate([tbl_ref[slot, :, part * SLAB:(part + 1) * SLAB],
                                q_t[:, slab * SLAB:(slab + 1) * SLAB]], axis=0)

    def values(j, slab):
        u = slab // slabs_per_tile - j
        return vprev_ref[...] if u < 0 else vcur_ref[u]

    _slabwise_softmax_step(m0, acc_ref, [key_tile(0), key_tile(1)], rhs, values)
    out = _normalized(acc_ref[...])
    for u in range(SWA_TILES):
        o_ref[u * BLOCK_Q:(u + 1) * BLOCK_Q, :] = _heads_to_columns(
            out[:, u * GROUP_LANES:(u + 1) * GROUP_LANES]).astype(o_ref.dtype)


SWA_TILES = 4


def _swa_attention(h_t, kk, v_aug, sink_rows, vb, batch, seq):
    nq = seq // BLOCK_Q
    nblk = nq // SWA_TILES
    width = SWA_TILES * BLOCK_Q
    tokens = batch * seq

    def prev_tile(i):
        return jnp.maximum(i * SWA_TILES - 1, 0)

    return pl.pallas_call(
        _swa_kernel,
        grid=(GROUPS, batch, nblk),
        in_specs=[
            pl.BlockSpec((GROUP_COLS, width), lambda g, b, i: (g, b * nblk + i)),
            pl.BlockSpec((None, None, BLOCK_Q, HEAD_DIM), lambda g, b, i: (0, g, b * nq + prev_tile(i), 0)),
            pl.BlockSpec((None, None, width, HEAD_DIM), lambda g, b, i: (0, g, b * nblk + i, 0)),
            pl.BlockSpec((None, None, None, V_AUG_ROWS, BLOCK_Q), lambda g, b, i: (g, b, prev_tile(i), 0, 0)),
            pl.BlockSpec((None, None, SWA_TILES, V_AUG_ROWS, BLOCK_Q), lambda g, b, i: (g, b, i, 0, 0)),
            pl.BlockSpec((None, 1, GROUP_LANES), lambda g, b, i: (g, 0, 0)),
            pl.BlockSpec((None, HEADS_PER_GROUP, 4, BLOCK_Q), lambda g, b, i: (g, 0, 0, 0)),
        ],
        out_specs=pl.BlockSpec((width, GROUP_COLS), lambda g, b, i: (b * nblk + i, g)),
        out_shape=jax.ShapeDtypeStruct((tokens, SWA_HEADS * HEAD_DIM), BF16),
        scratch_shapes=[pltpu.VMEM((3, BLOCK_Q, GROUP_LANES), BF16),
                        pltpu.VMEM((V_AUG_ROWS, SWA_TILES * GROUP_LANES), F32)],
        compiler_params=_params(3),
        name="swa_attention",
    )(h_t, kk, kk, v_aug, v_aug, sink_rows, vb)


def _compress_kernel(c_ref, pos_ref, w1_ref, w2_ref, w2t_ref, cn_ref, ct_ref):
    c = c_ref[...].astype(F32)
    top = (c + pos_ref[0]).astype(BF16)
    bot = (c + pos_ref[1]).astype(BF16)
    a = jnp.dot(top, w1_ref[0], preferred_element_type=F32)
    bm = jnp.dot(bot, w1_ref[1], preferred_element_type=F32)
    n = a.shape[0]
    pre = a + pltpu.roll(bm, n - 1, 0)
    hid = jax.nn.gelu(pre).astype(BF16)
    cn_ref[...] = jnp.dot(hid, w2_ref[...], preferred_element_type=F32).astype(cn_ref.dtype)
    ct_ref[...] = lax.dot_general(w2t_ref[...], hid, (((1,), (1,)), ((), ())),
                                  preferred_element_type=F32).astype(ct_ref.dtype)


def _compress(chunks, pos, w1, w2, w2t):
    _, g, b, ncp, width = chunks.shape
    hidden = w1.shape[-1]
    return pl.pallas_call(
        _compress_kernel,
        grid=(2, g, b),
        in_specs=[
            pl.BlockSpec((None, None, None, ncp, width), lambda t, g, b: (t, g, b, 0, 0)),
            pl.BlockSpec((None, 2, 1, width), lambda t, g, b: (t, 0, 0, 0)),
            pl.BlockSpec((None, 2, width, hidden), lambda t, g, b: (t, 0, 0, 0)),
            pl.BlockSpec((None, hidden, HEAD_DIM), lambda t, g, b: (t, 0, 0)),
            pl.BlockSpec((None, HEAD_DIM, hidden), lambda t, g, b: (t, 0, 0)),
        ],
        out_specs=[
            pl.BlockSpec((None, None, None, ncp, HEAD_DIM), lambda t, g, b: (t, g, b, 0, 0)),
            pl.BlockSpec((None, None, None, HEAD_DIM, ncp), lambda t, g, b: (t, g, b, 0, 0)),
        ],
        out_shape=[jax.ShapeDtypeStruct((2, g, b, ncp, HEAD_DIM), BF16),
                   jax.ShapeDtypeStruct((2, g, b, HEAD_DIM, ncp), BF16)],
        compiler_params=_params(3),
        name="nsa_compress",
    )(chunks, pos, w1, w2, w2t)


def _cmp_kernel(q_ref, kc_ref, vct_ref, vb_ref, ov_ref, oc_ref, sel_ref, bias_ref, skew_ref, bias16_ref, *, nq):
    b, i = pl.program_id(1), pl.program_id(2)

    @pl.when((b == 0) & (i == 0))
    def _():
        _build_cmp_bias_table(vb_ref, bias_ref, skew_ref, nq)
        rows = bias_ref.shape[0]
        bias16_ref[0] = bias_ref[...].astype(BF16)
        bias16_ref[1, 0:rows - 16, :] = bias_ref[8:rows - 8, :].astype(BF16)

    ncp = kc_ref.shape[0]
    nsel = sel_ref.shape[0]
    n_tiles = ncp // BLOCK_Q
    q_t = _gather_heads_to_lanes(q_ref)
    width = SUBTILES * BLOCK_Q
    eye = _identity_tile()

    slabs_per_tile = GROUP_LANES // SLAB
    lhs = [jnp.concatenate([eye, kc_ref[t * BLOCK_Q:(t + 1) * BLOCK_Q, :]], axis=1) for t in range(n_tiles)]
    items = [(sl, t) for sl in range(SUBTILES * slabs_per_tile) for t in range(n_tiles)]

    def scores(k):
        sl, t = items[k]
        u, part = sl // slabs_per_tile, sl % slabs_per_tile
        shifted = (nq - 1 - u) % 2
        row0 = pl.multiple_of((nq - 1 - (SUBTILES * i + u)) * 8 + t * BLOCK_Q - 8 * shifted, 16)
        bias = bias16_ref[shifted, pl.ds(row0, BLOCK_Q), part * SLAB:(part + 1) * SLAB]
        return jnp.dot(lhs[t], jnp.concatenate([bias, q_t[:, sl * SLAB:(sl + 1) * SLAB]], axis=0),
                       preferred_element_type=F32)

    q_pos = i * width + lax.broadcasted_iota(jnp.int32, (1, width), 1)
    sees_any = q_pos >= CMP_BLOCK - 1
    psum = [[jnp.zeros((BLOCK_Q, BLOCK_Q), F32) for _ in range(n_tiles)] for _ in range(SUBTILES)]
    pending = [scores(k) for k in range(min(LOOKAHEAD, len(items)))]
    for sl in range(SUBTILES * slabs_per_tile):
        u, part = sl // slabs_per_tile, sl % slabs_per_tile
        s = [pending.pop(0) for _ in range(n_tiles)]
        for k in range(sl * n_tiles + LOOKAHEAD, min((sl + 1) * n_tiles + LOOKAHEAD, len(items))):
            pending.append(scores(k))
        m = functools.reduce(jnp.maximum, [jnp.max(c, axis=0, keepdims=True) for c in s])
        e = [jnp.exp(c - m) for c in s]
        l = functools.reduce(jnp.add, [jnp.sum(c, axis=0, keepdims=True) for c in e])
        seen = sees_any[:, u * BLOCK_Q:(u + 1) * BLOCK_Q]
        inv = jnp.where(jnp.concatenate([seen] * (SLAB // BLOCK_Q), axis=1), 1.0 / l, 0.0)
        o_slab = jnp.zeros((HEAD_DIM, SLAB), F32)
        for t in range(n_tiles):
            p = e[t] * inv
            psum[u][t] = psum[u][t] + functools.reduce(
                jnp.add, [p[:, r * BLOCK_Q:(r + 1) * BLOCK_Q] for r in range(SLAB // BLOCK_Q)])
            o_slab = o_slab + jnp.dot(vct_ref[:, t * BLOCK_Q:(t + 1) * BLOCK_Q], p.astype(BF16),
                                      preferred_element_type=F32)
        oc_ref[u, :, part * SLAB:(part + 1) * SLAB] = o_slab.astype(oc_ref.dtype)

    psum = jnp.concatenate([jnp.concatenate(pu, axis=0) for pu in psum], axis=1)
    hi = psum.astype(BF16)
    lo = (psum - hi.astype(F32)).astype(BF16)
    ov = ov_ref[...]
    score = jnp.dot(ov, hi, preferred_element_type=F32) + jnp.dot(ov, lo, preferred_element_type=F32)

    j_io = lax.broadcasted_iota(jnp.int32, (nsel, width), 0)
    qpos = i * width + lax.broadcasted_iota(jnp.int32, (nsel, width), 1)
    causal = j_io * SEL_BLOCK <= qpos
    back = qpos // SEL_BLOCK - j_io
    forced = (j_io == 0) | ((back >= 0) & (back < SEL_LOCAL))
    score = jnp.where(causal, jnp.where(forced, FORCE_SCORE, score), -1.0)
    slab_rows = lax.broadcasted_iota(jnp.int32, (8, width), 0)
    slabs = [score[8 * g:8 * (g + 1), :] for g in range(nsel // 8)]
    ranks = [jnp.zeros((8, width), F32) for _ in slabs]
    for r in range(nsel):
        row = jnp.broadcast_to(score[r:r + 1, :], (8, width))
        for g, slab in enumerate(slabs):
            if g > r // 8:
                ahead = row >= slab
            elif g < r // 8:
                ahead = row > slab
            else:
                ranks[g] = ranks[g] + jnp.where(slab_rows > r % 8, jnp.where(row >= slab, 1.0, 0.0),
                                                jnp.where(row > slab, 1.0, 0.0))
                continue
            ranks[g] = ranks[g] + jnp.where(ahead, 1.0, 0.0)
    rank = jnp.concatenate(ranks, axis=0)
    sel_ref[...] = jnp.where((rank < min(SEL_TOPK, nsel)) & causal, 1.0, 0.0).astype(sel_ref.dtype)


def _cmp_attention(h_t, cn, ct, vb, overlap_t, batch, seq):
    nq = seq // BLOCK_Q
    ncp = seq // CMP_STRIDE
    nsel = seq // SEL_BLOCK
    q_blk0 = SWA_HEADS * HEAD_DIM // GROUP_COLS
    nblk = nq // SUBTILES
    width = SUBTILES * BLOCK_Q
    return pl.pallas_call(
        functools.partial(_cmp_kernel, nq=nq),
        grid=(GROUPS, batch, nblk),
        in_specs=[
            pl.BlockSpec((GROUP_COLS, width), lambda g, b, i: (q_blk0 + g, b * nblk + i)),
            pl.BlockSpec((None, None, None, ncp, HEAD_DIM), lambda g, b, i: (0, g, b, 0, 0)),
            pl.BlockSpec((None, None, None, HEAD_DIM, ncp), lambda g, b, i: (1, g, b, 0, 0)),
            pl.BlockSpec((None, HEADS_PER_GROUP, nq + 2, BLOCK_Q), lambda g, b, i: (g, 0, 0, 0)),
            pl.BlockSpec((nsel, ncp), lambda g, b, i: (0, 0)),
        ],
        out_specs=[
            pl.BlockSpec((None, None, SUBTILES, HEAD_DIM, GROUP_LANES), lambda g, b, i: (b, g, i, 0, 0)),
            pl.BlockSpec((None, None, nsel, width), lambda g, b, i: (b, g, 0, i)),
        ],
        out_shape=[jax.ShapeDtypeStruct((batch, GROUPS, nq, HEAD_DIM, GROUP_LANES), BF16),
                   jax.ShapeDtypeStruct((batch, GROUPS, nsel, seq), F32)],
        scratch_shapes=[pltpu.VMEM((2 * nq * 8, GROUP_LANES), F32), pltpu.VMEM((BLOCK_Q, GROUP_LANES), F32),
                        pltpu.VMEM((2, 2 * nq * 8, GROUP_LANES), BF16)],
        compiler_params=_params(3),
        name="nsa_cmp_select",
    )(h_t, cn, ct, vb, overlap_t)


N_WIN_PAIRS = -(-(NSA_WINDOW - 1) // (SUBTILES * BLOCK_Q)) + 1
N_WIN_SLOTS = SUBTILES * N_WIN_PAIRS + 1
SEL_ROWS = 16
SEL_STEPS = 2


def _selwin_kernel(q_ref, ksel_ref, kwin_ref, vsel_ref, vwin_ref, sel_ref, oc_ref, gate_ref, vbs_ref, vbw_ref,
                   o_ref, tsel_ref, twin_ref, accs_ref, accw_ref, *, nq):
    b, blk = pl.program_id(1), pl.program_id(2)

    @pl.when((b == 0) & (blk == 0))
    def _():
        _build_skew_table(vbs_ref, tsel_ref, nq + 1)
        _build_skew_table(vbw_ref, twin_ref, N_WIN_SLOTS)

    q_t = _gather_heads_to_lanes(q_ref)
    lanes = q_t.shape[1]
    eye = _identity_tile()
    pair = SUBTILES * BLOCK_Q
    m0 = jnp.full((1, lanes), NEG_INF, F32)
    accs_ref[...] = jnp.zeros_like(accs_ref)
    accw_ref[...] = jnp.zeros_like(accw_ref)
    blocks_per_tile = BLOCK_Q // SEL_BLOCK
    slabs_per_tile = GROUP_LANES // SLAB

    def attend(m, steps, k_ref, tbl_ref, v_ref, acc_ref, q_ext_fn, reach=None):
        tiles = [(n, jnp.clip(blk - n, 0, blk), j) for n in steps for j in range(SUBTILES)]
        lhs = [jnp.concatenate([eye, k_ref[pl.ds(pl.multiple_of(p * pair + j * BLOCK_Q, BLOCK_Q), BLOCK_Q), :]],
                               axis=1) for _, p, j in tiles]
        q_exts = [q_ext_fn(p, j) for _, p, j in tiles]

        def rhs(t, slab):
            n, _, j = tiles[t]
            u, part = slab // slabs_per_tile, slab % slabs_per_tile
            slot = jnp.where(n <= blk, 2 * n + u - j + 1, 0)
            bias = tbl_ref[slot, :, part * SLAB:(part + 1) * SLAB]
            return jnp.concatenate([bias, q_exts[t][:, slab * SLAB:(slab + 1) * SLAB]], axis=0)

        def values(t, slab):
            _, p, j = tiles[t]
            return v_ref[p, :, j * BLOCK_Q:(j + 1) * BLOCK_Q]

        def masked(t, slab):
            n, _, j = tiles[t]
            distance = 2 * n + slab // slabs_per_tile - j
            return distance < 0 or distance >= reach

        return _slabwise_softmax_step(m, acc_ref, lhs, rhs, values, masked if reach is not None else None)

    attend(m0, list(range(N_WIN_PAIRS)), kwin_ref, twin_ref, vwin_ref, accw_ref, lambda p, j: q_t,
           reach=-(-(NSA_WINDOW + BLOCK_Q - 1) // BLOCK_Q))

    def sel_q_ext(p, j):
        per_pair = SUBTILES * blocks_per_tile
        group = sel_ref[pl.ds(pl.multiple_of((p // 2) * 2 * per_pair, 8), 2 * per_pair), :]
        lo, hi = (group[half * per_pair + j * blocks_per_tile:half * per_pair + (j + 1) * blocks_per_tile]
                  for half in range(2))
        neg = jnp.where(jnp.where(p % 2 == 1, hi, lo) > 0.5, 0.0, NEG_INF)
        rows = jnp.concatenate([neg[:, u * BLOCK_Q:(u + 1) * BLOCK_Q]
                                for u in range(SUBTILES) for _ in range(HEADS_PER_GROUP)], axis=1)
        rows = jnp.concatenate([rows, jnp.zeros((SEL_ROWS - blocks_per_tile, lanes), F32)], axis=0)
        return jnp.concatenate([q_t, rows.astype(BF16)], axis=0)

    def sel_body(k, m):
        return attend(m, [SEL_STEPS * k + r for r in range(SEL_STEPS)], ksel_ref, tsel_ref, vsel_ref, accs_ref,
                      sel_q_ext)

    lax.fori_loop(0, (blk + SEL_STEPS) // SEL_STEPS, sel_body, m0)

    def gate_row(branch):
        g = jnp.concatenate([gate_ref[branch, h:h + 1, u * BLOCK_Q:(u + 1) * BLOCK_Q]
                             for u in range(SUBTILES) for h in range(HEADS_PER_GROUP)], axis=1)
        return jax.nn.sigmoid(g.astype(F32))

    o_c = jnp.concatenate([oc_ref[u] for u in range(SUBTILES)], axis=1).astype(F32)
    out = (gate_row(0) * o_c + gate_row(1) * _normalized(accs_ref[...])
           + gate_row(2) * _normalized(accw_ref[...]))
    for u in range(SUBTILES):
        o_ref[u * BLOCK_Q:(u + 1) * BLOCK_Q, :] = _heads_to_columns(
            out[:, u * GROUP_LANES:(u + 1) * GROUP_LANES]).astype(o_ref.dtype)


def _selwin_attention(h_t, ksel_aug, kk, vsel_aug, vwin_aug, sel, oc_t, gates_t, vb_sel, vb_win, batch, seq):
    nq = seq // BLOCK_Q
    nblk = nq // SUBTILES
    tokens = batch * seq
    q_blk0 = SWA_HEADS * HEAD_DIM // GROUP_COLS
    pair = SUBTILES * BLOCK_Q
    lanes = SUBTILES * GROUP_LANES
    return pl.pallas_call(
        functools.partial(_selwin_kernel, nq=nq),
        grid=(GROUPS, batch, nblk),
        in_specs=[
            pl.BlockSpec((GROUP_COLS, pair), lambda g, b, i: (q_blk0 + g, b * nblk + i)),
            pl.BlockSpec((None, seq, HEAD_DIM + SEL_ROWS), lambda g, b, i: (g, b, 0)),
            pl.BlockSpec((None, None, seq, HEAD_DIM), lambda g, b, i: (3, g, b, 0)),
            pl.BlockSpec((None, None, nblk, V_AUG_ROWS, pair), lambda g, b, i: (g, b, 0, 0, 0)),
            pl.BlockSpec((None, None, nblk, V_AUG_ROWS, pair), lambda g, b, i: (g, b, 0, 0, 0)),
            pl.BlockSpec((None, None, seq // SEL_BLOCK, pair), lambda g, b, i: (b, g, 0, i)),
            pl.BlockSpec((None, None, SUBTILES, HEAD_DIM, GROUP_LANES), lambda g, b, i: (b, g, i, 0, 0)),
            pl.BlockSpec((None, 3, HEADS_PER_GROUP, pair), lambda g, b, i: (g, 0, 0, b * nblk + i)),
            pl.BlockSpec((None, HEADS_PER_GROUP, nq + 2, BLOCK_Q), lambda g, b, i: (g, 0, 0, 0)),
            pl.BlockSpec((None, HEADS_PER_GROUP, N_WIN_SLOTS + 1, BLOCK_Q), lambda g, b, i: (g, 0, 0, 0)),
        ],
        out_specs=pl.BlockSpec((pair, GROUP_COLS), lambda g, b, i: (b * nblk + i, g)),
        out_shape=jax.ShapeDtypeStruct((tokens, NSA_HEADS * HEAD_DIM), BF16),
        scratch_shapes=[pltpu.VMEM((nq + 1, BLOCK_Q, GROUP_LANES), BF16),
                        pltpu.VMEM((N_WIN_SLOTS, BLOCK_Q, GROUP_LANES), BF16),
                        pltpu.VMEM((V_AUG_ROWS, lanes), F32),
                        pltpu.VMEM((V_AUG_ROWS, lanes), F32)],
        compiler_params=_params(3),
        name="nsa_sel_win",
    )(h_t, ksel_aug, kk, vsel_aug, vwin_aug, sel, oc_t, gates_t, vb_sel, vb_win)


ROW_SPLITS = 2


def _layer_norm(y, g_ref, b_ref):
    mu = jnp.mean(y, axis=-1, keepdims=True)
    yc = y - mu
    var = jnp.mean(yc * yc, axis=-1, keepdims=True)
    return yc * lax.rsqrt(var + LN_EPS) * g_ref[...] + b_ref[...]


def _merge_kernel(x_ref, ya_ref, yb_ref, wga_ref, wgb_ref, wa_ref, wb_ref, o_ref):
    x = x_ref[...]
    ga = jax.nn.sigmoid(jnp.dot(x, wga_ref[...], preferred_element_type=F32))
    gb = jax.nn.sigmoid(jnp.dot(x, wgb_ref[...], preferred_element_type=F32))
    a = jnp.dot(ya_ref[...], wa_ref[...], preferred_element_type=F32)
    bb = jnp.dot(yb_ref[...], wb_ref[...], preferred_element_type=F32)
    o_ref[...] = (ga * a + gb * bb).astype(o_ref.dtype)


def _merge(x_bf, y_a, y_b, w_ga, w_gb, w_a, w_b, *, tm, tn):
    tokens, d = x_bf.shape
    ya_cols, yb_cols = y_a.shape[1], y_b.shape[1]
    return pl.pallas_call(
        _merge_kernel,
        grid=(tokens // tm, d // tn),
        in_specs=[
            pl.BlockSpec((tm, d), lambda i, j: (i, 0)),
            pl.BlockSpec((tm, ya_cols), lambda i, j: (i, 0)),
            pl.BlockSpec((tm, yb_cols), lambda i, j: (i, 0)),
            pl.BlockSpec((d, tn), lambda i, j: (0, j)),
            pl.BlockSpec((d, tn), lambda i, j: (0, j)),
            pl.BlockSpec((ya_cols, tn), lambda i, j: (0, j)),
            pl.BlockSpec((yb_cols, tn), lambda i, j: (0, j)),
        ],
        out_specs=pl.BlockSpec((tm, tn), lambda i, j: (i, j)),
        out_shape=jax.ShapeDtypeStruct((tokens, d), BF16),
        compiler_params=_params(2),
        name="branch_merge",
    )(x_bf, y_a, y_b, w_ga, w_gb, w_a, w_b)


def _mix_ln_kernel(m_ref, x_ref, w_ref, g_ref, b_ref, o_ref, mix_scr):
    i = pl.program_id(0)

    @pl.when(i == 0)
    def _():
        mix_scr[1] = jnp.zeros(mix_scr.shape[1:], F32)

    def step(done, todo):
        mix_scr[todo] = jnp.dot(m_ref[...], w_ref[...], preferred_element_type=F32)
        o_ref[...] = _layer_norm(DN_ALPHA * x_ref[...] + mix_scr[done], g_ref, b_ref)

    @pl.when(i % 2 == 0)
    def _():
        step(1, 0)

    @pl.when(i % 2 == 1)
    def _():
        step(0, 1)


def _mix_ln(merged, x, w_mix, ln_g, ln_b, *, tm):
    tokens, d = x.shape
    tiles = tokens // tm
    return pl.pallas_call(
        _mix_ln_kernel,
        grid=(tiles + 1,),
        in_specs=[
            pl.BlockSpec((tm, d), lambda i: (jnp.minimum(i, tiles - 1), 0)),
            pl.BlockSpec((tm, d), lambda i: (jnp.maximum(i - 1, 0), 0)),
            pl.BlockSpec((d, d), lambda i: (0, 0)),
            pl.BlockSpec((1, d), lambda i: (0, 0)),
            pl.BlockSpec((1, d), lambda i: (0, 0)),
        ],
        out_specs=pl.BlockSpec((tm, d), lambda i: (jnp.maximum(i - 1, 0), 0)),
        out_shape=jax.ShapeDtypeStruct((tokens, d), F32),
        scratch_shapes=[pltpu.VMEM((2, tm, d), F32)],
        compiler_params=_params(1),
        name="mix_out_ln1",
    )(merged, x, w_mix, ln_g, ln_b)


def _xa_kernel(h_ref, wq_ref, k_ref, v_ref, wo_ref, g_ref, b_ref, o_ref):
    h = h_ref[...]
    q = jnp.dot(h.astype(BF16), wq_ref[...], preferred_element_type=F32) * (XA_HEAD_DIM ** -0.5)
    q = q.astype(BF16)
    outs = []
    for hd in range(XA_HEADS):
        cols = slice(hd * XA_HEAD_DIM, (hd + 1) * XA_HEAD_DIM)
        s = lax.dot_general(q[:, cols], k_ref[:, cols], (((1,), (1,)), ((), ())), preferred_element_type=F32)
        p = jnp.exp(s - jnp.max(s, axis=-1, keepdims=True))
        l = jnp.sum(p, axis=-1, keepdims=True)
        o = jnp.dot(p.astype(BF16), v_ref[:, cols], preferred_element_type=F32)
        outs.append(o * (1.0 / l))
    o = jnp.concatenate(outs, axis=1).astype(BF16)
    xa = jnp.dot(o, wo_ref[...], preferred_element_type=F32)
    o_ref[...] = _layer_norm(DN_ALPHA * h + xa, g_ref, b_ref)


def _cross_attention_ln(h, kv_mem, w_q, w_o, ln_g, ln_b, batch, seq, *, tm):
    tokens, d = h.shape
    mem_len = kv_mem.shape[0] // batch
    xa_dim = XA_HEADS * XA_HEAD_DIM
    nt = seq // tm
    return pl.pallas_call(
        _xa_kernel,
        grid=(batch, nt),
        in_specs=[
            pl.BlockSpec((tm, d), lambda b, i: (b * nt + i, 0)),
            pl.BlockSpec((d, xa_dim), lambda b, i: (0, 0)),
            pl.BlockSpec((mem_len, xa_dim), lambda b, i: (b, 0)),
            pl.BlockSpec((mem_len, xa_dim), lambda b, i: (b, 1)),
            pl.BlockSpec((xa_dim, d), lambda b, i: (0, 0)),
            pl.BlockSpec((1, d), lambda b, i: (0, 0)),
            pl.BlockSpec((1, d), lambda b, i: (0, 0)),
        ],
        out_specs=pl.BlockSpec((tm, d), lambda b, i: (b * nt + i, 0)),
        out_shape=jax.ShapeDtypeStruct((tokens, d), F32),
        compiler_params=_params(2),
        name="cross_attention_ln2",
    )(h, w_q, kv_mem, kv_mem, w_o, ln_g, ln_b)


def _mlp_kernel(h_ref, w1_ref, w2_ref, g_ref, b_ref, o_ref, hb_ref):
    j = pl.program_id(1)

    @pl.when(j == 0)
    def _():
        hb_ref[...] = h_ref[...].astype(BF16)
        o_ref[...] = jnp.zeros_like(o_ref)

    rows = hb_ref.shape[0] // ROW_SPLITS
    subs = [slice(r * rows, (r + 1) * rows) for r in range(ROW_SPLITS)]
    ups = [jnp.dot(hb_ref[sl, :], w1_ref[...], preferred_element_type=F32) for sl in subs]
    for sl, u in zip(subs, ups):
        u = jnp.square(jnp.maximum(u, 0.0)).astype(BF16)
        o_ref[sl, :] += jnp.dot(u, w2_ref[...], preferred_element_type=F32)

    @pl.when(j == pl.num_programs(1) - 1)
    def _():
        o_ref[...] = _layer_norm(DN_ALPHA * h_ref[...] + o_ref[...], g_ref, b_ref)


def _mlp_ln(h, w1, w2, ln_g, ln_b, *, tm, tf):
    tokens, d = h.shape
    d_ff = w1.shape[1]
    return pl.pallas_call(
        _mlp_kernel,
        grid=(tokens // tm, d_ff // tf),
        in_specs=[
            pl.BlockSpec((tm, d), lambda i, j: (i, 0)),
            pl.BlockSpec((d, tf), lambda i, j: (0, j)),
            pl.BlockSpec((tf, d), lambda i, j: (j, 0)),
            pl.BlockSpec((1, d), lambda i, j: (0, 0)),
            pl.BlockSpec((1, d), lambda i, j: (0, 0)),
        ],
        out_specs=pl.BlockSpec((tm, d), lambda i, j: (i, 0)),
        out_shape=jax.ShapeDtypeStruct((tokens, d), F32),
        scratch_shapes=[pltpu.VMEM((tm, d), BF16)],
        compiler_params=_params(2),
        name="mlp_ln3",
    )(h, w1, w2, ln_g, ln_b)


def _overlap_t(seq):
    ncp, nsel = seq // CMP_STRIDE, seq // SEL_BLOCK
    c_start = np.arange(ncp)[None, :] * CMP_STRIDE
    s_start = np.arange(nsel)[:, None] * SEL_BLOCK
    ov = (c_start < s_start + SEL_BLOCK) & (c_start + CMP_BLOCK > s_start) & (np.arange(ncp)[None, :] < ncp - 1)
    return jnp.asarray(ov, BF16)


def kernel(x, mem, w_in, attn_sinks, rel_bias_table, cmp_pos_k, cmp_w1_k, cmp_w2_k, cmp_pos_v, cmp_w1_v, cmp_w2_v,
           w_branch_swa, w_branch_nsa, w_mix_out, ln1_g, ln1_b, xa_w_q, xa_w_kv, xa_w_o, ln2_g, ln2_b,
           mlp_w1, mlp_w2, ln3_g, ln3_b):
    batch, seq, d = x.shape
    tokens = batch * seq
    nq = seq // BLOCK_Q
    assert w_in.shape[0] == 1, "one layer"
    assert seq % (8 * BLOCK_Q) == 0 and d % 512 == 0

    w = w_in[0]
    sizes = (SWA_HEADS * HEAD_DIM, 128, 128, NSA_HEADS * HEAD_DIM, 128, 128, 128, 128, 128, 128, 3 * NSA_HEADS, d, d)
    offs = np.concatenate([[0], np.cumsum(sizes)])
    (w_qa, w_ka, w_va, w_qb, w_kc, w_vc, w_ks, w_vs, w_kw, w_vw, w_gn, w_ga, w_gb) = [
        w[:, offs[n]:offs[n + 1]] for n in range(len(sizes))]
    w_rows = jnp.concatenate([w_ka, w_kc, w_ks, w_kw, w_vc], axis=1).astype(BF16)
    w_gn_pad = jnp.pad(w_gn, ((0, 0), (0, 128 - w_gn.shape[1])))
    w_cols_t = jnp.concatenate([w_qa, w_qb, w_va, w_vs, w_vw, w_gn_pad], axis=1).T.astype(BF16)

    x2 = x.reshape(tokens, d)

    h_t, x_bf, kk = _input_projections(x2, w_cols_t, w_rows, tm=512)

    def value_tiles(row0, width):
        v = h_t[row0:row0 + 128].reshape(GROUPS, HEAD_DIM, batch, seq // width, width).transpose(0, 2, 3, 1, 4)
        ones = jnp.ones(v.shape[:3] + (V_AUG_ROWS - HEAD_DIM, width), BF16)
        return jnp.concatenate([v, ones], axis=3)

    vswa_aug = value_tiles(2048, BLOCK_Q)
    vsel_aug, vwin_aug = value_tiles(2176, 2 * BLOCK_Q), value_tiles(2304, 2 * BLOCK_Q)
    gates_t = h_t[2432:2432 + 3 * NSA_HEADS].reshape(GROUPS, HEADS_PER_GROUP, 3, tokens).transpose(0, 2, 1, 3)
    block_of_key = (jnp.arange(tokens) % BLOCK_Q) // SEL_BLOCK
    onehot = (block_of_key[:, None] == jnp.arange(SEL_ROWS)[None, :]).astype(BF16)
    ksel_aug = jnp.concatenate([kk[2], jnp.broadcast_to(onehot, (GROUPS, tokens, SEL_ROWS))], axis=2)

    by_dist = rel_bias_table[_rel_bucket(jnp.arange(seq))].astype(F32).T
    vb_swa = _bias_by_distance(by_dist[:SWA_HEADS], 4, window=SWA_WINDOW)
    vb_sel = _bias_by_distance(by_dist[SWA_HEADS:], nq + 2)
    vb_win = _bias_by_distance(by_dist[SWA_HEADS:], N_WIN_SLOTS + 1, window=NSA_WINDOW)
    vb_cmp = vb_sel
    sink_rows = jnp.repeat(attn_sinks[0].astype(F32), BLOCK_Q).reshape(GROUPS, 1, GROUP_LANES)

    y_a = _swa_attention(h_t, kk, vswa_aug, sink_rows, vb_swa, batch, seq)

    ncp = seq // CMP_STRIDE
    chunk_w = CMP_STRIDE * HEAD_DIM
    chunks = jnp.stack([kk[1], kk[4]]).reshape(2, GROUPS, batch, ncp, chunk_w)
    pos = jnp.stack([cmp_pos_k[0], cmp_pos_v[0]]).astype(F32).reshape(2, 2, 1, chunk_w)
    w1 = jnp.stack([cmp_w1_k[0], cmp_w1_v[0]]).astype(BF16).reshape(2, 2, chunk_w, -1)
    w2 = jnp.stack([cmp_w2_k[0], cmp_w2_v[0]]).astype(BF16)
    cn, ct = _compress(chunks, pos, w1, w2, w2.transpose(0, 2, 1))
    oc_t, sel = _cmp_attention(h_t, cn, ct, vb_cmp, _overlap_t(seq), batch, seq)
    y_b = _selwin_attention(h_t, ksel_aug, kk, vsel_aug, vwin_aug, sel, oc_t, gates_t, vb_sel, vb_win, batch, seq)

    merged = _merge(x_bf, y_a, y_b, w_ga.astype(BF16), w_gb.astype(BF16),
                    w_branch_swa[0].astype(BF16), w_branch_nsa[0].astype(BF16), tm=min(1024, tokens), tn=512)
    h1 = _mix_ln(merged, x2, w_mix_out[0].astype(BF16), ln1_g, ln1_b, tm=512)

    mem_bf = mem.reshape(-1, d).astype(BF16)
    kv_mem = _matmul(mem_bf, xa_w_kv[0].astype(BF16), tm=mem_bf.shape[0], tn=512)
    h2 = _cross_attention_ln(h1, kv_mem, xa_w_q[0].astype(BF16), xa_w_o[0].astype(BF16), ln2_g, ln2_b,
                             batch, seq, tm=512)

    h3 = _mlp_ln(h2, mlp_w1[0].astype(BF16), mlp_w2[0].astype(BF16), ln3_g, ln3_b, tm=1024, tf=512)
    return h3.reshape(batch, seq, d)
```

```python
import functools
import math

import numpy as np
import jax
import jax.numpy as jnp
from jax import lax
from jax.experimental import pallas as pl
from jax.experimental.pallas import tpu as pltpu

F32 = jnp.float32
BF16 = jnp.bfloat16

HEAD_DIM = 64
BLOCK_Q = 128
SWA_HEADS = 16
SWA_KV_HEADS = 2
SWA_WINDOW = 128
NSA_HEADS = 16
NSA_KV_HEADS = 2
CMP_BLOCK = 32
CMP_STRIDE = 16
SEL_BLOCK = 64
SEL_TOPK = 16
SEL_LOCAL = 2
NSA_WINDOW = 512
REL_BUCKETS = 32
REL_MAX_DIST = 4096
XA_HEADS = 4
XA_HEAD_DIM = 128
DEPTH = 1
DN_ALPHA = (2.0 * DEPTH) ** 0.25
LN_EPS = 1e-5
NEG_INF = -1e30
FORCE_SCORE = 1e4

GROUPS = 2
HEADS_PER_GROUP = 8
GROUP_LANES = HEADS_PER_GROUP * BLOCK_Q
GROUP_COLS = HEADS_PER_GROUP * HEAD_DIM
SUBTILES = 2

V7X_VMEM_LIMIT_BYTES = 56 * 1024 * 1024


def _params(n_axes):
    return pltpu.CompilerParams(dimension_semantics=("arbitrary",) * n_axes,
                                vmem_limit_bytes=V7X_VMEM_LIMIT_BYTES)


def _mm_kernel(a_ref, b_ref, o_ref):
    o_ref[...] = jnp.dot(a_ref[...], b_ref[...], preferred_element_type=F32).astype(o_ref.dtype)


def _matmul(a, b, *, tm, tn, out_dtype=BF16):
    m, k = a.shape
    n = b.shape[1]
    return pl.pallas_call(
        _mm_kernel,
        grid=(m // tm, n // tn),
        in_specs=[pl.BlockSpec((tm, k), lambda i, j: (i, 0)), pl.BlockSpec((k, tn), lambda i, j: (0, j))],
        out_specs=pl.BlockSpec((tm, tn), lambda i, j: (i, j)),
        out_shape=jax.ShapeDtypeStruct((m, n), out_dtype),
        compiler_params=_params(2),
        name="mem_kv_projection",
    )(a, b)


def _in_proj_kernel(x_ref, wt_ref, wr_ref, ht_ref, xb_ref, kk_ref):
    xb = x_ref[...].astype(BF16)
    xb_ref[...] = xb
    ht_ref[...] = lax.dot_general(wt_ref[...], xb, (((1,), (1,)), ((), ())),
                                  preferred_element_type=F32).astype(ht_ref.dtype)
    rows = jnp.dot(xb, wr_ref[...], preferred_element_type=F32)
    for c in range(kk_ref.shape[0]):
        for g in range(GROUPS):
            col = (c * GROUPS + g) * HEAD_DIM
            kk_ref[c, g] = rows[:, col:col + HEAD_DIM].astype(kk_ref.dtype)


def _input_projections(x, w_cols_t, w_rows, *, tm):
    tokens, d = x.shape
    cols = w_cols_t.shape[0]
    n_rows = w_rows.shape[1] // (GROUPS * HEAD_DIM)
    return pl.pallas_call(
        _in_proj_kernel,
        grid=(tokens // tm,),
        in_specs=[
            pl.BlockSpec((tm, d), lambda i: (i, 0)),
            pl.BlockSpec((cols, d), lambda i: (0, 0)),
            pl.BlockSpec((d, w_rows.shape[1]), lambda i: (0, 0)),
        ],
        out_specs=[
            pl.BlockSpec((cols, tm), lambda i: (0, i)),
            pl.BlockSpec((tm, d), lambda i: (i, 0)),
            pl.BlockSpec((n_rows, GROUPS, tm, HEAD_DIM), lambda i: (0, 0, i, 0)),
        ],
        out_shape=[jax.ShapeDtypeStruct((cols, tokens), BF16),
                   jax.ShapeDtypeStruct((tokens, d), BF16),
                   jax.ShapeDtypeStruct((n_rows, GROUPS, tokens, HEAD_DIM), BF16)],
        compiler_params=_params(1),
        name="input_projections",
    )(x, w_cols_t, w_rows)


def _rel_bucket(dist):
    exact = REL_BUCKETS // 2
    d = jnp.maximum(dist, 0)
    log_ratio = jnp.log(jnp.maximum(d, 1).astype(F32) / exact) / math.log(REL_MAX_DIST / exact)
    large = jnp.minimum(exact + (log_ratio * (REL_BUCKETS - exact)).astype(jnp.int32), REL_BUCKETS - 1)
    return jnp.where(d < exact, d, large)


def _bias_by_distance(by_dist, rows, *, window=None):
    heads, seq = by_dist.shape
    if window is not None:
        by_dist = jnp.where(jnp.arange(seq) < window, by_dist, NEG_INF)
    padded = jnp.pad(by_dist, ((0, 0), (2 * BLOCK_Q, 0)), constant_values=NEG_INF)[:, :rows * BLOCK_Q]
    return padded.reshape(heads // HEADS_PER_GROUP, HEADS_PER_GROUP, rows, BLOCK_Q)


def _build_skew_table(vb_ref, tbl_ref, n_tiles):
    def body(d, carry):
        for h in range(HEADS_PER_GROUP):
            lo = jnp.broadcast_to(vb_ref[h, pl.ds(d, 1), :], (BLOCK_Q, BLOCK_Q))
            hi = jnp.broadcast_to(vb_ref[h, pl.ds(d + 1, 1), :], (BLOCK_Q, BLOCK_Q))
            y = pltpu.roll(jnp.concatenate([lo, hi], axis=1), 0, 1, stride=1, stride_axis=0)
            tbl_ref[d, :, h * BLOCK_Q:(h + 1) * BLOCK_Q] = y[:, BLOCK_Q:].astype(tbl_ref.dtype)
        return carry

    lax.fori_loop(0, n_tiles, body, 0)


def _build_cmp_bias_table(vb_ref, tbl_ref, skew_ref, nq):
    per_tile = BLOCK_Q // CMP_STRIDE
    assert per_tile == 8 and CMP_BLOCK - 1 + CMP_STRIDE * (per_tile - 1) - BLOCK_Q == CMP_STRIDE - 1
    tbl_ref[pl.ds(nq * 8, nq * 8), :] = jnp.full((nq * 8, GROUP_LANES), NEG_INF, F32)

    def body(d, last_row):
        for h in range(HEADS_PER_GROUP):
            lo = jnp.broadcast_to(vb_ref[h, pl.ds(d + 1, 1), :], (BLOCK_Q, BLOCK_Q))
            hi = jnp.broadcast_to(vb_ref[h, pl.ds(d + 2, 1), :], (BLOCK_Q, BLOCK_Q))
            y = pltpu.roll(jnp.concatenate([lo, hi], axis=1), 0, 1, stride=1, stride_axis=0)
            skew_ref[:, h * BLOCK_Q:(h + 1) * BLOCK_Q] = y[:, BLOCK_Q:]
        offsets = [CMP_BLOCK - 1 + CMP_STRIDE * c for c in range(per_tile - 1)]
        rows = [skew_ref[k:k + 1, :] for k in offsets] + [last_row]
        tbl_ref[pl.ds(pl.multiple_of((nq - 1 - d) * 8, 8), 8), :] = jnp.concatenate(rows, axis=0)
        return skew_ref[CMP_STRIDE - 1:CMP_STRIDE, :]

    lax.fori_loop(0, nq, body, jnp.full((1, GROUP_LANES), NEG_INF, F32))


def _gather_heads_to_lanes(q_ref):
    q = jnp.concatenate([q_ref[h * HEAD_DIM:(h + 1) * HEAD_DIM, u * BLOCK_Q:(u + 1) * BLOCK_Q]
                         for u in range(q_ref.shape[1] // BLOCK_Q) for h in range(HEADS_PER_GROUP)], axis=1)
    return q * jnp.asarray(HEAD_DIM ** -0.5, q.dtype)


def _heads_to_columns(o_t):
    stacked = jnp.concatenate([o_t[:, h * BLOCK_Q:(h + 1) * BLOCK_Q] for h in range(HEADS_PER_GROUP)], axis=0)
    return stacked.T


def _identity_tile():
    r = lax.broadcasted_iota(jnp.int32, (BLOCK_Q, BLOCK_Q), 0)
    c = lax.broadcasted_iota(jnp.int32, (BLOCK_Q, BLOCK_Q), 1)
    return jnp.where(r == c, 1.0, 0.0).astype(BF16)


SLAB = 2 * BLOCK_Q
LOOKAHEAD = 12


def _slabwise_softmax_step(m, acc_ref, lhs_tiles, rhs_fn, v_fn, masked_fn=None):
    per_slab = [[j for j in range(len(lhs_tiles)) if masked_fn is None or not masked_fn(j, sl)]
                for sl in range(m.shape[1] // SLAB)]
    items = [(sl, j) for sl, tiles in enumerate(per_slab) for j in tiles]

    def scores(k):
        sl, j = items[k]
        lhs = lhs_tiles[j](sl) if callable(lhs_tiles[j]) else lhs_tiles[j]
        return jnp.dot(lhs, rhs_fn(j, sl), preferred_element_type=F32)

    pending = [scores(k) for k in range(min(LOOKAHEAD, len(items)))]
    issued = len(pending)
    new_m = []
    for sl, tiles in enumerate(per_slab):
        lanes = slice(sl * SLAB, (sl + 1) * SLAB)
        m_s, acc_s = m[:, lanes], acc_ref[:, lanes]
        for c in range(0, len(tiles), 2):
            chunk = tiles[c:c + 2]
            m_old, ps = m_s, []
            for j in chunk:
                s = pending.pop(0)
                if issued < len(items):
                    pending.append(scores(issued))
                    issued += 1
                m_next = jnp.maximum(m_s, jnp.max(s, axis=0, keepdims=True))
                ps = [q * jnp.exp(m_s - m_next).astype(BF16) for q in ps]
                ps.append(jnp.exp((s - m_next).astype(BF16)))
                m_s = m_next
            v = jnp.concatenate([v_fn(j, sl) for j in chunk], axis=1)
            acc_s = jnp.exp(m_old - m_s) * acc_s + jnp.dot(v, jnp.concatenate(ps, axis=0),
                                                           preferred_element_type=F32)
        acc_ref[:, lanes] = acc_s
        new_m.append(m_s)
    return jnp.concatenate(new_m, axis=1)


V_AUG_ROWS = HEAD_DIM + 16


def _normalized(acc):
    return acc[:HEAD_DIM] * (1.0 / acc[HEAD_DIM:HEAD_DIM + 1])


def _swa_kernel(q_ref, kprev_ref, kcur_ref, vprev_ref, vcur_ref, sink_ref, vb_ref, o_ref, tbl_ref, acc_ref):
    b, i = pl.program_id(1), pl.program_id(2)

    @pl.when((b == 0) & (i == 0))
    def _():
        _build_skew_table(vb_ref, tbl_ref, 3)

    eye = _identity_tile()
    q_t = _gather_heads_to_lanes(q_ref)
    lanes = q_t.shape[1]
    slabs_per_tile = GROUP_LANES // SLAB
    m0 = jnp.concatenate([sink_ref[...]] * SWA_TILES, axis=1)
    acc_ref[...] = jnp.where(lax.broadcasted_iota(jnp.int32, (V_AUG_ROWS, lanes), 0) < HEAD_DIM, 0.0, 1.0)

    def key_tile(j):
        def at(slab):
            u = slab // slabs_per_tile - j
            k_t = kprev_ref[...] if u < 0 else kcur_ref[u * BLOCK_Q:(u + 1) * BLOCK_Q, :]
            return jnp.concatenate([eye, k_t], axis=1)
        return at

    def rhs(j, slab):
        u, part = slab // slabs_per_tile, slab % slabs_per_tile
        slot = 1 + j if (u > 0 or j == 0) else jnp.where(i > 0, 2, 0)
        return jnp.concatenate([tbl_ref[slot, :, part * SLAB:(part + 1) * SLAB],
                                q_t[:, slab * SLAB:(slab + 1) * SLAB]], axis=0)

    def values(j, slab):
        u = slab // slabs_per_tile - j
        return vprev_ref[...] if u < 0 else vcur_ref[u]

    _slabwise_softmax_step(m0, acc_ref, [key_tile(0), key_tile(1)], rhs, values)
    out = _normalized(acc_ref[...])
    for u in range(SWA_TILES):
        o_ref[u * BLOCK_Q:(u + 1) * BLOCK_Q, :] = _heads_to_columns(
            out[:, u * GROUP_LANES:(u + 1) * GROUP_LANES]).astype(o_ref.dtype)


SWA_TILES = 8


def _swa_attention(h_t, kk, v_aug, sink_rows, vb, batch, seq):
    nq = seq // BLOCK_Q
    nblk = nq // SWA_TILES
    width = SWA_TILES * BLOCK_Q
    tokens = batch * seq

    def prev_tile(i):
        return jnp.maximum(i * SWA_TILES - 1, 0)

    return pl.pallas_call(
        _swa_kernel,
        grid=(GROUPS, batch, nblk),
        in_specs=[
            pl.BlockSpec((GROUP_COLS, width), lambda g, b, i: (g, b * nblk + i)),
            pl.BlockSpec((None, None, BLOCK_Q, HEAD_DIM), lambda g, b, i: (0, g, b * nq + prev_tile(i), 0)),
            pl.BlockSpec((None, None, width, HEAD_DIM), lambda g, b, i: (0, g, b * nblk + i, 0)),
            pl.BlockSpec((None, None, None, V_AUG_ROWS, BLOCK_Q), lambda g, b, i: (g, b, prev_tile(i), 0, 0)),
            pl.BlockSpec((None, None, SWA_TILES, V_AUG_ROWS, BLOCK_Q), lambda g, b, i: (g, b, i, 0, 0)),
            pl.BlockSpec((None, 1, GROUP_LANES), lambda g, b, i: (g, 0, 0)),
            pl.BlockSpec((None, HEADS_PER_GROUP, 4, BLOCK_Q), lambda g, b, i: (g, 0, 0, 0)),
        ],
        out_specs=pl.BlockSpec((width, GROUP_COLS), lambda g, b, i: (b * nblk + i, g)),
        out_shape=jax.ShapeDtypeStruct((tokens, SWA_HEADS * HEAD_DIM), BF16),
        scratch_shapes=[pltpu.VMEM((3, BLOCK_Q, GROUP_LANES), BF16),
                        pltpu.VMEM((V_AUG_ROWS, SWA_TILES * GROUP_LANES), F32)],
        compiler_params=_params(3),
        name="swa_attention",
    )(h_t, kk, kk, v_aug, v_aug, sink_rows, vb)


def _compress_kernel(c_ref, pos_ref, w1_ref, w2_ref, w2t_ref, cn_ref, ct_ref):
    c = c_ref[...].astype(F32)
    top = (c + pos_ref[0]).astype(BF16)
    bot = (c + pos_ref[1]).astype(BF16)
    a = jnp.dot(top, w1_ref[0], preferred_element_type=F32)
    bm = jnp.dot(bot, w1_ref[1], preferred_element_type=F32)
    n = a.shape[0]
    pre = a + pltpu.roll(bm, n - 1, 0)
    hid = jax.nn.gelu(pre).astype(BF16)
    cn_ref[...] = jnp.dot(hid, w2_ref[...], preferred_element_type=F32).astype(cn_ref.dtype)
    ct_ref[...] = lax.dot_general(w2t_ref[...], hid, (((1,), (1,)), ((), ())),
                                  preferred_element_type=F32).astype(ct_ref.dtype)


def _compress(chunks, pos, w1, w2, w2t):
    _, g, b, ncp, width = chunks.shape
    hidden = w1.shape[-1]
    return pl.pallas_call(
        _compress_kernel,
        grid=(2, g, b),
        in_specs=[
            pl.BlockSpec((None, None, None, ncp, width), lambda t, g, b: (t, g, b, 0, 0)),
            pl.BlockSpec((None, 2, 1, width), lambda t, g, b: (t, 0, 0, 0)),
            pl.BlockSpec((None, 2, width, hidden), lambda t, g, b: (t, 0, 0, 0)),
            pl.BlockSpec((None, hidden, HEAD_DIM), lambda t, g, b: (t, 0, 0)),
            pl.BlockSpec((None, HEAD_DIM, hidden), lambda t, g, b: (t, 0, 0)),
        ],
        out_specs=[
            pl.BlockSpec((None, None, None, ncp, HEAD_DIM), lambda t, g, b: (t, g, b, 0, 0)),
            pl.BlockSpec((None, None, None, HEAD_DIM, ncp), lambda t, g, b: (t, g, b, 0, 0)),
        ],
        out_shape=[jax.ShapeDtypeStruct((2, g, b, ncp, HEAD_DIM), BF16),
                   jax.ShapeDtypeStruct((2, g, b, HEAD_DIM, ncp), BF16)],
        compiler_params=_params(3),
        name="nsa_compress",
    )(chunks, pos, w1, w2, w2t)


def _cmp_kernel(q_ref, kc_ref, vct_ref, vb_ref, ov_ref, oc_ref, sel_ref, bias_ref, skew_ref, bias16_ref, *, nq):
    b, i = pl.program_id(1), pl.program_id(2)

    @pl.when((b == 0) & (i == 0))
    def _():
        _build_cmp_bias_table(vb_ref, bias_ref, skew_ref, nq)
        rows = bias_ref.shape[0]
        bias16_ref[0] = bias_ref[...].astype(BF16)
        bias16_ref[1, 0:rows - 16, :] = bias_ref[8:rows - 8, :].astype(BF16)

    ncp = kc_ref.shape[0]
    nsel = sel_ref.shape[0]
    n_tiles = ncp // BLOCK_Q
    q_t = _gather_heads_to_lanes(q_ref)
    width = SUBTILES * BLOCK_Q
    eye = _identity_tile()

    slabs_per_tile = GROUP_LANES // SLAB
    lhs = [jnp.concatenate([eye, kc_ref[t * BLOCK_Q:(t + 1) * BLOCK_Q, :]], axis=1) for t in range(n_tiles)]
    items = [(sl, t) for sl in range(SUBTILES * slabs_per_tile) for t in range(n_tiles)]

    def scores(k):
        sl, t = items[k]
        u, part = sl // slabs_per_tile, sl % slabs_per_tile
        shifted = (nq - 1 - u) % 2
        row0 = pl.multiple_of((nq - 1 - (SUBTILES * i + u)) * 8 + t * BLOCK_Q - 8 * shifted, 16)
        bias = bias16_ref[shifted, pl.ds(row0, BLOCK_Q), part * SLAB:(part + 1) * SLAB]
        return jnp.dot(lhs[t], jnp.concatenate([bias, q_t[:, sl * SLAB:(sl + 1) * SLAB]], axis=0),
                       preferred_element_type=F32)

    q_pos = i * width + lax.broadcasted_iota(jnp.int32, (1, width), 1)
    sees_any = q_pos >= CMP_BLOCK - 1
    psum = [[jnp.zeros((BLOCK_Q, BLOCK_Q), F32) for _ in range(n_tiles)] for _ in range(SUBTILES)]
    pending = [scores(k) for k in range(min(LOOKAHEAD, len(items)))]
    for sl in range(SUBTILES * slabs_per_tile):
        u, part = sl // slabs_per_tile, sl % slabs_per_tile
        s = [pending.pop(0) for _ in range(n_tiles)]
        for k in range(sl * n_tiles + LOOKAHEAD, min((sl + 1) * n_tiles + LOOKAHEAD, len(items))):
            pending.append(scores(k))
        m = functools.reduce(jnp.maximum, [jnp.max(c, axis=0, keepdims=True) for c in s])
        e = [jnp.exp(c - m) for c in s]
        l = functools.reduce(jnp.add, [jnp.sum(c, axis=0, keepdims=True) for c in e])
        seen = sees_any[:, u * BLOCK_Q:(u + 1) * BLOCK_Q]
        inv = jnp.where(jnp.concatenate([seen] * (SLAB // BLOCK_Q), axis=1), 1.0 / l, 0.0)
        o_slab = jnp.zeros((HEAD_DIM, SLAB), F32)
        for t in range(n_tiles):
            p = e[t] * inv
            psum[u][t] = psum[u][t] + functools.reduce(
                jnp.add, [p[:, r * BLOCK_Q:(r + 1) * BLOCK_Q] for r in range(SLAB // BLOCK_Q)])
            o_slab = o_slab + jnp.dot(vct_ref[:, t * BLOCK_Q:(t + 1) * BLOCK_Q], p.astype(BF16),
                                      preferred_element_type=F32)
        oc_ref[u, :, part * SLAB:(part + 1) * SLAB] = o_slab.astype(oc_ref.dtype)

    psum = jnp.concatenate([jnp.concatenate(pu, axis=0) for pu in psum], axis=1)
    hi = psum.astype(BF16)
    lo = (psum - hi.astype(F32)).astype(BF16)
    ov = ov_ref[...]
    score = jnp.dot(ov, hi, preferred_element_type=F32) + jnp.dot(ov, lo, preferred_element_type=F32)

    j_io = lax.broadcasted_iota(jnp.int32, (nsel, width), 0)
    qpos = i * width + lax.broadcasted_iota(jnp.int32, (nsel, width), 1)
    causal = j_io * SEL_BLOCK <= qpos
    back = qpos // SEL_BLOCK - j_io
    forced = (j_io == 0) | ((back >= 0) & (back < SEL_LOCAL))
    score = jnp.where(causal, jnp.where(forced, FORCE_SCORE, score), -1.0)
    slab_rows = lax.broadcasted_iota(jnp.int32, (8, width), 0)
    slabs = [score[8 * g:8 * (g + 1), :] for g in range(nsel // 8)]
    ranks = [jnp.zeros((8, width), F32) for _ in slabs]
    for r in range(nsel):
        row = jnp.broadcast_to(score[r:r + 1, :], (8, width))
        for g, slab in enumerate(slabs):
            if g > r // 8:
                ahead = row >= slab
            elif g < r // 8:
                ahead = row > slab
            else:
                ranks[g] = ranks[g] + jnp.where(slab_rows > r % 8, jnp.where(row >= slab, 1.0, 0.0),
                                                jnp.where(row > slab, 1.0, 0.0))
                continue
            ranks[g] = ranks[g] + jnp.where(ahead, 1.0, 0.0)
    rank = jnp.concatenate(ranks, axis=0)
    sel_ref[...] = jnp.where((rank < min(SEL_TOPK, nsel)) & causal, 1.0, 0.0).astype(sel_ref.dtype)


def _cmp_attention(h_t, cn, ct, vb, overlap_t, batch, seq):
    nq = seq // BLOCK_Q
    ncp = seq // CMP_STRIDE
    nsel = seq // SEL_BLOCK
    q_blk0 = SWA_HEADS * HEAD_DIM // GROUP_COLS
    nblk = nq // SUBTILES
    width = SUBTILES * BLOCK_Q
    return pl.pallas_call(
        functools.partial(_cmp_kernel, nq=nq),
        grid=(GROUPS, batch, nblk),
        in_specs=[
            pl.BlockSpec((GROUP_COLS, width), lambda g, b, i: (q_blk0 + g, b * nblk + i)),
            pl.BlockSpec((None, None, None, ncp, HEAD_DIM), lambda g, b, i: (0, g, b, 0, 0)),
            pl.BlockSpec((None, None, None, HEAD_DIM, ncp), lambda g, b, i: (1, g, b, 0, 0)),
            pl.BlockSpec((None, HEADS_PER_GROUP, nq + 2, BLOCK_Q), lambda g, b, i: (g, 0, 0, 0)),
            pl.BlockSpec((nsel, ncp), lambda g, b, i: (0, 0)),
        ],
        out_specs=[
            pl.BlockSpec((None, None, SUBTILES, HEAD_DIM, GROUP_LANES), lambda g, b, i: (b, g, i, 0, 0)),
            pl.BlockSpec((None, None, nsel, width), lambda g, b, i: (b, g, 0, i)),
        ],
        out_shape=[jax.ShapeDtypeStruct((batch, GROUPS, nq, HEAD_DIM, GROUP_LANES), BF16),
                   jax.ShapeDtypeStruct((batch, GROUPS, nsel, seq), F32)],
        scratch_shapes=[pltpu.VMEM((2 * nq * 8, GROUP_LANES), F32), pltpu.VMEM((BLOCK_Q, GROUP_LANES), F32),
                        pltpu.VMEM((2, 2 * nq * 8, GROUP_LANES), BF16)],
        compiler_params=_params(3),
        name="nsa_cmp_select",
    )(h_t, cn, ct, vb, overlap_t)


N_WIN_PAIRS = -(-(NSA_WINDOW - 1) // (SUBTILES * BLOCK_Q)) + 1
N_WIN_SLOTS = SUBTILES * N_WIN_PAIRS + 1
SEL_ROWS = 16
SEL_STEPS = 2


def _selwin_kernel(q_ref, ksel_ref, kwin_ref, vsel_ref, vwin_ref, sel_ref, oc_ref, gate_ref, vbs_ref, vbw_ref,
                   o_ref, tsel_ref, twin_ref, accs_ref, accw_ref, *, nq):
    b, blk = pl.program_id(1), pl.program_id(2)

    @pl.when((b == 0) & (blk == 0))
    def _():
        _build_skew_table(vbs_ref, tsel_ref, nq + 1)
        _build_skew_table(vbw_ref, twin_ref, N_WIN_SLOTS)

    q_t = _gather_heads_to_lanes(q_ref)
    lanes = q_t.shape[1]
    eye = _identity_tile()
    pair = SUBTILES * BLOCK_Q
    m0 = jnp.full((1, lanes), NEG_INF, F32)
    accs_ref[...] = jnp.zeros_like(accs_ref)
    accw_ref[...] = jnp.zeros_like(accw_ref)
    blocks_per_tile = BLOCK_Q // SEL_BLOCK
    slabs_per_tile = GROUP_LANES // SLAB

    def attend(m, steps, k_ref, tbl_ref, v_ref, acc_ref, q_ext_fn, reach=None):
        tiles = [(n, jnp.clip(blk - n, 0, blk), j) for n in steps for j in range(SUBTILES)]
        lhs = [jnp.concatenate([eye, k_ref[pl.ds(pl.multiple_of(p * pair + j * BLOCK_Q, BLOCK_Q), BLOCK_Q), :]],
                               axis=1) for _, p, j in tiles]
        q_exts = [q_ext_fn(p, j) for _, p, j in tiles]

        def rhs(t, slab):
            n, _, j = tiles[t]
            u, part = slab // slabs_per_tile, slab % slabs_per_tile
            slot = jnp.where(n <= blk, 2 * n + u - j + 1, 0)
            bias = tbl_ref[slot, :, part * SLAB:(part + 1) * SLAB]
            return jnp.concatenate([bias, q_exts[t][:, slab * SLAB:(slab + 1) * SLAB]], axis=0)

        def values(t, slab):
            _, p, j = tiles[t]
            return v_ref[p, :, j * BLOCK_Q:(j + 1) * BLOCK_Q]

        def masked(t, slab):
            n, _, j = tiles[t]
            distance = 2 * n + slab // slabs_per_tile - j
            return distance < 0 or distance >= reach

        return _slabwise_softmax_step(m, acc_ref, lhs, rhs, values, masked if reach is not None else None)

    attend(m0, list(range(N_WIN_PAIRS)), kwin_ref, twin_ref, vwin_ref, accw_ref, lambda p, j: q_t,
           reach=-(-(NSA_WINDOW + BLOCK_Q - 1) // BLOCK_Q))

    def sel_q_ext(p, j):
        per_pair = SUBTILES * blocks_per_tile
        group = sel_ref[pl.ds(pl.multiple_of((p // 2) * 2 * per_pair, 8), 2 * per_pair), :]
        lo, hi = (group[half * per_pair + j * blocks_per_tile:half * per_pair + (j + 1) * blocks_per_tile]
                  for half in range(2))
        neg = jnp.where(jnp.where(p % 2 == 1, hi, lo) > 0.5, 0.0, NEG_INF)
        rows = jnp.concatenate([neg[:, u * BLOCK_Q:(u + 1) * BLOCK_Q]
                                for u in range(SUBTILES) for _ in range(HEADS_PER_GROUP)], axis=1)
        rows = jnp.concatenate([rows, jnp.zeros((SEL_ROWS - blocks_per_tile, lanes), F32)], axis=0)
        return jnp.concatenate([q_t, rows.astype(BF16)], axis=0)

    def sel_body(k, m):
        return attend(m, [SEL_STEPS * k + r for r in range(SEL_STEPS)], ksel_ref, tsel_ref, vsel_ref, accs_ref,
                      sel_q_ext)

    lax.fori_loop(0, (blk + SEL_STEPS) // SEL_STEPS, sel_body, m0)

    def gate_row(branch):
        g = jnp.concatenate([gate_ref[branch, h:h + 1, u * BLOCK_Q:(u + 1) * BLOCK_Q]
                             for u in range(SUBTILES) for h in range(HEADS_PER_GROUP)], axis=1)
        return jax.nn.sigmoid(g.astype(F32))

    o_c = jnp.concatenate([oc_ref[u] for u in range(SUBTILES)], axis=1).astype(F32)
    out = (gate_row(0) * o_c + gate_row(1) * _normalized(accs_ref[...])
           + gate_row(2) * _normalized(accw_ref[...]))
    for u in range(SUBTILES):
        o_ref[u * BLOCK_Q:(u + 1) * BLOCK_Q, :] = _heads_to_columns(
            out[:, u * GROUP_LANES:(u + 1) * GROUP_LANES]).astype(o_ref.dtype)


def _selwin_attention(h_t, ksel_aug, kk, vsel_aug, vwin_aug, sel, oc_t, gates_t, vb_sel, vb_win, batch, seq):
    nq = seq // BLOCK_Q
    nblk = nq // SUBTILES
    tokens = batch * seq
    q_blk0 = SWA_HEADS * HEAD_DIM // GROUP_COLS
    pair = SUBTILES * BLOCK_Q
    lanes = SUBTILES * GROUP_LANES
    return pl.pallas_call(
        functools.partial(_selwin_kernel, nq=nq),
        grid=(GROUPS, batch, nblk),
        in_specs=[
            pl.BlockSpec((GROUP_COLS, pair), lambda g, b, i: (q_blk0 + g, b * nblk + i)),
            pl.BlockSpec((None, seq, HEAD_DIM + SEL_ROWS), lambda g, b, i: (g, b, 0)),
            pl.BlockSpec((None, None, seq, HEAD_DIM), lambda g, b, i: (3, g, b, 0)),
            pl.BlockSpec((None, None, nblk, V_AUG_ROWS, pair), lambda g, b, i: (g, b, 0, 0, 0)),
            pl.BlockSpec((None, None, nblk, V_AUG_ROWS, pair), lambda g, b, i: (g, b, 0, 0, 0)),
            pl.BlockSpec((None, None, seq // SEL_BLOCK, pair), lambda g, b, i: (b, g, 0, i)),
            pl.BlockSpec((None, None, SUBTILES, HEAD_DIM, GROUP_LANES), lambda g, b, i: (b, g, i, 0, 0)),
            pl.BlockSpec((None, 3, HEADS_PER_GROUP, pair), lambda g, b, i: (g, 0, 0, b * nblk + i)),
            pl.BlockSpec((None, HEADS_PER_GROUP, nq + 2, BLOCK_Q), lambda g, b, i: (g, 0, 0, 0)),
            pl.BlockSpec((None, HEADS_PER_GROUP, N_WIN_SLOTS + 1, BLOCK_Q), lambda g, b, i: (g, 0, 0, 0)),
        ],
        out_specs=pl.BlockSpec((pair, GROUP_COLS), lambda g, b, i: (b * nblk + i, g)),
        out_shape=jax.ShapeDtypeStruct((tokens, NSA_HEADS * HEAD_DIM), BF16),
        scratch_shapes=[pltpu.VMEM((nq + 1, BLOCK_Q, GROUP_LANES), BF16),
                        pltpu.VMEM((N_WIN_SLOTS, BLOCK_Q, GROUP_LANES), BF16),
                        pltpu.VMEM((V_AUG_ROWS, lanes), F32),
                        pltpu.VMEM((V_AUG_ROWS, lanes), F32)],
        compiler_params=_params(3),
        name="nsa_sel_win",
    )(h_t, ksel_aug, kk, vsel_aug, vwin_aug, sel, oc_t, gates_t, vb_sel, vb_win)


ROW_SPLITS = 2


def _layer_norm(y, g_ref, b_ref):
    mu = jnp.mean(y, axis=-1, keepdims=True)
    yc = y - mu
    var = jnp.mean(yc * yc, axis=-1, keepdims=True)
    return yc * lax.rsqrt(var + LN_EPS) * g_ref[...] + b_ref[...]


def _merge_kernel(x_ref, ya_ref, yb_ref, wga_ref, wgb_ref, wa_ref, wb_ref, o_ref):
    x = x_ref[...]
    ga = jax.nn.sigmoid(jnp.dot(x, wga_ref[...], preferred_element_type=F32))
    gb = jax.nn.sigmoid(jnp.dot(x, wgb_ref[...], preferred_element_type=F32))
    a = jnp.dot(ya_ref[...], wa_ref[...], preferred_element_type=F32)
    bb = jnp.dot(yb_ref[...], wb_ref[...], preferred_element_type=F32)
    o_ref[...] = (ga * a + gb * bb).astype(o_ref.dtype)


def _merge(x_bf, y_a, y_b, w_ga, w_gb, w_a, w_b, *, tm, tn):
    tokens, d = x_bf.shape
    ya_cols, yb_cols = y_a.shape[1], y_b.shape[1]
    return pl.pallas_call(
        _merge_kernel,
        grid=(tokens // tm, d // tn),
        in_specs=[
            pl.BlockSpec((tm, d), lambda i, j: (i, 0)),
            pl.BlockSpec((tm, ya_cols), lambda i, j: (i, 0)),
            pl.BlockSpec((tm, yb_cols), lambda i, j: (i, 0)),
            pl.BlockSpec((d, tn), lambda i, j: (0, j)),
            pl.BlockSpec((d, tn), lambda i, j: (0, j)),
            pl.BlockSpec((ya_cols, tn), lambda i, j: (0, j)),
            pl.BlockSpec((yb_cols, tn), lambda i, j: (0, j)),
        ],
        out_specs=pl.BlockSpec((tm, tn), lambda i, j: (i, j)),
        out_shape=jax.ShapeDtypeStruct((tokens, d), BF16),
        compiler_params=_params(2),
        name="branch_merge",
    )(x_bf, y_a, y_b, w_ga, w_gb, w_a, w_b)


def _mix_ln_kernel(m_ref, x_ref, w_ref, g_ref, b_ref, o_ref):
    rows = m_ref.shape[0] // ROW_SPLITS
    for r in range(ROW_SPLITS):
        sl = slice(r * rows, (r + 1) * rows)
        mix = jnp.dot(m_ref[sl, :], w_ref[...], preferred_element_type=F32)
        o_ref[sl, :] = _layer_norm(DN_ALPHA * x_ref[sl, :] + mix, g_ref, b_ref)


def _mix_ln(merged, x, w_mix, ln_g, ln_b, *, tm):
    tokens, d = x.shape
    return pl.pallas_call(
        _mix_ln_kernel,
        grid=(tokens // tm,),
        in_specs=[
            pl.BlockSpec((tm, d), lambda i: (i, 0)),
            pl.BlockSpec((tm, d), lambda i: (i, 0)),
            pl.BlockSpec((d, d), lambda i: (0, 0)),
            pl.BlockSpec((1, d), lambda i: (0, 0)),
            pl.BlockSpec((1, d), lambda i: (0, 0)),
        ],
        out_specs=pl.BlockSpec((tm, d), lambda i: (i, 0)),
        out_shape=jax.ShapeDtypeStruct((tokens, d), F32),
        compiler_params=_params(1),
        name="mix_out_ln1",
    )(merged, x, w_mix, ln_g, ln_b)


def _xa_kernel(h_ref, wq_ref, k_ref, v_ref, wo_ref, g_ref, b_ref, o_ref):
    h = h_ref[...]
    q = jnp.dot(h.astype(BF16), wq_ref[...], preferred_element_type=F32) * (XA_HEAD_DIM ** -0.5)
    q = q.astype(BF16)
    outs = []
    for hd in range(XA_HEADS):
        cols = slice(hd * XA_HEAD_DIM, (hd + 1) * XA_HEAD_DIM)
        s = lax.dot_general(q[:, cols], k_ref[:, cols], (((1,), (1,)), ((), ())), preferred_element_type=F32)
        p = jnp.exp(s - jnp.max(s, axis=-1, keepdims=True))
        l = jnp.sum(p, axis=-1, keepdims=True)
        o = jnp.dot(p.astype(BF16), v_ref[:, cols], preferred_element_type=F32)
        outs.append(o * (1.0 / l))
    o = jnp.concatenate(outs, axis=1).astype(BF16)
    xa = jnp.dot(o, wo_ref[...], preferred_element_type=F32)
    o_ref[...] = _layer_norm(DN_ALPHA * h + xa, g_ref, b_ref)


def _cross_attention_ln(h, kv_mem, w_q, w_o, ln_g, ln_b, batch, seq, *, tm):
    tokens, d = h.shape
    mem_len = kv_mem.shape[0] // batch
    xa_dim = XA_HEADS * XA_HEAD_DIM
    nt = seq // tm
    return pl.pallas_call(
        _xa_kernel,
        grid=(batch, nt),
        in_specs=[
            pl.BlockSpec((tm, d), lambda b, i: (b * nt + i, 0)),
            pl.BlockSpec((d, xa_dim), lambda b, i: (0, 0)),
            pl.BlockSpec((mem_len, xa_dim), lambda b, i: (b, 0)),
            pl.BlockSpec((mem_len, xa_dim), lambda b, i: (b, 1)),
            pl.BlockSpec((xa_dim, d), lambda b, i: (0, 0)),
            pl.BlockSpec((1, d), lambda b, i: (0, 0)),
            pl.BlockSpec((1, d), lambda b, i: (0, 0)),
        ],
        out_specs=pl.BlockSpec((tm, d), lambda b, i: (b * nt + i, 0)),
        out_shape=jax.ShapeDtypeStruct((tokens, d), F32),
        compiler_params=_params(2),
        name="cross_attention_ln2",
    )(h, w_q, kv_mem, kv_mem, w_o, ln_g, ln_b)


def _mlp_kernel(h_ref, w1_ref, w2_ref, g_ref, b_ref, o_ref, hb_ref):
    j = pl.program_id(1)

    @pl.when(j == 0)
    def _():
        hb_ref[...] = h_ref[...].astype(BF16)
        o_ref[...] = jnp.zeros_like(o_ref)

    rows = hb_ref.shape[0] // ROW_SPLITS
    subs = [slice(r * rows, (r + 1) * rows) for r in range(ROW_SPLITS)]
    ups = [jnp.dot(hb_ref[sl, :], w1_ref[...], preferred_element_type=F32) for sl in subs]
    for sl, u in zip(subs, ups):
        u = jnp.square(jnp.maximum(u, 0.0)).astype(BF16)
        o_ref[sl, :] += jnp.dot(u, w2_ref[...], preferred_element_type=F32)

    @pl.when(j == pl.num_programs(1) - 1)
    def _():
        o_ref[...] = _layer_norm(DN_ALPHA * h_ref[...] + o_ref[...], g_ref, b_ref)


def _mlp_ln(h, w1, w2, ln_g, ln_b, *, tm, tf):
    tokens, d = h.shape
    d_ff = w1.shape[1]
    return pl.pallas_call(
        _mlp_kernel,
        grid=(tokens // tm, d_ff // tf),
        in_specs=[
            pl.BlockSpec((tm, d), lambda i, j: (i, 0)),
            pl.BlockSpec((d, tf), lambda i, j: (0, j)),
            pl.BlockSpec((tf, d), lambda i, j: (j, 0)),
            pl.BlockSpec((1, d), lambda i, j: (0, 0)),
            pl.BlockSpec((1, d), lambda i, j: (0, 0)),
        ],
        out_specs=pl.BlockSpec((tm, d), lambda i, j: (i, 0)),
        out_shape=jax.ShapeDtypeStruct((tokens, d), F32),
        scratch_shapes=[pltpu.VMEM((tm, d), BF16)],
        compiler_params=_params(2),
        name="mlp_ln3",
    )(h, w1, w2, ln_g, ln_b)


def _overlap_t(seq):
    ncp, nsel = seq // CMP_STRIDE, seq // SEL_BLOCK
    c_start = np.arange(ncp)[None, :] * CMP_STRIDE
    s_start = np.arange(nsel)[:, None] * SEL_BLOCK
    ov = (c_start < s_start + SEL_BLOCK) & (c_start + CMP_BLOCK > s_start) & (np.arange(ncp)[None, :] < ncp - 1)
    return jnp.asarray(ov, BF16)


def kernel(x, mem, w_in, attn_sinks, rel_bias_table, cmp_pos_k, cmp_w1_k, cmp_w2_k, cmp_pos_v, cmp_w1_v, cmp_w2_v,
           w_branch_swa, w_branch_nsa, w_mix_out, ln1_g, ln1_b, xa_w_q, xa_w_kv, xa_w_o, ln2_g, ln2_b,
           mlp_w1, mlp_w2, ln3_g, ln3_b):
    batch, seq, d = x.shape
    tokens = batch * seq
    nq = seq // BLOCK_Q
    assert w_in.shape[0] == 1, "one layer"
    assert seq % (8 * BLOCK_Q) == 0 and d % 512 == 0

    w = w_in[0]
    sizes = (SWA_HEADS * HEAD_DIM, 128, 128, NSA_HEADS * HEAD_DIM, 128, 128, 128, 128, 128, 128, 3 * NSA_HEADS, d, d)
    offs = np.concatenate([[0], np.cumsum(sizes)])
    (w_qa, w_ka, w_va, w_qb, w_kc, w_vc, w_ks, w_vs, w_kw, w_vw, w_gn, w_ga, w_gb) = [
        w[:, offs[n]:offs[n + 1]] for n in range(len(sizes))]
    w_rows = jnp.concatenate([w_ka, w_kc, w_ks, w_kw, w_vc], axis=1).astype(BF16)
    w_gn_pad = jnp.pad(w_gn, ((0, 0), (0, 128 - w_gn.shape[1])))
    w_cols_t = jnp.concatenate([w_qa, w_qb, w_va, w_vs, w_vw, w_gn_pad], axis=1).T.astype(BF16)

    x2 = x.reshape(tokens, d)

    h_t, x_bf, kk = _input_projections(x2, w_cols_t, w_rows, tm=512)

    def value_tiles(row0, width):
        v = h_t[row0:row0 + 128].reshape(GROUPS, HEAD_DIM, batch, seq // width, width).transpose(0, 2, 3, 1, 4)
        ones = jnp.ones(v.shape[:3] + (V_AUG_ROWS - HEAD_DIM, width), BF16)
        return jnp.concatenate([v, ones], axis=3)

    vswa_aug = value_tiles(2048, BLOCK_Q)
    vsel_aug, vwin_aug = value_tiles(2176, 2 * BLOCK_Q), value_tiles(2304, 2 * BLOCK_Q)
    gates_t = h_t[2432:2432 + 3 * NSA_HEADS].reshape(GROUPS, HEADS_PER_GROUP, 3, tokens).transpose(0, 2, 1, 3)
    block_of_key = (jnp.arange(tokens) % BLOCK_Q) // SEL_BLOCK
    onehot = (block_of_key[:, None] == jnp.arange(SEL_ROWS)[None, :]).astype(BF16)
    ksel_aug = jnp.concatenate([kk[2], jnp.broadcast_to(onehot, (GROUPS, tokens, SEL_ROWS))], axis=2)

    by_dist = rel_bias_table[_rel_bucket(jnp.arange(seq))].astype(F32).T
    vb_swa = _bias_by_distance(by_dist[:SWA_HEADS], 4, window=SWA_WINDOW)
    vb_sel = _bias_by_distance(by_dist[SWA_HEADS:], nq + 2)
    vb_win = _bias_by_distance(by_dist[SWA_HEADS:], N_WIN_SLOTS + 1, window=NSA_WINDOW)
    vb_cmp = vb_sel
    sink_rows = jnp.repeat(attn_sinks[0].astype(F32), BLOCK_Q).reshape(GROUPS, 1, GROUP_LANES)

    y_a = _swa_attention(h_t, kk, vswa_aug, sink_rows, vb_swa, batch, seq)

    ncp = seq // CMP_STRIDE
    chunk_w = CMP_STRIDE * HEAD_DIM
    chunks = jnp.stack([kk[1], kk[4]]).reshape(2, GROUPS, batch, ncp, chunk_w)
    pos = jnp.stack([cmp_pos_k[0], cmp_pos_v[0]]).astype(F32).reshape(2, 2, 1, chunk_w)
    w1 = jnp.stack([cmp_w1_k[0], cmp_w1_v[0]]).astype(BF16).reshape(2, 2, chunk_w, -1)
    w2 = jnp.stack([cmp_w2_k[0], cmp_w2_v[0]]).astype(BF16)
    cn, ct = _compress(chunks, pos, w1, w2, w2.transpose(0, 2, 1))
    oc_t, sel = _cmp_attention(h_t, cn, ct, vb_cmp, _overlap_t(seq), batch, seq)
    y_b = _selwin_attention(h_t, ksel_aug, kk, vsel_aug, vwin_aug, sel, oc_t, gates_t, vb_sel, vb_win, batch, seq)

    merged = _merge(x_bf, y_a, y_b, w_ga.astype(BF16), w_gb.astype(BF16),
                    w_branch_swa[0].astype(BF16), w_branch_nsa[0].astype(BF16), tm=min(1024, tokens), tn=1024)
    h1 = _mix_ln(merged, x2, w_mix_out[0].astype(BF16), ln1_g, ln1_b, tm=512)

    mem_bf = mem.reshape(-1, d).astype(BF16)
    kv_mem = _matmul(mem_bf, xa_w_kv[0].astype(BF16), tm=mem_bf.shape[0], tn=512)
    h2 = _cross_attention_ln(h1, kv_mem, xa_w_q[0].astype(BF16), xa_w_o[0].astype(BF16), ln2_g, ln2_b,
                             batch, seq, tm=1024)

    h3 = _mlp_ln(h2, mlp_w1[0].astype(BF16), mlp_w2[0].astype(BF16), ln3_g, ln3_b, tm=1024, tf=512)
    return h3.reshape(batch, seq, d)
```

```python
import functools
import math

import numpy as np
import jax
import jax.numpy as jnp
from jax import lax
from jax.experimental import pallas as pl
from jax.experimental.pallas import tpu as pltpu

F32 = jnp.float32
BF16 = jnp.bfloat16

HEAD_DIM = 64
BLOCK_Q = 128
SWA_HEADS = 16
SWA_KV_HEADS = 2
SWA_WINDOW = 128
NSA_HEADS = 16
NSA_KV_HEADS = 2
CMP_BLOCK = 32
CMP_STRIDE = 16
SEL_BLOCK = 64
SEL_TOPK = 16
SEL_LOCAL = 2
NSA_WINDOW = 512
REL_BUCKETS = 32
REL_MAX_DIST = 4096
XA_HEADS = 4
XA_HEAD_DIM = 128
DEPTH = 1
DN_ALPHA = (2.0 * DEPTH) ** 0.25
LN_EPS = 1e-5
NEG_INF = -1e30
FORCE_SCORE = 1e4

GROUPS = 2
HEADS_PER_GROUP = 8
GROUP_LANES = HEADS_PER_GROUP * BLOCK_Q
GROUP_COLS = HEADS_PER_GROUP * HEAD_DIM
SUBTILES = 2

V7X_VMEM_LIMIT_BYTES = 56 * 1024 * 1024


def _params(n_axes):
    return pltpu.CompilerParams(dimension_semantics=("arbitrary",) * n_axes,
                                vmem_limit_bytes=V7X_VMEM_LIMIT_BYTES)


def _mm_kernel(a_ref, b_ref, o_ref):
    o_ref[...] = jnp.dot(a_ref[...], b_ref[...], preferred_element_type=F32).astype(o_ref.dtype)


def _matmul(a, b, *, tm, tn, out_dtype=BF16):
    m, k = a.shape
    n = b.shape[1]
    return pl.pallas_call(
        _mm_kernel,
        grid=(m // tm, n // tn),
        in_specs=[pl.BlockSpec((tm, k), lambda i, j: (i, 0)), pl.BlockSpec((k, tn), lambda i, j: (0, j))],
        out_specs=pl.BlockSpec((tm, tn), lambda i, j: (i, j)),
        out_shape=jax.ShapeDtypeStruct((m, n), out_dtype),
        compiler_params=_params(2),
        name="mem_kv_projection",
    )(a, b)


def _in_proj_kernel(x_ref, wt_ref, wr_ref, ht_ref, xb_ref, kk_ref):
    xb = x_ref[...].astype(BF16)
    xb_ref[...] = xb
    ht_ref[...] = lax.dot_general(wt_ref[...], xb, (((1,), (1,)), ((), ())),
                                  preferred_element_type=F32).astype(ht_ref.dtype)
    rows = jnp.dot(xb, wr_ref[...], preferred_element_type=F32)
    for c in range(kk_ref.shape[0]):
        for g in range(GROUPS):
            col = (c * GROUPS + g) * HEAD_DIM
            kk_ref[c, g] = rows[:, col:col + HEAD_DIM].astype(kk_ref.dtype)


def _input_projections(x, w_cols_t, w_rows, *, tm):
    tokens, d = x.shape
    cols = w_cols_t.shape[0]
    n_rows = w_rows.shape[1] // (GROUPS * HEAD_DIM)
    return pl.pallas_call(
        _in_proj_kernel,
        grid=(tokens // tm,),
        in_specs=[
            pl.BlockSpec((tm, d), lambda i: (i, 0)),
            pl.BlockSpec((cols, d), lambda i: (0, 0)),
            pl.BlockSpec((d, w_rows.shape[1]), lambda i: (0, 0)),
        ],
        out_specs=[
            pl.BlockSpec((cols, tm), lambda i: (0, i)),
            pl.BlockSpec((tm, d), lambda i: (i, 0)),
            pl.BlockSpec((n_rows, GROUPS, tm, HEAD_DIM), lambda i: (0, 0, i, 0)),
        ],
        out_shape=[jax.ShapeDtypeStruct((cols, tokens), BF16),
                   jax.ShapeDtypeStruct((tokens, d), BF16),
                   jax.ShapeDtypeStruct((n_rows, GROUPS, tokens, HEAD_DIM), BF16)],
        compiler_params=_params(1),
        name="input_projections",
    )(x, w_cols_t, w_rows)


def _rel_bucket(dist):
    exact = REL_BUCKETS // 2
    d = jnp.maximum(dist, 0)
    log_ratio = jnp.log(jnp.maximum(d, 1).astype(F32) / exact) / math.log(REL_MAX_DIST / exact)
    large = jnp.minimum(exact + (log_ratio * (REL_BUCKETS - exact)).astype(jnp.int32), REL_BUCKETS - 1)
    return jnp.where(d < exact, d, large)


def _bias_by_distance(by_dist, rows, *, window=None):
    heads, seq = by_dist.shape
    if window is not None:
        by_dist = jnp.where(jnp.arange(seq) < window, by_dist, NEG_INF)
    padded = jnp.pad(by_dist, ((0, 0), (2 * BLOCK_Q, 0)), constant_values=NEG_INF)[:, :rows * BLOCK_Q]
    return padded.reshape(heads // HEADS_PER_GROUP, HEADS_PER_GROUP, rows, BLOCK_Q)


def _build_skew_table(vb_ref, tbl_ref, n_tiles):
    def body(d, carry):
        for h in range(HEADS_PER_GROUP):
            lo = jnp.broadcast_to(vb_ref[h, pl.ds(d, 1), :], (BLOCK_Q, BLOCK_Q))
            hi = jnp.broadcast_to(vb_ref[h, pl.ds(d + 1, 1), :], (BLOCK_Q, BLOCK_Q))
            y = pltpu.roll(jnp.concatenate([lo, hi], axis=1), 0, 1, stride=1, stride_axis=0)
            tbl_ref[d, :, h * BLOCK_Q:(h + 1) * BLOCK_Q] = y[:, BLOCK_Q:].astype(tbl_ref.dtype)
        return carry

    lax.fori_loop(0, n_tiles, body, 0)


def _build_cmp_bias_table(vb_ref, tbl_ref, skew_ref, nq):
    per_tile = BLOCK_Q // CMP_STRIDE
    assert per_tile == 8 and CMP_BLOCK - 1 + CMP_STRIDE * (per_tile - 1) - BLOCK_Q == CMP_STRIDE - 1
    tbl_ref[pl.ds(nq * 8, nq * 8), :] = jnp.full((nq * 8, GROUP_LANES), NEG_INF, F32)

    def body(d, last_row):
        for h in range(HEADS_PER_GROUP):
            lo = jnp.broadcast_to(vb_ref[h, pl.ds(d + 1, 1), :], (BLOCK_Q, BLOCK_Q))
            hi = jnp.broadcast_to(vb_ref[h, pl.ds(d + 2, 1), :], (BLOCK_Q, BLOCK_Q))
            y = pltpu.roll(jnp.concatenate([lo, hi], axis=1), 0, 1, stride=1, stride_axis=0)
            skew_ref[:, h * BLOCK_Q:(h + 1) * BLOCK_Q] = y[:, BLOCK_Q:]
        offsets = [CMP_BLOCK - 1 + CMP_STRIDE * c for c in range(per_tile - 1)]
        rows = [skew_ref[k:k + 1, :] for k in offsets] + [last_row]
        tbl_ref[pl.ds(pl.multiple_of((nq - 1 - d) * 8, 8), 8), :] = jnp.concatenate(rows, axis=0)
        return skew_ref[CMP_STRIDE - 1:CMP_STRIDE, :]

    lax.fori_loop(0, nq, body, jnp.full((1, GROUP_LANES), NEG_INF, F32))


def _gather_heads_to_lanes(q_ref):
    q = jnp.concatenate([q_ref[h * HEAD_DIM:(h + 1) * HEAD_DIM, u * BLOCK_Q:(u + 1) * BLOCK_Q]
                         for u in range(q_ref.shape[1] // BLOCK_Q) for h in range(HEADS_PER_GROUP)], axis=1)
    return q * jnp.asarray(HEAD_DIM ** -0.5, q.dtype)


def _heads_to_columns(o_t):
    stacked = jnp.concatenate([o_t[:, h * BLOCK_Q:(h + 1) * BLOCK_Q] for h in range(HEADS_PER_GROUP)], axis=0)
    return stacked.T


def _identity_tile():
    r = lax.broadcasted_iota(jnp.int32, (BLOCK_Q, BLOCK_Q), 0)
    c = lax.broadcasted_iota(jnp.int32, (BLOCK_Q, BLOCK_Q), 1)
    return jnp.where(r == c, 1.0, 0.0).astype(BF16)


SLAB = 2 * BLOCK_Q
LOOKAHEAD = 12


def _slabwise_softmax_step(m, acc_ref, lhs_tiles, rhs_fn, v_fn, masked_fn=None):
    per_slab = [[j for j in range(len(lhs_tiles)) if masked_fn is None or not masked_fn(j, sl)]
                for sl in range(m.shape[1] // SLAB)]
    items = [(sl, j) for sl, tiles in enumerate(per_slab) for j in tiles]

    def scores(k):
        sl, j = items[k]
        lhs = lhs_tiles[j](sl) if callable(lhs_tiles[j]) else lhs_tiles[j]
        return jnp.dot(lhs, rhs_fn(j, sl), preferred_element_type=F32)

    pending = [scores(k) for k in range(min(LOOKAHEAD, len(items)))]
    issued = len(pending)
    new_m = []
    for sl, tiles in enumerate(per_slab):
        lanes = slice(sl * SLAB, (sl + 1) * SLAB)
        m_s, acc_s = m[:, lanes], acc_ref[:, lanes]
        for c in range(0, len(tiles), 2):
            chunk = tiles[c:c + 2]
            m_old, ps = m_s, []
            for j in chunk:
                s = pending.pop(0)
                if issued < len(items):
                    pending.append(scores(issued))
                    issued += 1
                m_next = jnp.maximum(m_s, jnp.max(s, axis=0, keepdims=True))
                ps = [q * jnp.exp(m_s - m_next).astype(BF16) for q in ps]
                ps.append(jnp.exp((s - m_next).astype(BF16)))
                m_s = m_next
            v = jnp.concatenate([v_fn(j, sl) for j in chunk], axis=1)
            acc_s = jnp.exp(m_old - m_s) * acc_s + jnp.dot(v, jnp.concatenate(ps, axis=0),
                                                           preferred_element_type=F32)
        acc_ref[:, lanes] = acc_s
        new_m.append(m_s)
    return jnp.concatenate(new_m, axis=1)


V_AUG_ROWS = HEAD_DIM + 16


def _normalized(acc):
    return acc[:HEAD_DIM] * (1.0 / acc[HEAD_DIM:HEAD_DIM + 1])


def _swa_kernel(q_ref, kprev_ref, kcur_ref, vprev_ref, vcur_ref, sink_ref, vb_ref, o_ref, tbl_ref, acc_ref):
    b, i = pl.program_id(1), pl.program_id(2)

    @pl.when((b == 0) & (i == 0))
    def _():
        _build_skew_table(vb_ref, tbl_ref, 3)

    eye = _identity_tile()
    q_t = _gather_heads_to_lanes(q_ref)
    lanes = q_t.shape[1]
    slabs_per_tile = GROUP_LANES // SLAB
    m0 = jnp.concatenate([sink_ref[...]] * SWA_TILES, axis=1)
    acc_ref[...] = jnp.where(lax.broadcasted_iota(jnp.int32, (V_AUG_ROWS, lanes), 0) < HEAD_DIM, 0.0, 1.0)

    def key_tile(j):
        def at(slab):
            u = slab // slabs_per_tile - j
            k_t = kprev_ref[...] if u < 0 else kcur_ref[u * BLOCK_Q:(u + 1) * BLOCK_Q, :]
            return jnp.concatenate([eye, k_t], axis=1)
        return at

    def rhs(j, slab):
        u, part = slab // slabs_per_tile, slab % slabs_per_tile
        slot = 1 + j if (u > 0 or j == 0) else jnp.where(i > 0, 2, 0)
        return jnp.concatenate([tbl_ref[slot, :, part * SLAB:(part + 1) * SLAB],
                                q_t[:, slab * SLAB:(slab + 1) * SLAB]], axis=0)

    def values(j, slab):
        u = slab // slabs_per_tile - j
        return vprev_ref[...] if u < 0 else vcur_ref[u]

    _slabwise_softmax_step(m0, acc_ref, [key_tile(0), key_tile(1)], rhs, values)
    out = _normalized(acc_ref[...])
    for u in range(SWA_TILES):
        o_ref[u * BLOCK_Q:(u + 1) * BLOCK_Q, :] = _heads_to_columns(
            out[:, u * GROUP_LANES:(u + 1) * GROUP_LANES]).astype(o_ref.dtype)


SWA_TILES = 16


def _swa_attention(h_t, kk, v_aug, sink_rows, vb, batch, seq):
    nq = seq // BLOCK_Q
    nblk = nq // SWA_TILES
    width = SWA_TILES * BLOCK_Q
    tokens = batch * seq

    def prev_tile(i):
        return jnp.maximum(i * SWA_TILES - 1, 0)

    return pl.pallas_call(
        _swa_kernel,
        grid=(GROUPS, batch, nblk),
        in_specs=[
            pl.BlockSpec((GROUP_COLS, width), lambda g, b, i: (g, b * nblk + i)),
            pl.BlockSpec((None, None, BLOCK_Q, HEAD_DIM), lambda g, b, i: (0, g, b * nq + prev_tile(i), 0)),
            pl.BlockSpec((None, None, width, HEAD_DIM), lambda g, b, i: (0, g, b * nblk + i, 0)),
            pl.BlockSpec((None, None, None, V_AUG_ROWS, BLOCK_Q), lambda g, b, i: (g, b, prev_tile(i), 0, 0)),
            pl.BlockSpec((None, None, SWA_TILES, V_AUG_ROWS, BLOCK_Q), lambda g, b, i: (g, b, i, 0, 0)),
            pl.BlockSpec((None, 1, GROUP_LANES), lambda g, b, i: (g, 0, 0)),
            pl.BlockSpec((None, HEADS_PER_GROUP, 4, BLOCK_Q), lambda g, b, i: (g, 0, 0, 0)),
        ],
        out_specs=pl.BlockSpec((width, GROUP_COLS), lambda g, b, i: (b * nblk + i, g)),
        out_shape=jax.ShapeDtypeStruct((tokens, SWA_HEADS * HEAD_DIM), BF16),
        scratch_shapes=[pltpu.VMEM((3, BLOCK_Q, GROUP_LANES), BF16),
                        pltpu.VMEM((V_AUG_ROWS, SWA_TILES * GROUP_LANES), F32)],
        compiler_params=_params(3),
        name="swa_attention",
    )(h_t, kk, kk, v_aug, v_aug, sink_rows, vb)


def _compress_kernel(c_ref, pos_ref, w1_ref, w2_ref, w2t_ref, cn_ref, ct_ref):
    c = c_ref[...].astype(F32)
    top = (c + pos_ref[0]).astype(BF16)
    bot = (c + pos_ref[1]).astype(BF16)
    a = jnp.dot(top, w1_ref[0], preferred_element_type=F32)
    bm = jnp.dot(bot, w1_ref[1], preferred_element_type=F32)
    n = a.shape[0]
    pre = a + pltpu.roll(bm, n - 1, 0)
    hid = jax.nn.gelu(pre).astype(BF16)
    cn_ref[...] = jnp.dot(hid, w2_ref[...], preferred_element_type=F32).astype(cn_ref.dtype)
    ct_ref[...] = lax.dot_general(w2t_ref[...], hid, (((1,), (1,)), ((), ())),
                                  preferred_element_type=F32).astype(ct_ref.dtype)


def _compress(chunks, pos, w1, w2, w2t):
    _, g, b, ncp, width = chunks.shape
    hidden = w1.shape[-1]
    return pl.pallas_call(
        _compress_kernel,
        grid=(2, g, b),
        in_specs=[
            pl.BlockSpec((None, None, None, ncp, width), lambda t, g, b: (t, g, b, 0, 0)),
            pl.BlockSpec((None, 2, 1, width), lambda t, g, b: (t, 0, 0, 0)),
            pl.BlockSpec((None, 2, width, hidden), lambda t, g, b: (t, 0, 0, 0)),
            pl.BlockSpec((None, hidden, HEAD_DIM), lambda t, g, b: (t, 0, 0)),
            pl.BlockSpec((None, HEAD_DIM, hidden), lambda t, g, b: (t, 0, 0)),
        ],
        out_specs=[
            pl.BlockSpec((None, None, None, ncp, HEAD_DIM), lambda t, g, b: (t, g, b, 0, 0)),
            pl.BlockSpec((None, None, None, HEAD_DIM, ncp), lambda t, g, b: (t, g, b, 0, 0)),
        ],
        out_shape=[jax.ShapeDtypeStruct((2, g, b, ncp, HEAD_DIM), BF16),
                   jax.ShapeDtypeStruct((2, g, b, HEAD_DIM, ncp), BF16)],
        compiler_params=_params(3),
        name="nsa_compress",
    )(chunks, pos, w1, w2, w2t)


def _cmp_kernel(q_ref, kc_ref, vct_ref, vb_ref, ov_ref, oc_ref, sel_ref, bias_ref, skew_ref, bias16_ref, *, nq):
    b, i = pl.program_id(1), pl.program_id(2)

    @pl.when((b == 0) & (i == 0))
    def _():
        _build_cmp_bias_table(vb_ref, bias_ref, skew_ref, nq)
        rows = bias_ref.shape[0]
        bias16_ref[0] = bias_ref[...].astype(BF16)
        bias16_ref[1, 0:rows - 16, :] = bias_ref[8:rows - 8, :].astype(BF16)

    ncp = kc_ref.shape[0]
    nsel = sel_ref.shape[0]
    n_tiles = ncp // BLOCK_Q
    q_t = _gather_heads_to_lanes(q_ref)
    width = SUBTILES * BLOCK_Q
    eye = _identity_tile()

    slabs_per_tile = GROUP_LANES // SLAB
    lhs = [jnp.concatenate([eye, kc_ref[t * BLOCK_Q:(t + 1) * BLOCK_Q, :]], axis=1) for t in range(n_tiles)]
    items = [(sl, t) for sl in range(SUBTILES * slabs_per_tile) for t in range(n_tiles)]

    def scores(k):
        sl, t = items[k]
        u, part = sl // slabs_per_tile, sl % slabs_per_tile
        shifted = (nq - 1 - u) % 2
        row0 = pl.multiple_of((nq - 1 - (SUBTILES * i + u)) * 8 + t * BLOCK_Q - 8 * shifted, 16)
        bias = bias16_ref[shifted, pl.ds(row0, BLOCK_Q), part * SLAB:(part + 1) * SLAB]
        return jnp.dot(lhs[t], jnp.concatenate([bias, q_t[:, sl * SLAB:(sl + 1) * SLAB]], axis=0),
                       preferred_element_type=F32)

    q_pos = i * width + lax.broadcasted_iota(jnp.int32, (1, width), 1)
    sees_any = q_pos >= CMP_BLOCK - 1
    psum = [[jnp.zeros((BLOCK_Q, BLOCK_Q), F32) for _ in range(n_tiles)] for _ in range(SUBTILES)]
    pending = [scores(k) for k in range(min(LOOKAHEAD, len(items)))]
    for sl in range(SUBTILES * slabs_per_tile):
        u, part = sl // slabs_per_tile, sl % slabs_per_tile
        s = [pending.pop(0) for _ in range(n_tiles)]
        for k in range(sl * n_tiles + LOOKAHEAD, min((sl + 1) * n_tiles + LOOKAHEAD, len(items))):
            pending.append(scores(k))
        m = functools.reduce(jnp.maximum, [jnp.max(c, axis=0, keepdims=True) for c in s])
        e = [jnp.exp(c - m) for c in s]
        l = functools.reduce(jnp.add, [jnp.sum(c, axis=0, keepdims=True) for c in e])
        seen = sees_any[:, u * BLOCK_Q:(u + 1) * BLOCK_Q]
        inv = jnp.where(jnp.concatenate([seen] * (SLAB // BLOCK_Q), axis=1), 1.0 / l, 0.0)
        o_slab = jnp.zeros((HEAD_DIM, SLAB), F32)
        for t in range(n_tiles):
            p = e[t] * inv
            psum[u][t] = psum[u][t] + functools.reduce(
                jnp.add, [p[:, r * BLOCK_Q:(r + 1) * BLOCK_Q] for r in range(SLAB // BLOCK_Q)])
            o_slab = o_slab + jnp.dot(vct_ref[:, t * BLOCK_Q:(t + 1) * BLOCK_Q], p.astype(BF16),
                                      preferred_element_type=F32)
        oc_ref[u, :, part * SLAB:(part + 1) * SLAB] = o_slab.astype(oc_ref.dtype)

    psum = jnp.concatenate([jnp.concatenate(pu, axis=0) for pu in psum], axis=1)
    hi = psum.astype(BF16)
    lo = (psum - hi.astype(F32)).astype(BF16)
    ov = ov_ref[...]
    score = jnp.dot(ov, hi, preferred_element_type=F32) + jnp.dot(ov, lo, preferred_element_type=F32)

    j_io = lax.broadcasted_iota(jnp.int32, (nsel, width), 0)
    qpos = i * width + lax.broadcasted_iota(jnp.int32, (nsel, width), 1)
    causal = j_io * SEL_BLOCK <= qpos
    back = qpos // SEL_BLOCK - j_io
    forced = (j_io == 0) | ((back >= 0) & (back < SEL_LOCAL))
    score = jnp.where(causal, jnp.where(forced, FORCE_SCORE, score), -1.0)
    slab_rows = lax.broadcasted_iota(jnp.int32, (8, width), 0)
    slabs = [score[8 * g:8 * (g + 1), :] for g in range(nsel // 8)]
    ranks = [jnp.zeros((8, width), F32) for _ in slabs]
    for r in range(nsel):
        row = jnp.broadcast_to(score[r:r + 1, :], (8, width))
        for g, slab in enumerate(slabs):
            if g > r // 8:
                ahead = row >= slab
            elif g < r // 8:
                ahead = row > slab
            else:
                ranks[g] = ranks[g] + jnp.where(slab_rows > r % 8, jnp.where(row >= slab, 1.0, 0.0),
                                                jnp.where(row > slab, 1.0, 0.0))
                continue
            ranks[g] = ranks[g] + jnp.where(ahead, 1.0, 0.0)
    rank = jnp.concatenate(ranks, axis=0)
    sel_ref[...] = jnp.where((rank < min(SEL_TOPK, nsel)) & causal, 1.0, 0.0).astype(sel_ref.dtype)


def _cmp_attention(h_t, cn, ct, vb, overlap_t, batch, seq):
    nq = seq // BLOCK_Q
    ncp = seq // CMP_STRIDE
    nsel = seq // SEL_BLOCK
    q_blk0 = SWA_HEADS * HEAD_DIM // GROUP_COLS
    nblk = nq // SUBTILES
    width = SUBTILES * BLOCK_Q
    return pl.pallas_call(
        functools.partial(_cmp_kernel, nq=nq),
        grid=(GROUPS, batch, nblk),
        in_specs=[
            pl.BlockSpec((GROUP_COLS, width), lambda g, b, i: (q_blk0 + g, b * nblk + i)),
            pl.BlockSpec((None, None, None, ncp, HEAD_DIM), lambda g, b, i: (0, g, b, 0, 0)),
            pl.BlockSpec((None, None, None, HEAD_DIM, ncp), lambda g, b, i: (1, g, b, 0, 0)),
            pl.BlockSpec((None, HEADS_PER_GROUP, nq + 2, BLOCK_Q), lambda g, b, i: (g, 0, 0, 0)),
            pl.BlockSpec((nsel, ncp), lambda g, b, i: (0, 0)),
        ],
        out_specs=[
            pl.BlockSpec((None, None, SUBTILES, HEAD_DIM, GROUP_LANES), lambda g, b, i: (b, g, i, 0, 0)),
            pl.BlockSpec((None, None, nsel, width), lambda g, b, i: (b, g, 0, i)),
        ],
        out_shape=[jax.ShapeDtypeStruct((batch, GROUPS, nq, HEAD_DIM, GROUP_LANES), BF16),
                   jax.ShapeDtypeStruct((batch, GROUPS, nsel, seq), F32)],
        scratch_shapes=[pltpu.VMEM((2 * nq * 8, GROUP_LANES), F32), pltpu.VMEM((BLOCK_Q, GROUP_LANES), F32),
                        pltpu.VMEM((2, 2 * nq * 8, GROUP_LANES), BF16)],
        compiler_params=_params(3),
        name="nsa_cmp_select",
    )(h_t, cn, ct, vb, overlap_t)


N_WIN_PAIRS = -(-(NSA_WINDOW - 1) // (SUBTILES * BLOCK_Q)) + 1
N_WIN_SLOTS = SUBTILES * N_WIN_PAIRS + 1
SEL_ROWS = 16
SEL_STEPS = 2


def _selwin_kernel(q_ref, ksel_ref, kwin_ref, vsel_ref, vwin_ref, sel_ref, oc_ref, gate_ref, vbs_ref, vbw_ref,
                   o_ref, tsel_ref, twin_ref, accs_ref, accw_ref, *, nq):
    b, blk = pl.program_id(1), pl.program_id(2)

    @pl.when((b == 0) & (blk == 0))
    def _():
        _build_skew_table(vbs_ref, tsel_ref, nq + 1)
        _build_skew_table(vbw_ref, twin_ref, N_WIN_SLOTS)

    q_t = _gather_heads_to_lanes(q_ref)
    lanes = q_t.shape[1]
    eye = _identity_tile()
    pair = SUBTILES * BLOCK_Q
    m0 = jnp.full((1, lanes), NEG_INF, F32)
    accs_ref[...] = jnp.zeros_like(accs_ref)
    accw_ref[...] = jnp.zeros_like(accw_ref)
    blocks_per_tile = BLOCK_Q // SEL_BLOCK
    slabs_per_tile = GROUP_LANES // SLAB

    def attend(m, steps, k_ref, tbl_ref, v_ref, acc_ref, q_ext_fn, reach=None):
        tiles = [(n, jnp.clip(blk - n, 0, blk), j) for n in steps for j in range(SUBTILES)]
        lhs = [jnp.concatenate([eye, k_ref[pl.ds(pl.multiple_of(p * pair + j * BLOCK_Q, BLOCK_Q), BLOCK_Q), :]],
                               axis=1) for _, p, j in tiles]
        q_exts = [q_ext_fn(p, j) for _, p, j in tiles]

        def rhs(t, slab):
            n, _, j = tiles[t]
            u, part = slab // slabs_per_tile, slab % slabs_per_tile
            slot = jnp.where(n <= blk, 2 * n + u - j + 1, 0)
            bias = tbl_ref[slot, :, part * SLAB:(part + 1) * SLAB]
            return jnp.concatenate([bias, q_exts[t][:, slab * SLAB:(slab + 1) * SLAB]], axis=0)

        def values(t, slab):
            _, p, j = tiles[t]
            return v_ref[p, :, j * BLOCK_Q:(j + 1) * BLOCK_Q]

        def masked(t, slab):
            n, _, j = tiles[t]
            distance = 2 * n + slab // slabs_per_tile - j
            return distance < 0 or distance >= reach

        return _slabwise_softmax_step(m, acc_ref, lhs, rhs, values, masked if reach is not None else None)

    attend(m0, list(range(N_WIN_PAIRS)), kwin_ref, twin_ref, vwin_ref, accw_ref, lambda p, j: q_t,
           reach=-(-(NSA_WINDOW + BLOCK_Q - 1) // BLOCK_Q))

    def sel_q_ext(p, j):
        per_pair = SUBTILES * blocks_per_tile
        group = sel_ref[pl.ds(pl.multiple_of((p // 2) * 2 * per_pair, 8), 2 * per_pair), :]
        lo, hi = (group[half * per_pair + j * blocks_per_tile:half * per_pair + (j + 1) * blocks_per_tile]
                  for half in range(2))
        neg = jnp.where(jnp.where(p % 2 == 1, hi, lo) > 0.5, 0.0, NEG_INF)
        rows = jnp.concatenate([neg[:, u * BLOCK_Q:(u + 1) * BLOCK_Q]
                                for u in range(SUBTILES) for _ in range(HEADS_PER_GROUP)], axis=1)
        rows = jnp.concatenate([rows, jnp.zeros((SEL_ROWS - blocks_per_tile, lanes), F32)], axis=0)
        return jnp.concatenate([q_t, rows.astype(BF16)], axis=0)

    def sel_body(k, m):
        return attend(m, [SEL_STEPS * k + r for r in range(SEL_STEPS)], ksel_ref, tsel_ref, vsel_ref, accs_ref,
                      sel_q_ext)

    lax.fori_loop(0, (blk + SEL_STEPS) // SEL_STEPS, sel_body, m0)

    def gate_row(branch):
        g = jnp.concatenate([gate_ref[branch, h:h + 1, u * BLOCK_Q:(u + 1) * BLOCK_Q]
                             for u in range(SUBTILES) for h in range(HEADS_PER_GROUP)], axis=1)
        return jax.nn.sigmoid(g.astype(F32))

    o_c = jnp.concatenate([oc_ref[u] for u in range(SUBTILES)], axis=1).astype(F32)
    out = (gate_row(0) * o_c + gate_row(1) * _normalized(accs_ref[...])
           + gate_row(2) * _normalized(accw_ref[...]))
    for u in range(SUBTILES):
        o_ref[u * BLOCK_Q:(u + 1) * BLOCK_Q, :] = _heads_to_columns(
            out[:, u * GROUP_LANES:(u + 1) * GROUP_LANES]).astype(o_ref.dtype)


def _selwin_attention(h_t, ksel_aug, kk, vsel_aug, vwin_aug, sel, oc_t, gates_t, vb_sel, vb_win, batch, seq):
    nq = seq // BLOCK_Q
    nblk = nq // SUBTILES
    tokens = batch * seq
    q_blk0 = SWA_HEADS * HEAD_DIM // GROUP_COLS
    pair = SUBTILES * BLOCK_Q
    lanes = SUBTILES * GROUP_LANES
    return pl.pallas_call(
        functools.partial(_selwin_kernel, nq=nq),
        grid=(GROUPS, batch, nblk),
        in_specs=[
            pl.BlockSpec((GROUP_COLS, pair), lambda g, b, i: (q_blk0 + g, b * nblk + i)),
            pl.BlockSpec((None, seq, HEAD_DIM + SEL_ROWS), lambda g, b, i: (g, b, 0)),
            pl.BlockSpec((None, None, seq, HEAD_DIM), lambda g, b, i: (3, g, b, 0)),
            pl.BlockSpec((None, None, nblk, V_AUG_ROWS, pair), lambda g, b, i: (g, b, 0, 0, 0)),
            pl.BlockSpec((None, None, nblk, V_AUG_ROWS, pair), lambda g, b, i: (g, b, 0, 0, 0)),
            pl.BlockSpec((None, None, seq // SEL_BLOCK, pair), lambda g, b, i: (b, g, 0, i)),
            pl.BlockSpec((None, None, SUBTILES, HEAD_DIM, GROUP_LANES), lambda g, b, i: (b, g, i, 0, 0)),
            pl.BlockSpec((None, 3, HEADS_PER_GROUP, pair), lambda g, b, i: (g, 0, 0, b * nblk + i)),
            pl.BlockSpec((None, HEADS_PER_GROUP, nq + 2, BLOCK_Q), lambda g, b, i: (g, 0, 0, 0)),
            pl.BlockSpec((None, HEADS_PER_GROUP, N_WIN_SLOTS + 1, BLOCK_Q), lambda g, b, i: (g, 0, 0, 0)),
        ],
        out_specs=pl.BlockSpec((pair, GROUP_COLS), lambda g, b, i: (b * nblk + i, g)),
        out_shape=jax.ShapeDtypeStruct((tokens, NSA_HEADS * HEAD_DIM), BF16),
        scratch_shapes=[pltpu.VMEM((nq + 1, BLOCK_Q, GROUP_LANES), BF16),
                        pltpu.VMEM((N_WIN_SLOTS, BLOCK_Q, GROUP_LANES), BF16),
                        pltpu.VMEM((V_AUG_ROWS, lanes), F32),
                        pltpu.VMEM((V_AUG_ROWS, lanes), F32)],
        compiler_params=_params(3),
        name="nsa_sel_win",
    )(h_t, ksel_aug, kk, vsel_aug, vwin_aug, sel, oc_t, gates_t, vb_sel, vb_win)


ROW_SPLITS = 2
MLP_ROW_SPLITS = 4


def _layer_norm(y, g_ref, b_ref):
    mu = jnp.mean(y, axis=-1, keepdims=True)
    yc = y - mu
    var = jnp.mean(yc * yc, axis=-1, keepdims=True)
    return yc * lax.rsqrt(var + LN_EPS) * g_ref[...] + b_ref[...]


def _merge_kernel(x_ref, ya_ref, yb_ref, wga_ref, wgb_ref, wa_ref, wb_ref, o_ref):
    x = x_ref[...]
    ga = jax.nn.sigmoid(jnp.dot(x, wga_ref[...], preferred_element_type=F32))
    gb = jax.nn.sigmoid(jnp.dot(x, wgb_ref[...], preferred_element_type=F32))
    a = jnp.dot(ya_ref[...], wa_ref[...], preferred_element_type=F32)
    bb = jnp.dot(yb_ref[...], wb_ref[...], preferred_element_type=F32)
    o_ref[...] = (ga * a + gb * bb).astype(o_ref.dtype)


def _merge(x_bf, y_a, y_b, w_ga, w_gb, w_a, w_b, *, tm, tn):
    tokens, d = x_bf.shape
    ya_cols, yb_cols = y_a.shape[1], y_b.shape[1]
    return pl.pallas_call(
        _merge_kernel,
        grid=(tokens // tm, d // tn),
        in_specs=[
            pl.BlockSpec((tm, d), lambda i, j: (i, 0)),
            pl.BlockSpec((tm, ya_cols), lambda i, j: (i, 0)),
            pl.BlockSpec((tm, yb_cols), lambda i, j: (i, 0)),
            pl.BlockSpec((d, tn), lambda i, j: (0, j)),
            pl.BlockSpec((d, tn), lambda i, j: (0, j)),
            pl.BlockSpec((ya_cols, tn), lambda i, j: (0, j)),
            pl.BlockSpec((yb_cols, tn), lambda i, j: (0, j)),
        ],
        out_specs=pl.BlockSpec((tm, tn), lambda i, j: (i, j)),
        out_shape=jax.ShapeDtypeStruct((tokens, d), BF16),
        compiler_params=_params(2),
        name="branch_merge",
    )(x_bf, y_a, y_b, w_ga, w_gb, w_a, w_b)


def _mix_ln_kernel(m_ref, x_ref, w_ref, g_ref, b_ref, o_ref):
    rows = m_ref.shape[0] // ROW_SPLITS
    for r in range(ROW_SPLITS):
        sl = slice(r * rows, (r + 1) * rows)
        mix = jnp.dot(m_ref[sl, :], w_ref[...], preferred_element_type=F32)
        o_ref[sl, :] = _layer_norm(DN_ALPHA * x_ref[sl, :] + mix, g_ref, b_ref)


def _mix_ln(merged, x, w_mix, ln_g, ln_b, *, tm):
    tokens, d = x.shape
    return pl.pallas_call(
        _mix_ln_kernel,
        grid=(tokens // tm,),
        in_specs=[
            pl.BlockSpec((tm, d), lambda i: (i, 0)),
            pl.BlockSpec((tm, d), lambda i: (i, 0)),
            pl.BlockSpec((d, d), lambda i: (0, 0)),
            pl.BlockSpec((1, d), lambda i: (0, 0)),
            pl.BlockSpec((1, d), lambda i: (0, 0)),
        ],
        out_specs=pl.BlockSpec((tm, d), lambda i: (i, 0)),
        out_shape=jax.ShapeDtypeStruct((tokens, d), F32),
        compiler_params=_params(1),
        name="mix_out_ln1",
    )(merged, x, w_mix, ln_g, ln_b)


def _xa_kernel(h_ref, wq_ref, k_ref, v_ref, wo_ref, g_ref, b_ref, o_ref):
    h = h_ref[...]
    q = jnp.dot(h.astype(BF16), wq_ref[...], preferred_element_type=F32) * (XA_HEAD_DIM ** -0.5)
    q = q.astype(BF16)
    outs = []
    for hd in range(XA_HEADS):
        cols = slice(hd * XA_HEAD_DIM, (hd + 1) * XA_HEAD_DIM)
        s = lax.dot_general(q[:, cols], k_ref[:, cols], (((1,), (1,)), ((), ())), preferred_element_type=F32)
        p = jnp.exp(s - jnp.max(s, axis=-1, keepdims=True))
        l = jnp.sum(p, axis=-1, keepdims=True)
        o = jnp.dot(p.astype(BF16), v_ref[:, cols], preferred_element_type=F32)
        outs.append(o * (1.0 / l))
    o = jnp.concatenate(outs, axis=1).astype(BF16)
    xa = jnp.dot(o, wo_ref[...], preferred_element_type=F32)
    o_ref[...] = _layer_norm(DN_ALPHA * h + xa, g_ref, b_ref)


def _cross_attention_ln(h, kv_mem, w_q, w_o, ln_g, ln_b, batch, seq, *, tm):
    tokens, d = h.shape
    mem_len = kv_mem.shape[0] // batch
    xa_dim = XA_HEADS * XA_HEAD_DIM
    nt = seq // tm
    return pl.pallas_call(
        _xa_kernel,
        grid=(batch, nt),
        in_specs=[
            pl.BlockSpec((tm, d), lambda b, i: (b * nt + i, 0)),
            pl.BlockSpec((d, xa_dim), lambda b, i: (0, 0)),
            pl.BlockSpec((mem_len, xa_dim), lambda b, i: (b, 0)),
            pl.BlockSpec((mem_len, xa_dim), lambda b, i: (b, 1)),
            pl.BlockSpec((xa_dim, d), lambda b, i: (0, 0)),
            pl.BlockSpec((1, d), lambda b, i: (0, 0)),
            pl.BlockSpec((1, d), lambda b, i: (0, 0)),
        ],
        out_specs=pl.BlockSpec((tm, d), lambda b, i: (b * nt + i, 0)),
        out_shape=jax.ShapeDtypeStruct((tokens, d), F32),
        compiler_params=_params(2),
        name="cross_attention_ln2",
    )(h, w_q, kv_mem, kv_mem, w_o, ln_g, ln_b)


def _mlp_kernel(h_ref, w1_ref, w2_ref, g_ref, b_ref, o_ref, hb_ref):
    j = pl.program_id(1)

    @pl.when(j == 0)
    def _():
        hb_ref[...] = h_ref[...].astype(BF16)
        o_ref[...] = jnp.zeros_like(o_ref)

    rows = hb_ref.shape[0] // MLP_ROW_SPLITS
    subs = [slice(r * rows, (r + 1) * rows) for r in range(MLP_ROW_SPLITS)]
    ups = [jnp.dot(hb_ref[sl, :], w1_ref[...], preferred_element_type=F32) for sl in subs]
    for sl, u in zip(subs, ups):
        u = jnp.square(jnp.maximum(u, 0.0)).astype(BF16)
        o_ref[sl, :] += jnp.dot(u, w2_ref[...], preferred_element_type=F32)

    @pl.when(j == pl.num_programs(1) - 1)
    def _():
        o_ref[...] = _layer_norm(DN_ALPHA * h_ref[...] + o_ref[...], g_ref, b_ref)


def _mlp_ln(h, w1, w2, ln_g, ln_b, *, tm, tf):
    tokens, d = h.shape
    d_ff = w1.shape[1]
    return pl.pallas_call(
        _mlp_kernel,
        grid=(tokens // tm, d_ff // tf),
        in_specs=[
            pl.BlockSpec((tm, d), lambda i, j: (i, 0)),
            pl.BlockSpec((d, tf), lambda i, j: (0, j)),
            pl.BlockSpec((tf, d), lambda i, j: (j, 0)),
            pl.BlockSpec((1, d), lambda i, j: (0, 0)),
            pl.BlockSpec((1, d), lambda i, j: (0, 0)),
        ],
        out_specs=pl.BlockSpec((tm, d), lambda i, j: (i, 0)),
        out_shape=jax.ShapeDtypeStruct((tokens, d), F32),
        scratch_shapes=[pltpu.VMEM((tm, d), BF16)],
        compiler_params=_params(2),
        name="mlp_ln3",
    )(h, w1, w2, ln_g, ln_b)


def _overlap_t(seq):
    ncp, nsel = seq // CMP_STRIDE, seq // SEL_BLOCK
    c_start = np.arange(ncp)[None, :] * CMP_STRIDE
    s_start = np.arange(nsel)[:, None] * SEL_BLOCK
    ov = (c_start < s_start + SEL_BLOCK) & (c_start + CMP_BLOCK > s_start) & (np.arange(ncp)[None, :] < ncp - 1)
    return jnp.asarray(ov, BF16)


def kernel(x, mem, w_in, attn_sinks, rel_bias_table, cmp_pos_k, cmp_w1_k, cmp_w2_k, cmp_pos_v, cmp_w1_v, cmp_w2_v,
           w_branch_swa, w_branch_nsa, w_mix_out, ln1_g, ln1_b, xa_w_q, xa_w_kv, xa_w_o, ln2_g, ln2_b,
           mlp_w1, mlp_w2, ln3_g, ln3_b):
    batch, seq, d = x.shape
    tokens = batch * seq
    nq = seq // BLOCK_Q
    assert w_in.shape[0] == 1, "one layer"
    assert seq % (max(SWA_TILES, 8) * BLOCK_Q) == 0 and d % 512 == 0

    w = w_in[0]
    sizes = (SWA_HEADS * HEAD_DIM, 128, 128, NSA_HEADS * HEAD_DIM, 128, 128, 128, 128, 128, 128, 3 * NSA_HEADS, d, d)
    offs = np.concatenate([[0], np.cumsum(sizes)])
    (w_qa, w_ka, w_va, w_qb, w_kc, w_vc, w_ks, w_vs, w_kw, w_vw, w_gn, w_ga, w_gb) = [
        w[:, offs[n]:offs[n + 1]] for n in range(len(sizes))]
    w_rows = jnp.concatenate([w_ka, w_kc, w_ks, w_kw, w_vc], axis=1).astype(BF16)
    w_gn_pad = jnp.pad(w_gn, ((0, 0), (0, 128 - w_gn.shape[1])))
    w_cols_t = jnp.concatenate([w_qa, w_qb, w_va, w_vs, w_vw, w_gn_pad], axis=1).T.astype(BF16)

    x2 = x.reshape(tokens, d)

    h_t, x_bf, kk = _input_projections(x2, w_cols_t, w_rows, tm=512)

    def value_tiles(row0, width):
        v = h_t[row0:row0 + 128].reshape(GROUPS, HEAD_DIM, batch, seq // width, width).transpose(0, 2, 3, 1, 4)
        ones = jnp.ones(v.shape[:3] + (V_AUG_ROWS - HEAD_DIM, width), BF16)
        return jnp.concatenate([v, ones], axis=3)

    vswa_aug = value_tiles(2048, BLOCK_Q)
    vsel_aug, vwin_aug = value_tiles(2176, 2 * BLOCK_Q), value_tiles(2304, 2 * BLOCK_Q)
    gates_t = h_t[2432:2432 + 3 * NSA_HEADS].reshape(GROUPS, HEADS_PER_GROUP, 3, tokens).transpose(0, 2, 1, 3)
    block_of_key = (jnp.arange(tokens) % BLOCK_Q) // SEL_BLOCK
    onehot = (block_of_key[:, None] == jnp.arange(SEL_ROWS)[None, :]).astype(BF16)
    ksel_aug = jnp.concatenate([kk[2], jnp.broadcast_to(onehot, (GROUPS, tokens, SEL_ROWS))], axis=2)

    by_dist = rel_bias_table[_rel_bucket(jnp.arange(seq))].astype(F32).T
    vb_swa = _bias_by_distance(by_dist[:SWA_HEADS], 4, window=SWA_WINDOW)
    vb_sel = _bias_by_distance(by_dist[SWA_HEADS:], nq + 2)
    vb_win = _bias_by_distance(by_dist[SWA_HEADS:], N_WIN_SLOTS + 1, window=NSA_WINDOW)
    vb_cmp = vb_sel
    sink_rows = jnp.repeat(attn_sinks[0].astype(F32), BLOCK_Q).reshape(GROUPS, 1, GROUP_LANES)

    y_a = _swa_attention(h_t, kk, vswa_aug, sink_rows, vb_swa, batch, seq)

    ncp = seq // CMP_STRIDE
    chunk_w = CMP_STRIDE * HEAD_DIM
    chunks = jnp.stack([kk[1], kk[4]]).reshape(2, GROUPS, batch, ncp, chunk_w)
    pos = jnp.stack([cmp_pos_k[0], cmp_pos_v[0]]).astype(F32).reshape(2, 2, 1, chunk_w)
    w1 = jnp.stack([cmp_w1_k[0], cmp_w1_v[0]]).astype(BF16).reshape(2, 2, chunk_w, -1)
    w2 = jnp.stack([cmp_w2_k[0], cmp_w2_v[0]]).astype(BF16)
    cn, ct = _compress(chunks, pos, w1, w2, w2.transpose(0, 2, 1))
    oc_t, sel = _cmp_attention(h_t, cn, ct, vb_cmp, _overlap_t(seq), batch, seq)
    y_b = _selwin_attention(h_t, ksel_aug, kk, vsel_aug, vwin_aug, sel, oc_t, gates_t, vb_sel, vb_win, batch, seq)

    merged = _merge(x_bf, y_a, y_b, w_ga.astype(BF16), w_gb.astype(BF16),
                    w_branch_swa[0].astype(BF16), w_branch_nsa[0].astype(BF16), tm=min(1024, tokens), tn=1024)
    h1 = _mix_ln(merged, x2, w_mix_out[0].astype(BF16), ln1_g, ln1_b, tm=512)

    mem_bf = mem.reshape(-1, d).astype(BF16)
    kv_mem = _matmul(mem_bf, xa_w_kv[0].astype(BF16), tm=mem_bf.shape[0], tn=512)
    h2 = _cross_attention_ln(h1, kv_mem, xa_w_q[0].astype(BF16), xa_w_o[0].astype(BF16), ln2_g, ln2_b,
                             batch, seq, tm=1024)

    h3 = _mlp_ln(h2, mlp_w1[0].astype(BF16), mlp_w2[0].astype(BF16), ln3_g, ln3_b, tm=1024, tf=512)
    return h3.reshape(batch, seq, d)
```

```python
import functools
import math

import numpy as np
import jax
import jax.numpy as jnp
from jax import lax
from jax.experimental import pallas as pl
from jax.experimental.pallas import tpu as pltpu

F32 = jnp.float32
BF16 = jnp.bfloat16

HEAD_DIM = 64
BLOCK_Q = 128
SWA_HEADS = 16
SWA_KV_HEADS = 2
SWA_WINDOW = 128
NSA_HEADS = 16
NSA_KV_HEADS = 2
CMP_BLOCK = 32
CMP_STRIDE = 16
SEL_BLOCK = 64
SEL_TOPK = 16
SEL_LOCAL = 2
NSA_WINDOW = 512
REL_BUCKETS = 32
REL_MAX_DIST = 4096
XA_HEADS = 4
XA_HEAD_DIM = 128
DEPTH = 1
DN_ALPHA = (2.0 * DEPTH) ** 0.25
LN_EPS = 1e-5
NEG_INF = -1e30
FORCE_SCORE = 1e4

GROUPS = 2
HEADS_PER_GROUP = 8
GROUP_LANES = HEADS_PER_GROUP * BLOCK_Q
GROUP_COLS = HEADS_PER_GROUP * HEAD_DIM
SUBTILES = 2

V7X_VMEM_LIMIT_BYTES = 56 * 1024 * 1024


def _params(n_axes):
    return pltpu.CompilerParams(dimension_semantics=("arbitrary",) * n_axes,
                                vmem_limit_bytes=V7X_VMEM_LIMIT_BYTES)


def _mm_kernel(a_ref, b_ref, o_ref):
    o_ref[...] = jnp.dot(a_ref[...], b_ref[...], preferred_element_type=F32).astype(o_ref.dtype)


def _matmul(a, b, *, tm, tn, out_dtype=BF16):
    m, k = a.shape
    n = b.shape[1]
    return pl.pallas_call(
        _mm_kernel,
        grid=(m // tm, n // tn),
        in_specs=[pl.BlockSpec((tm, k), lambda i, j: (i, 0)), pl.BlockSpec((k, tn), lambda i, j: (0, j))],
        out_specs=pl.BlockSpec((tm, tn), lambda i, j: (i, j)),
        out_shape=jax.ShapeDtypeStruct((m, n), out_dtype),
        compiler_params=_params(2),
        name="mem_kv_projection",
    )(a, b)


def _in_proj_kernel(x_ref, wt_ref, wr_ref, ht_ref, xb_ref, kk_ref):
    xb = x_ref[...].astype(BF16)
    xb_ref[...] = xb
    ht_ref[...] = lax.dot_general(wt_ref[...], xb, (((1,), (1,)), ((), ())),
                                  preferred_element_type=F32).astype(ht_ref.dtype)
    rows = jnp.dot(xb, wr_ref[...], preferred_element_type=F32)
    for c in range(kk_ref.shape[0]):
        for g in range(GROUPS):
            col = (c * GROUPS + g) * HEAD_DIM
            kk_ref[c, g] = rows[:, col:col + HEAD_DIM].astype(kk_ref.dtype)


def _input_projections(x, w_cols_t, w_rows, *, tm):
    tokens, d = x.shape
    cols = w_cols_t.shape[0]
    n_rows = w_rows.shape[1] // (GROUPS * HEAD_DIM)
    return pl.pallas_call(
        _in_proj_kernel,
        grid=(tokens // tm,),
        in_specs=[
            pl.BlockSpec((tm, d), lambda i: (i, 0)),
            pl.BlockSpec((cols, d), lambda i: (0, 0)),
            pl.BlockSpec((d, w_rows.shape[1]), lambda i: (0, 0)),
        ],
        out_specs=[
            pl.BlockSpec((cols, tm), lambda i: (0, i)),
            pl.BlockSpec((tm, d), lambda i: (i, 0)),
            pl.BlockSpec((n_rows, GROUPS, tm, HEAD_DIM), lambda i: (0, 0, i, 0)),
        ],
        out_shape=[jax.ShapeDtypeStruct((cols, tokens), BF16),
                   jax.ShapeDtypeStruct((tokens, d), BF16),
                   jax.ShapeDtypeStruct((n_rows, GROUPS, tokens, HEAD_DIM), BF16)],
        compiler_params=_params(1),
        name="input_projections",
    )(x, w_cols_t, w_rows)


def _rel_bucket(dist):
    exact = REL_BUCKETS // 2
    d = jnp.maximum(dist, 0)
    log_ratio = jnp.log(jnp.maximum(d, 1).astype(F32) / exact) / math.log(REL_MAX_DIST / exact)
    large = jnp.minimum(exact + (log_ratio * (REL_BUCKETS - exact)).astype(jnp.int32), REL_BUCKETS - 1)
    return jnp.where(d < exact, d, large)


def _bias_by_distance(by_dist, rows, *, window=None):
    heads, seq = by_dist.shape
    if window is not None:
        by_dist = jnp.where(jnp.arange(seq) < window, by_dist, NEG_INF)
    padded = jnp.pad(by_dist, ((0, 0), (2 * BLOCK_Q, 0)), constant_values=NEG_INF)[:, :rows * BLOCK_Q]
    return padded.reshape(heads // HEADS_PER_GROUP, HEADS_PER_GROUP, rows, BLOCK_Q)


def _build_skew_table(vb_ref, tbl_ref, n_tiles):
    def body(d, carry):
        for h in range(HEADS_PER_GROUP):
            lo = jnp.broadcast_to(vb_ref[h, pl.ds(d, 1), :], (BLOCK_Q, BLOCK_Q))
            hi = jnp.broadcast_to(vb_ref[h, pl.ds(d + 1, 1), :], (BLOCK_Q, BLOCK_Q))
            y = pltpu.roll(jnp.concatenate([lo, hi], axis=1), 0, 1, stride=1, stride_axis=0)
            tbl_ref[d, :, h * BLOCK_Q:(h + 1) * BLOCK_Q] = y[:, BLOCK_Q:].astype(tbl_ref.dtype)
        return carry

    lax.fori_loop(0, n_tiles, body, 0)


def _build_cmp_bias_table(vb_ref, tbl_ref, skew_ref, nq):
    per_tile = BLOCK_Q // CMP_STRIDE
    assert per_tile == 8 and CMP_BLOCK - 1 + CMP_STRIDE * (per_tile - 1) - BLOCK_Q == CMP_STRIDE - 1
    tbl_ref[pl.ds(nq * 8, nq * 8), :] = jnp.full((nq * 8, GROUP_LANES), NEG_INF, F32)

    def body(d, last_row):
        for h in range(HEADS_PER_GROUP):
            lo = jnp.broadcast_to(vb_ref[h, pl.ds(d + 1, 1), :], (BLOCK_Q, BLOCK_Q))
            hi = jnp.broadcast_to(vb_ref[h, pl.ds(d + 2, 1), :], (BLOCK_Q, BLOCK_Q))
            y = pltpu.roll(jnp.concatenate([lo, hi], axis=1), 0, 1, stride=1, stride_axis=0)
            skew_ref[:, h * BLOCK_Q:(h + 1) * BLOCK_Q] = y[:, BLOCK_Q:]
        offsets = [CMP_BLOCK - 1 + CMP_STRIDE * c for c in range(per_tile - 1)]
        rows = [skew_ref[k:k + 1, :] for k in offsets] + [last_row]
        tbl_ref[pl.ds(pl.multiple_of((nq - 1 - d) * 8, 8), 8), :] = jnp.concatenate(rows, axis=0)
        return skew_ref[CMP_STRIDE - 1:CMP_STRIDE, :]

    lax.fori_loop(0, nq, body, jnp.full((1, GROUP_LANES), NEG_INF, F32))


def _gather_heads_to_lanes(q_ref):
    q = jnp.concatenate([q_ref[h * HEAD_DIM:(h + 1) * HEAD_DIM, u * BLOCK_Q:(u + 1) * BLOCK_Q]
                         for u in range(q_ref.shape[1] // BLOCK_Q) for h in range(HEADS_PER_GROUP)], axis=1)
    return q * jnp.asarray(HEAD_DIM ** -0.5, q.dtype)


def _heads_to_columns(o_t):
    stacked = jnp.concatenate([o_t[:, h * BLOCK_Q:(h + 1) * BLOCK_Q] for h in range(HEADS_PER_GROUP)], axis=0)
    return stacked.T


def _identity_tile():
    r = lax.broadcasted_iota(jnp.int32, (BLOCK_Q, BLOCK_Q), 0)
    c = lax.broadcasted_iota(jnp.int32, (BLOCK_Q, BLOCK_Q), 1)
    return jnp.where(r == c, 1.0, 0.0).astype(BF16)


SLAB = 2 * BLOCK_Q
LOOKAHEAD = 12


def _slabwise_softmax_step(m, acc_ref, lhs_tiles, rhs_fn, v_fn, masked_fn=None):
    per_slab = [[j for j in range(len(lhs_tiles)) if masked_fn is None or not masked_fn(j, sl)]
                for sl in range(m.shape[1] // SLAB)]
    items = [(sl, j) for sl, tiles in enumerate(per_slab) for j in tiles]

    def scores(k):
        sl, j = items[k]
        lhs = lhs_tiles[j](sl) if callable(lhs_tiles[j]) else lhs_tiles[j]
        return jnp.dot(lhs, rhs_fn(j, sl), preferred_element_type=F32)

    pending = [scores(k) for k in range(min(LOOKAHEAD, len(items)))]
    issued = len(pending)
    new_m = []
    for sl, tiles in enumerate(per_slab):
        lanes = slice(sl * SLAB, (sl + 1) * SLAB)
        m_s, acc_s = m[:, lanes], acc_ref[:, lanes]
        for c in range(0, len(tiles), 2):
            chunk = tiles[c:c + 2]
            m_old, ps = m_s, []
            for j in chunk:
                s = pending.pop(0)
                if issued < len(items):
                    pending.append(scores(issued))
                    issued += 1
                m_next = jnp.maximum(m_s, jnp.max(s, axis=0, keepdims=True))
                ps = [q * jnp.exp(m_s - m_next).astype(BF16) for q in ps]
                ps.append(jnp.exp((s - m_next).astype(BF16)))
                m_s = m_next
            v = jnp.concatenate([v_fn(j, sl) for j in chunk], axis=1)
            acc_s = jnp.exp(m_old - m_s) * acc_s + jnp.dot(v, jnp.concatenate(ps, axis=0),
                                                           preferred_element_type=F32)
        acc_ref[:, lanes] = acc_s
        new_m.append(m_s)
    return jnp.concatenate(new_m, axis=1)


V_AUG_ROWS = HEAD_DIM + 16


def _normalized(acc):
    return acc[:HEAD_DIM] * (1.0 / acc[HEAD_DIM:HEAD_DIM + 1])


def _swa_kernel(q_ref, kprev_ref, kcur_ref, vprev_ref, vcur_ref, sink_ref, vb_ref, o_ref, tbl_ref, acc_ref):
    b, i = pl.program_id(1), pl.program_id(2)

    @pl.when((b == 0) & (i == 0))
    def _():
        _build_skew_table(vb_ref, tbl_ref, 3)

    eye = _identity_tile()
    q_t = _gather_heads_to_lanes(q_ref)
    lanes = q_t.shape[1]
    slabs_per_tile = GROUP_LANES // SLAB
    m0 = jnp.concatenate([sink_ref[...]] * SWA_TILES, axis=1)
    acc_ref[...] = jnp.where(lax.broadcasted_iota(jnp.int32, (V_AUG_ROWS, lanes), 0) < HEAD_DIM, 0.0, 1.0)

    def key_tile(j):
        def at(slab):
            u = slab // slabs_per_tile - j
            k_t = kprev_ref[...] if u < 0 else kcur_ref[u * BLOCK_Q:(u + 1) * BLOCK_Q, :]
            return jnp.concatenate([eye, k_t], axis=1)
        return at

    def rhs(j, slab):
        u, part = slab // slabs_per_tile, slab % slabs_per_tile
        slot = 1 + j if (u > 0 or j == 0) else jnp.where(i > 0, 2, 0)
        return jnp.concatenate([tbl_ref[slot, :, part * SLAB:(part + 1) * SLAB],
                                q_t[:, slab * SLAB:(slab + 1) * SLAB]], axis=0)

    def values(j, slab):
        u = slab // slabs_per_tile - j
        return vprev_ref[...] if u < 0 else vcur_ref[u]

    _slabwise_softmax_step(m0, acc_ref, [key_tile(0), key_tile(1)], rhs, values)
    out = _normalized(acc_ref[...])
    for u in range(SWA_TILES):
        o_ref[u * BLOCK_Q:(u + 1) * BLOCK_Q, :] = _heads_to_columns(
            out[:, u * GROUP_LANES:(u + 1) * GROUP_LANES]).astype(o_ref.dtype)


SWA_TILES = 16


def _swa_attention(h_t, kk, v_aug, sink_rows, vb, batch, seq):
    nq = seq // BLOCK_Q
    nblk = nq // SWA_TILES
    width = SWA_TILES * BLOCK_Q
    tokens = batch * seq

    def prev_tile(i):
        return jnp.maximum(i * SWA_TILES - 1, 0)

    return pl.pallas_call(
        _swa_kernel,
        grid=(GROUPS, batch, nblk),
        in_specs=[
            pl.BlockSpec((GROUP_COLS, width), lambda g, b, i: (g, b * nblk + i)),
            pl.BlockSpec((None, None, BLOCK_Q, HEAD_DIM), lambda g, b, i: (0, g, b * nq + prev_tile(i), 0)),
            pl.BlockSpec((None, None, width, HEAD_DIM), lambda g, b, i: (0, g, b * nblk + i, 0)),
            pl.BlockSpec((None, None, None, V_AUG_ROWS, BLOCK_Q), lambda g, b, i: (g, b, prev_tile(i), 0, 0)),
            pl.BlockSpec((None, None, SWA_TILES, V_AUG_ROWS, BLOCK_Q), lambda g, b, i: (g, b, i, 0, 0)),
            pl.BlockSpec((None, 1, GROUP_LANES), lambda g, b, i: (g, 0, 0)),
            pl.BlockSpec((None, HEADS_PER_GROUP, 4, BLOCK_Q), lambda g, b, i: (g, 0, 0, 0)),
        ],
        out_specs=pl.BlockSpec((width, GROUP_COLS), lambda g, b, i: (b * nblk + i, g)),
        out_shape=jax.ShapeDtypeStruct((tokens, SWA_HEADS * HEAD_DIM), BF16),
        scratch_shapes=[pltpu.VMEM((3, BLOCK_Q, GROUP_LANES), BF16),
                        pltpu.VMEM((V_AUG_ROWS, SWA_TILES * GROUP_LANES), F32)],
        compiler_params=_params(3),
        name="swa_attention",
    )(h_t, kk, kk, v_aug, v_aug, sink_rows, vb)


def _compress_kernel(c_ref, pos_ref, w1_ref, w2_ref, w2t_ref, cn_ref, ct_ref):
    c = c_ref[...].astype(F32)
    top = (c + pos_ref[0]).astype(BF16)
    bot = (c + pos_ref[1]).astype(BF16)
    a = jnp.dot(top, w1_ref[0], preferred_element_type=F32)
    bm = jnp.dot(bot, w1_ref[1], preferred_element_type=F32)
    n = a.shape[0]
    pre = a + pltpu.roll(bm, n - 1, 0)
    hid = jax.nn.gelu(pre).astype(BF16)
    cn_ref[...] = jnp.dot(hid, w2_ref[...], preferred_element_type=F32).astype(cn_ref.dtype)
    ct_ref[...] = lax.dot_general(w2t_ref[...], hid, (((1,), (1,)), ((), ())),
                                  preferred_element_type=F32).astype(ct_ref.dtype)


def _compress(chunks, pos, w1, w2, w2t):
    _, g, b, ncp, width = chunks.shape
    hidden = w1.shape[-1]
    return pl.pallas_call(
        _compress_kernel,
        grid=(2, g, b),
        in_specs=[
            pl.BlockSpec((None, None, None, ncp, width), lambda t, g, b: (t, g, b, 0, 0)),
            pl.BlockSpec((None, 2, 1, width), lambda t, g, b: (t, 0, 0, 0)),
            pl.BlockSpec((None, 2, width, hidden), lambda t, g, b: (t, 0, 0, 0)),
            pl.BlockSpec((None, hidden, HEAD_DIM), lambda t, g, b: (t, 0, 0)),
            pl.BlockSpec((None, HEAD_DIM, hidden), lambda t, g, b: (t, 0, 0)),
        ],
        out_specs=[
            pl.BlockSpec((None, None, None, ncp, HEAD_DIM), lambda t, g, b: (t, g, b, 0, 0)),
            pl.BlockSpec((None, None, None, HEAD_DIM, ncp), lambda t, g, b: (t, g, b, 0, 0)),
        ],
        out_shape=[jax.ShapeDtypeStruct((2, g, b, ncp, HEAD_DIM), BF16),
                   jax.ShapeDtypeStruct((2, g, b, HEAD_DIM, ncp), BF16)],
        compiler_params=_params(3),
        name="nsa_compress",
    )(chunks, pos, w1, w2, w2t)


def _cmp_kernel(q_ref, kc_ref, vct_ref, vb_ref, ov_ref, oc_ref, sel_ref, bias_ref, skew_ref, bias16_ref, *, nq):
    b, i = pl.program_id(1), pl.program_id(2)

    @pl.when((b == 0) & (i == 0))
    def _():
        _build_cmp_bias_table(vb_ref, bias_ref, skew_ref, nq)
        rows = bias_ref.shape[0]
        bias16_ref[0] = bias_ref[...].astype(BF16)
        bias16_ref[1, 0:rows - 16, :] = bias_ref[8:rows - 8, :].astype(BF16)

    ncp = kc_ref.shape[0]
    nsel = sel_ref.shape[0]
    n_tiles = ncp // BLOCK_Q
    q_t = _gather_heads_to_lanes(q_ref)
    width = SUBTILES * BLOCK_Q
    eye = _identity_tile()

    slabs_per_tile = GROUP_LANES // SLAB
    lhs = [jnp.concatenate([eye, kc_ref[t * BLOCK_Q:(t + 1) * BLOCK_Q, :]], axis=1) for t in range(n_tiles)]
    items = [(sl, t) for sl in range(SUBTILES * slabs_per_tile) for t in range(n_tiles)]

    def scores(k):
        sl, t = items[k]
        u, part = sl // slabs_per_tile, sl % slabs_per_tile
        shifted = (nq - 1 - u) % 2
        row0 = pl.multiple_of((nq - 1 - (SUBTILES * i + u)) * 8 + t * BLOCK_Q - 8 * shifted, 16)
        bias = bias16_ref[shifted, pl.ds(row0, BLOCK_Q), part * SLAB:(part + 1) * SLAB]
        return jnp.dot(lhs[t], jnp.concatenate([bias, q_t[:, sl * SLAB:(sl + 1) * SLAB]], axis=0),
                       preferred_element_type=F32)

    q_pos = i * width + lax.broadcasted_iota(jnp.int32, (1, width), 1)
    sees_any = q_pos >= CMP_BLOCK - 1
    psum = [[jnp.zeros((BLOCK_Q, BLOCK_Q), F32) for _ in range(n_tiles)] for _ in range(SUBTILES)]
    pending = [scores(k) for k in range(min(LOOKAHEAD, len(items)))]
    for sl in range(SUBTILES * slabs_per_tile):
        u, part = sl // slabs_per_tile, sl % slabs_per_tile
        s = [pending.pop(0) for _ in range(n_tiles)]
        for k in range(sl * n_tiles + LOOKAHEAD, min((sl + 1) * n_tiles + LOOKAHEAD, len(items))):
            pending.append(scores(k))
        m = functools.reduce(jnp.maximum, [jnp.max(c, axis=0, keepdims=True) for c in s])
        e = [jnp.exp(c - m) for c in s]
        l = functools.reduce(jnp.add, [jnp.sum(c, axis=0, keepdims=True) for c in e])
        seen = sees_any[:, u * BLOCK_Q:(u + 1) * BLOCK_Q]
        inv = jnp.where(jnp.concatenate([seen] * (SLAB // BLOCK_Q), axis=1), 1.0 / l, 0.0)
        o_slab = jnp.zeros((HEAD_DIM, SLAB), F32)
        for t in range(n_tiles):
            p = e[t] * inv
            psum[u][t] = psum[u][t] + functools.reduce(
                jnp.add, [p[:, r * BLOCK_Q:(r + 1) * BLOCK_Q] for r in range(SLAB // BLOCK_Q)])
            o_slab = o_slab + jnp.dot(vct_ref[:, t * BLOCK_Q:(t + 1) * BLOCK_Q], p.astype(BF16),
                                      preferred_element_type=F32)
        oc_ref[u, :, part * SLAB:(part + 1) * SLAB] = o_slab.astype(oc_ref.dtype)

    psum = jnp.concatenate([jnp.concatenate(pu, axis=0) for pu in psum], axis=1)
    hi = psum.astype(BF16)
    lo = (psum - hi.astype(F32)).astype(BF16)
    ov = ov_ref[...]
    score = jnp.dot(ov, hi, preferred_element_type=F32) + jnp.dot(ov, lo, preferred_element_type=F32)

    j_io = lax.broadcasted_iota(jnp.int32, (nsel, width), 0)
    qpos = i * width + lax.broadcasted_iota(jnp.int32, (nsel, width), 1)
    causal = j_io * SEL_BLOCK <= qpos
    back = qpos // SEL_BLOCK - j_io
    forced = (j_io == 0) | ((back >= 0) & (back < SEL_LOCAL))
    score = jnp.where(causal, jnp.where(forced, FORCE_SCORE, score), -1.0)
    slab_rows = lax.broadcasted_iota(jnp.int32, (8, width), 0)
    slabs = [score[8 * g:8 * (g + 1), :] for g in range(nsel // 8)]
    ranks = [jnp.zeros((8, width), F32) for _ in slabs]
    for r in range(nsel):
        row = jnp.broadcast_to(score[r:r + 1, :], (8, width))
        for g, slab in enumerate(slabs):
            if g > r // 8:
                ahead = row >= slab
            elif g < r // 8:
                ahead = row > slab
            else:
                ranks[g] = ranks[g] + jnp.where(slab_rows > r % 8, jnp.where(row >= slab, 1.0, 0.0),
                                                jnp.where(row > slab, 1.0, 0.0))
                continue
            ranks[g] = ranks[g] + jnp.where(ahead, 1.0, 0.0)
    rank = jnp.concatenate(ranks, axis=0)
    sel_ref[...] = jnp.where((rank < min(SEL_TOPK, nsel)) & causal, 1.0, 0.0).astype(sel_ref.dtype)


def _cmp_attention(h_t, cn, ct, vb, overlap_t, batch, seq):
    nq = seq // BLOCK_Q
    ncp = seq // CMP_STRIDE
    nsel = seq // SEL_BLOCK
    q_blk0 = SWA_HEADS * HEAD_DIM // GROUP_COLS
    nblk = nq // SUBTILES
    width = SUBTILES * BLOCK_Q
    return pl.pallas_call(
        functools.partial(_cmp_kernel, nq=nq),
        grid=(GROUPS, batch, nblk),
        in_specs=[
            pl.BlockSpec((GROUP_COLS, width), lambda g, b, i: (q_blk0 + g, b * nblk + i)),
            pl.BlockSpec((None, None, None, ncp, HEAD_DIM), lambda g, b, i: (0, g, b, 0, 0)),
            pl.BlockSpec((None, None, None, HEAD_DIM, ncp), lambda g, b, i: (1, g, b, 0, 0)),
            pl.BlockSpec((None, HEADS_PER_GROUP, nq + 2, BLOCK_Q), lambda g, b, i: (g, 0, 0, 0)),
            pl.BlockSpec((nsel, ncp), lambda g, b, i: (0, 0)),
        ],
        out_specs=[
            pl.BlockSpec((None, None, SUBTILES, HEAD_DIM, GROUP_LANES), lambda g, b, i: (b, g, i, 0, 0)),
            pl.BlockSpec((None, None, nsel, width), lambda g, b, i: (b, g, 0, i)),
        ],
        out_shape=[jax.ShapeDtypeStruct((batch, GROUPS, nq, HEAD_DIM, GROUP_LANES), BF16),
                   jax.ShapeDtypeStruct((batch, GROUPS, nsel, seq), F32)],
        scratch_shapes=[pltpu.VMEM((2 * nq * 8, GROUP_LANES), F32), pltpu.VMEM((BLOCK_Q, GROUP_LANES), F32),
                        pltpu.VMEM((2, 2 * nq * 8, GROUP_LANES), BF16)],
        compiler_params=_params(3),
        name="nsa_cmp_select",
    )(h_t, cn, ct, vb, overlap_t)


N_WIN_PAIRS = -(-(NSA_WINDOW - 1) // (SUBTILES * BLOCK_Q)) + 1
N_WIN_SLOTS = SUBTILES * N_WIN_PAIRS + 1
SEL_ROWS = 16
SEL_STEPS = 2


def _selwin_kernel(q_ref, ksel_ref, kwin_ref, vsel_ref, vwin_ref, sel_ref, oc_ref, gate_ref, vbs_ref, vbw_ref,
                   o_ref, tsel_ref, twin_ref, accs_ref, accw_ref, *, nq):
    b, blk = pl.program_id(1), pl.program_id(2)

    @pl.when((b == 0) & (blk == 0))
    def _():
        _build_skew_table(vbs_ref, tsel_ref, nq + 1)
        _build_skew_table(vbw_ref, twin_ref, N_WIN_SLOTS)

    q_t = _gather_heads_to_lanes(q_ref)
    lanes = q_t.shape[1]
    eye = _identity_tile()
    pair = SUBTILES * BLOCK_Q
    m0 = jnp.full((1, lanes), NEG_INF, F32)
    accs_ref[...] = jnp.zeros_like(accs_ref)
    accw_ref[...] = jnp.zeros_like(accw_ref)
    blocks_per_tile = BLOCK_Q // SEL_BLOCK
    slabs_per_tile = GROUP_LANES // SLAB

    def attend(m, steps, k_ref, tbl_ref, v_ref, acc_ref, q_ext_fn, reach=None):
        tiles = [(n, jnp.clip(blk - n, 0, blk), j) for n in steps for j in range(SUBTILES)]
        lhs = [jnp.concatenate([eye, k_ref[pl.ds(pl.multiple_of(p * pair + j * BLOCK_Q, BLOCK_Q), BLOCK_Q), :]],
                               axis=1) for _, p, j in tiles]
        q_exts = [q_ext_fn(p, j) for _, p, j in tiles]

        def rhs(t, slab):
            n, _, j = tiles[t]
            u, part = slab // slabs_per_tile, slab % slabs_per_tile
            slot = jnp.where(n <= blk, 2 * n + u - j + 1, 0)
            bias = tbl_ref[slot, :, part * SLAB:(part + 1) * SLAB]
            return jnp.concatenate([bias, q_exts[t][:, slab * SLAB:(slab + 1) * SLAB]], axis=0)

        def values(t, slab):
            _, p, j = tiles[t]
            return v_ref[p, :, j * BLOCK_Q:(j + 1) * BLOCK_Q]

        def masked(t, slab):
            n, _, j = tiles[t]
            distance = 2 * n + slab // slabs_per_tile - j
            return distance < 0 or distance >= reach

        return _slabwise_softmax_step(m, acc_ref, lhs, rhs, values, masked if reach is not None else None)

    attend(m0, list(range(N_WIN_PAIRS)), kwin_ref, twin_ref, vwin_ref, accw_ref, lambda p, j: q_t,
           reach=-(-(NSA_WINDOW + BLOCK_Q - 1) // BLOCK_Q))

    def sel_q_ext(p, j):
        per_pair = SUBTILES * blocks_per_tile
        group = sel_ref[pl.ds(pl.multiple_of((p // 2) * 2 * per_pair, 8), 2 * per_pair), :]
        lo, hi = (group[half * per_pair + j * blocks_per_tile:half * per_pair + (j + 1) * blocks_per_tile]
                  for half in range(2))
        neg = jnp.where(jnp.where(p % 2 == 1, hi, lo) > 0.5, 0.0, NEG_INF)
        rows = jnp.concatenate([neg[:, u * BLOCK_Q:(u + 1) * BLOCK_Q]
                                for u in range(SUBTILES) for _ in range(HEADS_PER_GROUP)], axis=1)
        rows = jnp.concatenate([rows, jnp.zeros((SEL_ROWS - blocks_per_tile, lanes), F32)], axis=0)
        return jnp.concatenate([q_t, rows.astype(BF16)], axis=0)

    def sel_body(k, m):
        return attend(m, [SEL_STEPS * k + r for r in range(SEL_STEPS)], ksel_ref, tsel_ref, vsel_ref, accs_ref,
                      sel_q_ext)

    lax.fori_loop(0, (blk + SEL_STEPS) // SEL_STEPS, sel_body, m0)

    def gate_row(branch):
        g = jnp.concatenate([gate_ref[branch, h:h + 1, u * BLOCK_Q:(u + 1) * BLOCK_Q]
                             for u in range(SUBTILES) for h in range(HEADS_PER_GROUP)], axis=1)
        return jax.nn.sigmoid(g.astype(F32))

    o_c = jnp.concatenate([oc_ref[u] for u in range(SUBTILES)], axis=1).astype(F32)
    out = (gate_row(0) * o_c + gate_row(1) * _normalized(accs_ref[...])
           + gate_row(2) * _normalized(accw_ref[...]))
    for u in range(SUBTILES):
        o_ref[u * BLOCK_Q:(u + 1) * BLOCK_Q, :] = _heads_to_columns(
            out[:, u * GROUP_LANES:(u + 1) * GROUP_LANES]).astype(o_ref.dtype)


def _selwin_attention(h_t, ksel_aug, kk, vsel_aug, vwin_aug, sel, oc_t, gates_t, vb_sel, vb_win, batch, seq):
    nq = seq // BLOCK_Q
    nblk = nq // SUBTILES
    tokens = batch * seq
    q_blk0 = SWA_HEADS * HEAD_DIM // GROUP_COLS
    pair = SUBTILES * BLOCK_Q
    lanes = SUBTILES * GROUP_LANES
    return pl.pallas_call(
        functools.partial(_selwin_kernel, nq=nq),
        grid=(GROUPS, batch, nblk),
        in_specs=[
            pl.BlockSpec((GROUP_COLS, pair), lambda g, b, i: (q_blk0 + g, b * nblk + i)),
            pl.BlockSpec((None, seq, HEAD_DIM + SEL_ROWS), lambda g, b, i: (g, b, 0)),
            pl.BlockSpec((None, None, seq, HEAD_DIM), lambda g, b, i: (3, g, b, 0)),
            pl.BlockSpec((None, None, nblk, V_AUG_ROWS, pair), lambda g, b, i: (g, b, 0, 0, 0)),
            pl.BlockSpec((None, None, nblk, V_AUG_ROWS, pair), lambda g, b, i: (g, b, 0, 0, 0)),
            pl.BlockSpec((None, None, seq // SEL_BLOCK, pair), lambda g, b, i: (b, g, 0, i)),
            pl.BlockSpec((None, None, SUBTILES, HEAD_DIM, GROUP_LANES), lambda g, b, i: (b, g, i, 0, 0)),
            pl.BlockSpec((None, 3, HEADS_PER_GROUP, pair), lambda g, b, i: (g, 0, 0, b * nblk + i)),
            pl.BlockSpec((None, HEADS_PER_GROUP, nq + 2, BLOCK_Q), lambda g, b, i: (g, 0, 0, 0)),
            pl.BlockSpec((None, HEADS_PER_GROUP, N_WIN_SLOTS + 1, BLOCK_Q), lambda g, b, i: (g, 0, 0, 0)),
        ],
        out_specs=pl.BlockSpec((pair, GROUP_COLS), lambda g, b, i: (b * nblk + i, g)),
        out_shape=jax.ShapeDtypeStruct((tokens, NSA_HEADS * HEAD_DIM), BF16),
        scratch_shapes=[pltpu.VMEM((nq + 1, BLOCK_Q, GROUP_LANES), BF16),
                        pltpu.VMEM((N_WIN_SLOTS, BLOCK_Q, GROUP_LANES), BF16),
                        pltpu.VMEM((V_AUG_ROWS, lanes), F32),
                        pltpu.VMEM((V_AUG_ROWS, lanes), F32)],
        compiler_params=_params(3),
        name="nsa_sel_win",
    )(h_t, ksel_aug, kk, vsel_aug, vwin_aug, sel, oc_t, gates_t, vb_sel, vb_win)


ROW_SPLITS = 2


def _layer_norm(y, g_ref, b_ref):
    mu = jnp.mean(y, axis=-1, keepdims=True)
    yc = y - mu
    var = jnp.mean(yc * yc, axis=-1, keepdims=True)
    return yc * lax.rsqrt(var + LN_EPS) * g_ref[...] + b_ref[...]


def _merge_kernel(x_ref, ya_ref, yb_ref, wga_ref, wgb_ref, wa_ref, wb_ref, o_ref):
    x = x_ref[...]
    ga = jax.nn.sigmoid(jnp.dot(x, wga_ref[...], preferred_element_type=F32))
    gb = jax.nn.sigmoid(jnp.dot(x, wgb_ref[...], preferred_element_type=F32))
    a = jnp.dot(ya_ref[...], wa_ref[...], preferred_element_type=F32)
    bb = jnp.dot(yb_ref[...], wb_ref[...], preferred_element_type=F32)
    o_ref[...] = (ga * a + gb * bb).astype(o_ref.dtype)


def _merge(x_bf, y_a, y_b, w_ga, w_gb, w_a, w_b, *, tm, tn):
    tokens, d = x_bf.shape
    ya_cols, yb_cols = y_a.shape[1], y_b.shape[1]
    return pl.pallas_call(
        _merge_kernel,
        grid=(tokens // tm, d // tn),
        in_specs=[
            pl.BlockSpec((tm, d), lambda i, j: (i, 0)),
            pl.BlockSpec((tm, ya_cols), lambda i, j: (i, 0)),
            pl.BlockSpec((tm, yb_cols), lambda i, j: (i, 0)),
            pl.BlockSpec((d, tn), lambda i, j: (0, j)),
            pl.BlockSpec((d, tn), lambda i, j: (0, j)),
            pl.BlockSpec((ya_cols, tn), lambda i, j: (0, j)),
            pl.BlockSpec((yb_cols, tn), lambda i, j: (0, j)),
        ],
        out_specs=pl.BlockSpec((tm, tn), lambda i, j: (i, j)),
        out_shape=jax.ShapeDtypeStruct((tokens, d), BF16),
        compiler_params=_params(2),
        name="branch_merge",
    )(x_bf, y_a, y_b, w_ga, w_gb, w_a, w_b)


def _mix_ln_kernel(m_ref, x_ref, w_ref, g_ref, b_ref, o_ref):
    rows = m_ref.shape[0] // ROW_SPLITS
    for r in range(ROW_SPLITS):
        sl = slice(r * rows, (r + 1) * rows)
        mix = jnp.dot(m_ref[sl, :], w_ref[...], preferred_element_type=F32)
        o_ref[sl, :] = _layer_norm(DN_ALPHA * x_ref[sl, :] + mix, g_ref, b_ref)


def _mix_ln(merged, x, w_mix, ln_g, ln_b, *, tm):
    tokens, d = x.shape
    return pl.pallas_call(
        _mix_ln_kernel,
        grid=(tokens // tm,),
        in_specs=[
            pl.BlockSpec((tm, d), lambda i: (i, 0)),
            pl.BlockSpec((tm, d), lambda i: (i, 0)),
            pl.BlockSpec((d, d), lambda i: (0, 0)),
            pl.BlockSpec((1, d), lambda i: (0, 0)),
            pl.BlockSpec((1, d), lambda i: (0, 0)),
        ],
        out_specs=pl.BlockSpec((tm, d), lambda i: (i, 0)),
        out_shape=jax.ShapeDtypeStruct((tokens, d), F32),
        compiler_params=_params(1),
        name="mix_out_ln1",
    )(merged, x, w_mix, ln_g, ln_b)


def _xa_kernel(h_ref, wq_ref, k_ref, v_ref, wo_ref, g_ref, b_ref, o_ref):
    h = h_ref[...]
    q = jnp.dot(h.astype(BF16), wq_ref[...], preferred_element_type=F32) * (XA_HEAD_DIM ** -0.5)
    q = q.astype(BF16)
    outs = []
    for hd in range(XA_HEADS):
        cols = slice(hd * XA_HEAD_DIM, (hd + 1) * XA_HEAD_DIM)
        s = lax.dot_general(q[:, cols], k_ref[:, cols], (((1,), (1,)), ((), ())), preferred_element_type=F32)
        p = jnp.exp(s - jnp.max(s, axis=-1, keepdims=True))
        l = jnp.sum(p, axis=-1, keepdims=True)
        o = jnp.dot(p.astype(BF16), v_ref[:, cols], preferred_element_type=F32)
        outs.append(o * (1.0 / l))
    o = jnp.concatenate(outs, axis=1).astype(BF16)
    xa = jnp.dot(o, wo_ref[...], preferred_element_type=F32)
    o_ref[...] = _layer_norm(DN_ALPHA * h + xa, g_ref, b_ref)


def _cross_attention_ln(h, kv_mem, w_q, w_o, ln_g, ln_b, batch, seq, *, tm):
    tokens, d = h.shape
    mem_len = kv_mem.shape[0] // batch
    xa_dim = XA_HEADS * XA_HEAD_DIM
    nt = seq // tm
    return pl.pallas_call(
        _xa_kernel,
        grid=(batch, nt),
        in_specs=[
            pl.BlockSpec((tm, d), lambda b, i: (b * nt + i, 0)),
            pl.BlockSpec((d, xa_dim), lambda b, i: (0, 0)),
            pl.BlockSpec((mem_len, xa_dim), lambda b, i: (b, 0)),
            pl.BlockSpec((mem_len, xa_dim), lambda b, i: (b, 1)),
            pl.BlockSpec((xa_dim, d), lambda b, i: (0, 0)),
            pl.BlockSpec((1, d), lambda b, i: (0, 0)),
            pl.BlockSpec((1, d), lambda b, i: (0, 0)),
        ],
        out_specs=pl.BlockSpec((tm, d), lambda b, i: (b * nt + i, 0)),
        out_shape=jax.ShapeDtypeStruct((tokens, d), F32),
        compiler_params=_params(2),
        name="cross_attention_ln2",
    )(h, w_q, kv_mem, kv_mem, w_o, ln_g, ln_b)


def _mlp_kernel(h_ref, w1_ref, w2_ref, g_ref, b_ref, o_ref, hb_ref):
    j = pl.program_id(1)

    @pl.when(j == 0)
    def _():
        hb_ref[...] = h_ref[...].astype(BF16)
        o_ref[...] = jnp.zeros_like(o_ref)

    rows = hb_ref.shape[0] // ROW_SPLITS
    subs = [slice(r * rows, (r + 1) * rows) for r in range(ROW_SPLITS)]
    ups = [jnp.dot(hb_ref[sl, :], w1_ref[...], preferred_element_type=F32) for sl in subs]
    for sl, u in zip(subs, ups):
        u = jnp.square(jnp.maximum(u, 0.0)).astype(BF16)
        o_ref[sl, :] += jnp.dot(u, w2_ref[...], preferred_element_type=F32)

    @pl.when(j == pl.num_programs(1) - 1)
    def _():
        o_ref[...] = _layer_norm(DN_ALPHA * h_ref[...] + o_ref[...], g_ref, b_ref)


def _mlp_ln(h, w1, w2, ln_g, ln_b, *, tm, tf):
    tokens, d = h.shape
    d_ff = w1.shape[1]
    return pl.pallas_call(
        _mlp_kernel,
        grid=(tokens // tm, d_ff // tf),
        in_specs=[
            pl.BlockSpec((tm, d), lambda i, j: (i, 0)),
            pl.BlockSpec((d, tf), lambda i, j: (0, j)),
            pl.BlockSpec((tf, d), lambda i, j: (j, 0)),
            pl.BlockSpec((1, d), lambda i, j: (0, 0)),
            pl.BlockSpec((1, d), lambda i, j: (0, 0)),
        ],
        out_specs=pl.BlockSpec((tm, d), lambda i, j: (i, 0)),
        out_shape=jax.ShapeDtypeStruct((tokens, d), F32),
        scratch_shapes=[pltpu.VMEM((tm, d), BF16)],
        compiler_params=_params(2),
        name="mlp_ln3",
    )(h, w1, w2, ln_g, ln_b)


def _overlap_t(seq):
    ncp, nsel = seq // CMP_STRIDE, seq // SEL_BLOCK
    c_start = np.arange(ncp)[None, :] * CMP_STRIDE
    s_start = np.arange(nsel)[:, None] * SEL_BLOCK
    ov = (c_start < s_start + SEL_BLOCK) & (c_start + CMP_BLOCK > s_start) & (np.arange(ncp)[None, :] < ncp - 1)
    return jnp.asarray(ov, BF16)


def kernel(x, mem, w_in, attn_sinks, rel_bias_table, cmp_pos_k, cmp_w1_k, cmp_w2_k, cmp_pos_v, cmp_w1_v, cmp_w2_v,
           w_branch_swa, w_branch_nsa, w_mix_out, ln1_g, ln1_b, xa_w_q, xa_w_kv, xa_w_o, ln2_g, ln2_b,
           mlp_w1, mlp_w2, ln3_g, ln3_b):
    batch, seq, d = x.shape
    tokens = batch * seq
    nq = seq // BLOCK_Q
    assert w_in.shape[0] == 1, "one layer"
    assert seq % (max(SWA_TILES, 8) * BLOCK_Q) == 0 and d % 512 == 0

    w = w_in[0]
    sizes = (SWA_HEADS * HEAD_DIM, 128, 128, NSA_HEADS * HEAD_DIM, 128, 128, 128, 128, 128, 128, 3 * NSA_HEADS, d, d)
    offs = np.concatenate([[0], np.cumsum(sizes)])
    (w_qa, w_ka, w_va, w_qb, w_kc, w_vc, w_ks, w_vs, w_kw, w_vw, w_gn, w_ga, w_gb) = [
        w[:, offs[n]:offs[n + 1]] for n in range(len(sizes))]
    w_rows = jnp.concatenate([w_ka, w_kc, w_ks, w_kw, w_vc], axis=1).astype(BF16)
    w_gn_pad = jnp.pad(w_gn, ((0, 0), (0, 128 - w_gn.shape[1])))
    w_cols_t = jnp.concatenate([w_qa, w_qb, w_va, w_vs, w_vw, w_gn_pad], axis=1).T.astype(BF16)

    x2 = x.reshape(tokens, d)

    h_t, x_bf, kk = _input_projections(x2, w_cols_t, w_rows, tm=512)

    def value_tiles(row0, width):
        v = h_t[row0:row0 + 128].reshape(GROUPS, HEAD_DIM, batch, seq // width, width).transpose(0, 2, 3, 1, 4)
        ones = jnp.ones(v.shape[:3] + (V_AUG_ROWS - HEAD_DIM, width), BF16)
        return jnp.concatenate([v, ones], axis=3)

    vswa_aug = value_tiles(2048, BLOCK_Q)
    vsel_aug, vwin_aug = value_tiles(2176, 2 * BLOCK_Q), value_tiles(2304, 2 * BLOCK_Q)
    gates_t = h_t[2432:2432 + 3 * NSA_HEADS].reshape(GROUPS, HEADS_PER_GROUP, 3, tokens).transpose(0, 2, 1, 3)
    block_of_key = (jnp.arange(tokens) % BLOCK_Q) // SEL_BLOCK
    onehot = (block_of_key[:, None] == jnp.arange(SEL_ROWS)[None, :]).astype(BF16)
    ksel_aug = jnp.concatenate([kk[2], jnp.broadcast_to(onehot, (GROUPS, tokens, SEL_ROWS))], axis=2)

    by_dist = rel_bias_table[_rel_bucket(jnp.arange(seq))].astype(F32).T
    vb_swa = _bias_by_distance(by_dist[:SWA_HEADS], 4, window=SWA_WINDOW)
    vb_sel = _bias_by_distance(by_dist[SWA_HEADS:], nq + 2)
    vb_win = _bias_by_distance(by_dist[SWA_HEADS:], N_WIN_SLOTS + 1, window=NSA_WINDOW)
    vb_cmp = vb_sel
    sink_rows = jnp.repeat(attn_sinks[0].astype(F32), BLOCK_Q).reshape(GROUPS, 1, GROUP_LANES)

    y_a = _swa_attention(h_t, kk, vswa_aug, sink_rows, vb_swa, batch, seq)

    ncp = seq // CMP_STRIDE
    chunk_w = CMP_STRIDE * HEAD_DIM
    chunks = jnp.stack([kk[1], kk[4]]).reshape(2, GROUPS, batch, ncp, chunk_w)
    pos = jnp.stack([cmp_pos_k[0], cmp_pos_v[0]]).astype(F32).reshape(2, 2, 1, chunk_w)
    w1 = jnp.stack([cmp_w1_k[0], cmp_w1_v[0]]).astype(BF16).reshape(2, 2, chunk_w, -1)
    w2 = jnp.stack([cmp_w2_k[0], cmp_w2_v[0]]).astype(BF16)
    cn, ct = _compress(chunks, pos, w1, w2, w2.transpose(0, 2, 1))
    oc_t, sel = _cmp_attention(h_t, cn, ct, vb_cmp, _overlap_t(seq), batch, seq)
    y_b = _selwin_attention(h_t, ksel_aug, kk, vsel_aug, vwin_aug, sel, oc_t, gates_t, vb_sel, vb_win, batch, seq)

    merged = _merge(x_bf, y_a, y_b, w_ga.astype(BF16), w_gb.astype(BF16),
                    w_branch_swa[0].astype(BF16), w_branch_nsa[0].astype(BF16), tm=min(1024, tokens), tn=1024)
    h1 = _mix_ln(merged, x2, w_mix_out[0].astype(BF16), ln1_g, ln1_b, tm=512)

    mem_bf = mem.reshape(-1, d).astype(BF16)
    kv_mem = _matmul(mem_bf, xa_w_kv[0].astype(BF16), tm=mem_bf.shape[0], tn=512)
    h2 = _cross_attention_ln(h1, kv_mem, xa_w_q[0].astype(BF16), xa_w_o[0].astype(BF16), ln2_g, ln2_b,
                             batch, seq, tm=1024)

    h3 = _mlp_ln(h2, mlp_w1[0].astype(BF16), mlp_w2[0].astype(BF16), ln3_g, ln3_b, tm=1024, tf=512)
    return h3.reshape(batch, seq, d)
```
